```python
import math
import jax, jax.numpy as jnp
from jax import lax
import numpy as np

D_MODEL = 1024
BATCH = 8
SEQ = 2048
DEPTH = 4

CHUNK = 64
Q_BLOCK = 128
N_MIXERS = 3

SSD_EXPAND = 2
SSD_D_INNER = SSD_EXPAND * D_MODEL
SSD_HEAD_DIM = 64
SSD_HEADS = SSD_D_INNER // SSD_HEAD_DIM
SSD_GROUPS = 4
SSD_HEADS_PER_GROUP = SSD_HEADS // SSD_GROUPS
SSD_STATE = 128
SSD_CONV = 4
SSD_CONV_DIM = SSD_D_INNER + 2 * SSD_GROUPS * SSD_STATE
SSD_IN_DIM = SSD_D_INNER + SSD_CONV_DIM + SSD_HEADS

ATT_HEAD_DIM = 64
ATT_HEADS = D_MODEL // ATT_HEAD_DIM
ATT_DIM = ATT_HEADS * ATT_HEAD_DIM

FFN_DIM = 2816
N_EXPERTS = 8
TOP_K = 2
EXPERT_DIM = 3584

DN_ALPHA = (2.0 * DEPTH) ** 0.25
DN_BETA = (8.0 * DEPTH) ** -0.25
LN_EPS = 1e-5
RMS_EPS = 1e-5

kernel_name = "hybrid_ssd_stickbreak_fox_moe_deepnorm"

F32 = jnp.float32


def layer_norm(x, g, b):
    xf = x.astype(F32)
    mu = jnp.mean(xf, -1, keepdims=True)
    var = jnp.mean(jnp.square(xf - mu), -1, keepdims=True)
    return ((xf - mu) * lax.rsqrt(var + LN_EPS) * g + b).astype(x.dtype)


def causal_dwconv(u, w, b):
    c = u.shape[-1]
    y = lax.conv_general_dilated(u, w[:, None, :], window_strides=(1,),
                                 padding=[(SSD_CONV - 1, 0)],
                                 dimension_numbers=("NWC", "WIO", "NWC"),
                                 feature_group_count=c)
    return y + b


def segsum_exp(a_cs):
    n = a_cs.shape[-1]
    mask = jnp.tril(jnp.ones((n, n), bool))
    diff = a_cs[..., :, None] - a_cs[..., None, :]
    return jnp.where(mask, jnp.exp(jnp.where(mask, diff, 0.0)), 0.0)


def ssd_mixer(u, w_in, conv_w, conv_b, dt_bias, a_log, d_skip, norm_w, w_out):
    bsz, seq, _ = u.shape
    n_chunks = seq // CHUNK
    G, R, P, N = SSD_GROUPS, SSD_HEADS_PER_GROUP, SSD_HEAD_DIM, SSD_STATE
    zxbcdt = u @ w_in
    z, xbc, dt_raw = jnp.split(zxbcdt, [SSD_D_INNER, SSD_D_INNER + SSD_CONV_DIM], axis=-1)
    xbc = jax.nn.silu(causal_dwconv(xbc, conv_w, conv_b))
    xs, bm, cm = jnp.split(xbc, [SSD_D_INNER, SSD_D_INNER + G * N], axis=-1)
    dt = jax.nn.softplus((dt_raw + dt_bias).astype(F32))
    da = dt * (-jnp.exp(a_log.astype(F32)))
    xh = xs.reshape(bsz, seq, SSD_HEADS, P).astype(F32)
    xdt = xh * dt[..., None]

    xc = xdt.reshape(bsz, n_chunks, CHUNK, G, R, P)
    bc = bm.astype(F32).reshape(bsz, n_chunks, CHUNK, G, N)
    cc = cm.astype(F32).reshape(bsz, n_chunks, CHUNK, G, N)
    ac = da.reshape(bsz, n_chunks, CHUNK, G, R).transpose(0, 1, 3, 4, 2)
    a_cs = jnp.cumsum(ac, axis=-1)

    decay_in = segsum_exp(a_cs)
    cb = jnp.einsum("bclgn,bcsgn->bcgls", cc, bc)
    y_diag = jnp.einsum("bcgls,bcgrls,bcsgrp->bclgrp", cb, decay_in, xc)

    decay_to_end = jnp.exp(a_cs[..., -1:] - a_cs)
    chunk_states = jnp.einsum("bclgn,bcgrl,bclgrp->bcgrpn", bc, decay_to_end, xc)
    chunk_decay = jnp.exp(a_cs[..., -1])

    def step(h, inp):
        dec, st = inp
        return dec[..., None, None] * h + st, h

    h0 = jnp.zeros((bsz, G, R, P, N), F32)
    _, h_in = lax.scan(step, h0, (jnp.moveaxis(chunk_decay, 1, 0), jnp.moveaxis(chunk_states, 1, 0)))
    h_in = jnp.moveaxis(h_in, 0, 1)
    y_off = jnp.einsum("bclgn,bcgrpn,bcgrl->bclgrp", cc, h_in, jnp.exp(a_cs))

    y = (y_diag + y_off).reshape(bsz, seq, SSD_HEADS, P) + d_skip.astype(F32)[:, None] * xh
    y = y.reshape(bsz, seq, SSD_D_INNER) * jax.nn.silu(z.astype(F32))
    yg = y.reshape(bsz, seq, G, SSD_D_INNER // G)
    yg = yg * lax.rsqrt(jnp.mean(jnp.square(yg), -1, keepdims=True) + RMS_EPS)
    y = (yg.reshape(bsz, seq, SSD_D_INNER) * norm_w).astype(u.dtype)
    return y @ w_out


def stick_breaking_mixer(u, w_qkv, w_out):
    bsz, seq, _ = u.shape
    qkv = (u @ w_qkv).reshape(bsz, seq, 3, ATT_HEADS, ATT_HEAD_DIM)
    q, k, v = qkv[:, :, 0], qkv[:, :, 1], qkv[:, :, 2]
    scale = ATT_HEAD_DIM ** -0.5
    outs = []
    for blk in range(seq // Q_BLOCK):
        q0 = blk * Q_BLOCK
        kv_len = q0 + Q_BLOCK
        z = jnp.einsum("bqhd,bkhd->bhqk", q[:, q0:kv_len], k[:, :kv_len]).astype(F32) * scale
        t_idx = q0 + jnp.arange(Q_BLOCK)[:, None]
        s_idx = jnp.arange(kv_len)[None, :]
        strict = s_idx < t_idx
        log_keep = jnp.where(strict, jax.nn.log_sigmoid(-z), 0.0)
        after = lax.cumsum(log_keep, axis=3, reverse=True) - log_keep
        w = jnp.where(strict, jnp.exp(jax.nn.log_sigmoid(z) + after), 0.0)
        outs.append(jnp.einsum("bhqk,bkhd->bqhd", w.astype(v.dtype), v[:, :kv_len]))
    o = jnp.concatenate(outs, axis=1).reshape(bsz, seq, ATT_DIM)
    return o @ w_out


def forgetting_mixer(u, w_qkvf, b_f, w_out):
    bsz, seq, _ = u.shape
    proj = u @ w_qkvf
    qkv = proj[..., :3 * ATT_DIM].reshape(bsz, seq, 3, ATT_HEADS, ATT_HEAD_DIM)
    q, k, v = qkv[:, :, 0], qkv[:, :, 1], qkv[:, :, 2]
    log_f = jax.nn.log_sigmoid((proj[..., 3 * ATT_DIM:] + b_f).astype(F32))
    c = jnp.cumsum(log_f, axis=1).transpose(0, 2, 1)
    scale = ATT_HEAD_DIM ** -0.5
    outs = []
    for blk in range(seq // Q_BLOCK):
        q0 = blk * Q_BLOCK
        kv_len = q0 + Q_BLOCK
        s = jnp.einsum("bqhd,bkhd->bhqk", q[:, q0:kv_len], k[:, :kv_len]).astype(F32) * scale
        s = s + c[:, :, q0:kv_len, None] - c[:, :, None, :kv_len]
        t_idx = q0 + jnp.arange(Q_BLOCK)[:, None]
        s_idx = jnp.arange(kv_len)[None, :]
        s = jnp.where(s_idx <= t_idx, s, -jnp.inf)
        p = jax.nn.softmax(s, axis=-1)
        outs.append(jnp.einsum("bhqk,bkhd->bqhd", p.astype(v.dtype), v[:, :kv_len]))
    o = jnp.concatenate(outs, axis=1).reshape(bsz, seq, ATT_DIM)
    return o @ w_out


def dense_swiglu(u, w_gate, w_up, w_down):
    return (jax.nn.silu(u @ w_gate) * (u @ w_up)) @ w_down


def moe_swiglu(u, w_router, we_gate, we_up, we_down):
    bsz, seq, d = u.shape
    xt = u.reshape(-1, d)
    logits = (xt @ w_router).astype(F32)
    top_logits, top_idx = lax.top_k(logits, TOP_K)
    gates = jax.nn.softmax(top_logits, axis=-1)
    flat_e = top_idx.reshape(-1)
    order = jnp.argsort(flat_e)
    tok = order // TOP_K
    xs = xt[tok]
    sizes = jnp.bincount(flat_e, length=N_EXPERTS).astype(jnp.int32)
    h = jax.nn.silu(lax.ragged_dot(xs, we_gate, sizes)) * lax.ragged_dot(xs, we_up, sizes)
    ys = lax.ragged_dot(h, we_down, sizes)
    ys = ys * gates.reshape(-1)[order][:, None].astype(ys.dtype)
    out = jnp.zeros_like(xt).at[tok].add(ys)
    return out.reshape(bsz, seq, d)


def _normal(key, shape, std):
    return jax.random.normal(key, shape, F32) * std


def _init_ssd(key, p):
    ks = jax.random.split(key, 8)
    dt = jnp.exp(jax.random.uniform(ks[3], (SSD_HEADS,), F32, math.log(1e-3), math.log(1e-1)))
    return {
        f"{p}_ssd_w_in": _normal(ks[0], (D_MODEL, SSD_IN_DIM), D_MODEL ** -0.5),
        f"{p}_ssd_conv_w": _normal(ks[1], (SSD_CONV, SSD_CONV_DIM), SSD_CONV ** -0.5),
        f"{p}_ssd_conv_b": _normal(ks[2], (SSD_CONV_DIM,), 0.02),
        f"{p}_ssd_dt_bias": dt + jnp.log(-jnp.expm1(-dt)),
        f"{p}_ssd_a_log": jnp.log(jax.random.uniform(ks[4], (SSD_HEADS,), F32, 1.0, 16.0)),
        f"{p}_ssd_d_skip": 1.0 + _normal(ks[5], (SSD_HEADS,), 0.1),
        f"{p}_ssd_norm_w": 1.0 + _normal(ks[6], (SSD_D_INNER,), 0.02),
        f"{p}_ssd_w_out": _normal(ks[7], (SSD_D_INNER, D_MODEL), DN_BETA * SSD_D_INNER ** -0.5),
    }


def _init_stick(key, p):
    ks = jax.random.split(key, 2)
    return {
        f"{p}_sb_w_qkv": _normal(ks[0], (D_MODEL, 3 * ATT_DIM), D_MODEL ** -0.5),
        f"{p}_sb_w_out": _normal(ks[1], (ATT_DIM, D_MODEL), DN_BETA * ATT_DIM ** -0.5),
    }


def _init_fox(key, p):
    ks = jax.random.split(key, 3)
    return {
        f"{p}_fox_w_qkvf": _normal(ks[0], (D_MODEL, 3 * ATT_DIM + ATT_HEADS), D_MODEL ** -0.5),
        f"{p}_fox_b_f": 1.0 + _normal(ks[1], (ATT_HEADS,), 0.5),
        f"{p}_fox_w_out": _normal(ks[2], (ATT_DIM, D_MODEL), DN_BETA * ATT_DIM ** -0.5),
    }


def _init_ln(key, p):
    ks = jax.random.split(key, 2)
    return {
        f"{p}_g": 1.0 + _normal(ks[0], (D_MODEL,), 0.02),
        f"{p}_b": _normal(ks[1], (D_MODEL,), 0.02),
    }


def _init_dense(key, p):
    ks = jax.random.split(key, 3)
    return {
        f"{p}_ffn_w_gate": _normal(ks[0], (D_MODEL, FFN_DIM), D_MODEL ** -0.5),
        f"{p}_ffn_w_up": _normal(ks[1], (D_MODEL, FFN_DIM), D_MODEL ** -0.5),
        f"{p}_ffn_w_down": _normal(ks[2], (FFN_DIM, D_MODEL), DN_BETA * FFN_DIM ** -0.5),
    }


def _init_moe(key, p):
    ks = jax.random.split(key, 4)
    return {
        f"{p}_moe_w_router": _normal(ks[0], (D_MODEL, N_EXPERTS), D_MODEL ** -0.5),
        f"{p}_moe_w_gate": _normal(ks[1], (N_EXPERTS, D_MODEL, EXPERT_DIM), D_MODEL ** -0.5),
        f"{p}_moe_w_up": _normal(ks[2], (N_EXPERTS, D_MODEL, EXPERT_DIM), D_MODEL ** -0.5),
        f"{p}_moe_w_down": _normal(ks[3], (N_EXPERTS, EXPERT_DIM, D_MODEL), DN_BETA * EXPERT_DIM ** -0.5),
    }


def setup_inputs(seed: int = 0) -> dict:
    key = jax.random.key(seed)
    key, k_x = jax.random.split(key)
    params = {"x": jax.random.normal(k_x, (BATCH, SEQ, D_MODEL), F32)}
    mixer_inits = (_init_ssd, _init_stick, _init_fox)
    for i in range(DEPTH):
        key, k_mix, k_ln1, k_ffn, k_ln2 = jax.random.split(key, 5)
        params.update(mixer_inits[i % N_MIXERS](k_mix, f"l{i}"))
        params.update(_init_ln(k_ln1, f"l{i}_ln_mix"))
        params.update((_init_dense if i % 2 == 0 else _init_moe)(k_ffn, f"l{i}"))
        params.update(_init_ln(k_ln2, f"l{i}_ln_ffn"))
    return params


def reference(x,
              l0_ssd_w_in, l0_ssd_conv_w, l0_ssd_conv_b, l0_ssd_dt_bias, l0_ssd_a_log,
              l0_ssd_d_skip, l0_ssd_norm_w, l0_ssd_w_out, l0_ln_mix_g, l0_ln_mix_b,
              l0_ffn_w_gate, l0_ffn_w_up, l0_ffn_w_down, l0_ln_ffn_g, l0_ln_ffn_b,
              l1_sb_w_qkv, l1_sb_w_out, l1_ln_mix_g, l1_ln_mix_b,
              l1_moe_w_router, l1_moe_w_gate, l1_moe_w_up, l1_moe_w_down, l1_ln_ffn_g, l1_ln_ffn_b,
              l2_fox_w_qkvf, l2_fox_b_f, l2_fox_w_out, l2_ln_mix_g, l2_ln_mix_b,
              l2_ffn_w_gate, l2_ffn_w_up, l2_ffn_w_down, l2_ln_ffn_g, l2_ln_ffn_b,
              l3_ssd_w_in, l3_ssd_conv_w, l3_ssd_conv_b, l3_ssd_dt_bias, l3_ssd_a_log,
              l3_ssd_d_skip, l3_ssd_norm_w, l3_ssd_w_out, l3_ln_mix_g, l3_ln_mix_b,
              l3_moe_w_router, l3_moe_w_gate, l3_moe_w_up, l3_moe_w_down, l3_ln_ffn_g, l3_ln_ffn_b):
    mixers = [
        (ssd_mixer, (l0_ssd_w_in, l0_ssd_conv_w, l0_ssd_conv_b, l0_ssd_dt_bias, l0_ssd_a_log,
                     l0_ssd_d_skip, l0_ssd_norm_w, l0_ssd_w_out), l0_ln_mix_g, l0_ln_mix_b),
        (stick_breaking_mixer, (l1_sb_w_qkv, l1_sb_w_out), l1_ln_mix_g, l1_ln_mix_b),
        (forgetting_mixer, (l2_fox_w_qkvf, l2_fox_b_f, l2_fox_w_out), l2_ln_mix_g, l2_ln_mix_b),
        (ssd_mixer, (l3_ssd_w_in, l3_ssd_conv_w, l3_ssd_conv_b, l3_ssd_dt_bias, l3_ssd_a_log,
                     l3_ssd_d_skip, l3_ssd_norm_w, l3_ssd_w_out), l3_ln_mix_g, l3_ln_mix_b),
    ]
    ffns = [
        (dense_swiglu, (l0_ffn_w_gate, l0_ffn_w_up, l0_ffn_w_down), l0_ln_ffn_g, l0_ln_ffn_b),
        (moe_swiglu, (l1_moe_w_router, l1_moe_w_gate, l1_moe_w_up, l1_moe_w_down), l1_ln_ffn_g, l1_ln_ffn_b),
        (dense_swiglu, (l2_ffn_w_gate, l2_ffn_w_up, l2_ffn_w_down), l2_ln_ffn_g, l2_ln_ffn_b),
        (moe_swiglu, (l3_moe_w_router, l3_moe_w_gate, l3_moe_w_up, l3_moe_w_down), l3_ln_ffn_g, l3_ln_ffn_b),
    ]
    for i in range(DEPTH):
        mix_fn, mix_p, g1, b1 = mixers[i]
        x = layer_norm(DN_ALPHA * x + mix_fn(x, *mix_p), g1, b1)
        ffn_fn, ffn_p, g2, b2 = ffns[i]
        x = layer_norm(DN_ALPHA * x + ffn_fn(x, *ffn_p), g2, b2)
    return x
```

```python
import functools
import math

import jax
import jax.numpy as jnp
from jax import lax
from jax.experimental import pallas as pl
from jax.experimental.pallas import tpu as pltpu

F32 = jnp.float32
BF16 = jnp.bfloat16
HIGHEST = lax.Precision.HIGHEST

LANES = 128
SUBLANES = 8
VMEM_LIMIT_BYTES = 56 * 1024 * 1024

D_MODEL = 1024
DEPTH = 4
SSD_D_INNER = 2048
SSD_HEAD_DIM = 64
SSD_HEADS = 32
SSD_GROUPS = 4
SSD_STATE = 128
SSD_CONV = 4
SSD_GROUP_DIM = SSD_D_INNER // SSD_GROUPS
SSD_BC_DIM = 2 * SSD_GROUPS * SSD_STATE
ATT_HEAD_DIM = 64
ATT_HEADS = 16
ATT_DIM = 1024
N_EXPERTS = 8
TOP_K = 2
DN_ALPHA = (2.0 * DEPTH) ** 0.25
LN_EPS = 1e-5
RMS_EPS = 1e-5
NEG_BIG = -1e30


def _params(*semantics):
    return pltpu.CompilerParams(dimension_semantics=semantics, vmem_limit_bytes=VMEM_LIMIT_BYTES)


def _layer_norm_rows(h, g, b):
    mu = jnp.mean(h, axis=-1, keepdims=True)
    d = h - mu
    var = jnp.mean(d * d, axis=-1, keepdims=True)
    return d * lax.rsqrt(var + LN_EPS) * g + b


def _silu(x):
    return x * jax.nn.sigmoid(x)


def _mm_kernel(x_ref, w_ref, o_ref):
    o_ref[...] = jnp.dot(x_ref[...].astype(BF16), w_ref[...],
                         preferred_element_type=F32).astype(o_ref.dtype)


def matmul(x, w, out_dtype, tm=1024, tn=512):
    m, k = x.shape
    tm = min(tm, m)
    n = w.shape[1]
    return pl.pallas_call(
        _mm_kernel,
        grid=(m // tm, n // tn),
        in_specs=[pl.BlockSpec((tm, k), lambda i, j: (i, 0)),
                  pl.BlockSpec((k, tn), lambda i, j: (0, j))],
        out_specs=pl.BlockSpec((tm, tn), lambda i, j: (i, j)),
        out_shape=jax.ShapeDtypeStruct((m, n), out_dtype),
        compiler_params=_params("parallel", "parallel"),
        name="matmul",
    )(x, w)


def _mm_f32_kernel(x_ref, w_ref, o_ref):
    o_ref[...] = jnp.dot(x_ref[...], w_ref[...], preferred_element_type=F32, precision=HIGHEST)


def matmul_f32(x, w, tm=1024):
    m, k = x.shape
    tm = min(tm, m)
    n = w.shape[1]
    return pl.pallas_call(
        _mm_f32_kernel,
        grid=(m // tm,),
        in_specs=[pl.BlockSpec((tm, k), lambda i: (i, 0)),
                  pl.BlockSpec((k, n), lambda i: (0, 0))],
        out_specs=pl.BlockSpec((tm, n), lambda i: (i, 0)),
        out_shape=jax.ShapeDtypeStruct((m, n), F32),
        compiler_params=_params("parallel"),
        name="matmul_f32",
    )(x, w)


def _mm_ln_kernel(x_ref, w_ref, r_ref, g_ref, b_ref, o_ref):
    y = jnp.dot(x_ref[...], w_ref[...], preferred_element_type=F32)
    o_ref[...] = _layer_norm_rows(DN_ALPHA * r_ref[...] + y, g_ref[...], b_ref[...])


def matmul_deepnorm(x, w, resid, g, b, tm=512):
    m, k = x.shape
    tm = min(tm, m)
    d = w.shape[1]
    return pl.pallas_call(
        _mm_ln_kernel,
        grid=(m // tm,),
        in_specs=[pl.BlockSpec((tm, k), lambda i: (i, 0)),
                  pl.BlockSpec((k, d), lambda i: (0, 0)),
                  pl.BlockSpec((tm, d), lambda i: (i, 0)),
                  pl.BlockSpec((1, d), lambda i: (0, 0)),
                  pl.BlockSpec((1, d), lambda i: (0, 0))],
        out_specs=pl.BlockSpec((tm, d), lambda i: (i, 0)),
        out_shape=jax.ShapeDtypeStruct((m, d), F32),
        compiler_params=_params("parallel"),
        name="matmul_deepnorm",
    )(x, w, resid, g.reshape(1, d), b.reshape(1, d))


def _ffn_kernel(x_ref, wg_ref, wu_ref, wd_ref, g_ref, b_ref, o_ref, xb_ref, acc_ref):
    f = pl.program_id(1)

    @pl.when(f == 0)
    def _():
        xb_ref[...] = x_ref[...].astype(BF16)
        acc_ref[...] = jnp.zeros_like(acc_ref)

    xb = xb_ref[...]
    gate = jnp.dot(xb, wg_ref[...], preferred_element_type=F32)
    up = jnp.dot(xb, wu_ref[...], preferred_element_type=F32)
    h = (_silu(gate) * up).astype(BF16)
    acc_ref[...] += jnp.dot(h, wd_ref[...], preferred_element_type=F32)

    @pl.when(f == pl.num_programs(1) - 1)
    def _():
        o_ref[...] = _layer_norm_rows(DN_ALPHA * x_ref[...] + acc_ref[...], g_ref[...], b_ref[...])


def ffn_deepnorm(x, wg, wu, wd, g, b, tm=1024, tf=256):
    m, d = x.shape
    tm = min(tm, m)
    fdim = wg.shape[1]
    return pl.pallas_call(
        _ffn_kernel,
        grid=(m // tm, fdim // tf),
        in_specs=[pl.BlockSpec((tm, d), lambda i, f: (i, 0)),
                  pl.BlockSpec((d, tf), lambda i, f: (0, f)),
                  pl.BlockSpec((d, tf), lambda i, f: (0, f)),
                  pl.BlockSpec((tf, d), lambda i, f: (f, 0)),
                  pl.BlockSpec((1, d), lambda i, f: (0, 0)),
                  pl.BlockSpec((1, d), lambda i, f: (0, 0))],
        out_specs=pl.BlockSpec((tm, d), lambda i, f: (i, 0)),
        out_shape=jax.ShapeDtypeStruct((m, d), F32),
        scratch_shapes=[pltpu.VMEM((tm, d), BF16), pltpu.VMEM((tm, d), F32)],
        compiler_params=_params("parallel", "arbitrary"),
        name="ffn_deepnorm",
    )(x, wg, wu, wd, g.reshape(1, d), b.reshape(1, d))


def _split_bf16(v, pieces):
    out = []
    r = v
    for _ in range(pieces - 1):
        p = r.astype(BF16)
        out.append(p)
        r = r - p.astype(F32)
    out.append(r.astype(BF16))
    return out


def _expand(v, e_ref, pieces):
    stacked = jnp.concatenate(_split_bf16(v, pieces), axis=1)
    return jnp.dot(stacked, e_ref[...], preferred_element_type=F32)


def _tril_f32(n, strict=False):
    r = lax.broadcasted_iota(jnp.int32, (n, n), 0)
    c = lax.broadcasted_iota(jnp.int32, (n, n), 1)
    return ((r > c) if strict else (r >= c)).astype(F32)


def _ssd_kernel(z_ref, xs_ref, bc_ref, dt_ref, cwx_ref, cbx_ref, cwbc_ref, cbbc_ref,
                dtb_ref, alog_ref, dskip_ref, normw_ref, e64_ref, e128_ref, o_ref,
                convx_ref, convbc_ref, state_ref, xdt_ref, acol_ref, arow_ref, cb_ref, ydiag_ref,
                *, chunk):
    L = chunk
    c = pl.program_id(1)

    @pl.when(c == 0)
    def _():
        state_ref[...] = jnp.zeros_like(state_ref)
        convx_ref[0:SUBLANES, :] = jnp.zeros((SUBLANES, SSD_D_INNER), F32)
        convbc_ref[0:SUBLANES, :] = jnp.zeros((SUBLANES, SSD_BC_DIM), F32)

    @pl.when(c > 0)
    def _():
        convx_ref[0:SUBLANES, :] = convx_ref[L:L + SUBLANES, :]
        convbc_ref[0:SUBLANES, :] = convbc_ref[L:L + SUBLANES, :]

    convx_ref[SUBLANES:SUBLANES + L, :] = xs_ref[...]
    convbc_ref[SUBLANES:SUBLANES + L, :] = bc_ref[...]

    def conv_silu(buf_ref, w_ref, b_ref):
        acc = b_ref[...]
        for k in range(SSD_CONV):
            start = SUBLANES - (SSD_CONV - 1) + k
            acc = acc + w_ref[k:k + 1, :] * buf_ref[start:start + L, :]
        return _silu(acc)

    xs = conv_silu(convx_ref, cwx_ref, cbx_ref)
    bcv = conv_silu(convbc_ref, cwbc_ref, cbbc_ref)

    dt = jax.nn.softplus(dt_ref[...] + dtb_ref[...])
    da = dt * (-jnp.exp(alog_ref[...]))
    a_cs = jnp.dot(_tril_f32(L), da, preferred_element_type=F32, precision=HIGHEST)
    ea = jnp.exp(a_cs)
    dte = jnp.exp(a_cs[L - 1:L, :] - a_cs)

    dt_x = _expand(dt, e64_ref, 2)
    ea_x = _expand(ea, e64_ref, 2)
    dte_x = _expand(dte, e64_ref, 2)
    acol_ref[...] = _expand(a_cs, e128_ref, 3)
    a_t = a_cs.T
    for h in range(SSD_HEADS):
        arow_ref[h] = jnp.broadcast_to(a_t[h:h + 1, :], (SUBLANES, L))

    xdt = xs * dt_x
    xdt_ref[...] = xdt.astype(BF16)
    xdte = (xdt * dte_x).astype(BF16)

    y_off = []
    for g in range(SSD_GROUPS):
        bm = bcv[:, g * SSD_STATE:(g + 1) * SSD_STATE]
        cm = bcv[:, (SSD_GROUPS + g) * SSD_STATE:(SSD_GROUPS + g + 1) * SSD_STATE].astype(BF16)
        cb_ref[g] = lax.dot_general(cm, bm.astype(BF16), (((1,), (1,)), ((), ())),
                                    preferred_element_type=F32)
        gs = slice(g * SSD_GROUP_DIM, (g + 1) * SSD_GROUP_DIM)
        st = state_ref[g]
        y_off.append(jnp.dot(cm, st.astype(BF16), preferred_element_type=F32) * ea_x[:, gs])
        state_ref[g] = st * ea_x[L - 1:L, gs] + jnp.dot(
            bm.T.astype(BF16), xdte[:, gs], preferred_element_type=F32)

    row = lax.broadcasted_iota(jnp.int32, (L, L), 0)
    col = lax.broadcasted_iota(jnp.int32, (L, L), 1)
    causal = row >= col
    head0 = lax.broadcasted_iota(jnp.int32, (L, LANES), 1) < SSD_HEAD_DIM

    def pair_body(p, carry):
        lanes = pl.ds(pl.multiple_of(p * LANES, LANES), LANES)
        xpair = xdt_ref[:, lanes]
        cbg = cb_ref[p // (SSD_HEADS // SSD_GROUPS // 2)]
        ys = []
        for j in range(2):
            h = 2 * p + j
            a_l = acol_ref[:, pl.ds(pl.multiple_of(h * LANES, LANES), LANES)]
            a_l = jnp.concatenate([a_l] * (L // LANES), axis=1)
            a_s = arow_ref[h][0:1, :]
            decay = jnp.exp(jnp.where(causal, a_l - a_s, NEG_BIG))
            ys.append(jnp.dot((cbg * decay).astype(BF16), xpair, preferred_element_type=F32))
        ydiag_ref[:, lanes] = jnp.where(head0, ys[0], ys[1])
        return carry

    lax.fori_loop(0, SSD_HEADS // 2, pair_body, 0)

    y = ydiag_ref[...] + jnp.concatenate(y_off, axis=1) + dskip_ref[...] * xs
    y = y * _silu(z_ref[...])
    parts = []
    for g in range(SSD_GROUPS):
        yg = y[:, g * SSD_GROUP_DIM:(g + 1) * SSD_GROUP_DIM]
        parts.append(yg * lax.rsqrt(jnp.mean(yg * yg, axis=-1, keepdims=True) + RMS_EPS))
    o_ref[...] = (jnp.concatenate(parts, axis=1) * normw_ref[...]).astype(o_ref.dtype)


def _expansion_matrix(width, pieces):
    h = jnp.arange(LANES)[:, None]
    lane = jnp.arange(SSD_HEADS * width)[None, :]
    e = (lane // width == h).astype(BF16)
    return jnp.concatenate([e] * pieces, axis=0)


def ssd_core(zx, dt_raw, conv_w, conv_b, dt_bias, a_log, d_skip, norm_w, batch, seq, chunk=128):
    t = zx.shape[0]
    nc = seq // chunk
    pad = LANES - SSD_HEADS
    row = lambda v: v.reshape(1, -1).astype(F32)
    args = (
        zx, zx, zx, dt_raw,
        conv_w[:, :SSD_D_INNER], row(conv_b[:SSD_D_INNER]),
        conv_w[:, SSD_D_INNER:], row(conv_b[SSD_D_INNER:]),
        row(jnp.pad(dt_bias, (0, pad))), row(jnp.pad(a_log, (0, pad))),
        row(jnp.repeat(d_skip, SSD_HEAD_DIM)), row(norm_w),
        _expansion_matrix(SSD_HEAD_DIM, 2), _expansion_matrix(LANES, 3),
    )
    blk = lambda b, c: (b * nc + c, 0)
    const = lambda b, c: (0, 0)
    full = lambda a: pl.BlockSpec(a.shape, const)
    in_specs = [
        pl.BlockSpec((chunk, SSD_D_INNER), blk),
        pl.BlockSpec((chunk, SSD_D_INNER), lambda b, c: (b * nc + c, 1)),
        pl.BlockSpec((chunk, SSD_BC_DIM), lambda b, c: (b * nc + c, 4)),
        pl.BlockSpec((chunk, LANES), blk),
    ] + [full(a) for a in args[4:]]
    return pl.pallas_call(
        functools.partial(_ssd_kernel, chunk=chunk),
        grid=(batch, nc),
        in_specs=in_specs,
        out_specs=pl.BlockSpec((chunk, SSD_D_INNER), blk),
        out_shape=jax.ShapeDtypeStruct((t, SSD_D_INNER), BF16),
        scratch_shapes=[
            pltpu.VMEM((chunk + SUBLANES, SSD_D_INNER), F32),
            pltpu.VMEM((chunk + SUBLANES, SSD_BC_DIM), F32),
            pltpu.VMEM((SSD_GROUPS, SSD_STATE, SSD_GROUP_DIM), F32),
            pltpu.VMEM((chunk, SSD_D_INNER), BF16),
            pltpu.VMEM((chunk, SSD_HEADS * LANES), F32),
            pltpu.VMEM((SSD_HEADS, SUBLANES, chunk), F32),
            pltpu.VMEM((SSD_GROUPS, chunk, chunk), F32),
            pltpu.VMEM((chunk, SSD_D_INNER), F32),
        ],
        compiler_params=_params("parallel", "arbitrary"),
        name="ssd_core",
    )(*args)


def ssd_mixer_deepnorm(x, w_in, conv_w, conv_b, dt_bias, a_log, d_skip, norm_w, w_out, g, b,
                       batch, seq):
    n_zx = SSD_D_INNER + SSD_D_INNER + SSD_BC_DIM
    zx = matmul(x, w_in[:, :n_zx].astype(BF16), F32)
    w_dt = jnp.pad(w_in[:, n_zx:], ((0, 0), (0, LANES - SSD_HEADS)))
    dt_raw = matmul_f32(x, w_dt)
    y = ssd_core(zx, dt_raw, conv_w, conv_b, dt_bias, a_log, d_skip, norm_w, batch, seq)
    return matmul_deepnorm(y, w_out.astype(BF16), x, g, b)


ATT_BLOCK = 128
ATT_SCALE = ATT_HEAD_DIM ** -0.5


def _qk(qh, k2):
    return lax.dot_general(qh, k2, (((1,), (1,)), ((), ())), preferred_element_type=F32) * ATT_SCALE


def _sb_kernel(q_ref, k_ref, v_ref, u_ref, o_ref, *, seq):
    nq = seq // ATT_BLOCK
    head0 = lax.broadcasted_iota(jnp.int32, (ATT_BLOCK, LANES), 1) < ATT_HEAD_DIM
    row = lax.broadcasted_iota(jnp.int32, (ATT_BLOCK, ATT_BLOCK), 0)
    col = lax.broadcasted_iota(jnp.int32, (ATT_BLOCK, ATT_BLOCK), 1)
    strict = col < row

    def q_body(i, _):
        q0 = pl.multiple_of(i * ATT_BLOCK, ATT_BLOCK)
        q2 = q_ref[pl.ds(q0, ATT_BLOCK), :]
        zero = jnp.zeros_like(q2)
        qs = (jnp.where(head0, q2, zero), jnp.where(head0, zero, q2))

        def block(j, carry, masked):
            k0 = pl.multiple_of(j * ATT_BLOCK, ATT_BLOCK)
            k2 = k_ref[pl.ds(k0, ATT_BLOCK), :]
            v2 = v_ref[pl.ds(k0, ATT_BLOCK), :]
            new = []
            for h in range(2):
                later, acc = carry[h]
                z = _qk(qs[h], k2)
                log_beta = jax.nn.log_sigmoid(z)
                log_keep = log_beta - z
                if masked:
                    log_keep = jnp.where(strict, log_keep, 0.0)
                hi = log_keep.astype(BF16)
                lo = (log_keep - hi.astype(F32)).astype(BF16)
                sums = jnp.dot(jnp.concatenate([hi, lo], axis=1), u_ref[...],
                               preferred_element_type=F32)
                w = jnp.exp(log_beta + sums[:, :ATT_BLOCK] + later)
                if masked:
                    w = jnp.where(strict, w, 0.0)
                acc = acc + jnp.dot(w.astype(BF16), v2, preferred_element_type=F32)
                new.append((later + sums[:, ATT_BLOCK:], acc))
            return tuple(new)

        zf = jnp.zeros((ATT_BLOCK, LANES), F32)
        carry = block(i, ((zf, zf), (zf, zf)), True)
        carry = lax.fori_loop(0, i, lambda jj, c: block(i - 1 - jj, c, False), carry)
        o_ref[pl.ds(q0, ATT_BLOCK), :] = jnp.where(head0, carry[0][1], carry[1][1]).astype(o_ref.dtype)
        return 0

    lax.fori_loop(0, nq, q_body, 0)


def _suffix_sum_matrix():
    j = jnp.arange(2 * ATT_BLOCK)[:, None] % ATT_BLOCK
    s = jnp.arange(2 * ATT_BLOCK)[None, :]
    return jnp.where(s < ATT_BLOCK, j > s, True).astype(BF16)


def _attention_specs(seq, n_pairs):
    blk = lambda off: pl.BlockSpec((seq, LANES), lambda b, p: (b, off + p))
    return [blk(0), blk(n_pairs), blk(2 * n_pairs)], pl.BlockSpec((seq, LANES), lambda b, p: (b, p))


def sb_attention(qkv, batch, seq):
    n_pairs = ATT_HEADS // 2
    in_specs, out_spec = _attention_specs(seq, n_pairs)
    u = _suffix_sum_matrix()
    return pl.pallas_call(
        functools.partial(_sb_kernel, seq=seq),
        grid=(batch, n_pairs),
        in_specs=in_specs + [pl.BlockSpec(u.shape, lambda b, p: (0, 0))],
        out_specs=out_spec,
        out_shape=jax.ShapeDtypeStruct((batch * seq, ATT_DIM), BF16),
        compiler_params=_params("parallel", "parallel"),
        name="sb_attention",
    )(qkv, qkv, qkv, u)


def sb_mixer_deepnorm(x, w_qkv, w_out, g, b, batch, seq):
    qkv = matmul(x, w_qkv.astype(BF16), BF16)
    o = sb_attention(qkv, batch, seq)
    return matmul_deepnorm(o, w_out.astype(BF16), x, g, b)


CUMSUM_BLOCK = 256


def _fox_decay_kernel(f_ref, bf_ref, ccol_ref, crow_ref, *, seq):
    tri = _tril_f32(CUMSUM_BLOCK)
    carry = jnp.zeros((1, LANES), F32)
    for blk in range(seq // CUMSUM_BLOCK):
        rows = slice(blk * CUMSUM_BLOCK, (blk + 1) * CUMSUM_BLOCK)
        log_f = jax.nn.log_sigmoid(f_ref[rows, :] + bf_ref[...])
        c = jnp.dot(tri, log_f, preferred_element_type=F32, precision=HIGHEST) + carry
        carry = c[CUMSUM_BLOCK - 1:CUMSUM_BLOCK, :]
        ccol_ref[rows, :] = c
        c_t = c.T
        for p in range(ATT_HEADS // 2):
            crow_ref[0, p, :, rows] = c_t[2 * p:2 * p + 2, :]


def fox_decay(f_raw, b_f, batch, seq):
    return pl.pallas_call(
        functools.partial(_fox_decay_kernel, seq=seq),
        grid=(batch,),
        in_specs=[pl.BlockSpec((seq, LANES), lambda b: (b, 0)),
                  pl.BlockSpec((1, LANES), lambda b: (0, 0))],
        out_specs=[pl.BlockSpec((seq, LANES), lambda b: (b, 0)),
                   pl.BlockSpec((1, ATT_HEADS // 2, 2, seq), lambda b: (b, 0, 0, 0))],
        out_shape=[jax.ShapeDtypeStruct((batch * seq, LANES), F32),
                   jax.ShapeDtypeStruct((batch, ATT_HEADS // 2, 2, seq), F32)],
        compiler_params=_params("parallel"),
        name="fox_decay",
    )(f_raw, jnp.pad(b_f, (0, LANES - ATT_HEADS)).reshape(1, LANES))


def _fox_kernel(q_ref, k_ref, v_ref, ccol_ref, crow_ref, o_ref, crep_ref, *, seq):
    nq = seq // ATT_BLOCK
    pair = pl.program_id(1)
    head0 = lax.broadcasted_iota(jnp.int32, (ATT_BLOCK, LANES), 1) < ATT_HEAD_DIM
    row = lax.broadcasted_iota(jnp.int32, (ATT_BLOCK, ATT_BLOCK), 0)
    col = lax.broadcasted_iota(jnp.int32, (ATT_BLOCK, ATT_BLOCK), 1)
    causal = col <= row

    pieces = jnp.concatenate(_split_bf16(ccol_ref[...], 3), axis=1)
    sel_row = lax.broadcasted_iota(jnp.int32, (3 * LANES, LANES), 0) & (LANES - 1)
    for h in range(2):
        sel = jnp.where(sel_row == 2 * pair + h, 1.0, 0.0).astype(BF16)
        crep_ref[h] = jnp.dot(pieces, sel, preferred_element_type=F32)

    def q_body(i, _):
        q0 = pl.multiple_of(i * ATT_BLOCK, ATT_BLOCK)
        q2 = q_ref[pl.ds(q0, ATT_BLOCK), :]
        zero = jnp.zeros_like(q2)
        qs = (jnp.where(head0, q2, zero), jnp.where(head0, zero, q2))
        c_t = (crep_ref[0, pl.ds(q0, ATT_BLOCK), :], crep_ref[1, pl.ds(q0, ATT_BLOCK), :])

        def block(j, carry, masked):
            k0 = pl.multiple_of(j * ATT_BLOCK, ATT_BLOCK)
            k2 = k_ref[pl.ds(k0, ATT_BLOCK), :]
            v2 = v_ref[pl.ds(k0, ATT_BLOCK), :]
            new = []
            for h in range(2):
                m, l, acc = carry[h]
                c_s = crow_ref[0, 0, h:h + 1, pl.ds(k0, ATT_BLOCK)]
                s = _qk(qs[h], k2) + (c_t[h] - c_s)
                if masked:
                    s = jnp.where(causal, s, NEG_BIG)
                m_new = jnp.maximum(m, jnp.max(s, axis=-1, keepdims=True))
                alpha = jnp.exp(m - m_new)
                p = jnp.exp(s - m_new)
                l = alpha * l + jnp.sum(p, axis=-1, keepdims=True)
                acc = alpha * acc + jnp.dot(p.astype(BF16), v2, preferred_element_type=F32)
                new.append((m_new, l, acc))
            return tuple(new)

        init = (jnp.full((ATT_BLOCK, 1), NEG_BIG, F32), jnp.zeros((ATT_BLOCK, 1), F32),
                jnp.zeros((ATT_BLOCK, LANES), F32))
        carry = block(i, (init, init), True)
        carry = lax.fori_loop(0, i, lambda jj, c: block(i - 1 - jj, c, False), carry)
        o = jnp.where(head0, carry[0][2] / carry[0][1], carry[1][2] / carry[1][1])
        o_ref[pl.ds(q0, ATT_BLOCK), :] = o.astype(o_ref.dtype)
        return 0

    lax.fori_loop(0, nq, q_body, 0)


def fox_attention(qkv, ccol, crow, batch, seq):
    n_pairs = ATT_HEADS // 2
    in_specs, out_spec = _attention_specs(seq, n_pairs)
    in_specs += [pl.BlockSpec((seq, LANES), lambda b, p: (b, 0)),
                 pl.BlockSpec((1, 1, 2, seq), lambda b, p: (b, p, 0, 0))]
    return pl.pallas_call(
        functools.partial(_fox_kernel, seq=seq),
        grid=(batch, n_pairs),
        in_specs=in_specs,
        out_specs=out_spec,
        out_shape=jax.ShapeDtypeStruct((batch * seq, ATT_DIM), BF16),
        scratch_shapes=[pltpu.VMEM((2, seq, LANES), F32)],
        compiler_params=_params("parallel", "parallel"),
        name="fox_attention",
    )(qkv, qkv, qkv, ccol, crow)


def fox_mixer_deepnorm(x, w_qkvf, b_f, w_out, g, b, batch, seq):
    qkv = matmul(x, w_qkvf[:, :3 * ATT_DIM].astype(BF16), BF16)
    w_f = jnp.pad(w_qkvf[:, 3 * ATT_DIM:], ((0, 0), (0, LANES - ATT_HEADS)))
    ccol, crow = fox_decay(matmul_f32(x, w_f), b_f, batch, seq)
    o = fox_attention(qkv, ccol, crow, batch, seq)
    return matmul_deepnorm(o, w_out.astype(BF16), x, g, b)


MOE_TILE = 512
ROUTE_BLOCK = 512


def _router_kernel(x_ref, w_ref, idx_ref, gate_ref):
    logits = jnp.dot(x_ref[...], w_ref[...], preferred_element_type=F32, precision=HIGHEST)
    lane = lax.broadcasted_iota(jnp.int32, logits.shape, 1)
    logits = jnp.where(lane < N_EXPERTS, logits, NEG_BIG)
    m1 = jnp.max(logits, axis=-1, keepdims=True)
    i1 = jnp.min(jnp.where(logits == m1, lane, LANES), axis=-1, keepdims=True)
    rest = jnp.where(lane == i1, NEG_BIG, logits)
    m2 = jnp.max(rest, axis=-1, keepdims=True)
    i2 = jnp.min(jnp.where(rest == m2, lane, LANES), axis=-1, keepdims=True)
    e2 = jnp.exp(m2 - m1)
    denom = 1.0 + e2
    idx_ref[...] = jnp.where(lane == 0, i1, jnp.where(lane == 1, i2, 0))
    gate_ref[...] = jnp.where(lane == 0, 1.0 / denom, jnp.where(lane == 1, e2 / denom, 0.0))


def moe_router(x, w_router):
    t, d = x.shape
    tb = min(ROUTE_BLOCK, t)
    w = jnp.pad(w_router, ((0, 0), (0, LANES - N_EXPERTS)))
    return pl.pallas_call(
        _router_kernel,
        grid=(t // tb,),
        in_specs=[pl.BlockSpec((tb, d), lambda i: (i, 0)), pl.BlockSpec((d, LANES), lambda i: (0, 0))],
        out_specs=[pl.BlockSpec((tb, LANES), lambda i: (i, 0))] * 2,
        out_shape=[jax.ShapeDtypeStruct((t, LANES), jnp.int32), jax.ShapeDtypeStruct((t, LANES), F32)],
        compiler_params=_params("parallel"),
        name="moe_router",
    )(x, w)


def _rank_kernel(idx_ref, rank_ref, count_ref, run_ref):
    @pl.when(pl.program_id(0) == 0)
    def _():
        run_ref[...] = jnp.zeros_like(run_ref)

    idx = idx_ref[...]
    tb = idx.shape[0]
    lane = lax.broadcasted_iota(jnp.int32, idx.shape, 1)
    oh0 = lane == idx[:, 0:1]
    oh1 = lane == idx[:, 1:2]
    both = jnp.where(oh0 | oh1, 1.0, 0.0)
    before = jnp.dot(_tril_f32(tb, strict=True).astype(BF16), both.astype(BF16),
                     preferred_element_type=F32) + run_ref[...]
    r0 = jnp.sum(jnp.where(oh0, before, 0.0), axis=-1, keepdims=True)
    r1 = jnp.sum(jnp.where(oh1, before, 0.0), axis=-1, keepdims=True)
    rank_ref[...] = jnp.where(lane == 0, r0, jnp.where(lane == 1, r1, 0.0)).astype(jnp.int32)
    run_ref[...] += jnp.sum(both, axis=0, keepdims=True)
    count_ref[...] = run_ref[...]


def moe_rank(idx):
    t = idx.shape[0]
    tb = min(ROUTE_BLOCK, t)
    return pl.pallas_call(
        _rank_kernel,
        grid=(t // tb,),
        in_specs=[pl.BlockSpec((tb, LANES), lambda i: (i, 0))],
        out_specs=[pl.BlockSpec((tb, LANES), lambda i: (i, 0)), pl.BlockSpec((1, LANES), lambda i: (0, 0))],
        out_shape=[jax.ShapeDtypeStruct((t, LANES), jnp.int32), jax.ShapeDtypeStruct((1, LANES), F32)],
        scratch_shapes=[pltpu.VMEM((1, LANES), F32)],
        compiler_params=_params("arbitrary"),
        name="moe_rank",
    )(idx)


def _pos_kernel(idx_ref, rank_ref, off_ref, pos_ref):
    idx = idx_ref[...]
    lane = lax.broadcasted_iota(jnp.int32, idx.shape, 1)
    off = off_ref[...]
    p0 = jnp.sum(jnp.where(lane == idx[:, 0:1], off, 0), axis=-1, keepdims=True)
    p1 = jnp.sum(jnp.where(lane == idx[:, 1:2], off, 0), axis=-1, keepdims=True)
    pos_ref[...] = rank_ref[...] + jnp.where(lane == 0, p0, jnp.where(lane == 1, p1, 0))


def moe_positions(idx, rank, offsets):
    t = idx.shape[0]
    tb = min(ROUTE_BLOCK, t)
    off = jnp.pad(offsets, (0, LANES - N_EXPERTS)).reshape(1, LANES)
    blk = pl.BlockSpec((tb, LANES), lambda i: (i, 0))
    return pl.pallas_call(
        _pos_kernel,
        grid=(t // tb,),
        in_specs=[blk, blk, pl.BlockSpec((1, LANES), lambda i: (0, 0))],
        out_specs=blk,
        out_shape=jax.ShapeDtypeStruct((t, LANES), jnp.int32),
        compiler_params=_params("parallel"),
        name="moe_positions",
    )(idx, rank, off)


def _row_copy(src_ref, src_row, dst_ref, dst_row, sem):
    return pltpu.make_async_copy(src_ref.at[pl.ds(src_row, 1)], dst_ref.at[pl.ds(dst_row, 1)], sem)


def _dispatch_kernel(pos_ref, cnt_ref, off_ref, x_ref, xs_ref, zero_ref, sem, pad_sem, *, tb):
    base = pl.program_id(0) * (TOP_K * tb)

    def issue(a, _):
        _row_copy(x_ref, a // TOP_K, xs_ref, pos_ref[base + a], sem).start()
        return 0

    lax.fori_loop(0, TOP_K * tb, issue, 0)

    @pl.when(pl.program_id(0) == 0)
    def _():
        zero_ref[...] = jnp.zeros_like(zero_ref)
        for e in range(N_EXPERTS):
            start = off_ref[e] + cnt_ref[e]
            n_pad = (-cnt_ref[e]) & (MOE_TILE - 1)

            def fill(r, _):
                _row_copy(zero_ref, 0, xs_ref, start + r, pad_sem).start()
                return 0

            def drain(r, _):
                _row_copy(zero_ref, 0, xs_ref, start + r, pad_sem).wait()
                return 0

            lax.fori_loop(0, n_pad, fill, 0)
            lax.fori_loop(0, n_pad, drain, 0)

        last = N_EXPERTS - 1
        used = off_ref[last] + cnt_ref[last] + ((-cnt_ref[last]) & (MOE_TILE - 1))

        def tail_copy(r):
            rows = pl.ds(pl.multiple_of(used + r * SUBLANES, SUBLANES), SUBLANES)
            return pltpu.make_async_copy(zero_ref, xs_ref.at[rows], pad_sem)

        def tail_fill(r, _):
            tail_copy(r).start()
            return 0

        def tail_drain(r, _):
            tail_copy(r).wait()
            return 0

        n_tail = (xs_ref.shape[0] - used) // SUBLANES
        lax.fori_loop(0, n_tail, tail_fill, 0)
        lax.fori_loop(0, n_tail, tail_drain, 0)

    def drain_rows(a, _):
        _row_copy(x_ref, a // TOP_K, xs_ref, pos_ref[base + a], sem).wait()
        return 0

    lax.fori_loop(0, TOP_K * tb, drain_rows, 0)


def moe_dispatch(x, pos_flat, counts, offsets, n_rows):
    t, d = x.shape
    tb = min(ROUTE_BLOCK, t)
    grid_spec = pltpu.PrefetchScalarGridSpec(
        num_scalar_prefetch=3,
        grid=(t // tb,),
        in_specs=[pl.BlockSpec((tb, d), lambda i, *_: (i, 0))],
        out_specs=pl.BlockSpec(memory_space=pl.ANY),
        scratch_shapes=[pltpu.VMEM((SUBLANES, d), F32), pltpu.SemaphoreType.DMA(()),
                        pltpu.SemaphoreType.DMA(())],
    )
    return pl.pallas_call(
        functools.partial(_dispatch_kernel, tb=tb),
        grid_spec=grid_spec,
        out_shape=jax.ShapeDtypeStruct((n_rows, d), F32),
        compiler_params=_params("arbitrary"),
        name="moe_dispatch",
    )(pos_flat, counts, offsets, x)


def _moe_ffn_kernel(te_ref, nt_ref, x_ref, wg_ref, wu_ref, wd_ref, o_ref, xb_ref, acc_ref):
    i = pl.program_id(0)
    f = pl.program_id(1)

    @pl.when(i < nt_ref[0])
    def _():
        @pl.when(f == 0)
        def _():
            xb_ref[...] = x_ref[...].astype(BF16)
            acc_ref[...] = jnp.zeros_like(acc_ref)

        xb = xb_ref[...]
        gate = jnp.dot(xb, wg_ref[0], preferred_element_type=F32)
        up = jnp.dot(xb, wu_ref[0], preferred_element_type=F32)
        h = (_silu(gate) * up).astype(BF16)
        acc_ref[...] += jnp.dot(h, wd_ref[0], preferred_element_type=F32)

        @pl.when(f == pl.num_programs(1) - 1)
        def _():
            o_ref[...] = acc_ref[...]

    @pl.when((i >= nt_ref[0]) & (f == 0))
    def _():
        o_ref[...] = jnp.zeros_like(o_ref)


def moe_ffn(xs, tile_expert, n_tiles_used, wg, wu, wd, tf=512):
    n_rows, d = xs.shape
    n_tiles = n_rows // MOE_TILE
    fdim = wg.shape[2]
    nf = fdim // tf

    def live(i, nt):
        return jnp.minimum(i, nt[0] - 1)

    def fblk(i, f, nt):
        return jnp.where(i < nt[0], f, nf - 1)

    grid_spec = pltpu.PrefetchScalarGridSpec(
        num_scalar_prefetch=2,
        grid=(n_tiles, nf),
        in_specs=[pl.BlockSpec((MOE_TILE, d), lambda i, f, te, nt: (live(i, nt), 0)),
                  pl.BlockSpec((1, d, tf), lambda i, f, te, nt: (te[i], 0, fblk(i, f, nt))),
                  pl.BlockSpec((1, d, tf), lambda i, f, te, nt: (te[i], 0, fblk(i, f, nt))),
                  pl.BlockSpec((1, tf, d), lambda i, f, te, nt: (te[i], fblk(i, f, nt), 0))],
        out_specs=pl.BlockSpec((MOE_TILE, d), lambda i, f, te, nt: (i, 0)),
        scratch_shapes=[pltpu.VMEM((MOE_TILE, d), BF16), pltpu.VMEM((MOE_TILE, d), F32)],
    )
    return pl.pallas_call(
        _moe_ffn_kernel,
        grid_spec=grid_spec,
        out_shape=jax.ShapeDtypeStruct((n_rows, d), F32),
        compiler_params=_params("arbitrary", "arbitrary"),
        name="moe_ffn",
    )(tile_expert, n_tiles_used, xs, wg, wu, wd)


def _combine_kernel(pos_ref, x_ref, gate_ref, g_ref, b_ref, ys_ref, o_ref, buf_ref, sem, *, tb):
    base = pl.program_id(0) * (TOP_K * tb)

    def copy(a):
        return pltpu.make_async_copy(ys_ref.at[pl.ds(pos_ref[base + a], 1)],
                                     buf_ref.at[a % TOP_K, pl.ds(a // TOP_K, 1)], sem)

    def issue(a, _):
        copy(a).start()
        return 0

    def drain(a, _):
        copy(a).wait()
        return 0

    lax.fori_loop(0, TOP_K * tb, issue, 0)
    lax.fori_loop(0, TOP_K * tb, drain, 0)
    gates = gate_ref[...]
    y = buf_ref[0] * gates[:, 0:1] + buf_ref[1] * gates[:, 1:2]
    o_ref[...] = _layer_norm_rows(DN_ALPHA * x_ref[...] + y, g_ref[...], b_ref[...])


def moe_combine(x, ys, pos_flat, gates, g, b):
    t, d = x.shape
    tb = min(ROUTE_BLOCK, t)
    grid_spec = pltpu.PrefetchScalarGridSpec(
        num_scalar_prefetch=1,
        grid=(t // tb,),
        in_specs=[pl.BlockSpec((tb, d), lambda i, *_: (i, 0)),
                  pl.BlockSpec((tb, LANES), lambda i, *_: (i, 0)),
                  pl.BlockSpec((1, d), lambda i, *_: (0, 0)),
                  pl.BlockSpec((1, d), lambda i, *_: (0, 0)),
                  pl.BlockSpec(memory_space=pl.ANY)],
        out_specs=pl.BlockSpec((tb, d), lambda i, *_: (i, 0)),
        scratch_shapes=[pltpu.VMEM((TOP_K, tb, d), F32), pltpu.SemaphoreType.DMA(())],
    )
    return pl.pallas_call(
        functools.partial(_combine_kernel, tb=tb),
        grid_spec=grid_spec,
        out_shape=jax.ShapeDtypeStruct((t, d), F32),
        compiler_params=_params("arbitrary"),
        name="moe_combine",
    )(pos_flat, x, gates, g.reshape(1, d), b.reshape(1, d), ys)


def moe_deepnorm(x, w_router, wg, wu, wd, g, b):
    t = x.shape[0]
    idx, gates = moe_router(x, w_router)
    rank, counts_f = moe_rank(idx)
    counts = counts_f[0, :N_EXPERTS].astype(jnp.int32)
    padded = (counts + MOE_TILE - 1) // MOE_TILE * MOE_TILE
    ends = jnp.cumsum(padded)
    offsets = ends - padded
    n_tiles = (TOP_K * t) // MOE_TILE + N_EXPERTS
    n_used = (ends[-1] // MOE_TILE).astype(jnp.int32)
    tile_start = jnp.arange(n_tiles, dtype=jnp.int32) * MOE_TILE
    tile_start = jnp.minimum(tile_start, ends[-1] - MOE_TILE)
    tile_expert = jnp.sum(tile_start[:, None] >= ends[None, :], axis=1).astype(jnp.int32)
    pos = moe_positions(idx, rank, offsets)
    pos_flat = pos[:, :TOP_K].reshape(-1)
    xs = moe_dispatch(x, pos_flat, counts, offsets, n_tiles * MOE_TILE)
    ys = moe_ffn(xs, tile_expert, n_used.reshape(1), wg.astype(BF16), wu.astype(BF16), wd.astype(BF16))
    return moe_combine(x, ys, pos_flat, gates, g, b)


def kernel(x, l0_ssd_w_in, l0_ssd_conv_w, l0_ssd_conv_b, l0_ssd_dt_bias, l0_ssd_a_log, l0_ssd_d_skip, l0_ssd_norm_w, l0_ssd_w_out, l0_ln_mix_g, l0_ln_mix_b, l0_ffn_w_gate, l0_ffn_w_up, l0_ffn_w_down, l0_ln_ffn_g, l0_ln_ffn_b, l1_sb_w_qkv, l1_sb_w_out, l1_ln_mix_g, l1_ln_mix_b, l1_moe_w_router, l1_moe_w_gate, l1_moe_w_up, l1_moe_w_down, l1_ln_ffn_g, l1_ln_ffn_b, l2_fox_w_qkvf, l2_fox_b_f, l2_fox_w_out, l2_ln_mix_g, l2_ln_mix_b, l2_ffn_w_gate, l2_ffn_w_up, l2_ffn_w_down, l2_ln_ffn_g, l2_ln_ffn_b, l3_ssd_w_in, l3_ssd_conv_w, l3_ssd_conv_b, l3_ssd_dt_bias, l3_ssd_a_log, l3_ssd_d_skip, l3_ssd_norm_w, l3_ssd_w_out, l3_ln_mix_g, l3_ln_mix_b, l3_moe_w_router, l3_moe_w_gate, l3_moe_w_up, l3_moe_w_down, l3_ln_ffn_g, l3_ln_ffn_b):
    batch, seq, d = x.shape
    h = x.reshape(batch * seq, d)
    bf = lambda w: w.astype(BF16)
    h = ssd_mixer_deepnorm(h, l0_ssd_w_in, l0_ssd_conv_w, l0_ssd_conv_b, l0_ssd_dt_bias, l0_ssd_a_log,
                           l0_ssd_d_skip, l0_ssd_norm_w, l0_ssd_w_out, l0_ln_mix_g, l0_ln_mix_b, batch, seq)
    h = ffn_deepnorm(h, bf(l0_ffn_w_gate), bf(l0_ffn_w_up), bf(l0_ffn_w_down), l0_ln_ffn_g, l0_ln_ffn_b)
    h = sb_mixer_deepnorm(h, l1_sb_w_qkv, l1_sb_w_out, l1_ln_mix_g, l1_ln_mix_b, batch, seq)
    h = moe_deepnorm(h, l1_moe_w_router, l1_moe_w_gate, l1_moe_w_up, l1_moe_w_down, l1_ln_ffn_g, l1_ln_ffn_b)
    h = fox_mixer_deepnorm(h, l2_fox_w_qkvf, l2_fox_b_f, l2_fox_w_out, l2_ln_mix_g, l2_ln_mix_b, batch, seq)
    h = ffn_deepnorm(h, bf(l2_ffn_w_gate), bf(l2_ffn_w_up), bf(l2_ffn_w_down), l2_ln_ffn_g, l2_ln_ffn_b)
    h = ssd_mixer_deepnorm(h, l3_ssd_w_in, l3_ssd_conv_w, l3_ssd_conv_b, l3_ssd_dt_bias, l3_ssd_a_log,
                           l3_ssd_d_skip, l3_ssd_norm_w, l3_ssd_w_out, l3_ln_mix_g, l3_ln_mix_b, batch, seq)
    h = moe_deepnorm(h, l3_moe_w_router, l3_moe_w_gate, l3_moe_w_up, l3_moe_w_down, l3_ln_ffn_g, l3_ln_ffn_b)
    return h.reshape(batch, seq, d)
```

```python
import functools
import math

import jax
import jax.numpy as jnp
from jax import lax
from jax.experimental import pallas as pl
from jax.experimental.pallas import tpu as pltpu

F32 = jnp.float32
BF16 = jnp.bfloat16
HIGHEST = lax.Precision.HIGHEST

LANES = 128
SUBLANES = 8
VMEM_LIMIT_BYTES = 56 * 1024 * 1024

D_MODEL = 1024
DEPTH = 4
SSD_D_INNER = 2048
SSD_HEAD_DIM = 64
SSD_HEADS = 32
SSD_GROUPS = 4
SSD_STATE = 128
SSD_CONV = 4
SSD_GROUP_DIM = SSD_D_INNER // SSD_GROUPS
SSD_BC_DIM = 2 * SSD_GROUPS * SSD_STATE
ATT_HEAD_DIM = 64
ATT_HEADS = 16
ATT_DIM = 1024
N_EXPERTS = 8
TOP_K = 2
DN_ALPHA = (2.0 * DEPTH) ** 0.25
LN_EPS = 1e-5
RMS_EPS = 1e-5
NEG_BIG = -1e30


def _params(*semantics):
    return pltpu.CompilerParams(dimension_semantics=semantics, vmem_limit_bytes=VMEM_LIMIT_BYTES)


def _layer_norm_rows(h, g, b):
    mu = jnp.mean(h, axis=-1, keepdims=True)
    d = h - mu
    var = jnp.mean(d * d, axis=-1, keepdims=True)
    return d * lax.rsqrt(var + LN_EPS) * g + b


def _silu(x):
    return x * jax.nn.sigmoid(x)


def _mm_kernel(x_ref, w_ref, o_ref):
    o_ref[...] = jnp.dot(x_ref[...].astype(BF16), w_ref[...],
                         preferred_element_type=F32).astype(o_ref.dtype)


def matmul(x, w, out_dtype, tm=1024, tn=512):
    m, k = x.shape
    tm = min(tm, m)
    n = w.shape[1]
    return pl.pallas_call(
        _mm_kernel,
        grid=(m // tm, n // tn),
        in_specs=[pl.BlockSpec((tm, k), lambda i, j: (i, 0)),
                  pl.BlockSpec((k, tn), lambda i, j: (0, j))],
        out_specs=pl.BlockSpec((tm, tn), lambda i, j: (i, j)),
        out_shape=jax.ShapeDtypeStruct((m, n), out_dtype),
        compiler_params=_params("parallel", "parallel"),
        name="matmul",
    )(x, w)


def _mm_f32_kernel(x_ref, w_ref, o_ref):
    o_ref[...] = jnp.dot(x_ref[...], w_ref[...], preferred_element_type=F32, precision=HIGHEST)


def matmul_f32(x, w, tm=1024):
    m, k = x.shape
    tm = min(tm, m)
    n = w.shape[1]
    return pl.pallas_call(
        _mm_f32_kernel,
        grid=(m // tm,),
        in_specs=[pl.BlockSpec((tm, k), lambda i: (i, 0)),
                  pl.BlockSpec((k, n), lambda i: (0, 0))],
        out_specs=pl.BlockSpec((tm, n), lambda i: (i, 0)),
        out_shape=jax.ShapeDtypeStruct((m, n), F32),
        compiler_params=_params("parallel"),
        name="matmul_f32",
    )(x, w)


def _mm_ln_kernel(x_ref, w_ref, r_ref, g_ref, b_ref, o_ref):
    y = jnp.dot(x_ref[...], w_ref[...], preferred_element_type=F32)
    o_ref[...] = _layer_norm_rows(DN_ALPHA * r_ref[...] + y, g_ref[...], b_ref[...])


def matmul_deepnorm(x, w, resid, g, b, tm=512):
    m, k = x.shape
    tm = min(tm, m)
    d = w.shape[1]
    return pl.pallas_call(
        _mm_ln_kernel,
        grid=(m // tm,),
        in_specs=[pl.BlockSpec((tm, k), lambda i: (i, 0)),
                  pl.BlockSpec((k, d), lambda i: (0, 0)),
                  pl.BlockSpec((tm, d), lambda i: (i, 0)),
                  pl.BlockSpec((1, d), lambda i: (0, 0)),
                  pl.BlockSpec((1, d), lambda i: (0, 0))],
        out_specs=pl.BlockSpec((tm, d), lambda i: (i, 0)),
        out_shape=jax.ShapeDtypeStruct((m, d), F32),
        compiler_params=_params("parallel"),
        name="matmul_deepnorm",
    )(x, w, resid, g.reshape(1, d), b.reshape(1, d))


def _ffn_kernel(x_ref, wg_ref, wu_ref, wd_ref, g_ref, b_ref, o_ref, xb_ref, acc_ref):
    f = pl.program_id(1)

    @pl.when(f == 0)
    def _():
        xb_ref[...] = x_ref[...].astype(BF16)
        acc_ref[...] = jnp.zeros_like(acc_ref)

    xb = xb_ref[...]
    gate = jnp.dot(xb, wg_ref[...], preferred_element_type=F32)
    up = jnp.dot(xb, wu_ref[...], preferred_element_type=F32)
    h = (_silu(gate) * up).astype(BF16)
    acc_ref[...] += jnp.dot(h, wd_ref[...], preferred_element_type=F32)

    @pl.when(f == pl.num_programs(1) - 1)
    def _():
        o_ref[...] = _layer_norm_rows(DN_ALPHA * x_ref[...] + acc_ref[...], g_ref[...], b_ref[...])


def ffn_deepnorm(x, wg, wu, wd, g, b, tm=1024, tf=256):
    m, d = x.shape
    tm = min(tm, m)
    fdim = wg.shape[1]
    return pl.pallas_call(
        _ffn_kernel,
        grid=(m // tm, fdim // tf),
        in_specs=[pl.BlockSpec((tm, d), lambda i, f: (i, 0)),
                  pl.BlockSpec((d, tf), lambda i, f: (0, f)),
                  pl.BlockSpec((d, tf), lambda i, f: (0, f)),
                  pl.BlockSpec((tf, d), lambda i, f: (f, 0)),
                  pl.BlockSpec((1, d), lambda i, f: (0, 0)),
                  pl.BlockSpec((1, d), lambda i, f: (0, 0))],
        out_specs=pl.BlockSpec((tm, d), lambda i, f: (i, 0)),
        out_shape=jax.ShapeDtypeStruct((m, d), F32),
        scratch_shapes=[pltpu.VMEM((tm, d), BF16), pltpu.VMEM((tm, d), F32)],
        compiler_params=_params("parallel", "arbitrary"),
        name="ffn_deepnorm",
    )(x, wg, wu, wd, g.reshape(1, d), b.reshape(1, d))


def _split_bf16(v, pieces):
    out = []
    r = v
    for _ in range(pieces - 1):
        p = r.astype(BF16)
        out.append(p)
        r = r - p.astype(F32)
    out.append(r.astype(BF16))
    return out


def _expand(v, e_ref, pieces):
    stacked = jnp.concatenate(_split_bf16(v, pieces), axis=1)
    return jnp.dot(stacked, e_ref[...], preferred_element_type=F32)


def _tril_f32(n, strict=False):
    r = lax.broadcasted_iota(jnp.int32, (n, n), 0)
    c = lax.broadcasted_iota(jnp.int32, (n, n), 1)
    return ((r > c) if strict else (r >= c)).astype(F32)


def _ssd_kernel(z_ref, xs_ref, bc_ref, dt_ref, cwx_ref, cbx_ref, cwbc_ref, cbbc_ref,
                dtb_ref, alog_ref, dskip_ref, normw_ref, e64_ref, e128_ref, o_ref,
                convx_ref, convbc_ref, state_ref, xdt_ref, acol_ref, arow_ref, cb_ref, ydiag_ref,
                *, chunk):
    L = chunk
    c = pl.program_id(1)

    @pl.when(c == 0)
    def _():
        state_ref[...] = jnp.zeros_like(state_ref)
        convx_ref[0:SUBLANES, :] = jnp.zeros((SUBLANES, SSD_D_INNER), F32)
        convbc_ref[0:SUBLANES, :] = jnp.zeros((SUBLANES, SSD_BC_DIM), F32)

    @pl.when(c > 0)
    def _():
        convx_ref[0:SUBLANES, :] = convx_ref[L:L + SUBLANES, :]
        convbc_ref[0:SUBLANES, :] = convbc_ref[L:L + SUBLANES, :]

    convx_ref[SUBLANES:SUBLANES + L, :] = xs_ref[...]
    convbc_ref[SUBLANES:SUBLANES + L, :] = bc_ref[...]

    def conv_silu(buf_ref, w_ref, b_ref):
        acc = b_ref[...]
        for k in range(SSD_CONV):
            start = SUBLANES - (SSD_CONV - 1) + k
            acc = acc + w_ref[k:k + 1, :] * buf_ref[start:start + L, :]
        return _silu(acc)

    xs = conv_silu(convx_ref, cwx_ref, cbx_ref)
    bcv = conv_silu(convbc_ref, cwbc_ref, cbbc_ref)

    dt = jax.nn.softplus(dt_ref[...] + dtb_ref[...])
    da = dt * (-jnp.exp(alog_ref[...]))
    a_cs = jnp.dot(_tril_f32(L), da, preferred_element_type=F32, precision=HIGHEST)
    ea = jnp.exp(a_cs)
    dte = jnp.exp(a_cs[L - 1:L, :] - a_cs)

    dt_x = _expand(dt, e64_ref, 2)
    ea_x = _expand(ea, e64_ref, 2)
    dte_x = _expand(dte, e64_ref, 2)
    acol_ref[...] = _expand(a_cs, e128_ref, 3)
    a_t = a_cs.T
    for h in range(SSD_HEADS):
        arow_ref[h] = jnp.broadcast_to(a_t[h:h + 1, :], (SUBLANES, L))

    xdt = xs * dt_x
    xdt_ref[...] = xdt.astype(BF16)
    xdte = (xdt * dte_x).astype(BF16)

    y_off = []
    for g in range(SSD_GROUPS):
        bm = bcv[:, g * SSD_STATE:(g + 1) * SSD_STATE]
        cm = bcv[:, (SSD_GROUPS + g) * SSD_STATE:(SSD_GROUPS + g + 1) * SSD_STATE].astype(BF16)
        cb_ref[g] = lax.dot_general(cm, bm.astype(BF16), (((1,), (1,)), ((), ())),
                                    preferred_element_type=F32)
        gs = slice(g * SSD_GROUP_DIM, (g + 1) * SSD_GROUP_DIM)
        st = state_ref[g]
        y_off.append(jnp.dot(cm, st.astype(BF16), preferred_element_type=F32) * ea_x[:, gs])
        state_ref[g] = st * ea_x[L - 1:L, gs] + jnp.dot(
            bm.T.astype(BF16), xdte[:, gs], preferred_element_type=F32)

    row = lax.broadcasted_iota(jnp.int32, (L, L), 0)
    col = lax.broadcasted_iota(jnp.int32, (L, L), 1)
    causal = row >= col
    head0 = lax.broadcasted_iota(jnp.int32, (L, LANES), 1) < SSD_HEAD_DIM

    def pair_body(p, carry):
        lanes = pl.ds(pl.multiple_of(p * LANES, LANES), LANES)
        xpair = xdt_ref[:, lanes]
        cbg = cb_ref[p // (SSD_HEADS // SSD_GROUPS // 2)]
        ys = []
        for j in range(2):
            h = 2 * p + j
            a_l = acol_ref[:, pl.ds(pl.multiple_of(h * LANES, LANES), LANES)]
            a_l = jnp.concatenate([a_l] * (L // LANES), axis=1)
            a_s = arow_ref[h][0:1, :]
            decay = jnp.exp(jnp.where(causal, a_l - a_s, NEG_BIG))
            ys.append(jnp.dot((cbg * decay).astype(BF16), xpair, preferred_element_type=F32))
        ydiag_ref[:, lanes] = jnp.where(head0, ys[0], ys[1])
        return carry

    lax.fori_loop(0, SSD_HEADS // 2, pair_body, 0)

    y = ydiag_ref[...] + jnp.concatenate(y_off, axis=1) + dskip_ref[...] * xs
    y = y * _silu(z_ref[...])
    parts = []
    for g in range(SSD_GROUPS):
        yg = y[:, g * SSD_GROUP_DIM:(g + 1) * SSD_GROUP_DIM]
        parts.append(yg * lax.rsqrt(jnp.mean(yg * yg, axis=-1, keepdims=True) + RMS_EPS))
    o_ref[...] = (jnp.concatenate(parts, axis=1) * normw_ref[...]).astype(o_ref.dtype)


def _expansion_matrix(width, pieces):
    h = jnp.arange(LANES)[:, None]
    lane = jnp.arange(SSD_HEADS * width)[None, :]
    e = (lane // width == h).astype(BF16)
    return jnp.concatenate([e] * pieces, axis=0)


def ssd_core(zx, dt_raw, conv_w, conv_b, dt_bias, a_log, d_skip, norm_w, batch, seq, chunk=128):
    t = zx.shape[0]
    nc = seq // chunk
    pad = LANES - SSD_HEADS
    row = lambda v: v.reshape(1, -1).astype(F32)
    args = (
        zx, zx, zx, dt_raw,
        conv_w[:, :SSD_D_INNER], row(conv_b[:SSD_D_INNER]),
        conv_w[:, SSD_D_INNER:], row(conv_b[SSD_D_INNER:]),
        row(jnp.pad(dt_bias, (0, pad))), row(jnp.pad(a_log, (0, pad))),
        row(jnp.repeat(d_skip, SSD_HEAD_DIM)), row(norm_w),
        _expansion_matrix(SSD_HEAD_DIM, 2), _expansion_matrix(LANES, 3),
    )
    blk = lambda b, c: (b * nc + c, 0)
    const = lambda b, c: (0, 0)
    full = lambda a: pl.BlockSpec(a.shape, const)
    in_specs = [
        pl.BlockSpec((chunk, SSD_D_INNER), blk),
        pl.BlockSpec((chunk, SSD_D_INNER), lambda b, c: (b * nc + c, 1)),
        pl.BlockSpec((chunk, SSD_BC_DIM), lambda b, c: (b * nc + c, 4)),
        pl.BlockSpec((chunk, LANES), blk),
    ] + [full(a) for a in args[4:]]
    return pl.pallas_call(
        functools.partial(_ssd_kernel, chunk=chunk),
        grid=(batch, nc),
        in_specs=in_specs,
        out_specs=pl.BlockSpec((chunk, SSD_D_INNER), blk),
        out_shape=jax.ShapeDtypeStruct((t, SSD_D_INNER), BF16),
        scratch_shapes=[
            pltpu.VMEM((chunk + SUBLANES, SSD_D_INNER), F32),
            pltpu.VMEM((chunk + SUBLANES, SSD_BC_DIM), F32),
            pltpu.VMEM((SSD_GROUPS, SSD_STATE, SSD_GROUP_DIM), F32),
            pltpu.VMEM((chunk, SSD_D_INNER), BF16),
            pltpu.VMEM((chunk, SSD_HEADS * LANES), F32),
            pltpu.VMEM((SSD_HEADS, SUBLANES, chunk), F32),
            pltpu.VMEM((SSD_GROUPS, chunk, chunk), F32),
            pltpu.VMEM((chunk, SSD_D_INNER), F32),
        ],
        compiler_params=_params("parallel", "arbitrary"),
        name="ssd_core",
    )(*args)


def ssd_mixer_deepnorm(x, w_in, conv_w, conv_b, dt_bias, a_log, d_skip, norm_w, w_out, g, b,
                       batch, seq):
    n_zx = SSD_D_INNER + SSD_D_INNER + SSD_BC_DIM
    zx = matmul(x, w_in[:, :n_zx].astype(BF16), F32)
    w_dt = jnp.pad(w_in[:, n_zx:], ((0, 0), (0, LANES - SSD_HEADS)))
    dt_raw = matmul_f32(x, w_dt)
    y = ssd_core(zx, dt_raw, conv_w, conv_b, dt_bias, a_log, d_skip, norm_w, batch, seq)
    return matmul_deepnorm(y, w_out.astype(BF16), x, g, b)


ATT_BLOCK = 128
ATT_Q = 256
ATT_PAIRS = 2
ATT_STEP_LANES = ATT_PAIRS * LANES
ATT_SCALE = ATT_HEAD_DIM ** -0.5
LOG2E = 1.4426950408889634


def _split_heads(q_ref, rows, head0):
    qs = []
    for p in range(ATT_PAIRS):
        q2 = q_ref[rows, p * LANES:(p + 1) * LANES]
        zero = jnp.zeros_like(q2)
        qs += [jnp.where(head0, q2, zero), jnp.where(head0, zero, q2)]
    return qs


def _sb_kernel(q_ref, k_ref, v_ref, u_ref, o_ref, *scratch, seq):
    later_ref, acc_ref = scratch[:2 * ATT_PAIRS], scratch[2 * ATT_PAIRS:]
    nq = seq // ATT_Q
    ndiag = ATT_Q // ATT_BLOCK
    head0 = lax.broadcasted_iota(jnp.int32, (ATT_Q, LANES), 1) < ATT_HEAD_DIM
    row = lax.broadcasted_iota(jnp.int32, (ATT_Q, ATT_BLOCK), 0)
    col = lax.broadcasted_iota(jnp.int32, (ATT_Q, ATT_BLOCK), 1)

    def q_body(i, _):
        q0 = pl.multiple_of(i * ATT_Q, ATT_Q)
        rows = pl.ds(q0, ATT_Q)
        qs = _split_heads(q_ref, rows, head0)
        for n in range(2 * ATT_PAIRS):
            later_ref[n][...] = jnp.zeros((ATT_Q, LANES), F32)
            acc_ref[n][...] = jnp.zeros((ATT_Q, LANES), F32)

        def block(j, strict):
            keys = pl.ds(pl.multiple_of(j * ATT_BLOCK, ATT_BLOCK), ATT_BLOCK)
            heads = range(2 * ATT_PAIRS)
            k2 = [k_ref[keys, p * LANES:(p + 1) * LANES] for p in range(ATT_PAIRS)]
            v2 = [v_ref[keys, p * LANES:(p + 1) * LANES] for p in range(ATT_PAIRS)]
            z2 = [lax.dot_general(qs[n], k2[n // 2], (((1,), (1,)), ((), ())),
                                  preferred_element_type=F32) * (ATT_SCALE * LOG2E) for n in heads]
            log_beta, sums = [], []
            for n in heads:
                lb = jnp.minimum(z2[n], 0.0) - jnp.log2(1.0 + jnp.exp2(-jnp.abs(z2[n])))
                log_keep = lb - z2[n]
                if strict is not None:
                    log_keep = jnp.where(strict, log_keep, 0.0)
                hi = log_keep.astype(BF16)
                lo = (log_keep - hi.astype(F32)).astype(BF16)
                log_beta.append(lb)
                sums.append(jnp.dot(jnp.concatenate([hi, lo], axis=1), u_ref[...],
                                    preferred_element_type=F32))
            for n in heads:
                w = jnp.exp2(log_beta[n] + sums[n][:, :ATT_BLOCK] + later_ref[n][...])
                if strict is not None:
                    w = jnp.where(strict, w, 0.0)
                acc_ref[n][...] += jnp.dot(w.astype(BF16), v2[n // 2], preferred_element_type=F32)
                later_ref[n][...] += sums[n][:, ATT_BLOCK:]

        for d in reversed(range(ndiag)):
            block(i * ndiag + d, d * ATT_BLOCK + col < row)

        def off_diagonal(jj, c):
            block(i * ndiag - 1 - jj, None)
            return c

        lax.fori_loop(0, i * ndiag, off_diagonal, 0)
        for p in range(ATT_PAIRS):
            o_ref[rows, p * LANES:(p + 1) * LANES] = jnp.where(
                head0, acc_ref[2 * p][...], acc_ref[2 * p + 1][...]).astype(o_ref.dtype)
        return 0

    lax.fori_loop(0, nq, q_body, 0)


def _suffix_sum_matrix():
    j = jnp.arange(2 * ATT_BLOCK)[:, None] % ATT_BLOCK
    s = jnp.arange(2 * ATT_BLOCK)[None, :]
    return jnp.where(s < ATT_BLOCK, j > s, True).astype(BF16)


def _attention_specs(seq):
    n_steps = ATT_HEADS // 2 // ATT_PAIRS
    blk = lambda off: pl.BlockSpec((seq, ATT_STEP_LANES), lambda b, p: (b, off + p))
    specs = [blk(0), blk(n_steps), blk(2 * n_steps)]
    return n_steps, specs, pl.BlockSpec((seq, ATT_STEP_LANES), lambda b, p: (b, p))


def sb_attention(qkv, batch, seq):
    n_steps, in_specs, out_spec = _attention_specs(seq)
    u = _suffix_sum_matrix()
    return pl.pallas_call(
        functools.partial(_sb_kernel, seq=seq),
        grid=(batch, n_steps),
        in_specs=in_specs + [pl.BlockSpec(u.shape, lambda b, p: (0, 0))],
        out_specs=out_spec,
        out_shape=jax.ShapeDtypeStruct((batch * seq, ATT_DIM), BF16),
        scratch_shapes=[pltpu.VMEM((ATT_Q, LANES), F32)] * (4 * ATT_PAIRS),
        compiler_params=_params("parallel", "parallel"),
        name="sb_attention",
    )(qkv, qkv, qkv, u)


def sb_mixer_deepnorm(x, w_qkv, w_out, g, b, batch, seq):
    qkv = matmul(x, w_qkv.astype(BF16), BF16)
    o = sb_attention(qkv, batch, seq)
    return matmul_deepnorm(o, w_out.astype(BF16), x, g, b)


CUMSUM_BLOCK = 256


def _fox_decay_kernel(f_ref, bf_ref, ccol_ref, crow_ref, *, seq):
    tri = _tril_f32(CUMSUM_BLOCK)
    carry = jnp.zeros((1, LANES), F32)
    for blk in range(seq // CUMSUM_BLOCK):
        rows = slice(blk * CUMSUM_BLOCK, (blk + 1) * CUMSUM_BLOCK)
        log_f = jax.nn.log_sigmoid(f_ref[rows, :] + bf_ref[...])
        c = jnp.dot(tri, log_f, preferred_element_type=F32, precision=HIGHEST) + carry
        carry = c[CUMSUM_BLOCK - 1:CUMSUM_BLOCK, :]
        ccol_ref[rows, :] = c
        c_t = c.T
        for p in range(ATT_HEADS // 2):
            crow_ref[0, p, :, rows] = c_t[2 * p:2 * p + 2, :]


def fox_decay(f_raw, b_f, batch, seq):
    return pl.pallas_call(
        functools.partial(_fox_decay_kernel, seq=seq),
        grid=(batch,),
        in_specs=[pl.BlockSpec((seq, LANES), lambda b: (b, 0)),
                  pl.BlockSpec((1, LANES), lambda b: (0, 0))],
        out_specs=[pl.BlockSpec((seq, LANES), lambda b: (b, 0)),
                   pl.BlockSpec((1, ATT_HEADS // 2, 2, seq), lambda b: (b, 0, 0, 0))],
        out_shape=[jax.ShapeDtypeStruct((batch * seq, LANES), F32),
                   jax.ShapeDtypeStruct((batch, ATT_HEADS // 2, 2, seq), F32)],
        compiler_params=_params("parallel"),
        name="fox_decay",
    )(f_raw, jnp.pad(b_f, (0, LANES - ATT_HEADS)).reshape(1, LANES))


def _fox_kernel(q_ref, k_ref, v_ref, ccol_ref, crow_ref, o_ref, crep_ref, *scratch, seq):
    m_ref, acc_ref = scratch[:2 * ATT_PAIRS], scratch[2 * ATT_PAIRS:]
    nq = seq // ATT_Q
    ndiag = ATT_Q // ATT_BLOCK
    step = pl.program_id(1)
    head0 = lax.broadcasted_iota(jnp.int32, (ATT_Q, LANES), 1) < ATT_HEAD_DIM
    head0_keys = lax.broadcasted_iota(jnp.int32, (ATT_BLOCK, LANES), 1) < ATT_HEAD_DIM
    row = lax.broadcasted_iota(jnp.int32, (ATT_Q, ATT_BLOCK), 0)
    col = lax.broadcasted_iota(jnp.int32, (ATT_Q, ATT_BLOCK), 1)

    pieces = jnp.concatenate(_split_bf16(ccol_ref[...], 3), axis=1)
    sel_row = lax.broadcasted_iota(jnp.int32, (3 * LANES, LANES), 0) & (LANES - 1)
    for n in range(2 * ATT_PAIRS):
        sel = jnp.where(sel_row == 2 * ATT_PAIRS * step + n, 1.0, 0.0).astype(BF16)
        crep_ref[n] = jnp.dot(pieces, sel, preferred_element_type=F32) * LOG2E

    def q_body(i, _):
        q0 = pl.multiple_of(i * ATT_Q, ATT_Q)
        rows = pl.ds(q0, ATT_Q)
        qs = _split_heads(q_ref, rows, head0)
        for n in range(2 * ATT_PAIRS):
            m_ref[n][...] = jnp.full((ATT_Q, 1), NEG_BIG, F32)
            acc_ref[n][...] = jnp.zeros((ATT_Q, LANES), F32)

        def block(j, causal):
            keys = pl.ds(pl.multiple_of(j * ATT_BLOCK, ATT_BLOCK), ATT_BLOCK)
            heads = range(2 * ATT_PAIRS)
            k2 = [k_ref[keys, p * LANES:(p + 1) * LANES] for p in range(ATT_PAIRS)]
            v2 = [v_ref[keys, p * LANES:(p + 1) * LANES] for p in range(ATT_PAIRS)]
            s = [lax.dot_general(qs[n], k2[n // 2], (((1,), (1,)), ((), ())),
                                 preferred_element_type=F32) * (ATT_SCALE * LOG2E) for n in heads]
            for n in heads:
                c_s = crow_ref[0, n // 2, n % 2:n % 2 + 1, keys] * LOG2E
                sn = s[n] + (crep_ref[n, rows, :] - c_s)
                if causal is not None:
                    sn = jnp.where(causal, sn, NEG_BIG)
                m_old = m_ref[n][...]
                m_new = jnp.maximum(m_old, jnp.max(sn, axis=-1, keepdims=True))
                prob = jnp.exp2(sn - m_new).astype(BF16)
                one = jnp.ones_like(v2[n // 2])
                v1 = jnp.where(head0_keys, v2[n // 2], one) if n % 2 == 0 else jnp.where(
                    head0_keys, one, v2[n // 2])
                acc_ref[n][...] = jnp.exp2(m_old - m_new) * acc_ref[n][...] + jnp.dot(
                    prob, v1, preferred_element_type=F32)
                m_ref[n][...] = m_new

        for d in range(ndiag):
            block(i * ndiag + d, d * ATT_BLOCK + col <= row)

        def off_diagonal(jj, c):
            block(jj, None)
            return c

        lax.fori_loop(0, i * ndiag, off_diagonal, 0)
        for p in range(ATT_PAIRS):
            a0, a1 = acc_ref[2 * p][...], acc_ref[2 * p + 1][...]
            o = jnp.where(head0, a0 / pltpu.roll(a0, ATT_HEAD_DIM, axis=1),
                          a1 / pltpu.roll(a1, ATT_HEAD_DIM, axis=1))
            o_ref[rows, p * LANES:(p + 1) * LANES] = o.astype(o_ref.dtype)
        return 0

    lax.fori_loop(0, nq, q_body, 0)


def fox_attention(qkv, ccol, crow, batch, seq):
    n_steps, in_specs, out_spec = _attention_specs(seq)
    in_specs += [pl.BlockSpec((seq, LANES), lambda b, p: (b, 0)),
                 pl.BlockSpec((1, ATT_PAIRS, 2, seq), lambda b, p: (b, p, 0, 0))]
    return pl.pallas_call(
        functools.partial(_fox_kernel, seq=seq),
        grid=(batch, n_steps),
        in_specs=in_specs,
        out_specs=out_spec,
        out_shape=jax.ShapeDtypeStruct((batch * seq, ATT_DIM), BF16),
        scratch_shapes=([pltpu.VMEM((2 * ATT_PAIRS, seq, LANES), F32)]
                        + [pltpu.VMEM((ATT_Q, 1), F32)] * (2 * ATT_PAIRS)
                        + [pltpu.VMEM((ATT_Q, LANES), F32)] * (2 * ATT_PAIRS)),
        compiler_params=_params("parallel", "parallel"),
        name="fox_attention",
    )(qkv, qkv, qkv, ccol, crow)


def fox_mixer_deepnorm(x, w_qkvf, b_f, w_out, g, b, batch, seq):
    qkv = matmul(x, w_qkvf[:, :3 * ATT_DIM].astype(BF16), BF16)
    w_f = jnp.pad(w_qkvf[:, 3 * ATT_DIM:], ((0, 0), (0, LANES - ATT_HEADS)))
    ccol, crow = fox_decay(matmul_f32(x, w_f), b_f, batch, seq)
    o = fox_attention(qkv, ccol, crow, batch, seq)
    return matmul_deepnorm(o, w_out.astype(BF16), x, g, b)


MOE_TILE = 512
ROUTE_BLOCK = 512


def _router_kernel(x_ref, w_ref, idx_ref, gate_ref):
    logits = jnp.dot(x_ref[...], w_ref[...], preferred_element_type=F32, precision=HIGHEST)
    lane = lax.broadcasted_iota(jnp.int32, logits.shape, 1)
    logits = jnp.where(lane < N_EXPERTS, logits, NEG_BIG)
    m1 = jnp.max(logits, axis=-1, keepdims=True)
    i1 = jnp.min(jnp.where(logits == m1, lane, LANES), axis=-1, keepdims=True)
    rest = jnp.where(lane == i1, NEG_BIG, logits)
    m2 = jnp.max(rest, axis=-1, keepdims=True)
    i2 = jnp.min(jnp.where(rest == m2, lane, LANES), axis=-1, keepdims=True)
    e2 = jnp.exp(m2 - m1)
    denom = 1.0 + e2
    idx_ref[...] = jnp.where(lane == 0, i1, jnp.where(lane == 1, i2, 0))
    gate_ref[...] = jnp.where(lane == 0, 1.0 / denom, jnp.where(lane == 1, e2 / denom, 0.0))


def moe_router(x, w_router):
    t, d = x.shape
    tb = min(ROUTE_BLOCK, t)
    w = jnp.pad(w_router, ((0, 0), (0, LANES - N_EXPERTS)))
    return pl.pallas_call(
        _router_kernel,
        grid=(t // tb,),
        in_specs=[pl.BlockSpec((tb, d), lambda i: (i, 0)), pl.BlockSpec((d, LANES), lambda i: (0, 0))],
        out_specs=[pl.BlockSpec((tb, LANES), lambda i: (i, 0))] * 2,
        out_shape=[jax.ShapeDtypeStruct((t, LANES), jnp.int32), jax.ShapeDtypeStruct((t, LANES), F32)],
        compiler_params=_params("parallel"),
        name="moe_router",
    )(x, w)


def _rank_kernel(idx_ref, rank_ref, count_ref, run_ref):
    @pl.when(pl.program_id(0) == 0)
    def _():
        run_ref[...] = jnp.zeros_like(run_ref)

    idx = idx_ref[...]
    tb = idx.shape[0]
    lane = lax.broadcasted_iota(jnp.int32, idx.shape, 1)
    oh0 = lane == idx[:, 0:1]
    oh1 = lane == idx[:, 1:2]
    both = jnp.where(oh0 | oh1, 1.0, 0.0)
    before = jnp.dot(_tril_f32(tb, strict=True).astype(BF16), both.astype(BF16),
                     preferred_element_type=F32) + run_ref[...]
    r0 = jnp.sum(jnp.where(oh0, before, 0.0), axis=-1, keepdims=True)
    r1 = jnp.sum(jnp.where(oh1, before, 0.0), axis=-1, keepdims=True)
    rank_ref[...] = jnp.where(lane == 0, r0, jnp.where(lane == 1, r1, 0.0)).astype(jnp.int32)
    run_ref[...] += jnp.sum(both, axis=0, keepdims=True)
    count_ref[...] = run_ref[...]


def moe_rank(idx):
    t = idx.shape[0]
    tb = min(ROUTE_BLOCK, t)
    return pl.pallas_call(
        _rank_kernel,
        grid=(t // tb,),
        in_specs=[pl.BlockSpec((tb, LANES), lambda i: (i, 0))],
        out_specs=[pl.BlockSpec((tb, LANES), lambda i: (i, 0)), pl.BlockSpec((1, LANES), lambda i: (0, 0))],
        out_shape=[jax.ShapeDtypeStruct((t, LANES), jnp.int32), jax.ShapeDtypeStruct((1, LANES), F32)],
        scratch_shapes=[pltpu.VMEM((1, LANES), F32)],
        compiler_params=_params("arbitrary"),
        name="moe_rank",
    )(idx)


def _pos_kernel(idx_ref, rank_ref, off_ref, pos_ref):
    idx = idx_ref[...]
    lane = lax.broadcasted_iota(jnp.int32, idx.shape, 1)
    off = off_ref[...]
    p0 = jnp.sum(jnp.where(lane == idx[:, 0:1], off, 0), axis=-1, keepdims=True)
    p1 = jnp.sum(jnp.where(lane == idx[:, 1:2], off, 0), axis=-1, keepdims=True)
    pos_ref[...] = rank_ref[...] + jnp.where(lane == 0, p0, jnp.where(lane == 1, p1, 0))


def moe_positions(idx, rank, offsets):
    t = idx.shape[0]
    tb = min(ROUTE_BLOCK, t)
    off = jnp.pad(offsets, (0, LANES - N_EXPERTS)).reshape(1, LANES)
    blk = pl.BlockSpec((tb, LANES), lambda i: (i, 0))
    return pl.pallas_call(
        _pos_kernel,
        grid=(t // tb,),
        in_specs=[blk, blk, pl.BlockSpec((1, LANES), lambda i: (0, 0))],
        out_specs=blk,
        out_shape=jax.ShapeDtypeStruct((t, LANES), jnp.int32),
        compiler_params=_params("parallel"),
        name="moe_positions",
    )(idx, rank, off)


def _row_copy(src_ref, src_row, dst_ref, dst_row, sem):
    return pltpu.make_async_copy(src_ref.at[pl.ds(src_row, 1)], dst_ref.at[pl.ds(dst_row, 1)], sem)


def _dispatch_kernel(pos_ref, cnt_ref, off_ref, x_ref, xs_ref, zero_ref, sem, pad_sem, *, tb):
    base = pl.program_id(0) * (TOP_K * tb)

    def issue(a, _):
        _row_copy(x_ref, a // TOP_K, xs_ref, pos_ref[base + a], sem).start()
        return 0

    lax.fori_loop(0, TOP_K * tb, issue, 0)

    @pl.when(pl.program_id(0) == 0)
    def _():
        zero_ref[...] = jnp.zeros_like(zero_ref)
        for e in range(N_EXPERTS):
            start = off_ref[e] + cnt_ref[e]
            n_pad = (-cnt_ref[e]) & (MOE_TILE - 1)

            def fill(r, _):
                _row_copy(zero_ref, 0, xs_ref, start + r, pad_sem).start()
                return 0

            def drain(r, _):
                _row_copy(zero_ref, 0, xs_ref, start + r, pad_sem).wait()
                return 0

            lax.fori_loop(0, n_pad, fill, 0)
            lax.fori_loop(0, n_pad, drain, 0)

        last = N_EXPERTS - 1
        used = off_ref[last] + cnt_ref[last] + ((-cnt_ref[last]) & (MOE_TILE - 1))

        def tail_copy(r):
            rows = pl.ds(pl.multiple_of(used + r * SUBLANES, SUBLANES), SUBLANES)
            return pltpu.make_async_copy(zero_ref, xs_ref.at[rows], pad_sem)

        def tail_fill(r, _):
            tail_copy(r).start()
            return 0

        def tail_drain(r, _):
            tail_copy(r).wait()
            return 0

        n_tail = (xs_ref.shape[0] - used) // SUBLANES
        lax.fori_loop(0, n_tail, tail_fill, 0)
        lax.fori_loop(0, n_tail, tail_drain, 0)

    def drain_rows(a, _):
        _row_copy(x_ref, a // TOP_K, xs_ref, pos_ref[base + a], sem).wait()
        return 0

    lax.fori_loop(0, TOP_K * tb, drain_rows, 0)


def moe_dispatch(x, pos_flat, counts, offsets, n_rows):
    t, d = x.shape
    tb = min(ROUTE_BLOCK, t)
    grid_spec = pltpu.PrefetchScalarGridSpec(
        num_scalar_prefetch=3,
        grid=(t // tb,),
        in_specs=[pl.BlockSpec((tb, d), lambda i, *_: (i, 0))],
        out_specs=pl.BlockSpec(memory_space=pl.ANY),
        scratch_shapes=[pltpu.VMEM((SUBLANES, d), F32), pltpu.SemaphoreType.DMA(()),
                        pltpu.SemaphoreType.DMA(())],
    )
    return pl.pallas_call(
        functools.partial(_dispatch_kernel, tb=tb),
        grid_spec=grid_spec,
        out_shape=jax.ShapeDtypeStruct((n_rows, d), F32),
        compiler_params=_params("arbitrary"),
        name="moe_dispatch",
    )(pos_flat, counts, offsets, x)


def _moe_ffn_kernel(te_ref, nt_ref, x_ref, wg_ref, wu_ref, wd_ref, o_ref, xb_ref, acc_ref):
    i = pl.program_id(0)
    f = pl.program_id(1)

    @pl.when(i < nt_ref[0])
    def _():
        @pl.when(f == 0)
        def _():
            xb_ref[...] = x_ref[...].astype(BF16)
            acc_ref[...] = jnp.zeros_like(acc_ref)

        xb = xb_ref[...]
        gate = jnp.dot(xb, wg_ref[0], preferred_element_type=F32)
        up = jnp.dot(xb, wu_ref[0], preferred_element_type=F32)
        h = (_silu(gate) * up).astype(BF16)
        acc_ref[...] += jnp.dot(h, wd_ref[0], preferred_element_type=F32)

        @pl.when(f == pl.num_programs(1) - 1)
        def _():
            o_ref[...] = acc_ref[...]

    @pl.when((i >= nt_ref[0]) & (f == 0))
    def _():
        o_ref[...] = jnp.zeros_like(o_ref)


def moe_ffn(xs, tile_expert, n_tiles_used, wg, wu, wd, tf=512):
    n_rows, d = xs.shape
    n_tiles = n_rows // MOE_TILE
    fdim = wg.shape[2]
    nf = fdim // tf

    def live(i, nt):
        return jnp.minimum(i, nt[0] - 1)

    def fblk(i, f, nt):
        return jnp.where(i < nt[0], f, nf - 1)

    grid_spec = pltpu.PrefetchScalarGridSpec(
        num_scalar_prefetch=2,
        grid=(n_tiles, nf),
        in_specs=[pl.BlockSpec((MOE_TILE, d), lambda i, f, te, nt: (live(i, nt), 0)),
                  pl.BlockSpec((1, d, tf), lambda i, f, te, nt: (te[i], 0, fblk(i, f, nt))),
                  pl.BlockSpec((1, d, tf), lambda i, f, te, nt: (te[i], 0, fblk(i, f, nt))),
                  pl.BlockSpec((1, tf, d), lambda i, f, te, nt: (te[i], fblk(i, f, nt), 0))],
        out_specs=pl.BlockSpec((MOE_TILE, d), lambda i, f, te, nt: (i, 0)),
        scratch_shapes=[pltpu.VMEM((MOE_TILE, d), BF16), pltpu.VMEM((MOE_TILE, d), F32)],
    )
    return pl.pallas_call(
        _moe_ffn_kernel,
        grid_spec=grid_spec,
        out_shape=jax.ShapeDtypeStruct((n_rows, d), F32),
        compiler_params=_params("arbitrary", "arbitrary"),
        name="moe_ffn",
    )(tile_expert, n_tiles_used, xs, wg, wu, wd)


def _combine_kernel(pos_ref, x_ref, gate_ref, g_ref, b_ref, ys_ref, o_ref, buf_ref, sem, *, tb):
    base = pl.program_id(0) * (TOP_K * tb)

    def copy(a):
        return pltpu.make_async_copy(ys_ref.at[pl.ds(pos_ref[base + a], 1)],
                                     buf_ref.at[a % TOP_K, pl.ds(a // TOP_K, 1)], sem)

    def issue(a, _):
        copy(a).start()
        return 0

    def drain(a, _):
        copy(a).wait()
        return 0

    lax.fori_loop(0, TOP_K * tb, issue, 0)
    lax.fori_loop(0, TOP_K * tb, drain, 0)
    gates = gate_ref[...]
    y = buf_ref[0] * gates[:, 0:1] + buf_ref[1] * gates[:, 1:2]
    o_ref[...] = _layer_norm_rows(DN_ALPHA * x_ref[...] + y, g_ref[...], b_ref[...])


def moe_combine(x, ys, pos_flat, gates, g, b):
    t, d = x.shape
    tb = min(ROUTE_BLOCK, t)
    grid_spec = pltpu.PrefetchScalarGridSpec(
        num_scalar_prefetch=1,
        grid=(t // tb,),
        in_specs=[pl.BlockSpec((tb, d), lambda i, *_: (i, 0)),
                  pl.BlockSpec((tb, LANES), lambda i, *_: (i, 0)),
                  pl.BlockSpec((1, d), lambda i, *_: (0, 0)),
                  pl.BlockSpec((1, d), lambda i, *_: (0, 0)),
                  pl.BlockSpec(memory_space=pl.ANY)],
        out_specs=pl.BlockSpec((tb, d), lambda i, *_: (i, 0)),
        scratch_shapes=[pltpu.VMEM((TOP_K, tb, d), F32), pltpu.SemaphoreType.DMA(())],
    )
    return pl.pallas_call(
        functools.partial(_combine_kernel, tb=tb),
        grid_spec=grid_spec,
        out_shape=jax.ShapeDtypeStruct((t, d), F32),
        compiler_params=_params("arbitrary"),
        name="moe_combine",
    )(pos_flat, x, gates, g.reshape(1, d), b.reshape(1, d), ys)


def moe_deepnorm(x, w_router, wg, wu, wd, g, b):
    t = x.shape[0]
    idx, gates = moe_router(x, w_router)
    rank, counts_f = moe_rank(idx)
    counts = counts_f[0, :N_EXPERTS].astype(jnp.int32)
    padded = (counts + MOE_TILE - 1) // MOE_TILE * MOE_TILE
    ends = jnp.cumsum(padded)
    offsets = ends - padded
    n_tiles = (TOP_K * t) // MOE_TILE + N_EXPERTS
    n_used = (ends[-1] // MOE_TILE).astype(jnp.int32)
    tile_start = jnp.arange(n_tiles, dtype=jnp.int32) * MOE_TILE
    tile_start = jnp.minimum(tile_start, ends[-1] - MOE_TILE)
    tile_expert = jnp.sum(tile_start[:, None] >= ends[None, :], axis=1).astype(jnp.int32)
    pos = moe_positions(idx, rank, offsets)
    pos_flat = pos[:, :TOP_K].reshape(-1)
    xs = moe_dispatch(x, pos_flat, counts, offsets, n_tiles * MOE_TILE)
    ys = moe_ffn(xs, tile_expert, n_used.reshape(1), wg.astype(BF16), wu.astype(BF16), wd.astype(BF16))
    return moe_combine(x, ys, pos_flat, gates, g, b)


def kernel(x, l0_ssd_w_in, l0_ssd_conv_w, l0_ssd_conv_b, l0_ssd_dt_bias, l0_ssd_a_log, l0_ssd_d_skip, l0_ssd_norm_w, l0_ssd_w_out, l0_ln_mix_g, l0_ln_mix_b, l0_ffn_w_gate, l0_ffn_w_up, l0_ffn_w_down, l0_ln_ffn_g, l0_ln_ffn_b, l1_sb_w_qkv, l1_sb_w_out, l1_ln_mix_g, l1_ln_mix_b, l1_moe_w_router, l1_moe_w_gate, l1_moe_w_up, l1_moe_w_down, l1_ln_ffn_g, l1_ln_ffn_b, l2_fox_w_qkvf, l2_fox_b_f, l2_fox_w_out, l2_ln_mix_g, l2_ln_mix_b, l2_ffn_w_gate, l2_ffn_w_up, l2_ffn_w_down, l2_ln_ffn_g, l2_ln_ffn_b, l3_ssd_w_in, l3_ssd_conv_w, l3_ssd_conv_b, l3_ssd_dt_bias, l3_ssd_a_log, l3_ssd_d_skip, l3_ssd_norm_w, l3_ssd_w_out, l3_ln_mix_g, l3_ln_mix_b, l3_moe_w_router, l3_moe_w_gate, l3_moe_w_up, l3_moe_w_down, l3_ln_ffn_g, l3_ln_ffn_b):
    batch, seq, d = x.shape
    h = x.reshape(batch * seq, d)
    bf = lambda w: w.astype(BF16)
    h = ssd_mixer_deepnorm(h, l0_ssd_w_in, l0_ssd_conv_w, l0_ssd_conv_b, l0_ssd_dt_bias, l0_ssd_a_log,
                           l0_ssd_d_skip, l0_ssd_norm_w, l0_ssd_w_out, l0_ln_mix_g, l0_ln_mix_b, batch, seq)
    h = ffn_deepnorm(h, bf(l0_ffn_w_gate), bf(l0_ffn_w_up), bf(l0_ffn_w_down), l0_ln_ffn_g, l0_ln_ffn_b)
    h = sb_mixer_deepnorm(h, l1_sb_w_qkv, l1_sb_w_out, l1_ln_mix_g, l1_ln_mix_b, batch, seq)
    h = moe_deepnorm(h, l1_moe_w_router, l1_moe_w_gate, l1_moe_w_up, l1_moe_w_down, l1_ln_ffn_g, l1_ln_ffn_b)
    h = fox_mixer_deepnorm(h, l2_fox_w_qkvf, l2_fox_b_f, l2_fox_w_out, l2_ln_mix_g, l2_ln_mix_b, batch, seq)
    h = ffn_deepnorm(h, bf(l2_ffn_w_gate), bf(l2_ffn_w_up), bf(l2_ffn_w_down), l2_ln_ffn_g, l2_ln_ffn_b)
    h = ssd_mixer_deepnorm(h, l3_ssd_w_in, l3_ssd_conv_w, l3_ssd_conv_b, l3_ssd_dt_bias, l3_ssd_a_log,
                           l3_ssd_d_skip, l3_ssd_norm_w, l3_ssd_w_out, l3_ln_mix_g, l3_ln_mix_b, batch, seq)
    h = moe_deepnorm(h, l3_moe_w_router, l3_moe_w_gate, l3_moe_w_up, l3_moe_w_down, l3_ln_ffn_g, l3_ln_ffn_b)
    return h.reshape(batch, seq, d)
```

```python
import functools
import math

import jax
import jax.numpy as jnp
from jax import lax
from jax.experimental import pallas as pl
from jax.experimental.pallas import tpu as pltpu

F32 = jnp.float32
BF16 = jnp.bfloat16
HIGHEST = lax.Precision.HIGHEST

LANES = 128
SUBLANES = 8
VMEM_LIMIT_BYTES = 56 * 1024 * 1024

D_MODEL = 1024
DEPTH = 4
SSD_D_INNER = 2048
SSD_HEAD_DIM = 64
SSD_HEADS = 32
SSD_GROUPS = 4
SSD_STATE = 128
SSD_CONV = 4
SSD_GROUP_DIM = SSD_D_INNER // SSD_GROUPS
SSD_BC_DIM = 2 * SSD_GROUPS * SSD_STATE
ATT_HEAD_DIM = 64
ATT_HEADS = 16
ATT_DIM = 1024
N_EXPERTS = 8
TOP_K = 2
DN_ALPHA = (2.0 * DEPTH) ** 0.25
LN_EPS = 1e-5
RMS_EPS = 1e-5
NEG_BIG = -1e30


def _params(*semantics):
    return pltpu.CompilerParams(dimension_semantics=semantics, vmem_limit_bytes=VMEM_LIMIT_BYTES)


def _layer_norm_rows(h, g, b):
    mu = jnp.mean(h, axis=-1, keepdims=True)
    d = h - mu
    var = jnp.mean(d * d, axis=-1, keepdims=True)
    return d * lax.rsqrt(var + LN_EPS) * g + b


def _silu(x):
    return x * jax.nn.sigmoid(x)


def _mm_kernel(x_ref, w_ref, o_ref):
    o_ref[...] = jnp.dot(x_ref[...].astype(BF16), w_ref[...],
                         preferred_element_type=F32).astype(o_ref.dtype)


def matmul(x, w, out_dtype, tm=1024, tn=512):
    m, k = x.shape
    tm = min(tm, m)
    n = w.shape[1]
    return pl.pallas_call(
        _mm_kernel,
        grid=(m // tm, n // tn),
        in_specs=[pl.BlockSpec((tm, k), lambda i, j: (i, 0)),
                  pl.BlockSpec((k, tn), lambda i, j: (0, j))],
        out_specs=pl.BlockSpec((tm, tn), lambda i, j: (i, j)),
        out_shape=jax.ShapeDtypeStruct((m, n), out_dtype),
        compiler_params=_params("parallel", "parallel"),
        name="matmul",
    )(x, w)


def _mm_f32_kernel(x_ref, w_ref, o_ref):
    o_ref[...] = jnp.dot(x_ref[...], w_ref[...], preferred_element_type=F32, precision=HIGHEST)


def matmul_f32(x, w, tm=1024):
    m, k = x.shape
    tm = min(tm, m)
    n = w.shape[1]
    return pl.pallas_call(
        _mm_f32_kernel,
        grid=(m // tm,),
        in_specs=[pl.BlockSpec((tm, k), lambda i: (i, 0)),
                  pl.BlockSpec((k, n), lambda i: (0, 0))],
        out_specs=pl.BlockSpec((tm, n), lambda i: (i, 0)),
        out_shape=jax.ShapeDtypeStruct((m, n), F32),
        compiler_params=_params("parallel"),
        name="matmul_f32",
    )(x, w)


ROW_CHUNKS = D_MODEL // LANES


def _store_row_tiled(ref, value):
    n = value.shape[0]
    for j in range(ROW_CHUNKS):
        ref[pl.ds(j, n, stride=ROW_CHUNKS), :] = value[:, j * LANES:(j + 1) * LANES]


def _load_row_tiled(ref, j, n):
    return ref[pl.ds(j, n, stride=ROW_CHUNKS), :]


def _mm_ln_kernel(x_ref, w_ref, r_ref, g_ref, b_ref, o_ref, *maybe_tiled_ref):
    y = jnp.dot(x_ref[...], w_ref[...], preferred_element_type=F32)
    o = _layer_norm_rows(DN_ALPHA * r_ref[...] + y, g_ref[...], b_ref[...])
    o_ref[...] = o
    for t_ref in maybe_tiled_ref:
        _store_row_tiled(t_ref, o)


def matmul_deepnorm(x, w, resid, g, b, tm=512, also_row_tiled=False):
    m, k = x.shape
    tm = min(tm, m)
    d = w.shape[1]
    out_specs = [pl.BlockSpec((tm, d), lambda i: (i, 0))]
    out_shape = [jax.ShapeDtypeStruct((m, d), F32)]
    if also_row_tiled:
        out_specs.append(pl.BlockSpec((tm * ROW_CHUNKS, LANES), lambda i: (i, 0)))
        out_shape.append(jax.ShapeDtypeStruct((m * ROW_CHUNKS, LANES), F32))
    out = pl.pallas_call(
        _mm_ln_kernel,
        grid=(m // tm,),
        in_specs=[pl.BlockSpec((tm, k), lambda i: (i, 0)),
                  pl.BlockSpec((k, d), lambda i: (0, 0)),
                  pl.BlockSpec((tm, d), lambda i: (i, 0)),
                  pl.BlockSpec((1, d), lambda i: (0, 0)),
                  pl.BlockSpec((1, d), lambda i: (0, 0))],
        out_specs=out_specs,
        out_shape=out_shape,
        compiler_params=_params("parallel"),
        name="matmul_deepnorm",
    )(x, w, resid, g.reshape(1, d), b.reshape(1, d))
    return tuple(out) if also_row_tiled else out[0]


def _ffn_kernel(x_ref, wg_ref, wu_ref, wd_ref, g_ref, b_ref, o_ref, xb_ref, acc_ref):
    f = pl.program_id(1)

    @pl.when(f == 0)
    def _():
        xb_ref[...] = x_ref[...].astype(BF16)
        acc_ref[...] = jnp.zeros_like(acc_ref)

    xb = xb_ref[...]
    gate = jnp.dot(xb, wg_ref[...], preferred_element_type=F32)
    up = jnp.dot(xb, wu_ref[...], preferred_element_type=F32)
    h = (_silu(gate) * up).astype(BF16)
    acc_ref[...] += jnp.dot(h, wd_ref[...], preferred_element_type=F32)

    @pl.when(f == pl.num_programs(1) - 1)
    def _():
        o_ref[...] = _layer_norm_rows(DN_ALPHA * x_ref[...] + acc_ref[...], g_ref[...], b_ref[...])


def ffn_deepnorm(x, wg, wu, wd, g, b, tm=1024, tf=256):
    m, d = x.shape
    tm = min(tm, m)
    fdim = wg.shape[1]
    return pl.pallas_call(
        _ffn_kernel,
        grid=(m // tm, fdim // tf),
        in_specs=[pl.BlockSpec((tm, d), lambda i, f: (i, 0)),
                  pl.BlockSpec((d, tf), lambda i, f: (0, f)),
                  pl.BlockSpec((d, tf), lambda i, f: (0, f)),
                  pl.BlockSpec((tf, d), lambda i, f: (f, 0)),
                  pl.BlockSpec((1, d), lambda i, f: (0, 0)),
                  pl.BlockSpec((1, d), lambda i, f: (0, 0))],
        out_specs=pl.BlockSpec((tm, d), lambda i, f: (i, 0)),
        out_shape=jax.ShapeDtypeStruct((m, d), F32),
        scratch_shapes=[pltpu.VMEM((tm, d), BF16), pltpu.VMEM((tm, d), F32)],
        compiler_params=_params("parallel", "arbitrary"),
        name="ffn_deepnorm",
    )(x, wg, wu, wd, g.reshape(1, d), b.reshape(1, d))


def _split_bf16(v, pieces):
    out = []
    r = v
    for _ in range(pieces - 1):
        p = r.astype(BF16)
        out.append(p)
        r = r - p.astype(F32)
    out.append(r.astype(BF16))
    return out


def _expand(v, e_ref, pieces):
    stacked = jnp.concatenate(_split_bf16(v, pieces), axis=1)
    return jnp.dot(stacked, e_ref[...], preferred_element_type=F32)


def _tril_f32(n, strict=False):
    r = lax.broadcasted_iota(jnp.int32, (n, n), 0)
    c = lax.broadcasted_iota(jnp.int32, (n, n), 1)
    return ((r > c) if strict else (r >= c)).astype(F32)


def _ssd_kernel(z_ref, xs_ref, bc_ref, dt_ref, cwx_ref, cbx_ref, cwbc_ref, cbbc_ref,
                dtb_ref, alog_ref, dskip_ref, normw_ref, e64_ref, e128_ref, o_ref,
                convx_ref, convbc_ref, state_ref, xdt_ref, acol_ref, arow_ref, cb_ref, ydiag_ref,
                *, chunk):
    L = chunk
    c = pl.program_id(1)

    @pl.when(c == 0)
    def _():
        state_ref[...] = jnp.zeros_like(state_ref)
        convx_ref[0:SUBLANES, :] = jnp.zeros((SUBLANES, SSD_D_INNER), F32)
        convbc_ref[0:SUBLANES, :] = jnp.zeros((SUBLANES, SSD_BC_DIM), F32)

    @pl.when(c > 0)
    def _():
        convx_ref[0:SUBLANES, :] = convx_ref[L:L + SUBLANES, :]
        convbc_ref[0:SUBLANES, :] = convbc_ref[L:L + SUBLANES, :]

    convx_ref[SUBLANES:SUBLANES + L, :] = xs_ref[...]
    convbc_ref[SUBLANES:SUBLANES + L, :] = bc_ref[...]

    def conv_silu(buf_ref, w_ref, b_ref):
        acc = b_ref[...]
        for k in range(SSD_CONV):
            start = SUBLANES - (SSD_CONV - 1) + k
            acc = acc + w_ref[k:k + 1, :] * buf_ref[start:start + L, :]
        return _silu(acc)

    xs = conv_silu(convx_ref, cwx_ref, cbx_ref)
    bcv = conv_silu(convbc_ref, cwbc_ref, cbbc_ref)

    dt = jax.nn.softplus(dt_ref[...] + dtb_ref[...])
    da = dt * (-jnp.exp(alog_ref[...]))
    a_cs = jnp.dot(_tril_f32(L), da, preferred_element_type=F32, precision=HIGHEST)
    ea = jnp.exp(a_cs)
    dte = jnp.exp(a_cs[L - 1:L, :] - a_cs)

    dt_x = _expand(dt, e64_ref, 2)
    ea_x = _expand(ea, e64_ref, 2)
    dte_x = _expand(dte, e64_ref, 2)
    acol_ref[...] = _expand(a_cs, e128_ref, 3)
    a_t = a_cs.T
    for h in range(SSD_HEADS):
        arow_ref[h] = jnp.broadcast_to(a_t[h:h + 1, :], (SUBLANES, L))

    xdt = xs * dt_x
    xdt_ref[...] = xdt.astype(BF16)
    xdte = (xdt * dte_x).astype(BF16)

    y_off = []
    for g in range(SSD_GROUPS):
        bm = bcv[:, g * SSD_STATE:(g + 1) * SSD_STATE]
        cm = bcv[:, (SSD_GROUPS + g) * SSD_STATE:(SSD_GROUPS + g + 1) * SSD_STATE].astype(BF16)
        cb_ref[g] = lax.dot_general(cm, bm.astype(BF16), (((1,), (1,)), ((), ())),
                                    preferred_element_type=F32)
        gs = slice(g * SSD_GROUP_DIM, (g + 1) * SSD_GROUP_DIM)
        st = state_ref[g]
        y_off.append(jnp.dot(cm, st.astype(BF16), preferred_element_type=F32) * ea_x[:, gs])
        state_ref[g] = st * ea_x[L - 1:L, gs] + jnp.dot(
            bm.T.astype(BF16), xdte[:, gs], preferred_element_type=F32)

    row = lax.broadcasted_iota(jnp.int32, (L, L), 0)
    col = lax.broadcasted_iota(jnp.int32, (L, L), 1)
    causal = row >= col
    head0 = lax.broadcasted_iota(jnp.int32, (L, LANES), 1) < SSD_HEAD_DIM

    def pair_body(p, carry):
        lanes = pl.ds(pl.multiple_of(p * LANES, LANES), LANES)
        xpair = xdt_ref[:, lanes]
        cbg = cb_ref[p // (SSD_HEADS // SSD_GROUPS // 2)]
        ys = []
        for j in range(2):
            h = 2 * p + j
            a_l = acol_ref[:, pl.ds(pl.multiple_of(h * LANES, LANES), LANES)]
            a_l = jnp.concatenate([a_l] * (L // LANES), axis=1)
            a_s = arow_ref[h][0:1, :]
            decay = jnp.exp(jnp.where(causal, a_l - a_s, NEG_BIG))
            ys.append(jnp.dot((cbg * decay).astype(BF16), xpair, preferred_element_type=F32))
        ydiag_ref[:, lanes] = jnp.where(head0, ys[0], ys[1])
        return carry

    lax.fori_loop(0, SSD_HEADS // 2, pair_body, 0)

    y = ydiag_ref[...] + jnp.concatenate(y_off, axis=1) + dskip_ref[...] * xs
    y = y * _silu(z_ref[...])
    parts = []
    for g in range(SSD_GROUPS):
        yg = y[:, g * SSD_GROUP_DIM:(g + 1) * SSD_GROUP_DIM]
        parts.append(yg * lax.rsqrt(jnp.mean(yg * yg, axis=-1, keepdims=True) + RMS_EPS))
    o_ref[...] = (jnp.concatenate(parts, axis=1) * normw_ref[...]).astype(o_ref.dtype)


def _expansion_matrix(width, pieces):
    h = jnp.arange(LANES)[:, None]
    lane = jnp.arange(SSD_HEADS * width)[None, :]
    e = (lane // width == h).astype(BF16)
    return jnp.concatenate([e] * pieces, axis=0)


def ssd_core(zx, dt_raw, conv_w, conv_b, dt_bias, a_log, d_skip, norm_w, batch, seq, chunk=128):
    t = zx.shape[0]
    nc = seq // chunk
    pad = LANES - SSD_HEADS
    row = lambda v: v.reshape(1, -1).astype(F32)
    args = (
        zx, zx, zx, dt_raw,
        conv_w[:, :SSD_D_INNER], row(conv_b[:SSD_D_INNER]),
        conv_w[:, SSD_D_INNER:], row(conv_b[SSD_D_INNER:]),
        row(jnp.pad(dt_bias, (0, pad))), row(jnp.pad(a_log, (0, pad))),
        row(jnp.repeat(d_skip, SSD_HEAD_DIM)), row(norm_w),
        _expansion_matrix(SSD_HEAD_DIM, 2), _expansion_matrix(LANES, 3),
    )
    blk = lambda b, c: (b * nc + c, 0)
    const = lambda b, c: (0, 0)
    full = lambda a: pl.BlockSpec(a.shape, const)
    in_specs = [
        pl.BlockSpec((chunk, SSD_D_INNER), blk),
        pl.BlockSpec((chunk, SSD_D_INNER), lambda b, c: (b * nc + c, 1)),
        pl.BlockSpec((chunk, SSD_BC_DIM), lambda b, c: (b * nc + c, 4)),
        pl.BlockSpec((chunk, LANES), blk),
    ] + [full(a) for a in args[4:]]
    return pl.pallas_call(
        functools.partial(_ssd_kernel, chunk=chunk),
        grid=(batch, nc),
        in_specs=in_specs,
        out_specs=pl.BlockSpec((chunk, SSD_D_INNER), blk),
        out_shape=jax.ShapeDtypeStruct((t, SSD_D_INNER), BF16),
        scratch_shapes=[
            pltpu.VMEM((chunk + SUBLANES, SSD_D_INNER), F32),
            pltpu.VMEM((chunk + SUBLANES, SSD_BC_DIM), F32),
            pltpu.VMEM((SSD_GROUPS, SSD_STATE, SSD_GROUP_DIM), F32),
            pltpu.VMEM((chunk, SSD_D_INNER), BF16),
            pltpu.VMEM((chunk, SSD_HEADS * LANES), F32),
            pltpu.VMEM((SSD_HEADS, SUBLANES, chunk), F32),
            pltpu.VMEM((SSD_GROUPS, chunk, chunk), F32),
            pltpu.VMEM((chunk, SSD_D_INNER), F32),
        ],
        compiler_params=_params("parallel", "arbitrary"),
        name="ssd_core",
    )(*args)


def ssd_mixer_deepnorm(x, w_in, conv_w, conv_b, dt_bias, a_log, d_skip, norm_w, w_out, g, b,
                       batch, seq, also_row_tiled=False):
    n_zx = SSD_D_INNER + SSD_D_INNER + SSD_BC_DIM
    zx = matmul(x, w_in[:, :n_zx].astype(BF16), F32)
    w_dt = jnp.pad(w_in[:, n_zx:], ((0, 0), (0, LANES - SSD_HEADS)))
    dt_raw = matmul_f32(x, w_dt)
    y = ssd_core(zx, dt_raw, conv_w, conv_b, dt_bias, a_log, d_skip, norm_w, batch, seq)
    return matmul_deepnorm(y, w_out.astype(BF16), x, g, b, also_row_tiled=also_row_tiled)


ATT_BLOCK = 128
ATT_Q = 256
ATT_PAIRS = 2
ATT_STEP_LANES = ATT_PAIRS * LANES
ATT_SCALE = ATT_HEAD_DIM ** -0.5
LOG2E = 1.4426950408889634


def _split_heads(q_ref, rows, head0):
    qs = []
    for p in range(ATT_PAIRS):
        q2 = q_ref[rows, p * LANES:(p + 1) * LANES]
        zero = jnp.zeros_like(q2)
        qs += [jnp.where(head0, q2, zero), jnp.where(head0, zero, q2)]
    return qs


def _sb_kernel(q_ref, k_ref, v_ref, u_ref, o_ref, *scratch, seq):
    later_ref, acc_ref = scratch[:2 * ATT_PAIRS], scratch[2 * ATT_PAIRS:]
    nq = seq // ATT_Q
    ndiag = ATT_Q // ATT_BLOCK
    head0 = lax.broadcasted_iota(jnp.int32, (ATT_Q, LANES), 1) < ATT_HEAD_DIM
    row = lax.broadcasted_iota(jnp.int32, (ATT_Q, ATT_BLOCK), 0)
    col = lax.broadcasted_iota(jnp.int32, (ATT_Q, ATT_BLOCK), 1)

    def q_body(i, _):
        q0 = pl.multiple_of(i * ATT_Q, ATT_Q)
        rows = pl.ds(q0, ATT_Q)
        qs = _split_heads(q_ref, rows, head0)
        for n in range(2 * ATT_PAIRS):
            later_ref[n][...] = jnp.zeros((ATT_Q, LANES), F32)
            acc_ref[n][...] = jnp.zeros((ATT_Q, LANES), F32)

        def block(j, strict):
            keys = pl.ds(pl.multiple_of(j * ATT_BLOCK, ATT_BLOCK), ATT_BLOCK)
            heads = range(2 * ATT_PAIRS)
            k2 = [k_ref[keys, p * LANES:(p + 1) * LANES] for p in range(ATT_PAIRS)]
            v2 = [v_ref[keys, p * LANES:(p + 1) * LANES] for p in range(ATT_PAIRS)]
            z2 = [lax.dot_general(qs[n], k2[n // 2], (((1,), (1,)), ((), ())),
                                  preferred_element_type=F32) * (ATT_SCALE * LOG2E) for n in heads]
            log_beta, sums = [], []
            for n in heads:
                lb = jnp.minimum(z2[n], 0.0) - jnp.log2(1.0 + jnp.exp2(-jnp.abs(z2[n])))
                log_keep = lb - z2[n]
                if strict is not None:
                    log_keep = jnp.where(strict, log_keep, 0.0)
                hi = log_keep.astype(BF16)
                lo = (log_keep - hi.astype(F32)).astype(BF16)
                log_beta.append(lb)
                sums.append(jnp.dot(jnp.concatenate([hi, lo], axis=1), u_ref[...],
                                    preferred_element_type=F32))
            for n in heads:
                w = jnp.exp2(log_beta[n] + sums[n][:, :ATT_BLOCK] + later_ref[n][...])
                if strict is not None:
                    w = jnp.where(strict, w, 0.0)
                acc_ref[n][...] += jnp.dot(w.astype(BF16), v2[n // 2], preferred_element_type=F32)
                later_ref[n][...] += sums[n][:, ATT_BLOCK:]

        for d in reversed(range(ndiag)):
            block(i * ndiag + d, d * ATT_BLOCK + col < row)

        def off_diagonal(jj, c):
            block(i * ndiag - 1 - jj, None)
            return c

        lax.fori_loop(0, i * ndiag, off_diagonal, 0)
        for p in range(ATT_PAIRS):
            o_ref[rows, p * LANES:(p + 1) * LANES] = jnp.where(
                head0, acc_ref[2 * p][...], acc_ref[2 * p + 1][...]).astype(o_ref.dtype)
        return 0

    lax.fori_loop(0, nq, q_body, 0)


def _suffix_sum_matrix():
    j = jnp.arange(2 * ATT_BLOCK)[:, None] % ATT_BLOCK
    s = jnp.arange(2 * ATT_BLOCK)[None, :]
    return jnp.where(s < ATT_BLOCK, j > s, True).astype(BF16)


def _attention_specs(seq):
    n_steps = ATT_HEADS // 2 // ATT_PAIRS
    blk = lambda off: pl.BlockSpec((seq, ATT_STEP_LANES), lambda b, p: (b, off + p))
    specs = [blk(0), blk(n_steps), blk(2 * n_steps)]
    return n_steps, specs, pl.BlockSpec((seq, ATT_STEP_LANES), lambda b, p: (b, p))


def sb_attention(qkv, batch, seq):
    n_steps, in_specs, out_spec = _attention_specs(seq)
    u = _suffix_sum_matrix()
    return pl.pallas_call(
        functools.partial(_sb_kernel, seq=seq),
        grid=(batch, n_steps),
        in_specs=in_specs + [pl.BlockSpec(u.shape, lambda b, p: (0, 0))],
        out_specs=out_spec,
        out_shape=jax.ShapeDtypeStruct((batch * seq, ATT_DIM), BF16),
        scratch_shapes=[pltpu.VMEM((ATT_Q, LANES), F32)] * (4 * ATT_PAIRS),
        compiler_params=_params("parallel", "parallel"),
        name="sb_attention",
    )(qkv, qkv, qkv, u)


def sb_mixer_deepnorm(x, w_qkv, w_out, g, b, batch, seq, also_row_tiled=False):
    qkv = matmul(x, w_qkv.astype(BF16), BF16)
    o = sb_attention(qkv, batch, seq)
    return matmul_deepnorm(o, w_out.astype(BF16), x, g, b, also_row_tiled=also_row_tiled)


CUMSUM_BLOCK = 256


def _fox_decay_kernel(f_ref, bf_ref, ccol_ref, crow_ref, *, seq):
    tri = _tril_f32(CUMSUM_BLOCK)
    carry = jnp.zeros((1, LANES), F32)
    for blk in range(seq // CUMSUM_BLOCK):
        rows = slice(blk * CUMSUM_BLOCK, (blk + 1) * CUMSUM_BLOCK)
        log_f = jax.nn.log_sigmoid(f_ref[rows, :] + bf_ref[...])
        c = jnp.dot(tri, log_f, preferred_element_type=F32, precision=HIGHEST) + carry
        carry = c[CUMSUM_BLOCK - 1:CUMSUM_BLOCK, :]
        ccol_ref[rows, :] = c
        c_t = c.T
        for p in range(ATT_HEADS // 2):
            crow_ref[0, p, :, rows] = c_t[2 * p:2 * p + 2, :]


def fox_decay(f_raw, b_f, batch, seq):
    return pl.pallas_call(
        functools.partial(_fox_decay_kernel, seq=seq),
        grid=(batch,),
        in_specs=[pl.BlockSpec((seq, LANES), lambda b: (b, 0)),
                  pl.BlockSpec((1, LANES), lambda b: (0, 0))],
        out_specs=[pl.BlockSpec((seq, LANES), lambda b: (b, 0)),
                   pl.BlockSpec((1, ATT_HEADS // 2, 2, seq), lambda b: (b, 0, 0, 0))],
        out_shape=[jax.ShapeDtypeStruct((batch * seq, LANES), F32),
                   jax.ShapeDtypeStruct((batch, ATT_HEADS // 2, 2, seq), F32)],
        compiler_params=_params("parallel"),
        name="fox_decay",
    )(f_raw, jnp.pad(b_f, (0, LANES - ATT_HEADS)).reshape(1, LANES))


def _fox_kernel(q_ref, k_ref, v_ref, ccol_ref, crow_ref, o_ref, crep_ref, *scratch, seq):
    m_ref, acc_ref = scratch[:2 * ATT_PAIRS], scratch[2 * ATT_PAIRS:]
    nq = seq // ATT_Q
    ndiag = ATT_Q // ATT_BLOCK
    step = pl.program_id(1)
    head0 = lax.broadcasted_iota(jnp.int32, (ATT_Q, LANES), 1) < ATT_HEAD_DIM
    head0_keys = lax.broadcasted_iota(jnp.int32, (ATT_BLOCK, LANES), 1) < ATT_HEAD_DIM
    row = lax.broadcasted_iota(jnp.int32, (ATT_Q, ATT_BLOCK), 0)
    col = lax.broadcasted_iota(jnp.int32, (ATT_Q, ATT_BLOCK), 1)

    pieces = jnp.concatenate(_split_bf16(ccol_ref[...], 3), axis=1)
    sel_row = lax.broadcasted_iota(jnp.int32, (3 * LANES, LANES), 0) & (LANES - 1)
    for n in range(2 * ATT_PAIRS):
        sel = jnp.where(sel_row == 2 * ATT_PAIRS * step + n, 1.0, 0.0).astype(BF16)
        crep_ref[n] = jnp.dot(pieces, sel, preferred_element_type=F32) * LOG2E

    def q_body(i, _):
        q0 = pl.multiple_of(i * ATT_Q, ATT_Q)
        rows = pl.ds(q0, ATT_Q)
        qs = _split_heads(q_ref, rows, head0)
        for n in range(2 * ATT_PAIRS):
            m_ref[n][...] = jnp.full((ATT_Q, 1), NEG_BIG, F32)
            acc_ref[n][...] = jnp.zeros((ATT_Q, LANES), F32)

        def block(j, causal):
            keys = pl.ds(pl.multiple_of(j * ATT_BLOCK, ATT_BLOCK), ATT_BLOCK)
            heads = range(2 * ATT_PAIRS)
            k2 = [k_ref[keys, p * LANES:(p + 1) * LANES] for p in range(ATT_PAIRS)]
            v2 = [v_ref[keys, p * LANES:(p + 1) * LANES] for p in range(ATT_PAIRS)]
            s = [lax.dot_general(qs[n], k2[n // 2], (((1,), (1,)), ((), ())),
                                 preferred_element_type=F32) * (ATT_SCALE * LOG2E) for n in heads]
            for n in heads:
                c_s = crow_ref[0, n // 2, n % 2:n % 2 + 1, keys] * LOG2E
                sn = s[n] + (crep_ref[n, rows, :] - c_s)
                if causal is not None:
                    sn = jnp.where(causal, sn, NEG_BIG)
                m_old = m_ref[n][...]
                m_new = jnp.maximum(m_old, jnp.max(sn, axis=-1, keepdims=True))
                prob = jnp.exp2(sn - m_new).astype(BF16)
                one = jnp.ones_like(v2[n // 2])
                v1 = jnp.where(head0_keys, v2[n // 2], one) if n % 2 == 0 else jnp.where(
                    head0_keys, one, v2[n // 2])
                acc_ref[n][...] = jnp.exp2(m_old - m_new) * acc_ref[n][...] + jnp.dot(
                    prob, v1, preferred_element_type=F32)
                m_ref[n][...] = m_new

        for d in range(ndiag):
            block(i * ndiag + d, d * ATT_BLOCK + col <= row)

        def off_diagonal(jj, c):
            block(jj, None)
            return c

        lax.fori_loop(0, i * ndiag, off_diagonal, 0)
        for p in range(ATT_PAIRS):
            a0, a1 = acc_ref[2 * p][...], acc_ref[2 * p + 1][...]
            o = jnp.where(head0, a0 / pltpu.roll(a0, ATT_HEAD_DIM, axis=1),
                          a1 / pltpu.roll(a1, ATT_HEAD_DIM, axis=1))
            o_ref[rows, p * LANES:(p + 1) * LANES] = o.astype(o_ref.dtype)
        return 0

    lax.fori_loop(0, nq, q_body, 0)


def fox_attention(qkv, ccol, crow, batch, seq):
    n_steps, in_specs, out_spec = _attention_specs(seq)
    in_specs += [pl.BlockSpec((seq, LANES), lambda b, p: (b, 0)),
                 pl.BlockSpec((1, ATT_PAIRS, 2, seq), lambda b, p: (b, p, 0, 0))]
    return pl.pallas_call(
        functools.partial(_fox_kernel, seq=seq),
        grid=(batch, n_steps),
        in_specs=in_specs,
        out_specs=out_spec,
        out_shape=jax.ShapeDtypeStruct((batch * seq, ATT_DIM), BF16),
        scratch_shapes=([pltpu.VMEM((2 * ATT_PAIRS, seq, LANES), F32)]
                        + [pltpu.VMEM((ATT_Q, 1), F32)] * (2 * ATT_PAIRS)
                        + [pltpu.VMEM((ATT_Q, LANES), F32)] * (2 * ATT_PAIRS)),
        compiler_params=_params("parallel", "parallel"),
        name="fox_attention",
    )(qkv, qkv, qkv, ccol, crow)


def fox_mixer_deepnorm(x, w_qkvf, b_f, w_out, g, b, batch, seq):
    qkv = matmul(x, w_qkvf[:, :3 * ATT_DIM].astype(BF16), BF16)
    w_f = jnp.pad(w_qkvf[:, 3 * ATT_DIM:], ((0, 0), (0, LANES - ATT_HEADS)))
    ccol, crow = fox_decay(matmul_f32(x, w_f), b_f, batch, seq)
    o = fox_attention(qkv, ccol, crow, batch, seq)
    return matmul_deepnorm(o, w_out.astype(BF16), x, g, b)


MOE_TILE = 512
ROUTE_BLOCK = 512


def _router_kernel(x_ref, w_ref, idx_ref, gate_ref):
    logits = jnp.dot(x_ref[...], w_ref[...], preferred_element_type=F32, precision=HIGHEST)
    lane = lax.broadcasted_iota(jnp.int32, logits.shape, 1)
    logits = jnp.where(lane < N_EXPERTS, logits, NEG_BIG)
    m1 = jnp.max(logits, axis=-1, keepdims=True)
    i1 = jnp.min(jnp.where(logits == m1, lane, LANES), axis=-1, keepdims=True)
    rest = jnp.where(lane == i1, NEG_BIG, logits)
    m2 = jnp.max(rest, axis=-1, keepdims=True)
    i2 = jnp.min(jnp.where(rest == m2, lane, LANES), axis=-1, keepdims=True)
    e2 = jnp.exp(m2 - m1)
    denom = 1.0 + e2
    idx_ref[...] = jnp.where(lane == 0, i1, jnp.where(lane == 1, i2, 0))
    gate_ref[...] = jnp.where(lane == 0, 1.0 / denom, jnp.where(lane == 1, e2 / denom, 0.0))


def moe_router(x, w_router):
    t, d = x.shape
    tb = min(ROUTE_BLOCK, t)
    w = jnp.pad(w_router, ((0, 0), (0, LANES - N_EXPERTS)))
    return pl.pallas_call(
        _router_kernel,
        grid=(t // tb,),
        in_specs=[pl.BlockSpec((tb, d), lambda i: (i, 0)), pl.BlockSpec((d, LANES), lambda i: (0, 0))],
        out_specs=[pl.BlockSpec((tb, LANES), lambda i: (i, 0))] * 2,
        out_shape=[jax.ShapeDtypeStruct((t, LANES), jnp.int32), jax.ShapeDtypeStruct((t, LANES), F32)],
        compiler_params=_params("parallel"),
        name="moe_router",
    )(x, w)


def _rank_kernel(idx_ref, rank_ref, count_ref, run_ref):
    @pl.when(pl.program_id(0) == 0)
    def _():
        run_ref[...] = jnp.zeros_like(run_ref)

    idx = idx_ref[...]
    tb = idx.shape[0]
    lane = lax.broadcasted_iota(jnp.int32, idx.shape, 1)
    oh0 = lane == idx[:, 0:1]
    oh1 = lane == idx[:, 1:2]
    both = jnp.where(oh0 | oh1, 1.0, 0.0)
    before = jnp.dot(_tril_f32(tb, strict=True).astype(BF16), both.astype(BF16),
                     preferred_element_type=F32) + run_ref[...]
    r0 = jnp.sum(jnp.where(oh0, before, 0.0), axis=-1, keepdims=True)
    r1 = jnp.sum(jnp.where(oh1, before, 0.0), axis=-1, keepdims=True)
    rank_ref[...] = jnp.where(lane == 0, r0, jnp.where(lane == 1, r1, 0.0)).astype(jnp.int32)
    run_ref[...] += jnp.sum(both, axis=0, keepdims=True)
    count_ref[...] = run_ref[...]


def moe_rank(idx):
    t = idx.shape[0]
    tb = min(ROUTE_BLOCK, t)
    return pl.pallas_call(
        _rank_kernel,
        grid=(t // tb,),
        in_specs=[pl.BlockSpec((tb, LANES), lambda i: (i, 0))],
        out_specs=[pl.BlockSpec((tb, LANES), lambda i: (i, 0)), pl.BlockSpec((1, LANES), lambda i: (0, 0))],
        out_shape=[jax.ShapeDtypeStruct((t, LANES), jnp.int32), jax.ShapeDtypeStruct((1, LANES), F32)],
        scratch_shapes=[pltpu.VMEM((1, LANES), F32)],
        compiler_params=_params("arbitrary"),
        name="moe_rank",
    )(idx)


def _pos_kernel(idx_ref, rank_ref, off_ref, pos_ref):
    idx = idx_ref[...]
    lane = lax.broadcasted_iota(jnp.int32, idx.shape, 1)
    off = off_ref[...]
    p0 = jnp.sum(jnp.where(lane == idx[:, 0:1], off, 0), axis=-1, keepdims=True)
    p1 = jnp.sum(jnp.where(lane == idx[:, 1:2], off, 0), axis=-1, keepdims=True)
    pos_ref[...] = rank_ref[...] + jnp.where(lane == 0, p0, jnp.where(lane == 1, p1, 0))


def moe_positions(idx, rank, offsets):
    t = idx.shape[0]
    tb = min(ROUTE_BLOCK, t)
    off = jnp.pad(offsets, (0, LANES - N_EXPERTS)).reshape(1, LANES)
    blk = pl.BlockSpec((tb, LANES), lambda i: (i, 0))
    return pl.pallas_call(
        _pos_kernel,
        grid=(t // tb,),
        in_specs=[blk, blk, pl.BlockSpec((1, LANES), lambda i: (0, 0))],
        out_specs=blk,
        out_shape=jax.ShapeDtypeStruct((t, LANES), jnp.int32),
        compiler_params=_params("parallel"),
        name="moe_positions",
    )(idx, rank, off)


DMA_ISSUE_UNROLL = 8


def _tile_rows(row):
    return pl.ds(pl.multiple_of(row * ROW_CHUNKS, ROW_CHUNKS), ROW_CHUNKS)


def _dispatch_kernel(pos_ref, cnt_ref, off_ref, x_ref, xs_ref, zero_ref, sem, pad_sem, *, tb):
    base = pl.program_id(0) * (TOP_K * tb)

    def issue(t, _):
        for k in range(TOP_K):
            pltpu.make_async_copy(x_ref.at[_tile_rows(t)],
                                  xs_ref.at[_tile_rows(pos_ref[base + TOP_K * t + k])], sem).start()
        return 0

    lax.fori_loop(0, tb, issue, 0, unroll=DMA_ISSUE_UNROLL)

    @pl.when(pl.program_id(0) == 0)
    def _():
        zero_ref[...] = jnp.zeros_like(zero_ref)

        def fill_range(first, count):
            def copy(r):
                return pltpu.make_async_copy(zero_ref, xs_ref.at[_tile_rows(first + r)], pad_sem)

            def fill(r, _):
                copy(r).start()
                return 0

            def drain(r, _):
                copy(r).wait()
                return 0

            lax.fori_loop(0, count, fill, 0)
            lax.fori_loop(0, count, drain, 0)

        for e in range(N_EXPERTS):
            fill_range(off_ref[e] + cnt_ref[e], (-cnt_ref[e]) & (MOE_TILE - 1))
        last = N_EXPERTS - 1
        used = off_ref[last] + cnt_ref[last] + ((-cnt_ref[last]) & (MOE_TILE - 1))
        fill_range(used, xs_ref.shape[0] // ROW_CHUNKS - used)

    for _ in range(TOP_K):
        pltpu.make_async_copy(x_ref, xs_ref.at[pl.ds(0, tb * ROW_CHUNKS)], sem).wait()


def moe_dispatch(x_tiled, pos_flat, counts, offsets, n_rows):
    t = x_tiled.shape[0] // ROW_CHUNKS
    tb = min(ROUTE_BLOCK, t)
    grid_spec = pltpu.PrefetchScalarGridSpec(
        num_scalar_prefetch=3,
        grid=(t // tb,),
        in_specs=[pl.BlockSpec((tb * ROW_CHUNKS, LANES), lambda i, *_: (i, 0))],
        out_specs=pl.BlockSpec(memory_space=pl.ANY),
        scratch_shapes=[pltpu.VMEM((ROW_CHUNKS, LANES), F32), pltpu.SemaphoreType.DMA(()),
                        pltpu.SemaphoreType.DMA(())],
    )
    return pl.pallas_call(
        functools.partial(_dispatch_kernel, tb=tb),
        grid_spec=grid_spec,
        out_shape=jax.ShapeDtypeStruct((n_rows * ROW_CHUNKS, LANES), F32),
        compiler_params=_params("arbitrary"),
        name="moe_dispatch",
    )(pos_flat, counts, offsets, x_tiled)


def _moe_ffn_kernel(te_ref, nt_ref, x_ref, wg_ref, wu_ref, wd_ref, o_ref, xb_ref, acc_ref):
    i = pl.program_id(0)
    f = pl.program_id(1)

    @pl.when(i < nt_ref[0])
    def _():
        @pl.when(f == 0)
        def _():
            for j in range(ROW_CHUNKS):
                xb_ref[:, j * LANES:(j + 1) * LANES] = _load_row_tiled(x_ref, j, MOE_TILE).astype(BF16)
            acc_ref[...] = jnp.zeros_like(acc_ref)

        xb = xb_ref[...]
        gate = jnp.dot(xb, wg_ref[0], preferred_element_type=F32)
        up = jnp.dot(xb, wu_ref[0], preferred_element_type=F32)
        h = (_silu(gate) * up).astype(BF16)
        acc_ref[...] += jnp.dot(h, wd_ref[0], preferred_element_type=F32)

        @pl.when(f == pl.num_programs(1) - 1)
        def _():
            _store_row_tiled(o_ref, acc_ref[...])

    @pl.when((i >= nt_ref[0]) & (f == 0))
    def _():
        o_ref[...] = jnp.zeros_like(o_ref)


def moe_ffn(xs, tile_expert, n_tiles_used, wg, wu, wd, tf=512):
    n_rows, d = xs.shape[0] // ROW_CHUNKS, D_MODEL
    n_tiles = n_rows // MOE_TILE
    tile_spec = lambda index: pl.BlockSpec((MOE_TILE * ROW_CHUNKS, LANES), index)
    fdim = wg.shape[2]
    nf = fdim // tf

    def live(i, nt):
        return jnp.minimum(i, nt[0] - 1)

    def fblk(i, f, nt):
        return jnp.where(i < nt[0], f, nf - 1)

    grid_spec = pltpu.PrefetchScalarGridSpec(
        num_scalar_prefetch=2,
        grid=(n_tiles, nf),
        in_specs=[tile_spec(lambda i, f, te, nt: (live(i, nt), 0)),
                  pl.BlockSpec((1, d, tf), lambda i, f, te, nt: (te[i], 0, fblk(i, f, nt))),
                  pl.BlockSpec((1, d, tf), lambda i, f, te, nt: (te[i], 0, fblk(i, f, nt))),
                  pl.BlockSpec((1, tf, d), lambda i, f, te, nt: (te[i], fblk(i, f, nt), 0))],
        out_specs=tile_spec(lambda i, f, te, nt: (i, 0)),
        scratch_shapes=[pltpu.VMEM((MOE_TILE, d), BF16), pltpu.VMEM((MOE_TILE, d), F32)],
    )
    return pl.pallas_call(
        _moe_ffn_kernel,
        grid_spec=grid_spec,
        out_shape=jax.ShapeDtypeStruct(xs.shape, F32),
        compiler_params=_params("arbitrary", "arbitrary"),
        name="moe_ffn",
    )(tile_expert, n_tiles_used, xs, wg, wu, wd)


def _combine_kernel(pos_ref, x_ref, gate_ref, g_ref, b_ref, ys_ref, o_ref, buf_ref, sem, *, tb):
    base = pl.program_id(0) * (TOP_K * tb)

    def issue(t, _):
        for k in range(TOP_K):
            pltpu.make_async_copy(ys_ref.at[_tile_rows(pos_ref[base + TOP_K * t + k])],
                                  buf_ref.at[k, _tile_rows(t)], sem).start()
        return 0

    lax.fori_loop(0, tb, issue, 0, unroll=DMA_ISSUE_UNROLL)
    for k in range(TOP_K):
        pltpu.make_async_copy(ys_ref.at[pl.ds(0, tb * ROW_CHUNKS)], buf_ref.at[k], sem).wait()
    gates = gate_ref[...]
    y = jnp.concatenate(
        [buf_ref[0, pl.ds(j, tb, stride=ROW_CHUNKS), :] * gates[:, 0:1]
         + buf_ref[1, pl.ds(j, tb, stride=ROW_CHUNKS), :] * gates[:, 1:2] for j in range(ROW_CHUNKS)],
        axis=1)
    o_ref[...] = _layer_norm_rows(DN_ALPHA * x_ref[...] + y, g_ref[...], b_ref[...])


def moe_combine(x, ys, pos_flat, gates, g, b):
    t, d = x.shape
    tb = min(ROUTE_BLOCK, t)
    grid_spec = pltpu.PrefetchScalarGridSpec(
        num_scalar_prefetch=1,
        grid=(t // tb,),
        in_specs=[pl.BlockSpec((tb, d), lambda i, *_: (i, 0)),
                  pl.BlockSpec((tb, LANES), lambda i, *_: (i, 0)),
                  pl.BlockSpec((1, d), lambda i, *_: (0, 0)),
                  pl.BlockSpec((1, d), lambda i, *_: (0, 0)),
                  pl.BlockSpec(memory_space=pl.ANY)],
        out_specs=pl.BlockSpec((tb, d), lambda i, *_: (i, 0)),
        scratch_shapes=[pltpu.VMEM((TOP_K, tb * ROW_CHUNKS, LANES), F32), pltpu.SemaphoreType.DMA(())],
    )
    return pl.pallas_call(
        functools.partial(_combine_kernel, tb=tb),
        grid_spec=grid_spec,
        out_shape=jax.ShapeDtypeStruct((t, d), F32),
        compiler_params=_params("arbitrary"),
        name="moe_combine",
    )(pos_flat, x, gates, g.reshape(1, d), b.reshape(1, d), ys)


def moe_deepnorm(x, x_tiled, w_router, wg, wu, wd, g, b):
    t = x.shape[0]
    idx, gates = moe_router(x, w_router)
    rank, counts_f = moe_rank(idx)
    counts = counts_f[0, :N_EXPERTS].astype(jnp.int32)
    padded = (counts + MOE_TILE - 1) // MOE_TILE * MOE_TILE
    ends = jnp.cumsum(padded)
    offsets = ends - padded
    n_tiles = (TOP_K * t) // MOE_TILE + N_EXPERTS
    n_used = (ends[-1] // MOE_TILE).astype(jnp.int32)
    tile_start = jnp.arange(n_tiles, dtype=jnp.int32) * MOE_TILE
    tile_start = jnp.minimum(tile_start, ends[-1] - MOE_TILE)
    tile_expert = jnp.sum(tile_start[:, None] >= ends[None, :], axis=1).astype(jnp.int32)
    pos = moe_positions(idx, rank, offsets)
    pos_flat = pos[:, :TOP_K].reshape(-1)
    xs = moe_dispatch(x_tiled, pos_flat, counts, offsets, n_tiles * MOE_TILE)
    ys = moe_ffn(xs, tile_expert, n_used.reshape(1), wg.astype(BF16), wu.astype(BF16), wd.astype(BF16))
    return moe_combine(x, ys, pos_flat, gates, g, b)


def kernel(x, l0_ssd_w_in, l0_ssd_conv_w, l0_ssd_conv_b, l0_ssd_dt_bias, l0_ssd_a_log, l0_ssd_d_skip, l0_ssd_norm_w, l0_ssd_w_out, l0_ln_mix_g, l0_ln_mix_b, l0_ffn_w_gate, l0_ffn_w_up, l0_ffn_w_down, l0_ln_ffn_g, l0_ln_ffn_b, l1_sb_w_qkv, l1_sb_w_out, l1_ln_mix_g, l1_ln_mix_b, l1_moe_w_router, l1_moe_w_gate, l1_moe_w_up, l1_moe_w_down, l1_ln_ffn_g, l1_ln_ffn_b, l2_fox_w_qkvf, l2_fox_b_f, l2_fox_w_out, l2_ln_mix_g, l2_ln_mix_b, l2_ffn_w_gate, l2_ffn_w_up, l2_ffn_w_down, l2_ln_ffn_g, l2_ln_ffn_b, l3_ssd_w_in, l3_ssd_conv_w, l3_ssd_conv_b, l3_ssd_dt_bias, l3_ssd_a_log, l3_ssd_d_skip, l3_ssd_norm_w, l3_ssd_w_out, l3_ln_mix_g, l3_ln_mix_b, l3_moe_w_router, l3_moe_w_gate, l3_moe_w_up, l3_moe_w_down, l3_ln_ffn_g, l3_ln_ffn_b):
    batch, seq, d = x.shape
    h = x.reshape(batch * seq, d)
    bf = lambda w: w.astype(BF16)
    h = ssd_mixer_deepnorm(h, l0_ssd_w_in, l0_ssd_conv_w, l0_ssd_conv_b, l0_ssd_dt_bias, l0_ssd_a_log,
                           l0_ssd_d_skip, l0_ssd_norm_w, l0_ssd_w_out, l0_ln_mix_g, l0_ln_mix_b, batch, seq)
    h = ffn_deepnorm(h, bf(l0_ffn_w_gate), bf(l0_ffn_w_up), bf(l0_ffn_w_down), l0_ln_ffn_g, l0_ln_ffn_b)
    h, h_tiled = sb_mixer_deepnorm(h, l1_sb_w_qkv, l1_sb_w_out, l1_ln_mix_g, l1_ln_mix_b, batch, seq,
                                   also_row_tiled=True)
    h = moe_deepnorm(h, h_tiled, l1_moe_w_router, l1_moe_w_gate, l1_moe_w_up, l1_moe_w_down, l1_ln_ffn_g, l1_ln_ffn_b)
    h = fox_mixer_deepnorm(h, l2_fox_w_qkvf, l2_fox_b_f, l2_fox_w_out, l2_ln_mix_g, l2_ln_mix_b, batch, seq)
    h = ffn_deepnorm(h, bf(l2_ffn_w_gate), bf(l2_ffn_w_up), bf(l2_ffn_w_down), l2_ln_ffn_g, l2_ln_ffn_b)
    h, h_tiled = ssd_mixer_deepnorm(h, l3_ssd_w_in, l3_ssd_conv_w, l3_ssd_conv_b, l3_ssd_dt_bias, l3_ssd_a_log,
                                    l3_ssd_d_skip, l3_ssd_norm_w, l3_ssd_w_out, l3_ln_mix_g, l3_ln_mix_b,
                                    batch, seq, also_row_tiled=True)
    h = moe_deepnorm(h, h_tiled, l3_moe_w_router, l3_moe_w_gate, l3_moe_w_up, l3_moe_w_down, l3_ln_ffn_g, l3_ln_ffn_b)
    return h.reshape(batch, seq, d)
```

```python
import functools
import math

import jax
import jax.numpy as jnp
from jax import lax
from jax.experimental import pallas as pl
from jax.experimental.pallas import tpu as pltpu

F32 = jnp.float32
BF16 = jnp.bfloat16
HIGHEST = lax.Precision.HIGHEST

LANES = 128
SUBLANES = 8
VMEM_LIMIT_BYTES = 56 * 1024 * 1024

D_MODEL = 1024
DEPTH = 4
SSD_D_INNER = 2048
SSD_HEAD_DIM = 64
SSD_HEADS = 32
SSD_GROUPS = 4
SSD_STATE = 128
SSD_CONV = 4
SSD_GROUP_DIM = SSD_D_INNER // SSD_GROUPS
SSD_BC_DIM = 2 * SSD_GROUPS * SSD_STATE
ATT_HEAD_DIM = 64
ATT_HEADS = 16
ATT_DIM = 1024
N_EXPERTS = 8
TOP_K = 2
DN_ALPHA = (2.0 * DEPTH) ** 0.25
LN_EPS = 1e-5
RMS_EPS = 1e-5
NEG_BIG = -1e30


def _params(*semantics):
    return pltpu.CompilerParams(dimension_semantics=semantics, vmem_limit_bytes=VMEM_LIMIT_BYTES)


def _layer_norm_rows(h, g, b):
    mu = jnp.mean(h, axis=-1, keepdims=True)
    d = h - mu
    var = jnp.mean(d * d, axis=-1, keepdims=True)
    return d * lax.rsqrt(var + LN_EPS) * g + b


def _silu(x):
    return x * jax.nn.sigmoid(x)


def _mm_kernel(x_ref, w_ref, o_ref):
    o_ref[...] = jnp.dot(x_ref[...].astype(BF16), w_ref[...],
                         preferred_element_type=F32).astype(o_ref.dtype)


def matmul(x, w, out_dtype, tm=1024, tn=512):
    m, k = x.shape
    tm = min(tm, m)
    n = w.shape[1]
    return pl.pallas_call(
        _mm_kernel,
        grid=(m // tm, n // tn),
        in_specs=[pl.BlockSpec((tm, k), lambda i, j: (i, 0)),
                  pl.BlockSpec((k, tn), lambda i, j: (0, j))],
        out_specs=pl.BlockSpec((tm, tn), lambda i, j: (i, j)),
        out_shape=jax.ShapeDtypeStruct((m, n), out_dtype),
        compiler_params=_params("parallel", "parallel"),
        name="matmul",
    )(x, w)


def _mm_f32_kernel(x_ref, w_ref, o_ref):
    o_ref[...] = jnp.dot(x_ref[...], w_ref[...], preferred_element_type=F32, precision=HIGHEST)


def matmul_f32(x, w, tm=1024):
    m, k = x.shape
    tm = min(tm, m)
    n = w.shape[1]
    return pl.pallas_call(
        _mm_f32_kernel,
        grid=(m // tm,),
        in_specs=[pl.BlockSpec((tm, k), lambda i: (i, 0)),
                  pl.BlockSpec((k, n), lambda i: (0, 0))],
        out_specs=pl.BlockSpec((tm, n), lambda i: (i, 0)),
        out_shape=jax.ShapeDtypeStruct((m, n), F32),
        compiler_params=_params("parallel"),
        name="matmul_f32",
    )(x, w)


ROW_CHUNKS = D_MODEL // LANES


def _store_row_tiled(ref, value):
    n = value.shape[0]
    for j in range(ROW_CHUNKS):
        ref[pl.ds(j, n, stride=ROW_CHUNKS), :] = value[:, j * LANES:(j + 1) * LANES]


def _load_row_tiled(ref, j, n):
    return ref[pl.ds(j, n, stride=ROW_CHUNKS), :]


def _mm_ln_kernel(x_ref, w_ref, r_ref, g_ref, b_ref, o_ref, *maybe_tiled_ref):
    y = jnp.dot(x_ref[...], w_ref[...], preferred_element_type=F32)
    o = _layer_norm_rows(DN_ALPHA * r_ref[...] + y, g_ref[...], b_ref[...])
    o_ref[...] = o
    for t_ref in maybe_tiled_ref:
        _store_row_tiled(t_ref, o)


def matmul_deepnorm(x, w, resid, g, b, tm=512, also_row_tiled=False):
    m, k = x.shape
    tm = min(tm, m)
    d = w.shape[1]
    out_specs = [pl.BlockSpec((tm, d), lambda i: (i, 0))]
    out_shape = [jax.ShapeDtypeStruct((m, d), F32)]
    if also_row_tiled:
        out_specs.append(pl.BlockSpec((tm * ROW_CHUNKS, LANES), lambda i: (i, 0)))
        out_shape.append(jax.ShapeDtypeStruct((m * ROW_CHUNKS, LANES), F32))
    out = pl.pallas_call(
        _mm_ln_kernel,
        grid=(m // tm,),
        in_specs=[pl.BlockSpec((tm, k), lambda i: (i, 0)),
                  pl.BlockSpec((k, d), lambda i: (0, 0)),
                  pl.BlockSpec((tm, d), lambda i: (i, 0)),
                  pl.BlockSpec((1, d), lambda i: (0, 0)),
                  pl.BlockSpec((1, d), lambda i: (0, 0))],
        out_specs=out_specs,
        out_shape=out_shape,
        compiler_params=_params("parallel"),
        name="matmul_deepnorm",
    )(x, w, resid, g.reshape(1, d), b.reshape(1, d))
    return tuple(out) if also_row_tiled else out[0]


def _ffn_kernel(x_ref, wg_ref, wu_ref, wd_ref, g_ref, b_ref, o_ref, xb_ref, acc_ref):
    f = pl.program_id(1)

    @pl.when(f == 0)
    def _():
        xb_ref[...] = x_ref[...].astype(BF16)
        acc_ref[...] = jnp.zeros_like(acc_ref)

    xb = xb_ref[...]
    gate = jnp.dot(xb, wg_ref[...], preferred_element_type=F32)
    up = jnp.dot(xb, wu_ref[...], preferred_element_type=F32)
    h = (_silu(gate) * up).astype(BF16)
    acc_ref[...] += jnp.dot(h, wd_ref[...], preferred_element_type=F32)

    @pl.when(f == pl.num_programs(1) - 1)
    def _():
        o_ref[...] = _layer_norm_rows(DN_ALPHA * x_ref[...] + acc_ref[...], g_ref[...], b_ref[...])


def ffn_deepnorm(x, wg, wu, wd, g, b, tm=1024, tf=256):
    m, d = x.shape
    tm = min(tm, m)
    fdim = wg.shape[1]
    return pl.pallas_call(
        _ffn_kernel,
        grid=(m // tm, fdim // tf),
        in_specs=[pl.BlockSpec((tm, d), lambda i, f: (i, 0)),
                  pl.BlockSpec((d, tf), lambda i, f: (0, f)),
                  pl.BlockSpec((d, tf), lambda i, f: (0, f)),
                  pl.BlockSpec((tf, d), lambda i, f: (f, 0)),
                  pl.BlockSpec((1, d), lambda i, f: (0, 0)),
                  pl.BlockSpec((1, d), lambda i, f: (0, 0))],
        out_specs=pl.BlockSpec((tm, d), lambda i, f: (i, 0)),
        out_shape=jax.ShapeDtypeStruct((m, d), F32),
        scratch_shapes=[pltpu.VMEM((tm, d), BF16), pltpu.VMEM((tm, d), F32)],
        compiler_params=_params("parallel", "arbitrary"),
        name="ffn_deepnorm",
    )(x, wg, wu, wd, g.reshape(1, d), b.reshape(1, d))


def _split_bf16(v, pieces):
    out = []
    r = v
    for _ in range(pieces - 1):
        p = r.astype(BF16)
        out.append(p)
        r = r - p.astype(F32)
    out.append(r.astype(BF16))
    return out


def _expand(v, e_ref, pieces):
    stacked = jnp.concatenate(_split_bf16(v, pieces), axis=1)
    return jnp.dot(stacked, e_ref[...], preferred_element_type=F32)


def _tril_f32(n, strict=False):
    r = lax.broadcasted_iota(jnp.int32, (n, n), 0)
    c = lax.broadcasted_iota(jnp.int32, (n, n), 1)
    return ((r > c) if strict else (r >= c)).astype(F32)


def _ssd_kernel(z_ref, xs_ref, bc_ref, dt_ref, cwx_ref, cbx_ref, cwbc_ref, cbbc_ref,
                dtb_ref, alog_ref, dskip_ref, normw_ref, e64_ref, e128_ref, o_ref,
                convx_ref, convbc_ref, state_ref, xdt_ref, acol_ref, arow_ref, cb_ref, ydiag_ref,
                *, chunk):
    L = chunk
    c = pl.program_id(1)

    @pl.when(c == 0)
    def _():
        state_ref[...] = jnp.zeros_like(state_ref)
        convx_ref[0:SUBLANES, :] = jnp.zeros((SUBLANES, SSD_D_INNER), F32)
        convbc_ref[0:SUBLANES, :] = jnp.zeros((SUBLANES, SSD_BC_DIM), F32)

    @pl.when(c > 0)
    def _():
        convx_ref[0:SUBLANES, :] = convx_ref[L:L + SUBLANES, :]
        convbc_ref[0:SUBLANES, :] = convbc_ref[L:L + SUBLANES, :]

    convx_ref[SUBLANES:SUBLANES + L, :] = xs_ref[...]
    convbc_ref[SUBLANES:SUBLANES + L, :] = bc_ref[...]

    def conv_silu(buf_ref, w_ref, b_ref):
        acc = b_ref[...]
        for k in range(SSD_CONV):
            start = SUBLANES - (SSD_CONV - 1) + k
            acc = acc + w_ref[k:k + 1, :] * buf_ref[start:start + L, :]
        return _silu(acc)

    xs = conv_silu(convx_ref, cwx_ref, cbx_ref)
    bcv = conv_silu(convbc_ref, cwbc_ref, cbbc_ref)

    dt = jax.nn.softplus(dt_ref[...] + dtb_ref[...])
    da = dt * (-jnp.exp(alog_ref[...]))
    a_cs = jnp.dot(_tril_f32(L), da, preferred_element_type=F32, precision=HIGHEST)
    ea = jnp.exp(a_cs)
    dte = jnp.exp(a_cs[L - 1:L, :] - a_cs)

    dt_x = _expand(dt, e64_ref, 2)
    ea_x = _expand(ea, e64_ref, 2)
    dte_x = _expand(dte, e64_ref, 2)
    acol_ref[...] = _expand(a_cs, e128_ref, 3)
    a_t = a_cs.T
    for h in range(SSD_HEADS):
        arow_ref[h] = jnp.broadcast_to(a_t[h:h + 1, :], (SUBLANES, L))

    xdt = xs * dt_x
    xdt_ref[...] = xdt.astype(BF16)
    xdte = (xdt * dte_x).astype(BF16)

    y_off = []
    for g in range(SSD_GROUPS):
        bm = bcv[:, g * SSD_STATE:(g + 1) * SSD_STATE]
        cm = bcv[:, (SSD_GROUPS + g) * SSD_STATE:(SSD_GROUPS + g + 1) * SSD_STATE].astype(BF16)
        cb_ref[g] = lax.dot_general(cm, bm.astype(BF16), (((1,), (1,)), ((), ())),
                                    preferred_element_type=F32)
        gs = slice(g * SSD_GROUP_DIM, (g + 1) * SSD_GROUP_DIM)
        st = state_ref[g]
        y_off.append(jnp.dot(cm, st.astype(BF16), preferred_element_type=F32) * ea_x[:, gs])
        state_ref[g] = st * ea_x[L - 1:L, gs] + jnp.dot(
            bm.T.astype(BF16), xdte[:, gs], preferred_element_type=F32)

    row = lax.broadcasted_iota(jnp.int32, (L, L), 0)
    col = lax.broadcasted_iota(jnp.int32, (L, L), 1)
    causal = row >= col
    head0 = lax.broadcasted_iota(jnp.int32, (L, LANES), 1) < SSD_HEAD_DIM

    def pair_body(p, carry):
        lanes = pl.ds(pl.multiple_of(p * LANES, LANES), LANES)
        xpair = xdt_ref[:, lanes]
        cbg = cb_ref[p // (SSD_HEADS // SSD_GROUPS // 2)]
        ys = []
        for j in range(2):
            h = 2 * p + j
            a_l = acol_ref[:, pl.ds(pl.multiple_of(h * LANES, LANES), LANES)]
            a_l = jnp.concatenate([a_l] * (L // LANES), axis=1)
            a_s = arow_ref[h][0:1, :]
            decay = jnp.exp(jnp.where(causal, a_l - a_s, NEG_BIG))
            ys.append(jnp.dot((cbg * decay).astype(BF16), xpair, preferred_element_type=F32))
        ydiag_ref[:, lanes] = jnp.where(head0, ys[0], ys[1])
        return carry

    lax.fori_loop(0, SSD_HEADS // 2, pair_body, 0)

    y = ydiag_ref[...] + jnp.concatenate(y_off, axis=1) + dskip_ref[...] * xs
    y = y * _silu(z_ref[...])
    parts = []
    for g in range(SSD_GROUPS):
        yg = y[:, g * SSD_GROUP_DIM:(g + 1) * SSD_GROUP_DIM]
        parts.append(yg * lax.rsqrt(jnp.mean(yg * yg, axis=-1, keepdims=True) + RMS_EPS))
    o_ref[...] = (jnp.concatenate(parts, axis=1) * normw_ref[...]).astype(o_ref.dtype)


def _expansion_matrix(width, pieces):
    h = jnp.arange(LANES)[:, None]
    lane = jnp.arange(SSD_HEADS * width)[None, :]
    e = (lane // width == h).astype(BF16)
    return jnp.concatenate([e] * pieces, axis=0)


def ssd_core(zx, dt_raw, conv_w, conv_b, dt_bias, a_log, d_skip, norm_w, batch, seq, chunk=128):
    t = zx.shape[0]
    nc = seq // chunk
    pad = LANES - SSD_HEADS
    row = lambda v: v.reshape(1, -1).astype(F32)
    args = (
        zx, zx, zx, dt_raw,
        conv_w[:, :SSD_D_INNER], row(conv_b[:SSD_D_INNER]),
        conv_w[:, SSD_D_INNER:], row(conv_b[SSD_D_INNER:]),
        row(jnp.pad(dt_bias, (0, pad))), row(jnp.pad(a_log, (0, pad))),
        row(jnp.repeat(d_skip, SSD_HEAD_DIM)), row(norm_w),
        _expansion_matrix(SSD_HEAD_DIM, 2), _expansion_matrix(LANES, 3),
    )
    blk = lambda b, c: (b * nc + c, 0)
    const = lambda b, c: (0, 0)
    full = lambda a: pl.BlockSpec(a.shape, const)
    in_specs = [
        pl.BlockSpec((chunk, SSD_D_INNER), blk),
        pl.BlockSpec((chunk, SSD_D_INNER), lambda b, c: (b * nc + c, 1)),
        pl.BlockSpec((chunk, SSD_BC_DIM), lambda b, c: (b * nc + c, 4)),
        pl.BlockSpec((chunk, LANES), blk),
    ] + [full(a) for a in args[4:]]
    return pl.pallas_call(
        functools.partial(_ssd_kernel, chunk=chunk),
        grid=(batch, nc),
        in_specs=in_specs,
        out_specs=pl.BlockSpec((chunk, SSD_D_INNER), blk),
        out_shape=jax.ShapeDtypeStruct((t, SSD_D_INNER), BF16),
        scratch_shapes=[
            pltpu.VMEM((chunk + SUBLANES, SSD_D_INNER), F32),
            pltpu.VMEM((chunk + SUBLANES, SSD_BC_DIM), F32),
            pltpu.VMEM((SSD_GROUPS, SSD_STATE, SSD_GROUP_DIM), F32),
            pltpu.VMEM((chunk, SSD_D_INNER), BF16),
            pltpu.VMEM((chunk, SSD_HEADS * LANES), F32),
            pltpu.VMEM((SSD_HEADS, SUBLANES, chunk), F32),
            pltpu.VMEM((SSD_GROUPS, chunk, chunk), F32),
            pltpu.VMEM((chunk, SSD_D_INNER), F32),
        ],
        compiler_params=_params("parallel", "arbitrary"),
        name="ssd_core",
    )(*args)


def ssd_mixer_deepnorm(x, w_in, conv_w, conv_b, dt_bias, a_log, d_skip, norm_w, w_out, g, b,
                       batch, seq, also_row_tiled=False):
    n_zx = SSD_D_INNER + SSD_D_INNER + SSD_BC_DIM
    zx = matmul(x, w_in[:, :n_zx].astype(BF16), F32)
    w_dt = jnp.pad(w_in[:, n_zx:], ((0, 0), (0, LANES - SSD_HEADS)))
    dt_raw = matmul_f32(x, w_dt)
    y = ssd_core(zx, dt_raw, conv_w, conv_b, dt_bias, a_log, d_skip, norm_w, batch, seq)
    return matmul_deepnorm(y, w_out.astype(BF16), x, g, b, also_row_tiled=also_row_tiled)


ATT_BLOCK = 128
ATT_Q = 256
ATT_PAIRS = 4
ATT_STEP_LANES = ATT_PAIRS * LANES
ATT_SCALE = ATT_HEAD_DIM ** -0.5
LOG2E = 1.4426950408889634


def _split_heads(q_ref, rows, head0):
    qs = []
    for p in range(ATT_PAIRS):
        q2 = q_ref[rows, p * LANES:(p + 1) * LANES]
        zero = jnp.zeros_like(q2)
        qs += [jnp.where(head0, q2, zero), jnp.where(head0, zero, q2)]
    return qs


def _sb_kernel(q_ref, k_ref, v_ref, u_ref, o_ref, *scratch, seq):
    n_heads = 2 * ATT_PAIRS
    later_ref, acc_ref, z_ref = scratch[:n_heads], scratch[n_heads:2 * n_heads], scratch[2 * n_heads:]
    nq = seq // ATT_Q
    ndiag = ATT_Q // ATT_BLOCK
    head0 = lax.broadcasted_iota(jnp.int32, (ATT_Q, LANES), 1) < ATT_HEAD_DIM
    row = lax.broadcasted_iota(jnp.int32, (ATT_Q, ATT_BLOCK), 0)
    col = lax.broadcasted_iota(jnp.int32, (ATT_Q, ATT_BLOCK), 1)

    def q_body(i, _):
        q0 = pl.multiple_of(i * ATT_Q, ATT_Q)
        rows = pl.ds(q0, ATT_Q)
        qs = _split_heads(q_ref, rows, head0)
        for n in range(2 * ATT_PAIRS):
            later_ref[n][...] = jnp.zeros((ATT_Q, LANES), F32)
            acc_ref[n][...] = jnp.zeros((ATT_Q, LANES), F32)

        heads = range(2 * ATT_PAIRS)

        def key_rows(j):
            return pl.ds(pl.multiple_of(j * ATT_BLOCK, ATT_BLOCK), ATT_BLOCK)

        def scores(j):
            k2 = [k_ref[key_rows(j), p * LANES:(p + 1) * LANES] for p in range(ATT_PAIRS)]
            return [lax.dot_general(qs[n], k2[n // 2], (((1,), (1,)), ((), ())),
                                    preferred_element_type=F32) * (ATT_SCALE * LOG2E) for n in heads]

        def consume(z2, j, strict):
            keys = key_rows(j)
            v2 = [v_ref[keys, p * LANES:(p + 1) * LANES] for p in range(ATT_PAIRS)]
            log_beta, sums = [], []
            for n in heads:
                lb = jnp.minimum(z2[n], 0.0) - jnp.log2(1.0 + jnp.exp2(-jnp.abs(z2[n])))
                log_keep = lb - z2[n]
                if strict is not None:
                    log_keep = jnp.where(strict, log_keep, 0.0)
                hi = log_keep.astype(BF16)
                lo = (log_keep - hi.astype(F32)).astype(BF16)
                log_beta.append(lb)
                sums.append(jnp.dot(jnp.concatenate([hi, lo], axis=1), u_ref[...],
                                    preferred_element_type=F32))
            for n in heads:
                w = jnp.exp2(log_beta[n] + sums[n][:, :ATT_BLOCK] + later_ref[n][...])
                if strict is not None:
                    w = jnp.where(strict, w, 0.0)
                acc_ref[n][...] += jnp.dot(w.astype(BF16), v2[n // 2], preferred_element_type=F32)
                later_ref[n][...] += sums[n][:, ATT_BLOCK:]

        for d in reversed(range(ndiag)):
            consume(scores(i * ndiag + d), i * ndiag + d, d * ATT_BLOCK + col < row)

        n_off = i * ndiag
        for n, zn in enumerate(scores(jnp.maximum(n_off - 1, 0))):
            z_ref[n][...] = zn

        def off_diagonal(jj, c):
            z2 = [z_ref[n][...] for n in heads]
            ahead = scores(jnp.maximum(n_off - 2 - jj, 0))
            consume(z2, n_off - 1 - jj, None)
            for n in heads:
                z_ref[n][...] = ahead[n]
            return c

        lax.fori_loop(0, n_off, off_diagonal, 0)
        for p in range(ATT_PAIRS):
            o_ref[rows, p * LANES:(p + 1) * LANES] = jnp.where(
                head0, acc_ref[2 * p][...], acc_ref[2 * p + 1][...]).astype(o_ref.dtype)
        return 0

    lax.fori_loop(0, nq, q_body, 0)


def _suffix_sum_matrix():
    j = jnp.arange(2 * ATT_BLOCK)[:, None] % ATT_BLOCK
    s = jnp.arange(2 * ATT_BLOCK)[None, :]
    return jnp.where(s < ATT_BLOCK, j > s, True).astype(BF16)


def _attention_specs(seq):
    n_steps = ATT_HEADS // 2 // ATT_PAIRS
    blk = lambda off: pl.BlockSpec((seq, ATT_STEP_LANES), lambda b, p: (b, off + p))
    specs = [blk(0), blk(n_steps), blk(2 * n_steps)]
    return n_steps, specs, pl.BlockSpec((seq, ATT_STEP_LANES), lambda b, p: (b, p))


def sb_attention(qkv, batch, seq):
    n_steps, in_specs, out_spec = _attention_specs(seq)
    u = _suffix_sum_matrix()
    return pl.pallas_call(
        functools.partial(_sb_kernel, seq=seq),
        grid=(batch, n_steps),
        in_specs=in_specs + [pl.BlockSpec(u.shape, lambda b, p: (0, 0))],
        out_specs=out_spec,
        out_shape=jax.ShapeDtypeStruct((batch * seq, ATT_DIM), BF16),
        scratch_shapes=[pltpu.VMEM((ATT_Q, LANES), F32)] * (6 * ATT_PAIRS),
        compiler_params=_params("parallel", "parallel"),
        name="sb_attention",
    )(qkv, qkv, qkv, u)


def sb_mixer_deepnorm(x, w_qkv, w_out, g, b, batch, seq, also_row_tiled=False):
    qkv = matmul(x, w_qkv.astype(BF16), BF16)
    o = sb_attention(qkv, batch, seq)
    return matmul_deepnorm(o, w_out.astype(BF16), x, g, b, also_row_tiled=also_row_tiled)


CUMSUM_BLOCK = 256


def _fox_decay_kernel(f_ref, bf_ref, ccol_ref, crow_ref, *, seq):
    tri = _tril_f32(CUMSUM_BLOCK)
    carry = jnp.zeros((1, LANES), F32)
    for blk in range(seq // CUMSUM_BLOCK):
        rows = slice(blk * CUMSUM_BLOCK, (blk + 1) * CUMSUM_BLOCK)
        log_f = jax.nn.log_sigmoid(f_ref[rows, :] + bf_ref[...])
        c = jnp.dot(tri, log_f, preferred_element_type=F32, precision=HIGHEST) + carry
        carry = c[CUMSUM_BLOCK - 1:CUMSUM_BLOCK, :]
        ccol_ref[rows, :] = c
        c_t = c.T
        for p in range(ATT_HEADS // 2):
            crow_ref[0, p, :, rows] = c_t[2 * p:2 * p + 2, :]


def fox_decay(f_raw, b_f, batch, seq):
    return pl.pallas_call(
        functools.partial(_fox_decay_kernel, seq=seq),
        grid=(batch,),
        in_specs=[pl.BlockSpec((seq, LANES), lambda b: (b, 0)),
                  pl.BlockSpec((1, LANES), lambda b: (0, 0))],
        out_specs=[pl.BlockSpec((seq, LANES), lambda b: (b, 0)),
                   pl.BlockSpec((1, ATT_HEADS // 2, 2, seq), lambda b: (b, 0, 0, 0))],
        out_shape=[jax.ShapeDtypeStruct((batch * seq, LANES), F32),
                   jax.ShapeDtypeStruct((batch, ATT_HEADS // 2, 2, seq), F32)],
        compiler_params=_params("parallel"),
        name="fox_decay",
    )(f_raw, jnp.pad(b_f, (0, LANES - ATT_HEADS)).reshape(1, LANES))


def _fox_kernel(q_ref, k_ref, v_ref, ccol_ref, crow_ref, o_ref, crep_ref, vt_ref, *scratch, seq):
    n_heads = 2 * ATT_PAIRS
    m_ref, acc_ref, s_ref = scratch[:n_heads], scratch[n_heads:2 * n_heads], scratch[2 * n_heads:]
    nq = seq // ATT_Q
    ndiag = ATT_Q // ATT_BLOCK
    step = pl.program_id(1)
    heads = range(n_heads)
    head0 = lax.broadcasted_iota(jnp.int32, (ATT_Q, LANES), 1) < ATT_HEAD_DIM
    head0_rows = lax.broadcasted_iota(jnp.int32, (LANES, ATT_BLOCK), 0) < ATT_HEAD_DIM
    key = lax.broadcasted_iota(jnp.int32, (ATT_BLOCK, ATT_Q), 0)
    qry = lax.broadcasted_iota(jnp.int32, (ATT_BLOCK, ATT_Q), 1)

    pieces = jnp.concatenate(_split_bf16(ccol_ref[...], 3), axis=1)
    sel_row = lax.broadcasted_iota(jnp.int32, (3 * LANES, LANES), 0) & (LANES - 1)
    for n in heads:
        sel = jnp.where(sel_row == 2 * ATT_PAIRS * step + n, 1.0, 0.0).astype(BF16)
        crep_ref[n] = jnp.dot(pieces, sel, preferred_element_type=F32) * LOG2E
    for p in range(ATT_PAIRS):
        for blk in range(seq // ATT_BLOCK):
            rows = slice(blk * ATT_BLOCK, (blk + 1) * ATT_BLOCK)
            vt_ref[p, :, rows] = v_ref[rows, p * LANES:(p + 1) * LANES].astype(F32).T.astype(BF16)

    def q_body(i, _):
        q0 = pl.multiple_of(i * ATT_Q, ATT_Q)
        rows = pl.ds(q0, ATT_Q)
        qs = _split_heads(q_ref, rows, head0)
        c_q = [crow_ref[0, n // 2, n % 2:n % 2 + 1, rows] * LOG2E for n in heads]
        for n in heads:
            m_ref[n][...] = jnp.full((1, ATT_Q), NEG_BIG, F32)
            acc_ref[n][...] = jnp.zeros((LANES, ATT_Q), F32)

        def key_rows(j):
            return pl.ds(pl.multiple_of(j * ATT_BLOCK, ATT_BLOCK), ATT_BLOCK)

        def scores(j):
            k2 = [k_ref[key_rows(j), p * LANES:(p + 1) * LANES] for p in range(ATT_PAIRS)]
            return [lax.dot_general(k2[n // 2], qs[n], (((1,), (1,)), ((), ())),
                                    preferred_element_type=F32) * (ATT_SCALE * LOG2E) for n in heads]

        def consume(s, j, causal):
            keys = key_rows(j)
            vt = [vt_ref[p, :, keys] for p in range(ATT_PAIRS)]
            for n in heads:
                c_k = crep_ref[n, keys, :]
                sn = s[n] + (c_q[n] - jnp.concatenate([c_k] * (ATT_Q // LANES), axis=1))
                if causal is not None:
                    sn = jnp.where(causal, sn, NEG_BIG)
                m_old = m_ref[n][...]
                m_new = jnp.maximum(m_old, jnp.max(sn, axis=0, keepdims=True))
                prob = jnp.exp2(sn - m_new).astype(BF16)
                one = jnp.ones_like(vt[n // 2])
                v1t = jnp.where(head0_rows, vt[n // 2], one) if n % 2 == 0 else jnp.where(
                    head0_rows, one, vt[n // 2])
                acc_ref[n][...] = jnp.exp2(m_old - m_new) * acc_ref[n][...] + jnp.dot(
                    v1t, prob, preferred_element_type=F32)
                m_ref[n][...] = m_new

        for d in range(ndiag):
            consume(scores(i * ndiag + d), i * ndiag + d, d * ATT_BLOCK + key <= qry)

        n_off = i * ndiag
        for n, sn in enumerate(scores(0)):
            s_ref[n][...] = sn

        def off_diagonal(jj, c):
            s = [s_ref[n][...] for n in heads]
            ahead = scores(jnp.minimum(jj + 1, n_off - 1))
            consume(s, jj, None)
            for n in heads:
                s_ref[n][...] = ahead[n]
            return c

        lax.fori_loop(0, n_off, off_diagonal, 0)
        for p in range(ATT_PAIRS):
            a0, a1 = acc_ref[2 * p][...], acc_ref[2 * p + 1][...]
            o_t = jnp.concatenate([a0[:ATT_HEAD_DIM] / a0[ATT_HEAD_DIM:],
                                   a1[ATT_HEAD_DIM:] / a1[:ATT_HEAD_DIM]], axis=0)
            o_ref[rows, p * LANES:(p + 1) * LANES] = o_t.T.astype(o_ref.dtype)
        return 0

    lax.fori_loop(0, nq, q_body, 0)


def fox_attention(qkv, ccol, crow, batch, seq):
    n_steps, in_specs, out_spec = _attention_specs(seq)
    in_specs += [pl.BlockSpec((seq, LANES), lambda b, p: (b, 0)),
                 pl.BlockSpec((1, ATT_PAIRS, 2, seq), lambda b, p: (b, p, 0, 0))]
    return pl.pallas_call(
        functools.partial(_fox_kernel, seq=seq),
        grid=(batch, n_steps),
        in_specs=in_specs,
        out_specs=out_spec,
        out_shape=jax.ShapeDtypeStruct((batch * seq, ATT_DIM), BF16),
        scratch_shapes=([pltpu.VMEM((2 * ATT_PAIRS, seq, LANES), F32),
                         pltpu.VMEM((ATT_PAIRS, LANES, seq), BF16)]
                        + [pltpu.VMEM((1, ATT_Q), F32)] * (2 * ATT_PAIRS)
                        + [pltpu.VMEM((LANES, ATT_Q), F32)] * (2 * ATT_PAIRS)
                        + [pltpu.VMEM((ATT_BLOCK, ATT_Q), F32)] * (2 * ATT_PAIRS)),
        compiler_params=_params("parallel", "parallel"),
        name="fox_attention",
    )(qkv, qkv, qkv, ccol, crow)


def fox_mixer_deepnorm(x, w_qkvf, b_f, w_out, g, b, batch, seq):
    qkv = matmul(x, w_qkvf[:, :3 * ATT_DIM].astype(BF16), BF16)
    w_f = jnp.pad(w_qkvf[:, 3 * ATT_DIM:], ((0, 0), (0, LANES - ATT_HEADS)))
    ccol, crow = fox_decay(matmul_f32(x, w_f), b_f, batch, seq)
    o = fox_attention(qkv, ccol, crow, batch, seq)
    return matmul_deepnorm(o, w_out.astype(BF16), x, g, b)


MOE_TILE = 512
ROUTE_BLOCK = 512


def _router_kernel(x_ref, w_ref, idx_ref, gate_ref):
    logits = jnp.dot(x_ref[...], w_ref[...], preferred_element_type=F32, precision=HIGHEST)
    lane = lax.broadcasted_iota(jnp.int32, logits.shape, 1)
    logits = jnp.where(lane < N_EXPERTS, logits, NEG_BIG)
    m1 = jnp.max(logits, axis=-1, keepdims=True)
    i1 = jnp.min(jnp.where(logits == m1, lane, LANES), axis=-1, keepdims=True)
    rest = jnp.where(lane == i1, NEG_BIG, logits)
    m2 = jnp.max(rest, axis=-1, keepdims=True)
    i2 = jnp.min(jnp.where(rest == m2, lane, LANES), axis=-1, keepdims=True)
    e2 = jnp.exp(m2 - m1)
    denom = 1.0 + e2
    idx_ref[...] = jnp.where(lane == 0, i1, jnp.where(lane == 1, i2, 0))
    gate_ref[...] = jnp.where(lane == 0, 1.0 / denom, jnp.where(lane == 1, e2 / denom, 0.0))


def moe_router(x, w_router):
    t, d = x.shape
    tb = min(ROUTE_BLOCK, t)
    w = jnp.pad(w_router, ((0, 0), (0, LANES - N_EXPERTS)))
    return pl.pallas_call(
        _router_kernel,
        grid=(t // tb,),
        in_specs=[pl.BlockSpec((tb, d), lambda i: (i, 0)), pl.BlockSpec((d, LANES), lambda i: (0, 0))],
        out_specs=[pl.BlockSpec((tb, LANES), lambda i: (i, 0))] * 2,
        out_shape=[jax.ShapeDtypeStruct((t, LANES), jnp.int32), jax.ShapeDtypeStruct((t, LANES), F32)],
        compiler_params=_params("parallel"),
        name="moe_router",
    )(x, w)


def _rank_kernel(idx_ref, rank_ref, count_ref, run_ref):
    @pl.when(pl.program_id(0) == 0)
    def _():
        run_ref[...] = jnp.zeros_like(run_ref)

    idx = idx_ref[...]
    tb = idx.shape[0]
    lane = lax.broadcasted_iota(jnp.int32, idx.shape, 1)
    oh0 = lane == idx[:, 0:1]
    oh1 = lane == idx[:, 1:2]
    both = jnp.where(oh0 | oh1, 1.0, 0.0)
    before = jnp.dot(_tril_f32(tb, strict=True).astype(BF16), both.astype(BF16),
                     preferred_element_type=F32) + run_ref[...]
    r0 = jnp.sum(jnp.where(oh0, before, 0.0), axis=-1, keepdims=True)
    r1 = jnp.sum(jnp.where(oh1, before, 0.0), axis=-1, keepdims=True)
    rank_ref[...] = jnp.where(lane == 0, r0, jnp.where(lane == 1, r1, 0.0)).astype(jnp.int32)
    run_ref[...] += jnp.sum(both, axis=0, keepdims=True)
    count_ref[...] = run_ref[...]


def moe_rank(idx):
    t = idx.shape[0]
    tb = min(ROUTE_BLOCK, t)
    return pl.pallas_call(
        _rank_kernel,
        grid=(t // tb,),
        in_specs=[pl.BlockSpec((tb, LANES), lambda i: (i, 0))],
        out_specs=[pl.BlockSpec((tb, LANES), lambda i: (i, 0)), pl.BlockSpec((1, LANES), lambda i: (0, 0))],
        out_shape=[jax.ShapeDtypeStruct((t, LANES), jnp.int32), jax.ShapeDtypeStruct((1, LANES), F32)],
        scratch_shapes=[pltpu.VMEM((1, LANES), F32)],
        compiler_params=_params("arbitrary"),
        name="moe_rank",
    )(idx)


def _pos_kernel(idx_ref, rank_ref, off_ref, pos_ref):
    idx = idx_ref[...]
    lane = lax.broadcasted_iota(jnp.int32, idx.shape, 1)
    off = off_ref[...]
    p0 = jnp.sum(jnp.where(lane == idx[:, 0:1], off, 0), axis=-1, keepdims=True)
    p1 = jnp.sum(jnp.where(lane == idx[:, 1:2], off, 0), axis=-1, keepdims=True)
    pos_ref[...] = rank_ref[...] + jnp.where(lane == 0, p0, jnp.where(lane == 1, p1, 0))


def moe_positions(idx, rank, offsets):
    t = idx.shape[0]
    tb = min(ROUTE_BLOCK, t)
    off = jnp.pad(offsets, (0, LANES - N_EXPERTS)).reshape(1, LANES)
    blk = pl.BlockSpec((tb, LANES), lambda i: (i, 0))
    return pl.pallas_call(
        _pos_kernel,
        grid=(t // tb,),
        in_specs=[blk, blk, pl.BlockSpec((1, LANES), lambda i: (0, 0))],
        out_specs=blk,
        out_shape=jax.ShapeDtypeStruct((t, LANES), jnp.int32),
        compiler_params=_params("parallel"),
        name="moe_positions",
    )(idx, rank, off)


DMA_ISSUE_UNROLL = 8


def _tile_rows(row):
    return pl.ds(pl.multiple_of(row * ROW_CHUNKS, ROW_CHUNKS), ROW_CHUNKS)


def _dispatch_kernel(pos_ref, cnt_ref, off_ref, x_ref, xs_ref, zero_ref, sem, pad_sem, *, tb):
    base = pl.program_id(0) * (TOP_K * tb)

    def issue(t, _):
        for k in range(TOP_K):
            pltpu.make_async_copy(x_ref.at[_tile_rows(t)],
                                  xs_ref.at[_tile_rows(pos_ref[base + TOP_K * t + k])], sem).start()
        return 0

    lax.fori_loop(0, tb, issue, 0, unroll=DMA_ISSUE_UNROLL)

    @pl.when(pl.program_id(0) == 0)
    def _():
        zero_ref[...] = jnp.zeros_like(zero_ref)

        def fill_range(first, count):
            def copy(r):
                return pltpu.make_async_copy(zero_ref, xs_ref.at[_tile_rows(first + r)], pad_sem)

            def fill(r, _):
                copy(r).start()
                return 0

            def drain(r, _):
                copy(r).wait()
                return 0

            lax.fori_loop(0, count, fill, 0)
            lax.fori_loop(0, count, drain, 0)

        for e in range(N_EXPERTS):
            fill_range(off_ref[e] + cnt_ref[e], (-cnt_ref[e]) & (MOE_TILE - 1))
        last = N_EXPERTS - 1
        used = off_ref[last] + cnt_ref[last] + ((-cnt_ref[last]) & (MOE_TILE - 1))
        fill_range(used, xs_ref.shape[0] // ROW_CHUNKS - used)

    for _ in range(TOP_K):
        pltpu.make_async_copy(x_ref, xs_ref.at[pl.ds(0, tb * ROW_CHUNKS)], sem).wait()


def moe_dispatch(x_tiled, pos_flat, counts, offsets, n_rows):
    t = x_tiled.shape[0] // ROW_CHUNKS
    tb = min(ROUTE_BLOCK, t)
    grid_spec = pltpu.PrefetchScalarGridSpec(
        num_scalar_prefetch=3,
        grid=(t // tb,),
        in_specs=[pl.BlockSpec((tb * ROW_CHUNKS, LANES), lambda i, *_: (i, 0))],
        out_specs=pl.BlockSpec(memory_space=pl.ANY),
        scratch_shapes=[pltpu.VMEM((ROW_CHUNKS, LANES), F32), pltpu.SemaphoreType.DMA(()),
                        pltpu.SemaphoreType.DMA(())],
    )
    return pl.pallas_call(
        functools.partial(_dispatch_kernel, tb=tb),
        grid_spec=grid_spec,
        out_shape=jax.ShapeDtypeStruct((n_rows * ROW_CHUNKS, LANES), F32),
        compiler_params=_params("arbitrary"),
        name="moe_dispatch",
    )(pos_flat, counts, offsets, x_tiled)


def _moe_ffn_kernel(te_ref, nt_ref, x_ref, wg_ref, wu_ref, wd_ref, o_ref, xb_ref, acc_ref):
    i = pl.program_id(0)
    f = pl.program_id(1)

    @pl.when(i < nt_ref[0])
    def _():
        @pl.when(f == 0)
        def _():
            for j in range(ROW_CHUNKS):
                xb_ref[:, j * LANES:(j + 1) * LANES] = _load_row_tiled(x_ref, j, MOE_TILE).astype(BF16)
            acc_ref[...] = jnp.zeros_like(acc_ref)

        xb = xb_ref[...]
        gate = jnp.dot(xb, wg_ref[0], preferred_element_type=F32)
        up = jnp.dot(xb, wu_ref[0], preferred_element_type=F32)
        h = (_silu(gate) * up).astype(BF16)
        acc_ref[...] += jnp.dot(h, wd_ref[0], preferred_element_type=F32)

        @pl.when(f == pl.num_programs(1) - 1)
        def _():
            _store_row_tiled(o_ref, acc_ref[...])

    @pl.when((i >= nt_ref[0]) & (f == 0))
    def _():
        o_ref[...] = jnp.zeros_like(o_ref)


def moe_ffn(xs, tile_expert, n_tiles_used, wg, wu, wd, tf=512):
    n_rows, d = xs.shape[0] // ROW_CHUNKS, D_MODEL
    n_tiles = n_rows // MOE_TILE
    tile_spec = lambda index: pl.BlockSpec((MOE_TILE * ROW_CHUNKS, LANES), index)
    fdim = wg.shape[2]
    nf = fdim // tf

    def live(i, nt):
        return jnp.minimum(i, nt[0] - 1)

    def fblk(i, f, nt):
        return jnp.where(i < nt[0], f, nf - 1)

    grid_spec = pltpu.PrefetchScalarGridSpec(
        num_scalar_prefetch=2,
        grid=(n_tiles, nf),
        in_specs=[tile_spec(lambda i, f, te, nt: (live(i, nt), 0)),
                  pl.BlockSpec((1, d, tf), lambda i, f, te, nt: (te[i], 0, fblk(i, f, nt))),
                  pl.BlockSpec((1, d, tf), lambda i, f, te, nt: (te[i], 0, fblk(i, f, nt))),
                  pl.BlockSpec((1, tf, d), lambda i, f, te, nt: (te[i], fblk(i, f, nt), 0))],
        out_specs=tile_spec(lambda i, f, te, nt: (i, 0)),
        scratch_shapes=[pltpu.VMEM((MOE_TILE, d), BF16), pltpu.VMEM((MOE_TILE, d), F32)],
    )
    return pl.pallas_call(
        _moe_ffn_kernel,
        grid_spec=grid_spec,
        out_shape=jax.ShapeDtypeStruct(xs.shape, F32),
        compiler_params=_params("arbitrary", "arbitrary"),
        name="moe_ffn",
    )(tile_expert, n_tiles_used, xs, wg, wu, wd)


def _combine_kernel(pos_ref, x_ref, gate_ref, g_ref, b_ref, ys_ref, o_ref, buf_ref, sem, *, tb):
    base = pl.program_id(0) * (TOP_K * tb)

    def issue(t, _):
        for k in range(TOP_K):
            pltpu.make_async_copy(ys_ref.at[_tile_rows(pos_ref[base + TOP_K * t + k])],
                                  buf_ref.at[k, _tile_rows(t)], sem).start()
        return 0

    lax.fori_loop(0, tb, issue, 0, unroll=DMA_ISSUE_UNROLL)
    for k in range(TOP_K):
        pltpu.make_async_copy(ys_ref.at[pl.ds(0, tb * ROW_CHUNKS)], buf_ref.at[k], sem).wait()
    gates = gate_ref[...]
    y = jnp.concatenate(
        [buf_ref[0, pl.ds(j, tb, stride=ROW_CHUNKS), :] * gates[:, 0:1]
         + buf_ref[1, pl.ds(j, tb, stride=ROW_CHUNKS), :] * gates[:, 1:2] for j in range(ROW_CHUNKS)],
        axis=1)
    o_ref[...] = _layer_norm_rows(DN_ALPHA * x_ref[...] + y, g_ref[...], b_ref[...])


def moe_combine(x, ys, pos_flat, gates, g, b):
    t, d = x.shape
    tb = min(ROUTE_BLOCK, t)
    grid_spec = pltpu.PrefetchScalarGridSpec(
        num_scalar_prefetch=1,
        grid=(t // tb,),
        in_specs=[pl.BlockSpec((tb, d), lambda i, *_: (i, 0)),
                  pl.BlockSpec((tb, LANES), lambda i, *_: (i, 0)),
                  pl.BlockSpec((1, d), lambda i, *_: (0, 0)),
                  pl.BlockSpec((1, d), lambda i, *_: (0, 0)),
                  pl.BlockSpec(memory_space=pl.ANY)],
        out_specs=pl.BlockSpec((tb, d), lambda i, *_: (i, 0)),
        scratch_shapes=[pltpu.VMEM((TOP_K, tb * ROW_CHUNKS, LANES), F32), pltpu.SemaphoreType.DMA(())],
    )
    return pl.pallas_call(
        functools.partial(_combine_kernel, tb=tb),
        grid_spec=grid_spec,
        out_shape=jax.ShapeDtypeStruct((t, d), F32),
        compiler_params=_params("arbitrary"),
        name="moe_combine",
    )(pos_flat, x, gates, g.reshape(1, d), b.reshape(1, d), ys)


def moe_deepnorm(x, x_tiled, w_router, wg, wu, wd, g, b):
    t = x.shape[0]
    idx, gates = moe_router(x, w_router)
    rank, counts_f = moe_rank(idx)
    counts = counts_f[0, :N_EXPERTS].astype(jnp.int32)
    padded = (counts + MOE_TILE - 1) // MOE_TILE * MOE_TILE
    ends = jnp.cumsum(padded)
    offsets = ends - padded
    n_tiles = (TOP_K * t) // MOE_TILE + N_EXPERTS
    n_used = (ends[-1] // MOE_TILE).astype(jnp.int32)
    tile_start = jnp.arange(n_tiles, dtype=jnp.int32) * MOE_TILE
    tile_start = jnp.minimum(tile_start, ends[-1] - MOE_TILE)
    tile_expert = jnp.sum(tile_start[:, None] >= ends[None, :], axis=1).astype(jnp.int32)
    pos = moe_positions(idx, rank, offsets)
    pos_flat = pos[:, :TOP_K].reshape(-1)
    xs = moe_dispatch(x_tiled, pos_flat, counts, offsets, n_tiles * MOE_TILE)
    ys = moe_ffn(xs, tile_expert, n_used.reshape(1), wg.astype(BF16), wu.astype(BF16), wd.astype(BF16))
    return moe_combine(x, ys, pos_flat, gates, g, b)


def kernel(x, l0_ssd_w_in, l0_ssd_conv_w, l0_ssd_conv_b, l0_ssd_dt_bias, l0_ssd_a_log, l0_ssd_d_skip, l0_ssd_norm_w, l0_ssd_w_out, l0_ln_mix_g, l0_ln_mix_b, l0_ffn_w_gate, l0_ffn_w_up, l0_ffn_w_down, l0_ln_ffn_g, l0_ln_ffn_b, l1_sb_w_qkv, l1_sb_w_out, l1_ln_mix_g, l1_ln_mix_b, l1_moe_w_router, l1_moe_w_gate, l1_moe_w_up, l1_moe_w_down, l1_ln_ffn_g, l1_ln_ffn_b, l2_fox_w_qkvf, l2_fox_b_f, l2_fox_w_out, l2_ln_mix_g, l2_ln_mix_b, l2_ffn_w_gate, l2_ffn_w_up, l2_ffn_w_down, l2_ln_ffn_g, l2_ln_ffn_b, l3_ssd_w_in, l3_ssd_conv_w, l3_ssd_conv_b, l3_ssd_dt_bias, l3_ssd_a_log, l3_ssd_d_skip, l3_ssd_norm_w, l3_ssd_w_out, l3_ln_mix_g, l3_ln_mix_b, l3_moe_w_router, l3_moe_w_gate, l3_moe_w_up, l3_moe_w_down, l3_ln_ffn_g, l3_ln_ffn_b):
    batch, seq, d = x.shape
    h = x.reshape(batch * seq, d)
    bf = lambda w: w.astype(BF16)
    h = ssd_mixer_deepnorm(h, l0_ssd_w_in, l0_ssd_conv_w, l0_ssd_conv_b, l0_ssd_dt_bias, l0_ssd_a_log,
                           l0_ssd_d_skip, l0_ssd_norm_w, l0_ssd_w_out, l0_ln_mix_g, l0_ln_mix_b, batch, seq)
    h = ffn_deepnorm(h, bf(l0_ffn_w_gate), bf(l0_ffn_w_up), bf(l0_ffn_w_down), l0_ln_ffn_g, l0_ln_ffn_b)
    h, h_tiled = sb_mixer_deepnorm(h, l1_sb_w_qkv, l1_sb_w_out, l1_ln_mix_g, l1_ln_mix_b, batch, seq,
                                   also_row_tiled=True)
    h = moe_deepnorm(h, h_tiled, l1_moe_w_router, l1_moe_w_gate, l1_moe_w_up, l1_moe_w_down, l1_ln_ffn_g, l1_ln_ffn_b)
    h = fox_mixer_deepnorm(h, l2_fox_w_qkvf, l2_fox_b_f, l2_fox_w_out, l2_ln_mix_g, l2_ln_mix_b, batch, seq)
    h = ffn_deepnorm(h, bf(l2_ffn_w_gate), bf(l2_ffn_w_up), bf(l2_ffn_w_down), l2_ln_ffn_g, l2_ln_ffn_b)
    h, h_tiled = ssd_mixer_deepnorm(h, l3_ssd_w_in, l3_ssd_conv_w, l3_ssd_conv_b, l3_ssd_dt_bias, l3_ssd_a_log,
                                    l3_ssd_d_skip, l3_ssd_norm_w, l3_ssd_w_out, l3_ln_mix_g, l3_ln_mix_b,
                                    batch, seq, also_row_tiled=True)
    h = moe_deepnorm(h, h_tiled, l3_moe_w_router, l3_moe_w_gate, l3_moe_w_up, l3_moe_w_down, l3_ln_ffn_g, l3_ln_ffn_b)
    return h.reshape(batch, seq, d)
```

```python
import functools
import math

import jax
import jax.numpy as jnp
from jax import lax
from jax.experimental import pallas as pl
from jax.experimental.pallas import tpu as pltpu

F32 = jnp.float32
BF16 = jnp.bfloat16
HIGHEST = lax.Precision.HIGHEST

LANES = 128
SUBLANES = 8
VMEM_LIMIT_BYTES = 56 * 1024 * 1024

D_MODEL = 1024
DEPTH = 4
SSD_D_INNER = 2048
SSD_HEAD_DIM = 64
SSD_HEADS = 32
SSD_GROUPS = 4
SSD_STATE = 128
SSD_CONV = 4
SSD_GROUP_DIM = SSD_D_INNER // SSD_GROUPS
SSD_BC_DIM = 2 * SSD_GROUPS * SSD_STATE
ATT_HEAD_DIM = 64
ATT_HEADS = 16
ATT_DIM = 1024
N_EXPERTS = 8
TOP_K = 2
DN_ALPHA = (2.0 * DEPTH) ** 0.25
LN_EPS = 1e-5
RMS_EPS = 1e-5
NEG_BIG = -1e30


def _params(*semantics):
    return pltpu.CompilerParams(dimension_semantics=semantics, vmem_limit_bytes=VMEM_LIMIT_BYTES)


def _layer_norm_rows(h, g, b):
    mu = jnp.mean(h, axis=-1, keepdims=True)
    d = h - mu
    var = jnp.mean(d * d, axis=-1, keepdims=True)
    return d * lax.rsqrt(var + LN_EPS) * g + b


def _silu(x):
    return x * jax.nn.sigmoid(x)


def _mm_kernel(x_ref, w_ref, o_ref):
    o_ref[...] = jnp.dot(x_ref[...].astype(BF16), w_ref[...],
                         preferred_element_type=F32).astype(o_ref.dtype)


def matmul(x, w, out_dtype, tm=1024, tn=512):
    m, k = x.shape
    tm = min(tm, m)
    n = w.shape[1]
    return pl.pallas_call(
        _mm_kernel,
        grid=(m // tm, n // tn),
        in_specs=[pl.BlockSpec((tm, k), lambda i, j: (i, 0)),
                  pl.BlockSpec((k, tn), lambda i, j: (0, j))],
        out_specs=pl.BlockSpec((tm, tn), lambda i, j: (i, j)),
        out_shape=jax.ShapeDtypeStruct((m, n), out_dtype),
        compiler_params=_params("parallel", "parallel"),
        name="matmul",
    )(x, w)


def _mm_f32_kernel(x_ref, w_ref, o_ref):
    o_ref[...] = jnp.dot(x_ref[...], w_ref[...], preferred_element_type=F32, precision=HIGHEST)


def matmul_f32(x, w, tm=1024):
    m, k = x.shape
    tm = min(tm, m)
    n = w.shape[1]
    return pl.pallas_call(
        _mm_f32_kernel,
        grid=(m // tm,),
        in_specs=[pl.BlockSpec((tm, k), lambda i: (i, 0)),
                  pl.BlockSpec((k, n), lambda i: (0, 0))],
        out_specs=pl.BlockSpec((tm, n), lambda i: (i, 0)),
        out_shape=jax.ShapeDtypeStruct((m, n), F32),
        compiler_params=_params("parallel"),
        name="matmul_f32",
    )(x, w)


ROW_CHUNKS = D_MODEL // LANES


def _store_row_tiled(ref, value):
    n = value.shape[0]
    for j in range(ROW_CHUNKS):
        ref[pl.ds(j, n, stride=ROW_CHUNKS), :] = value[:, j * LANES:(j + 1) * LANES]


def _load_row_tiled(ref, j, n):
    return ref[pl.ds(j, n, stride=ROW_CHUNKS), :]


def _mm_ln_kernel(x_ref, w_ref, r_ref, g_ref, b_ref, o_ref, *maybe_tiled_ref):
    y = jnp.dot(x_ref[...], w_ref[...], preferred_element_type=F32)
    o = _layer_norm_rows(DN_ALPHA * r_ref[...] + y, g_ref[...], b_ref[...])
    o_ref[...] = o
    for t_ref in maybe_tiled_ref:
        _store_row_tiled(t_ref, o)


def matmul_deepnorm(x, w, resid, g, b, tm=512, also_row_tiled=False):
    m, k = x.shape
    tm = min(tm, m)
    d = w.shape[1]
    out_specs = [pl.BlockSpec((tm, d), lambda i: (i, 0))]
    out_shape = [jax.ShapeDtypeStruct((m, d), F32)]
    if also_row_tiled:
        out_specs.append(pl.BlockSpec((tm * ROW_CHUNKS, LANES), lambda i: (i, 0)))
        out_shape.append(jax.ShapeDtypeStruct((m * ROW_CHUNKS, LANES), F32))
    out = pl.pallas_call(
        _mm_ln_kernel,
        grid=(m // tm,),
        in_specs=[pl.BlockSpec((tm, k), lambda i: (i, 0)),
                  pl.BlockSpec((k, d), lambda i: (0, 0)),
                  pl.BlockSpec((tm, d), lambda i: (i, 0)),
                  pl.BlockSpec((1, d), lambda i: (0, 0)),
                  pl.BlockSpec((1, d), lambda i: (0, 0))],
        out_specs=out_specs,
        out_shape=out_shape,
        compiler_params=_params("parallel"),
        name="matmul_deepnorm",
    )(x, w, resid, g.reshape(1, d), b.reshape(1, d))
    return tuple(out) if also_row_tiled else out[0]


def _ffn_kernel(x_ref, wg_ref, wu_ref, wd_ref, g_ref, b_ref, o_ref):
    x = x_ref[...]
    xb = x.astype(BF16)
    gate = jnp.dot(xb, wg_ref[...], preferred_element_type=F32)
    up = jnp.dot(xb, wu_ref[...], preferred_element_type=F32)
    h = (_silu(gate) * up).astype(BF16)
    y = jnp.dot(h, wd_ref[...], preferred_element_type=F32)
    o_ref[...] = _layer_norm_rows(DN_ALPHA * x + y, g_ref[...], b_ref[...])


def ffn_deepnorm(x, wg, wu, wd, g, b, tm=512):
    m, d = x.shape
    tm = min(tm, m)
    resident = lambda a: pl.BlockSpec(a.shape, lambda i: (0, 0), pipeline_mode=pl.Buffered(1))
    return pl.pallas_call(
        _ffn_kernel,
        grid=(m // tm,),
        in_specs=[pl.BlockSpec((tm, d), lambda i: (i, 0)),
                  resident(wg), resident(wu), resident(wd),
                  pl.BlockSpec((1, d), lambda i: (0, 0)),
                  pl.BlockSpec((1, d), lambda i: (0, 0))],
        out_specs=pl.BlockSpec((tm, d), lambda i: (i, 0)),
        out_shape=jax.ShapeDtypeStruct((m, d), F32),
        compiler_params=_params("parallel"),
        name="ffn_deepnorm",
    )(x, wg, wu, wd, g.reshape(1, d), b.reshape(1, d))


def _split_bf16(v, pieces):
    out = []
    r = v
    for _ in range(pieces - 1):
        p = r.astype(BF16)
        out.append(p)
        r = r - p.astype(F32)
    out.append(r.astype(BF16))
    return out


def _expand(v, e_ref, pieces):
    stacked = jnp.concatenate(_split_bf16(v, pieces), axis=1)
    return jnp.dot(stacked, e_ref[...], preferred_element_type=F32)


def _tril_f32(n, strict=False):
    r = lax.broadcasted_iota(jnp.int32, (n, n), 0)
    c = lax.broadcasted_iota(jnp.int32, (n, n), 1)
    return ((r > c) if strict else (r >= c)).astype(F32)


def _ssd_kernel(z_ref, xs_ref, bc_ref, dt_ref, cwx_ref, cbx_ref, cwbc_ref, cbbc_ref,
                dtb_ref, alog_ref, dskip_ref, normw_ref, e64_ref, e128_ref, o_ref,
                convx_ref, convbc_ref, state_ref, xdt_ref, acol_ref, arow_ref, cb_ref, ydiag_ref,
                *, chunk):
    L = chunk
    c = pl.program_id(1)

    @pl.when(c == 0)
    def _():
        state_ref[...] = jnp.zeros_like(state_ref)
        convx_ref[0:SUBLANES, :] = jnp.zeros((SUBLANES, SSD_D_INNER), F32)
        convbc_ref[0:SUBLANES, :] = jnp.zeros((SUBLANES, SSD_BC_DIM), F32)

    @pl.when(c > 0)
    def _():
        convx_ref[0:SUBLANES, :] = convx_ref[L:L + SUBLANES, :]
        convbc_ref[0:SUBLANES, :] = convbc_ref[L:L + SUBLANES, :]

    convx_ref[SUBLANES:SUBLANES + L, :] = xs_ref[...]
    convbc_ref[SUBLANES:SUBLANES + L, :] = bc_ref[...]

    def conv_silu(buf_ref, w_ref, b_ref):
        acc = b_ref[...]
        for k in range(SSD_CONV):
            start = SUBLANES - (SSD_CONV - 1) + k
            acc = acc + w_ref[k:k + 1, :] * buf_ref[start:start + L, :]
        return _silu(acc)

    xs = conv_silu(convx_ref, cwx_ref, cbx_ref)
    bcv = conv_silu(convbc_ref, cwbc_ref, cbbc_ref)

    dt = jax.nn.softplus(dt_ref[...] + dtb_ref[...])
    da = dt * (-jnp.exp(alog_ref[...]))
    a_cs = jnp.dot(_tril_f32(L), da, preferred_element_type=F32, precision=HIGHEST)
    ea = jnp.exp(a_cs)
    dte = jnp.exp(a_cs[L - 1:L, :] - a_cs)

    dt_x = _expand(dt, e64_ref, 2)
    ea_x = _expand(ea, e64_ref, 2)
    dte_x = _expand(dte, e64_ref, 2)
    acol_ref[...] = _expand(a_cs, e128_ref, 3)
    a_t = a_cs.T
    for h in range(SSD_HEADS):
        arow_ref[h] = jnp.broadcast_to(a_t[h:h + 1, :], (SUBLANES, L))

    xdt = xs * dt_x
    xdt_ref[...] = xdt.astype(BF16)
    xdte = (xdt * dte_x).astype(BF16)

    y_off = []
    for g in range(SSD_GROUPS):
        bm = bcv[:, g * SSD_STATE:(g + 1) * SSD_STATE]
        cm = bcv[:, (SSD_GROUPS + g) * SSD_STATE:(SSD_GROUPS + g + 1) * SSD_STATE].astype(BF16)
        cb_ref[g] = lax.dot_general(cm, bm.astype(BF16), (((1,), (1,)), ((), ())),
                                    preferred_element_type=F32)
        gs = slice(g * SSD_GROUP_DIM, (g + 1) * SSD_GROUP_DIM)
        st = state_ref[g]
        y_off.append(jnp.dot(cm, st.astype(BF16), preferred_element_type=F32) * ea_x[:, gs])
        state_ref[g] = st * ea_x[L - 1:L, gs] + jnp.dot(
            bm.T.astype(BF16), xdte[:, gs], preferred_element_type=F32)

    row = lax.broadcasted_iota(jnp.int32, (L, L), 0)
    col = lax.broadcasted_iota(jnp.int32, (L, L), 1)
    causal = row >= col
    head0 = lax.broadcasted_iota(jnp.int32, (L, LANES), 1) < SSD_HEAD_DIM

    def pair_body(p, carry):
        lanes = pl.ds(pl.multiple_of(p * LANES, LANES), LANES)
        xpair = xdt_ref[:, lanes]
        cbg = cb_ref[p // (SSD_HEADS // SSD_GROUPS // 2)]
        ys = []
        for j in range(2):
            h = 2 * p + j
            a_l = acol_ref[:, pl.ds(pl.multiple_of(h * LANES, LANES), LANES)]
            a_l = jnp.concatenate([a_l] * (L // LANES), axis=1)
            a_s = arow_ref[h][0:1, :]
            decay = jnp.exp(jnp.where(causal, a_l - a_s, NEG_BIG))
            ys.append(jnp.dot((cbg * decay).astype(BF16), xpair, preferred_element_type=F32))
        ydiag_ref[:, lanes] = jnp.where(head0, ys[0], ys[1])
        return carry

    lax.fori_loop(0, SSD_HEADS // 2, pair_body, 0)

    y = ydiag_ref[...] + jnp.concatenate(y_off, axis=1) + dskip_ref[...] * xs
    y = y * _silu(z_ref[...])
    parts = []
    for g in range(SSD_GROUPS):
        yg = y[:, g * SSD_GROUP_DIM:(g + 1) * SSD_GROUP_DIM]
        parts.append(yg * lax.rsqrt(jnp.mean(yg * yg, axis=-1, keepdims=True) + RMS_EPS))
    o_ref[...] = (jnp.concatenate(parts, axis=1) * normw_ref[...]).astype(o_ref.dtype)


def _expansion_matrix(width, pieces):
    h = jnp.arange(LANES)[:, None]
    lane = jnp.arange(SSD_HEADS * width)[None, :]
    e = (lane // width == h).astype(BF16)
    return jnp.concatenate([e] * pieces, axis=0)


def ssd_core(zx, dt_raw, conv_w, conv_b, dt_bias, a_log, d_skip, norm_w, batch, seq, chunk=128):
    t = zx.shape[0]
    nc = seq // chunk
    pad = LANES - SSD_HEADS
    row = lambda v: v.reshape(1, -1).astype(F32)
    args = (
        zx, zx, zx, dt_raw,
        conv_w[:, :SSD_D_INNER], row(conv_b[:SSD_D_INNER]),
        conv_w[:, SSD_D_INNER:], row(conv_b[SSD_D_INNER:]),
        row(jnp.pad(dt_bias, (0, pad))), row(jnp.pad(a_log, (0, pad))),
        row(jnp.repeat(d_skip, SSD_HEAD_DIM)), row(norm_w),
        _expansion_matrix(SSD_HEAD_DIM, 2), _expansion_matrix(LANES, 3),
    )
    blk = lambda b, c: (b * nc + c, 0)
    const = lambda b, c: (0, 0)
    full = lambda a: pl.BlockSpec(a.shape, const)
    in_specs = [
        pl.BlockSpec((chunk, SSD_D_INNER), blk),
        pl.BlockSpec((chunk, SSD_D_INNER), lambda b, c: (b * nc + c, 1)),
        pl.BlockSpec((chunk, SSD_BC_DIM), lambda b, c: (b * nc + c, 4)),
        pl.BlockSpec((chunk, LANES), blk),
    ] + [full(a) for a in args[4:]]
    return pl.pallas_call(
        functools.partial(_ssd_kernel, chunk=chunk),
        grid=(batch, nc),
        in_specs=in_specs,
        out_specs=pl.BlockSpec((chunk, SSD_D_INNER), blk),
        out_shape=jax.ShapeDtypeStruct((t, SSD_D_INNER), BF16),
        scratch_shapes=[
            pltpu.VMEM((chunk + SUBLANES, SSD_D_INNER), F32),
            pltpu.VMEM((chunk + SUBLANES, SSD_BC_DIM), F32),
            pltpu.VMEM((SSD_GROUPS, SSD_STATE, SSD_GROUP_DIM), F32),
            pltpu.VMEM((chunk, SSD_D_INNER), BF16),
            pltpu.VMEM((chunk, SSD_HEADS * LANES), F32),
            pltpu.VMEM((SSD_HEADS, SUBLANES, chunk), F32),
            pltpu.VMEM((SSD_GROUPS, chunk, chunk), F32),
            pltpu.VMEM((chunk, SSD_D_INNER), F32),
        ],
        compiler_params=_params("parallel", "arbitrary"),
        name="ssd_core",
    )(*args)


def ssd_mixer_deepnorm(x, w_in, conv_w, conv_b, dt_bias, a_log, d_skip, norm_w, w_out, g, b,
                       batch, seq, also_row_tiled=False):
    n_zx = SSD_D_INNER + SSD_D_INNER + SSD_BC_DIM
    zx = matmul(x, w_in[:, :n_zx].astype(BF16), F32)
    w_dt = jnp.pad(w_in[:, n_zx:], ((0, 0), (0, LANES - SSD_HEADS)))
    dt_raw = matmul_f32(x, w_dt)
    y = ssd_core(zx, dt_raw, conv_w, conv_b, dt_bias, a_log, d_skip, norm_w, batch, seq)
    return matmul_deepnorm(y, w_out.astype(BF16), x, g, b, also_row_tiled=also_row_tiled)


ATT_BLOCK = 128
ATT_Q = 256
ATT_PAIRS = 4
ATT_STEP_LANES = ATT_PAIRS * LANES
ATT_SCALE = ATT_HEAD_DIM ** -0.5
LOG2E = 1.4426950408889634


def _split_heads(q_ref, rows, head0):
    qs = []
    for p in range(ATT_PAIRS):
        q2 = q_ref[rows, p * LANES:(p + 1) * LANES]
        zero = jnp.zeros_like(q2)
        qs += [jnp.where(head0, q2, zero), jnp.where(head0, zero, q2)]
    return qs


def _sb_kernel(q_ref, k_ref, v_ref, u_ref, o_ref, *scratch, seq):
    n_heads = 2 * ATT_PAIRS
    later_ref, acc_ref, z_ref = scratch[:n_heads], scratch[n_heads:2 * n_heads], scratch[2 * n_heads:]
    nq = seq // ATT_Q
    ndiag = ATT_Q // ATT_BLOCK
    head0 = lax.broadcasted_iota(jnp.int32, (ATT_Q, LANES), 1) < ATT_HEAD_DIM
    row = lax.broadcasted_iota(jnp.int32, (ATT_Q, ATT_BLOCK), 0)
    col = lax.broadcasted_iota(jnp.int32, (ATT_Q, ATT_BLOCK), 1)

    def q_body(i, _):
        q0 = pl.multiple_of(i * ATT_Q, ATT_Q)
        rows = pl.ds(q0, ATT_Q)
        qs = _split_heads(q_ref, rows, head0)
        for n in range(2 * ATT_PAIRS):
            later_ref[n][...] = jnp.zeros((ATT_Q, LANES), F32)
            acc_ref[n][...] = jnp.zeros((ATT_Q, LANES), F32)

        heads = range(2 * ATT_PAIRS)

        def key_rows(j):
            return pl.ds(pl.multiple_of(j * ATT_BLOCK, ATT_BLOCK), ATT_BLOCK)

        def scores(j):
            k2 = [k_ref[key_rows(j), p * LANES:(p + 1) * LANES] for p in range(ATT_PAIRS)]
            return [lax.dot_general(qs[n], k2[n // 2], (((1,), (1,)), ((), ())),
                                    preferred_element_type=F32) * (ATT_SCALE * LOG2E) for n in heads]

        def consume(z2, j, strict):
            keys = key_rows(j)
            v2 = [v_ref[keys, p * LANES:(p + 1) * LANES] for p in range(ATT_PAIRS)]
            log_beta, sums = [], []
            for n in heads:
                lb = jnp.minimum(z2[n], 0.0) - jnp.log2(1.0 + jnp.exp2(-jnp.abs(z2[n])))
                log_keep = lb - z2[n]
                if strict is not None:
                    log_keep = jnp.where(strict, log_keep, 0.0)
                hi = log_keep.astype(BF16)
                lo = (log_keep - hi.astype(F32)).astype(BF16)
                log_beta.append(lb)
                sums.append(jnp.dot(jnp.concatenate([hi, lo], axis=1), u_ref[...],
                                    preferred_element_type=F32))
            for n in heads:
                w = jnp.exp2(log_beta[n] + sums[n][:, :ATT_BLOCK] + later_ref[n][...])
                if strict is not None:
                    w = jnp.where(strict, w, 0.0)
                acc_ref[n][...] += jnp.dot(w.astype(BF16), v2[n // 2], preferred_element_type=F32)
                later_ref[n][...] += sums[n][:, ATT_BLOCK:]

        for d in reversed(range(ndiag)):
            consume(scores(i * ndiag + d), i * ndiag + d, d * ATT_BLOCK + col < row)

        n_off = i * ndiag
        for n, zn in enumerate(scores(jnp.maximum(n_off - 1, 0))):
            z_ref[n][...] = zn

        def off_diagonal(jj, c):
            z2 = [z_ref[n][...] for n in heads]
            ahead = scores(jnp.maximum(n_off - 2 - jj, 0))
            consume(z2, n_off - 1 - jj, None)
            for n in heads:
                z_ref[n][...] = ahead[n]
            return c

        lax.fori_loop(0, n_off, off_diagonal, 0)
        for p in range(ATT_PAIRS):
            o_ref[rows, p * LANES:(p + 1) * LANES] = jnp.where(
                head0, acc_ref[2 * p][...], acc_ref[2 * p + 1][...]).astype(o_ref.dtype)
        return 0

    lax.fori_loop(0, nq, q_body, 0)


def _suffix_sum_matrix():
    j = jnp.arange(2 * ATT_BLOCK)[:, None] % ATT_BLOCK
    s = jnp.arange(2 * ATT_BLOCK)[None, :]
    return jnp.where(s < ATT_BLOCK, j > s, True).astype(BF16)


def _attention_specs(seq):
    n_steps = ATT_HEADS // 2 // ATT_PAIRS
    blk = lambda off: pl.BlockSpec((seq, ATT_STEP_LANES), lambda b, p: (b, off + p))
    specs = [blk(0), blk(n_steps), blk(2 * n_steps)]
    return n_steps, specs, pl.BlockSpec((seq, ATT_STEP_LANES), lambda b, p: (b, p))


def sb_attention(qkv, batch, seq):
    n_steps, in_specs, out_spec = _attention_specs(seq)
    u = _suffix_sum_matrix()
    return pl.pallas_call(
        functools.partial(_sb_kernel, seq=seq),
        grid=(batch, n_steps),
        in_specs=in_specs + [pl.BlockSpec(u.shape, lambda b, p: (0, 0))],
        out_specs=out_spec,
        out_shape=jax.ShapeDtypeStruct((batch * seq, ATT_DIM), BF16),
        scratch_shapes=[pltpu.VMEM((ATT_Q, LANES), F32)] * (6 * ATT_PAIRS),
        compiler_params=_params("parallel", "parallel"),
        name="sb_attention",
    )(qkv, qkv, qkv, u)


def sb_mixer_deepnorm(x, w_qkv, w_out, g, b, batch, seq, also_row_tiled=False):
    qkv = matmul(x, w_qkv.astype(BF16), BF16)
    o = sb_attention(qkv, batch, seq)
    return matmul_deepnorm(o, w_out.astype(BF16), x, g, b, also_row_tiled=also_row_tiled)


CUMSUM_BLOCK = 256


def _fox_decay_kernel(f_ref, bf_ref, ccol_ref, crow_ref, *, seq):
    tri = _tril_f32(CUMSUM_BLOCK)
    carry = jnp.zeros((1, LANES), F32)
    for blk in range(seq // CUMSUM_BLOCK):
        rows = slice(blk * CUMSUM_BLOCK, (blk + 1) * CUMSUM_BLOCK)
        log_f = jax.nn.log_sigmoid(f_ref[rows, :] + bf_ref[...])
        c = jnp.dot(tri, log_f, preferred_element_type=F32, precision=HIGHEST) + carry
        carry = c[CUMSUM_BLOCK - 1:CUMSUM_BLOCK, :]
        ccol_ref[rows, :] = c
        c_t = c.T
        for p in range(ATT_HEADS // 2):
            crow_ref[0, p, :, rows] = c_t[2 * p:2 * p + 2, :]


def fox_decay(f_raw, b_f, batch, seq):
    return pl.pallas_call(
        functools.partial(_fox_decay_kernel, seq=seq),
        grid=(batch,),
        in_specs=[pl.BlockSpec((seq, LANES), lambda b: (b, 0)),
                  pl.BlockSpec((1, LANES), lambda b: (0, 0))],
        out_specs=[pl.BlockSpec((seq, LANES), lambda b: (b, 0)),
                   pl.BlockSpec((1, ATT_HEADS // 2, 2, seq), lambda b: (b, 0, 0, 0))],
        out_shape=[jax.ShapeDtypeStruct((batch * seq, LANES), F32),
                   jax.ShapeDtypeStruct((batch, ATT_HEADS // 2, 2, seq), F32)],
        compiler_params=_params("parallel"),
        name="fox_decay",
    )(f_raw, jnp.pad(b_f, (0, LANES - ATT_HEADS)).reshape(1, LANES))


def _fox_kernel(q_ref, k_ref, v_ref, ccol_ref, crow_ref, o_ref, crep_ref, vt_ref, *scratch, seq):
    n_heads = 2 * ATT_PAIRS
    m_ref, acc_ref, s_ref = scratch[:n_heads], scratch[n_heads:2 * n_heads], scratch[2 * n_heads:]
    nq = seq // ATT_Q
    ndiag = ATT_Q // ATT_BLOCK
    step = pl.program_id(1)
    heads = range(n_heads)
    head0 = lax.broadcasted_iota(jnp.int32, (ATT_Q, LANES), 1) < ATT_HEAD_DIM
    head0_rows = lax.broadcasted_iota(jnp.int32, (LANES, ATT_BLOCK), 0) < ATT_HEAD_DIM
    key = lax.broadcasted_iota(jnp.int32, (ATT_BLOCK, ATT_Q), 0)
    qry = lax.broadcasted_iota(jnp.int32, (ATT_BLOCK, ATT_Q), 1)

    pieces = jnp.concatenate(_split_bf16(ccol_ref[...], 3), axis=1)
    sel_row = lax.broadcasted_iota(jnp.int32, (3 * LANES, LANES), 0) & (LANES - 1)
    for n in heads:
        sel = jnp.where(sel_row == 2 * ATT_PAIRS * step + n, 1.0, 0.0).astype(BF16)
        crep_ref[n] = jnp.dot(pieces, sel, preferred_element_type=F32) * LOG2E
    for p in range(ATT_PAIRS):
        for blk in range(seq // ATT_BLOCK):
            rows = slice(blk * ATT_BLOCK, (blk + 1) * ATT_BLOCK)
            vt_ref[p, :, rows] = v_ref[rows, p * LANES:(p + 1) * LANES].astype(F32).T.astype(BF16)

    def q_body(i, _):
        q0 = pl.multiple_of(i * ATT_Q, ATT_Q)
        rows = pl.ds(q0, ATT_Q)
        qs = _split_heads(q_ref, rows, head0)
        c_q = [crow_ref[0, n // 2, n % 2:n % 2 + 1, rows] * LOG2E for n in heads]
        for n in heads:
            m_ref[n][...] = jnp.full((1, ATT_Q), NEG_BIG, F32)
            acc_ref[n][...] = jnp.zeros((LANES, ATT_Q), F32)

        def key_rows(j):
            return pl.ds(pl.multiple_of(j * ATT_BLOCK, ATT_BLOCK), ATT_BLOCK)

        def scores(j):
            k2 = [k_ref[key_rows(j), p * LANES:(p + 1) * LANES] for p in range(ATT_PAIRS)]
            return [lax.dot_general(k2[n // 2], qs[n], (((1,), (1,)), ((), ())),
                                    preferred_element_type=F32) * (ATT_SCALE * LOG2E) for n in heads]

        def consume(s, j, causal):
            keys = key_rows(j)
            vt = [vt_ref[p, :, keys] for p in range(ATT_PAIRS)]
            for n in heads:
                c_k = crep_ref[n, keys, :]
                sn = s[n] + (c_q[n] - jnp.concatenate([c_k] * (ATT_Q // LANES), axis=1))
                if causal is not None:
                    sn = jnp.where(causal, sn, NEG_BIG)
                m_old = m_ref[n][...]
                m_new = jnp.maximum(m_old, jnp.max(sn, axis=0, keepdims=True))
                prob = jnp.exp2(sn - m_new).astype(BF16)
                one = jnp.ones_like(vt[n // 2])
                v1t = jnp.where(head0_rows, vt[n // 2], one) if n % 2 == 0 else jnp.where(
                    head0_rows, one, vt[n // 2])
                acc_ref[n][...] = jnp.exp2(m_old - m_new) * acc_ref[n][...] + jnp.dot(
                    v1t, prob, preferred_element_type=F32)
                m_ref[n][...] = m_new

        for d in range(ndiag):
            consume(scores(i * ndiag + d), i * ndiag + d, d * ATT_BLOCK + key <= qry)

        n_off = i * ndiag
        for n, sn in enumerate(scores(0)):
            s_ref[n][...] = sn

        def off_diagonal(jj, c):
            s = [s_ref[n][...] for n in heads]
            ahead = scores(jnp.minimum(jj + 1, n_off - 1))
            consume(s, jj, None)
            for n in heads:
                s_ref[n][...] = ahead[n]
            return c

        lax.fori_loop(0, n_off, off_diagonal, 0)
        for p in range(ATT_PAIRS):
            a0, a1 = acc_ref[2 * p][...], acc_ref[2 * p + 1][...]
            o_t = jnp.concatenate([a0[:ATT_HEAD_DIM] / a0[ATT_HEAD_DIM:],
                                   a1[ATT_HEAD_DIM:] / a1[:ATT_HEAD_DIM]], axis=0)
            o_ref[rows, p * LANES:(p + 1) * LANES] = o_t.T.astype(o_ref.dtype)
        return 0

    lax.fori_loop(0, nq, q_body, 0)


def fox_attention(qkv, ccol, crow, batch, seq):
    n_steps, in_specs, out_spec = _attention_specs(seq)
    in_specs += [pl.BlockSpec((seq, LANES), lambda b, p: (b, 0)),
                 pl.BlockSpec((1, ATT_PAIRS, 2, seq), lambda b, p: (b, p, 0, 0))]
    return pl.pallas_call(
        functools.partial(_fox_kernel, seq=seq),
        grid=(batch, n_steps),
        in_specs=in_specs,
        out_specs=out_spec,
        out_shape=jax.ShapeDtypeStruct((batch * seq, ATT_DIM), BF16),
        scratch_shapes=([pltpu.VMEM((2 * ATT_PAIRS, seq, LANES), F32),
                         pltpu.VMEM((ATT_PAIRS, LANES, seq), BF16)]
                        + [pltpu.VMEM((1, ATT_Q), F32)] * (2 * ATT_PAIRS)
                        + [pltpu.VMEM((LANES, ATT_Q), F32)] * (2 * ATT_PAIRS)
                        + [pltpu.VMEM((ATT_BLOCK, ATT_Q), F32)] * (2 * ATT_PAIRS)),
        compiler_params=_params("parallel", "parallel"),
        name="fox_attention",
    )(qkv, qkv, qkv, ccol, crow)


def fox_mixer_deepnorm(x, w_qkvf, b_f, w_out, g, b, batch, seq):
    qkv = matmul(x, w_qkvf[:, :3 * ATT_DIM].astype(BF16), BF16)
    w_f = jnp.pad(w_qkvf[:, 3 * ATT_DIM:], ((0, 0), (0, LANES - ATT_HEADS)))
    ccol, crow = fox_decay(matmul_f32(x, w_f), b_f, batch, seq)
    o = fox_attention(qkv, ccol, crow, batch, seq)
    return matmul_deepnorm(o, w_out.astype(BF16), x, g, b)


MOE_TILE = 512
ROUTE_BLOCK = 512


def _router_kernel(x_ref, w_ref, idx_ref, gate_ref):
    logits = jnp.dot(x_ref[...], w_ref[...], preferred_element_type=F32, precision=HIGHEST)
    lane = lax.broadcasted_iota(jnp.int32, logits.shape, 1)
    logits = jnp.where(lane < N_EXPERTS, logits, NEG_BIG)
    m1 = jnp.max(logits, axis=-1, keepdims=True)
    i1 = jnp.min(jnp.where(logits == m1, lane, LANES), axis=-1, keepdims=True)
    rest = jnp.where(lane == i1, NEG_BIG, logits)
    m2 = jnp.max(rest, axis=-1, keepdims=True)
    i2 = jnp.min(jnp.where(rest == m2, lane, LANES), axis=-1, keepdims=True)
    e2 = jnp.exp(m2 - m1)
    denom = 1.0 + e2
    idx_ref[...] = jnp.where(lane == 0, i1, jnp.where(lane == 1, i2, 0))
    gate_ref[...] = jnp.where(lane == 0, 1.0 / denom, jnp.where(lane == 1, e2 / denom, 0.0))


def moe_router(x, w_router):
    t, d = x.shape
    tb = min(ROUTE_BLOCK, t)
    w = jnp.pad(w_router, ((0, 0), (0, LANES - N_EXPERTS)))
    return pl.pallas_call(
        _router_kernel,
        grid=(t // tb,),
        in_specs=[pl.BlockSpec((tb, d), lambda i: (i, 0)), pl.BlockSpec((d, LANES), lambda i: (0, 0))],
        out_specs=[pl.BlockSpec((tb, LANES), lambda i: (i, 0))] * 2,
        out_shape=[jax.ShapeDtypeStruct((t, LANES), jnp.int32), jax.ShapeDtypeStruct((t, LANES), F32)],
        compiler_params=_params("parallel"),
        name="moe_router",
    )(x, w)


def _rank_kernel(idx_ref, rank_ref, count_ref, run_ref):
    @pl.when(pl.program_id(0) == 0)
    def _():
        run_ref[...] = jnp.zeros_like(run_ref)

    idx = idx_ref[...]
    tb = idx.shape[0]
    lane = lax.broadcasted_iota(jnp.int32, idx.shape, 1)
    oh0 = lane == idx[:, 0:1]
    oh1 = lane == idx[:, 1:2]
    both = jnp.where(oh0 | oh1, 1.0, 0.0)
    before = jnp.dot(_tril_f32(tb, strict=True).astype(BF16), both.astype(BF16),
                     preferred_element_type=F32) + run_ref[...]
    r0 = jnp.sum(jnp.where(oh0, before, 0.0), axis=-1, keepdims=True)
    r1 = jnp.sum(jnp.where(oh1, before, 0.0), axis=-1, keepdims=True)
    rank_ref[...] = jnp.where(lane == 0, r0, jnp.where(lane == 1, r1, 0.0)).astype(jnp.int32)
    run_ref[...] += jnp.sum(both, axis=0, keepdims=True)
    count_ref[...] = run_ref[...]


def moe_rank(idx):
    t = idx.shape[0]
    tb = min(ROUTE_BLOCK, t)
    return pl.pallas_call(
        _rank_kernel,
        grid=(t // tb,),
        in_specs=[pl.BlockSpec((tb, LANES), lambda i: (i, 0))],
        out_specs=[pl.BlockSpec((tb, LANES), lambda i: (i, 0)), pl.BlockSpec((1, LANES), lambda i: (0, 0))],
        out_shape=[jax.ShapeDtypeStruct((t, LANES), jnp.int32), jax.ShapeDtypeStruct((1, LANES), F32)],
        scratch_shapes=[pltpu.VMEM((1, LANES), F32)],
        compiler_params=_params("arbitrary"),
        name="moe_rank",
    )(idx)


def _pos_kernel(idx_ref, rank_ref, off_ref, pos_ref):
    idx = idx_ref[...]
    lane = lax.broadcasted_iota(jnp.int32, idx.shape, 1)
    off = off_ref[...]
    p0 = jnp.sum(jnp.where(lane == idx[:, 0:1], off, 0), axis=-1, keepdims=True)
    p1 = jnp.sum(jnp.where(lane == idx[:, 1:2], off, 0), axis=-1, keepdims=True)
    pos_ref[...] = rank_ref[...] + jnp.where(lane == 0, p0, jnp.where(lane == 1, p1, 0))


def moe_positions(idx, rank, offsets):
    t = idx.shape[0]
    tb = min(ROUTE_BLOCK, t)
    off = jnp.pad(offsets, (0, LANES - N_EXPERTS)).reshape(1, LANES)
    blk = pl.BlockSpec((tb, LANES), lambda i: (i, 0))
    return pl.pallas_call(
        _pos_kernel,
        grid=(t // tb,),
        in_specs=[blk, blk, pl.BlockSpec((1, LANES), lambda i: (0, 0))],
        out_specs=blk,
        out_shape=jax.ShapeDtypeStruct((t, LANES), jnp.int32),
        compiler_params=_params("parallel"),
        name="moe_positions",
    )(idx, rank, off)


DMA_ISSUE_UNROLL = 8


def _tile_rows(row):
    return pl.ds(pl.multiple_of(row * ROW_CHUNKS, ROW_CHUNKS), ROW_CHUNKS)


def _dispatch_kernel(pos_ref, cnt_ref, off_ref, x_ref, xs_ref, zero_ref, sem, pad_sem, *, tb):
    base = pl.program_id(0) * (TOP_K * tb)

    def issue(t, _):
        for k in range(TOP_K):
            pltpu.make_async_copy(x_ref.at[_tile_rows(t)],
                                  xs_ref.at[_tile_rows(pos_ref[base + TOP_K * t + k])], sem).start()
        return 0

    lax.fori_loop(0, tb, issue, 0, unroll=DMA_ISSUE_UNROLL)

    @pl.when(pl.program_id(0) == 0)
    def _():
        zero_ref[...] = jnp.zeros_like(zero_ref)

        def fill_range(first, count):
            def copy(r):
                return pltpu.make_async_copy(zero_ref, xs_ref.at[_tile_rows(first + r)], pad_sem)

            def fill(r, _):
                copy(r).start()
                return 0

            def drain(r, _):
                copy(r).wait()
                return 0

            lax.fori_loop(0, count, fill, 0)
            lax.fori_loop(0, count, drain, 0)

        for e in range(N_EXPERTS):
            fill_range(off_ref[e] + cnt_ref[e], (-cnt_ref[e]) & (MOE_TILE - 1))
        last = N_EXPERTS - 1
        used = off_ref[last] + cnt_ref[last] + ((-cnt_ref[last]) & (MOE_TILE - 1))
        fill_range(used, xs_ref.shape[0] // ROW_CHUNKS - used)

    for _ in range(TOP_K):
        pltpu.make_async_copy(x_ref, xs_ref.at[pl.ds(0, tb * ROW_CHUNKS)], sem).wait()


def moe_dispatch(x_tiled, pos_flat, counts, offsets, n_rows):
    t = x_tiled.shape[0] // ROW_CHUNKS
    tb = min(ROUTE_BLOCK, t)
    grid_spec = pltpu.PrefetchScalarGridSpec(
        num_scalar_prefetch=3,
        grid=(t // tb,),
        in_specs=[pl.BlockSpec((tb * ROW_CHUNKS, LANES), lambda i, *_: (i, 0))],
        out_specs=pl.BlockSpec(memory_space=pl.ANY),
        scratch_shapes=[pltpu.VMEM((ROW_CHUNKS, LANES), F32), pltpu.SemaphoreType.DMA(()),
                        pltpu.SemaphoreType.DMA(())],
    )
    return pl.pallas_call(
        functools.partial(_dispatch_kernel, tb=tb),
        grid_spec=grid_spec,
        out_shape=jax.ShapeDtypeStruct((n_rows * ROW_CHUNKS, LANES), F32),
        compiler_params=_params("arbitrary"),
        name="moe_dispatch",
    )(pos_flat, counts, offsets, x_tiled)


def _moe_ffn_kernel(te_ref, nt_ref, x_ref, wg_ref, wu_ref, wd_ref, o_ref, xb_ref, acc_ref):
    i = pl.program_id(0)
    f = pl.program_id(1)

    @pl.when(i < nt_ref[0])
    def _():
        @pl.when(f == 0)
        def _():
            for j in range(ROW_CHUNKS):
                xb_ref[:, j * LANES:(j + 1) * LANES] = _load_row_tiled(x_ref, j, MOE_TILE).astype(BF16)
            acc_ref[...] = jnp.zeros_like(acc_ref)

        xb = xb_ref[...]
        gate = jnp.dot(xb, wg_ref[0], preferred_element_type=F32)
        up = jnp.dot(xb, wu_ref[0], preferred_element_type=F32)
        h = (_silu(gate) * up).astype(BF16)
        acc_ref[...] += jnp.dot(h, wd_ref[0], preferred_element_type=F32)

        @pl.when(f == pl.num_programs(1) - 1)
        def _():
            _store_row_tiled(o_ref, acc_ref[...])

    @pl.when((i >= nt_ref[0]) & (f == 0))
    def _():
        o_ref[...] = jnp.zeros_like(o_ref)


def moe_ffn(xs, tile_expert, n_tiles_used, wg, wu, wd, tf=1792):
    n_rows, d = xs.shape[0] // ROW_CHUNKS, D_MODEL
    n_tiles = n_rows // MOE_TILE
    tile_spec = lambda index: pl.BlockSpec((MOE_TILE * ROW_CHUNKS, LANES), index)
    fdim = wg.shape[2]
    nf = fdim // tf

    def live(i, nt):
        return jnp.minimum(i, nt[0] - 1)

    def fblk(i, f, nt):
        return jnp.where(i < nt[0], f, nf - 1)

    grid_spec = pltpu.PrefetchScalarGridSpec(
        num_scalar_prefetch=2,
        grid=(n_tiles, nf),
        in_specs=[tile_spec(lambda i, f, te, nt: (live(i, nt), 0)),
                  pl.BlockSpec((1, d, tf), lambda i, f, te, nt: (te[i], 0, fblk(i, f, nt))),
                  pl.BlockSpec((1, d, tf), lambda i, f, te, nt: (te[i], 0, fblk(i, f, nt))),
                  pl.BlockSpec((1, tf, d), lambda i, f, te, nt: (te[i], fblk(i, f, nt), 0))],
        out_specs=tile_spec(lambda i, f, te, nt: (i, 0)),
        scratch_shapes=[pltpu.VMEM((MOE_TILE, d), BF16), pltpu.VMEM((MOE_TILE, d), F32)],
    )
    return pl.pallas_call(
        _moe_ffn_kernel,
        grid_spec=grid_spec,
        out_shape=jax.ShapeDtypeStruct(xs.shape, F32),
        compiler_params=_params("arbitrary", "arbitrary"),
        name="moe_ffn",
    )(tile_expert, n_tiles_used, xs, wg, wu, wd)


def _combine_kernel(pos_ref, x_ref, gate_ref, g_ref, b_ref, ys_ref, o_ref, buf_ref, sem, *, tb):
    base = pl.program_id(0) * (TOP_K * tb)

    def issue(t, _):
        for k in range(TOP_K):
            pltpu.make_async_copy(ys_ref.at[_tile_rows(pos_ref[base + TOP_K * t + k])],
                                  buf_ref.at[k, _tile_rows(t)], sem).start()
        return 0

    lax.fori_loop(0, tb, issue, 0, unroll=DMA_ISSUE_UNROLL)
    for k in range(TOP_K):
        pltpu.make_async_copy(ys_ref.at[pl.ds(0, tb * ROW_CHUNKS)], buf_ref.at[k], sem).wait()
    gates = gate_ref[...]
    y = jnp.concatenate(
        [buf_ref[0, pl.ds(j, tb, stride=ROW_CHUNKS), :] * gates[:, 0:1]
         + buf_ref[1, pl.ds(j, tb, stride=ROW_CHUNKS), :] * gates[:, 1:2] for j in range(ROW_CHUNKS)],
        axis=1)
    o_ref[...] = _layer_norm_rows(DN_ALPHA * x_ref[...] + y, g_ref[...], b_ref[...])


def moe_combine(x, ys, pos_flat, gates, g, b):
    t, d = x.shape
    tb = min(ROUTE_BLOCK, t)
    grid_spec = pltpu.PrefetchScalarGridSpec(
        num_scalar_prefetch=1,
        grid=(t // tb,),
        in_specs=[pl.BlockSpec((tb, d), lambda i, *_: (i, 0)),
                  pl.BlockSpec((tb, LANES), lambda i, *_: (i, 0)),
                  pl.BlockSpec((1, d), lambda i, *_: (0, 0)),
                  pl.BlockSpec((1, d), lambda i, *_: (0, 0)),
                  pl.BlockSpec(memory_space=pl.ANY)],
        out_specs=pl.BlockSpec((tb, d), lambda i, *_: (i, 0)),
        scratch_shapes=[pltpu.VMEM((TOP_K, tb * ROW_CHUNKS, LANES), F32), pltpu.SemaphoreType.DMA(())],
    )
    return pl.pallas_call(
        functools.partial(_combine_kernel, tb=tb),
        grid_spec=grid_spec,
        out_shape=jax.ShapeDtypeStruct((t, d), F32),
        compiler_params=_params("arbitrary"),
        name="moe_combine",
    )(pos_flat, x, gates, g.reshape(1, d), b.reshape(1, d), ys)


def moe_deepnorm(x, x_tiled, w_router, wg, wu, wd, g, b):
    t = x.shape[0]
    idx, gates = moe_router(x, w_router)
    rank, counts_f = moe_rank(idx)
    counts = counts_f[0, :N_EXPERTS].astype(jnp.int32)
    padded = (counts + MOE_TILE - 1) // MOE_TILE * MOE_TILE
    ends = jnp.cumsum(padded)
    offsets = ends - padded
    n_tiles = (TOP_K * t) // MOE_TILE + N_EXPERTS
    n_used = (ends[-1] // MOE_TILE).astype(jnp.int32)
    tile_start = jnp.arange(n_tiles, dtype=jnp.int32) * MOE_TILE
    tile_start = jnp.minimum(tile_start, ends[-1] - MOE_TILE)
    tile_expert = jnp.sum(tile_start[:, None] >= ends[None, :], axis=1).astype(jnp.int32)
    pos = moe_positions(idx, rank, offsets)
    pos_flat = pos[:, :TOP_K].reshape(-1)
    xs = moe_dispatch(x_tiled, pos_flat, counts, offsets, n_tiles * MOE_TILE)
    ys = moe_ffn(xs, tile_expert, n_used.reshape(1), wg.astype(BF16), wu.astype(BF16), wd.astype(BF16))
    return moe_combine(x, ys, pos_flat, gates, g, b)


def kernel(x, l0_ssd_w_in, l0_ssd_conv_w, l0_ssd_conv_b, l0_ssd_dt_bias, l0_ssd_a_log, l0_ssd_d_skip, l0_ssd_norm_w, l0_ssd_w_out, l0_ln_mix_g, l0_ln_mix_b, l0_ffn_w_gate, l0_ffn_w_up, l0_ffn_w_down, l0_ln_ffn_g, l0_ln_ffn_b, l1_sb_w_qkv, l1_sb_w_out, l1_ln_mix_g, l1_ln_mix_b, l1_moe_w_router, l1_moe_w_gate, l1_moe_w_up, l1_moe_w_down, l1_ln_ffn_g, l1_ln_ffn_b, l2_fox_w_qkvf, l2_fox_b_f, l2_fox_w_out, l2_ln_mix_g, l2_ln_mix_b, l2_ffn_w_gate, l2_ffn_w_up, l2_ffn_w_down, l2_ln_ffn_g, l2_ln_ffn_b, l3_ssd_w_in, l3_ssd_conv_w, l3_ssd_conv_b, l3_ssd_dt_bias, l3_ssd_a_log, l3_ssd_d_skip, l3_ssd_norm_w, l3_ssd_w_out, l3_ln_mix_g, l3_ln_mix_b, l3_moe_w_router, l3_moe_w_gate, l3_moe_w_up, l3_moe_w_down, l3_ln_ffn_g, l3_ln_ffn_b):
    batch, seq, d = x.shape
    h = x.reshape(batch * seq, d)
    bf = lambda w: w.astype(BF16)
    h = ssd_mixer_deepnorm(h, l0_ssd_w_in, l0_ssd_conv_w, l0_ssd_conv_b, l0_ssd_dt_bias, l0_ssd_a_log,
                           l0_ssd_d_skip, l0_ssd_norm_w, l0_ssd_w_out, l0_ln_mix_g, l0_ln_mix_b, batch, seq)
    h = ffn_deepnorm(h, bf(l0_ffn_w_gate), bf(l0_ffn_w_up), bf(l0_ffn_w_down), l0_ln_ffn_g, l0_ln_ffn_b)
    h, h_tiled = sb_mixer_deepnorm(h, l1_sb_w_qkv, l1_sb_w_out, l1_ln_mix_g, l1_ln_mix_b, batch, seq,
                                   also_row_tiled=True)
    h = moe_deepnorm(h, h_tiled, l1_moe_w_router, l1_moe_w_gate, l1_moe_w_up, l1_moe_w_down, l1_ln_ffn_g, l1_ln_ffn_b)
    h = fox_mixer_deepnorm(h, l2_fox_w_qkvf, l2_fox_b_f, l2_fox_w_out, l2_ln_mix_g, l2_ln_mix_b, batch, seq)
    h = ffn_deepnorm(h, bf(l2_ffn_w_gate), bf(l2_ffn_w_up), bf(l2_ffn_w_down), l2_ln_ffn_g, l2_ln_ffn_b)
    h, h_tiled = ssd_mixer_deepnorm(h, l3_ssd_w_in, l3_ssd_conv_w, l3_ssd_conv_b, l3_ssd_dt_bias, l3_ssd_a_log,
                                    l3_ssd_d_skip, l3_ssd_norm_w, l3_ssd_w_out, l3_ln_mix_g, l3_ln_mix_b,
                                    batch, seq, also_row_tiled=True)
    h = moe_deepnorm(h, h_tiled, l3_moe_w_router, l3_moe_w_gate, l3_moe_w_up, l3_moe_w_down, l3_ln_ffn_g, l3_ln_ffn_b)
    return h.reshape(batch, seq, d)
```

```python
import functools
import math

import jax
import jax.numpy as jnp
from jax import lax
from jax.experimental import pallas as pl
from jax.experimental.pallas import tpu as pltpu

F32 = jnp.float32
BF16 = jnp.bfloat16
HIGHEST = lax.Precision.HIGHEST

LANES = 128
SUBLANES = 8
VMEM_LIMIT_BYTES = 56 * 1024 * 1024

D_MODEL = 1024
DEPTH = 4
SSD_D_INNER = 2048
SSD_HEAD_DIM = 64
SSD_HEADS = 32
SSD_GROUPS = 4
SSD_STATE = 128
SSD_CONV = 4
SSD_GROUP_DIM = SSD_D_INNER // SSD_GROUPS
SSD_BC_DIM = 2 * SSD_GROUPS * SSD_STATE
ATT_HEAD_DIM = 64
ATT_HEADS = 16
ATT_DIM = 1024
N_EXPERTS = 8
TOP_K = 2
DN_ALPHA = (2.0 * DEPTH) ** 0.25
LN_EPS = 1e-5
RMS_EPS = 1e-5
NEG_BIG = -1e30


def _params(*semantics):
    return pltpu.CompilerParams(dimension_semantics=semantics, vmem_limit_bytes=VMEM_LIMIT_BYTES)


def _layer_norm_rows(h, g, b):
    mu = jnp.mean(h, axis=-1, keepdims=True)
    d = h - mu
    var = jnp.mean(d * d, axis=-1, keepdims=True)
    return d * lax.rsqrt(var + LN_EPS) * g + b


def _silu(x):
    half = 0.5 * x
    return half * (1.0 + jnp.tanh(half))


def _mm_kernel(x_ref, w_ref, o_ref):
    o_ref[...] = jnp.dot(x_ref[...].astype(BF16), w_ref[...],
                         preferred_element_type=F32).astype(o_ref.dtype)


def matmul(x, w, out_dtype, tm=1024, tn=512):
    m, k = x.shape
    tm = min(tm, m)
    n = w.shape[1]
    return pl.pallas_call(
        _mm_kernel,
        grid=(m // tm, n // tn),
        in_specs=[pl.BlockSpec((tm, k), lambda i, j: (i, 0)),
                  pl.BlockSpec((k, tn), lambda i, j: (0, j))],
        out_specs=pl.BlockSpec((tm, tn), lambda i, j: (i, j)),
        out_shape=jax.ShapeDtypeStruct((m, n), out_dtype),
        compiler_params=_params("parallel", "parallel"),
        name="matmul",
    )(x, w)


def _mm_f32_kernel(x_ref, w_ref, o_ref):
    o_ref[...] = jnp.dot(x_ref[...], w_ref[...], preferred_element_type=F32, precision=HIGHEST)


def matmul_f32(x, w, tm=1024):
    m, k = x.shape
    tm = min(tm, m)
    n = w.shape[1]
    return pl.pallas_call(
        _mm_f32_kernel,
        grid=(m // tm,),
        in_specs=[pl.BlockSpec((tm, k), lambda i: (i, 0)),
                  pl.BlockSpec((k, n), lambda i: (0, 0))],
        out_specs=pl.BlockSpec((tm, n), lambda i: (i, 0)),
        out_shape=jax.ShapeDtypeStruct((m, n), F32),
        compiler_params=_params("parallel"),
        name="matmul_f32",
    )(x, w)


ROW_CHUNKS = D_MODEL // LANES


def _store_row_tiled(ref, value):
    n = value.shape[0]
    for j in range(ROW_CHUNKS):
        ref[pl.ds(j, n, stride=ROW_CHUNKS), :] = value[:, j * LANES:(j + 1) * LANES]


def _load_row_tiled(ref, j, n):
    return ref[pl.ds(j, n, stride=ROW_CHUNKS), :]


def _mm_ln_kernel(x_ref, w_ref, r_ref, g_ref, b_ref, o_ref, *maybe_tiled_ref):
    y = jnp.dot(x_ref[...], w_ref[...], preferred_element_type=F32)
    o = _layer_norm_rows(DN_ALPHA * r_ref[...] + y, g_ref[...], b_ref[...])
    o_ref[...] = o
    for t_ref in maybe_tiled_ref:
        _store_row_tiled(t_ref, o)


def matmul_deepnorm(x, w, resid, g, b, tm=512, also_row_tiled=False):
    m, k = x.shape
    tm = min(tm, m)
    d = w.shape[1]
    out_specs = [pl.BlockSpec((tm, d), lambda i: (i, 0))]
    out_shape = [jax.ShapeDtypeStruct((m, d), F32)]
    if also_row_tiled:
        out_specs.append(pl.BlockSpec((tm * ROW_CHUNKS, LANES), lambda i: (i, 0)))
        out_shape.append(jax.ShapeDtypeStruct((m * ROW_CHUNKS, LANES), F32))
    out = pl.pallas_call(
        _mm_ln_kernel,
        grid=(m // tm,),
        in_specs=[pl.BlockSpec((tm, k), lambda i: (i, 0)),
                  pl.BlockSpec((k, d), lambda i: (0, 0)),
                  pl.BlockSpec((tm, d), lambda i: (i, 0)),
                  pl.BlockSpec((1, d), lambda i: (0, 0)),
                  pl.BlockSpec((1, d), lambda i: (0, 0))],
        out_specs=out_specs,
        out_shape=out_shape,
        compiler_params=_params("parallel"),
        name="matmul_deepnorm",
    )(x, w, resid, g.reshape(1, d), b.reshape(1, d))
    return tuple(out) if also_row_tiled else out[0]


def _ffn_kernel(x_ref, wg_ref, wu_ref, wd_ref, g_ref, b_ref, o_ref):
    x = x_ref[...]
    xb = x.astype(BF16)
    gate = jnp.dot(xb, wg_ref[...], preferred_element_type=F32)
    up = jnp.dot(xb, wu_ref[...], preferred_element_type=F32)
    h = (_silu(gate) * up).astype(BF16)
    y = jnp.dot(h, wd_ref[...], preferred_element_type=F32)
    o_ref[...] = _layer_norm_rows(DN_ALPHA * x + y, g_ref[...], b_ref[...])


def ffn_deepnorm(x, wg, wu, wd, g, b, tm=512):
    m, d = x.shape
    tm = min(tm, m)
    resident = lambda a: pl.BlockSpec(a.shape, lambda i: (0, 0), pipeline_mode=pl.Buffered(1))
    return pl.pallas_call(
        _ffn_kernel,
        grid=(m // tm,),
        in_specs=[pl.BlockSpec((tm, d), lambda i: (i, 0)),
                  resident(wg), resident(wu), resident(wd),
                  pl.BlockSpec((1, d), lambda i: (0, 0)),
                  pl.BlockSpec((1, d), lambda i: (0, 0))],
        out_specs=pl.BlockSpec((tm, d), lambda i: (i, 0)),
        out_shape=jax.ShapeDtypeStruct((m, d), F32),
        compiler_params=_params("parallel"),
        name="ffn_deepnorm",
    )(x, wg, wu, wd, g.reshape(1, d), b.reshape(1, d))


def _split_bf16(v, pieces):
    out = []
    r = v
    for _ in range(pieces - 1):
        p = r.astype(BF16)
        out.append(p)
        r = r - p.astype(F32)
    out.append(r.astype(BF16))
    return out


def _expand(v, e_ref, pieces):
    stacked = jnp.concatenate(_split_bf16(v, pieces), axis=1)
    return jnp.dot(stacked, e_ref[...], preferred_element_type=F32)


def _tril_f32(n, strict=False):
    r = lax.broadcasted_iota(jnp.int32, (n, n), 0)
    c = lax.broadcasted_iota(jnp.int32, (n, n), 1)
    return ((r > c) if strict else (r >= c)).astype(F32)


def _ssd_kernel(z_ref, xs_ref, bc_ref, dt_ref,
                dtb_ref, alog_ref, dskip_ref, normw_ref, e64_ref, e128_ref, o_ref,
                state_ref, xdt_ref, acol_ref, arow_ref, cb_ref, ydiag_ref,
                *, chunk):
    L = chunk
    c = pl.program_id(1)

    @pl.when(c == 0)
    def _():
        state_ref[...] = jnp.zeros_like(state_ref)

    xs = xs_ref[...]
    bcv = bc_ref[...]

    dt = jax.nn.softplus(dt_ref[...] + dtb_ref[...])
    da = dt * (-jnp.exp(alog_ref[...]))
    a_cs = jnp.dot(_tril_f32(L), da, preferred_element_type=F32, precision=HIGHEST)
    ea = jnp.exp(a_cs)
    dte = jnp.exp(a_cs[L - 1:L, :] - a_cs)

    dt_x = _expand(dt, e64_ref, 2)
    ea_x = _expand(ea, e64_ref, 2)
    dte_x = _expand(dte, e64_ref, 2)
    acol_ref[...] = _expand(a_cs, e128_ref, 3)
    a_t = a_cs.T
    for h in range(SSD_HEADS):
        arow_ref[h] = jnp.broadcast_to(a_t[h:h + 1, :], (SUBLANES, L))

    xdt = xs * dt_x
    xdt_ref[...] = xdt.astype(BF16)
    xdte = (xdt * dte_x).astype(BF16)

    y_off = []
    for g in range(SSD_GROUPS):
        bm = bcv[:, g * SSD_STATE:(g + 1) * SSD_STATE]
        cm = bcv[:, (SSD_GROUPS + g) * SSD_STATE:(SSD_GROUPS + g + 1) * SSD_STATE].astype(BF16)
        cb_ref[g] = lax.dot_general(cm, bm.astype(BF16), (((1,), (1,)), ((), ())),
                                    preferred_element_type=F32)
        gs = slice(g * SSD_GROUP_DIM, (g + 1) * SSD_GROUP_DIM)
        st = state_ref[g]
        y_off.append(jnp.dot(cm, st.astype(BF16), preferred_element_type=F32) * ea_x[:, gs])
        state_ref[g] = st * ea_x[L - 1:L, gs] + jnp.dot(
            bm.T.astype(BF16), xdte[:, gs], preferred_element_type=F32)

    row = lax.broadcasted_iota(jnp.int32, (L, L), 0)
    col = lax.broadcasted_iota(jnp.int32, (L, L), 1)
    causal = row >= col
    head0 = lax.broadcasted_iota(jnp.int32, (L, LANES), 1) < SSD_HEAD_DIM

    heads_per_group = SSD_HEADS // SSD_GROUPS

    def group_body(g, carry):
        cbg = cb_ref[g]
        scores = []
        for j in range(heads_per_group):
            h = g * heads_per_group + j
            a_l = acol_ref[:, pl.ds(pl.multiple_of(h * LANES, LANES), LANES)]
            a_l = jnp.concatenate([a_l] * (L // LANES), axis=1)
            a_s = arow_ref[h][0:1, :]
            scores.append((cbg * jnp.exp(jnp.where(causal, a_l - a_s, NEG_BIG))).astype(BF16))
        for pp in range(heads_per_group // 2):
            lanes = pl.ds(pl.multiple_of((g * (heads_per_group // 2) + pp) * LANES, LANES), LANES)
            xpair = xdt_ref[:, lanes]
            ys = [jnp.dot(scores[2 * pp + j], xpair, preferred_element_type=F32) for j in range(2)]
            ydiag_ref[:, lanes] = jnp.where(head0, ys[0], ys[1])
        return carry

    lax.fori_loop(0, SSD_GROUPS, group_body, 0)

    y = ydiag_ref[...] + jnp.concatenate(y_off, axis=1) + dskip_ref[...] * xs
    y = y * _silu(z_ref[...])
    parts = []
    for g in range(SSD_GROUPS):
        yg = y[:, g * SSD_GROUP_DIM:(g + 1) * SSD_GROUP_DIM]
        parts.append(yg * lax.rsqrt(jnp.mean(yg * yg, axis=-1, keepdims=True) + RMS_EPS))
    o_ref[...] = (jnp.concatenate(parts, axis=1) * normw_ref[...]).astype(o_ref.dtype)


def _expansion_matrix(width, pieces):
    h = jnp.arange(LANES)[:, None]
    lane = jnp.arange(SSD_HEADS * width)[None, :]
    e = (lane // width == h).astype(BF16)
    return jnp.concatenate([e] * pieces, axis=0)


def _mm_conv_kernel(x_ref, halo_ref, w_ref, cw_ref, cb_ref, o_ref, ext_ref, *, tiles_per_seq):
    tm = x_ref.shape[0]
    w = w_ref[...]
    halo = jnp.dot(halo_ref[...].astype(BF16), w, preferred_element_type=F32)
    first = pl.program_id(0) % tiles_per_seq == 0
    ext_ref[0:SUBLANES, :] = jnp.where(first, 0.0, halo)
    ext_ref[SUBLANES:SUBLANES + tm, :] = jnp.dot(x_ref[...].astype(BF16), w, preferred_element_type=F32)
    acc = cb_ref[...]
    for k in range(SSD_CONV):
        start = SUBLANES - (SSD_CONV - 1) + k
        acc = acc + cw_ref[k:k + 1, :] * ext_ref[start:start + tm, :]
    o_ref[...] = _silu(acc)


def matmul_conv_silu(x, w, conv_w, conv_b, seq, tm=1024, tn=512):
    m, k = x.shape
    tm = min(tm, seq)
    n = w.shape[1]
    halo_blocks = tm // SUBLANES
    return pl.pallas_call(
        functools.partial(_mm_conv_kernel, tiles_per_seq=seq // tm),
        grid=(m // tm, n // tn),
        in_specs=[pl.BlockSpec((tm, k), lambda i, j: (i, 0)),
                  pl.BlockSpec((SUBLANES, k), lambda i, j: (jnp.maximum(i * halo_blocks - 1, 0), 0)),
                  pl.BlockSpec((k, tn), lambda i, j: (0, j)),
                  pl.BlockSpec((SSD_CONV, tn), lambda i, j: (0, j)),
                  pl.BlockSpec((1, tn), lambda i, j: (0, j))],
        out_specs=pl.BlockSpec((tm, tn), lambda i, j: (i, j)),
        out_shape=jax.ShapeDtypeStruct((m, n), F32),
        scratch_shapes=[pltpu.VMEM((tm + SUBLANES, tn), F32)],
        compiler_params=_params("parallel", "parallel"),
        name="matmul_conv_silu",
    )(x, x, w, conv_w, conv_b.reshape(1, n))


def ssd_core(z, xbc, dt_raw, dt_bias, a_log, d_skip, norm_w, batch, seq, chunk=128):
    t = z.shape[0]
    nc = seq // chunk
    pad = LANES - SSD_HEADS
    row = lambda v: v.reshape(1, -1).astype(F32)
    args = (
        z, xbc, xbc, dt_raw,
        row(jnp.pad(dt_bias, (0, pad))), row(jnp.pad(a_log, (0, pad))),
        row(jnp.repeat(d_skip, SSD_HEAD_DIM)), row(norm_w),
        _expansion_matrix(SSD_HEAD_DIM, 2), _expansion_matrix(LANES, 3),
    )
    blk = lambda b, c: (b * nc + c, 0)
    const = lambda b, c: (0, 0)
    full = lambda a: pl.BlockSpec(a.shape, const)
    in_specs = [
        pl.BlockSpec((chunk, SSD_D_INNER), blk),
        pl.BlockSpec((chunk, SSD_D_INNER), blk),
        pl.BlockSpec((chunk, SSD_BC_DIM), lambda b, c: (b * nc + c, 2)),
        pl.BlockSpec((chunk, LANES), blk),
    ] + [full(a) for a in args[4:]]
    return pl.pallas_call(
        functools.partial(_ssd_kernel, chunk=chunk),
        grid=(batch, nc),
        in_specs=in_specs,
        out_specs=pl.BlockSpec((chunk, SSD_D_INNER), blk),
        out_shape=jax.ShapeDtypeStruct((t, SSD_D_INNER), BF16),
        scratch_shapes=[
            pltpu.VMEM((SSD_GROUPS, SSD_STATE, SSD_GROUP_DIM), F32),
            pltpu.VMEM((chunk, SSD_D_INNER), BF16),
            pltpu.VMEM((chunk, SSD_HEADS * LANES), F32),
            pltpu.VMEM((SSD_HEADS, SUBLANES, chunk), F32),
            pltpu.VMEM((SSD_GROUPS, chunk, chunk), F32),
            pltpu.VMEM((chunk, SSD_D_INNER), F32),
        ],
        compiler_params=_params("parallel", "arbitrary"),
        name="ssd_core",
    )(*args)


def ssd_mixer_deepnorm(x, w_in, conv_w, conv_b, dt_bias, a_log, d_skip, norm_w, w_out, g, b,
                       batch, seq, also_row_tiled=False):
    n_zx = SSD_D_INNER + SSD_D_INNER + SSD_BC_DIM
    z = matmul(x, w_in[:, :SSD_D_INNER].astype(BF16), F32)
    xbc = matmul_conv_silu(x, w_in[:, SSD_D_INNER:n_zx].astype(BF16), conv_w, conv_b, seq)
    w_dt = jnp.pad(w_in[:, n_zx:], ((0, 0), (0, LANES - SSD_HEADS)))
    dt_raw = matmul_f32(x, w_dt)
    y = ssd_core(z, xbc, dt_raw, dt_bias, a_log, d_skip, norm_w, batch, seq)
    return matmul_deepnorm(y, w_out.astype(BF16), x, g, b, also_row_tiled=also_row_tiled)


ATT_BLOCK = 128
ATT_Q = 256
ATT_PAIRS = 4
ATT_STEP_LANES = ATT_PAIRS * LANES
ATT_SCALE = ATT_HEAD_DIM ** -0.5
LOG2E = 1.4426950408889634


def _split_heads(q_ref, rows, head0):
    qs = []
    for p in range(ATT_PAIRS):
        q2 = q_ref[rows, p * LANES:(p + 1) * LANES]
        zero = jnp.zeros_like(q2)
        qs += [jnp.where(head0, q2, zero), jnp.where(head0, zero, q2)]
    return qs


def _sb_kernel(q_ref, k_ref, v_ref, u_ref, o_ref, *scratch, seq):
    n_heads = 2 * ATT_PAIRS
    later_ref, acc_ref, z_ref = scratch[:n_heads], scratch[n_heads:2 * n_heads], scratch[2 * n_heads:]
    nq = seq // ATT_Q
    ndiag = ATT_Q // ATT_BLOCK
    head0 = lax.broadcasted_iota(jnp.int32, (ATT_Q, LANES), 1) < ATT_HEAD_DIM
    row = lax.broadcasted_iota(jnp.int32, (ATT_Q, ATT_BLOCK), 0)
    col = lax.broadcasted_iota(jnp.int32, (ATT_Q, ATT_BLOCK), 1)

    def q_body(i, _):
        q0 = pl.multiple_of(i * ATT_Q, ATT_Q)
        rows = pl.ds(q0, ATT_Q)
        qs = _split_heads(q_ref, rows, head0)
        for n in range(2 * ATT_PAIRS):
            later_ref[n][...] = jnp.zeros((ATT_Q, LANES), F32)
            acc_ref[n][...] = jnp.zeros((ATT_Q, LANES), F32)

        heads = range(2 * ATT_PAIRS)

        def key_rows(j):
            return pl.ds(pl.multiple_of(j * ATT_BLOCK, ATT_BLOCK), ATT_BLOCK)

        def scores(j):
            k2 = [k_ref[key_rows(j), p * LANES:(p + 1) * LANES] for p in range(ATT_PAIRS)]
            return [lax.dot_general(qs[n], k2[n // 2], (((1,), (1,)), ((), ())),
                                    preferred_element_type=F32) * (ATT_SCALE * LOG2E) for n in heads]

        def consume(z2, j, strict):
            keys = key_rows(j)
            v2 = [v_ref[keys, p * LANES:(p + 1) * LANES] for p in range(ATT_PAIRS)]
            log_beta, sums = [], []
            for n in heads:
                lb = jnp.minimum(z2[n], 0.0) - jnp.log2(1.0 + jnp.exp2(-jnp.abs(z2[n])))
                log_keep = lb - z2[n]
                if strict is not None:
                    log_keep = jnp.where(strict, log_keep, 0.0)
                hi = log_keep.astype(BF16)
                lo = (log_keep - hi.astype(F32)).astype(BF16)
                log_beta.append(lb)
                sums.append(jnp.dot(jnp.concatenate([hi, lo], axis=1), u_ref[...],
                                    preferred_element_type=F32))
            for n in heads:
                w = jnp.exp2(log_beta[n] + sums[n][:, :ATT_BLOCK] + later_ref[n][...])
                if strict is not None:
                    w = jnp.where(strict, w, 0.0)
                acc_ref[n][...] += jnp.dot(w.astype(BF16), v2[n // 2], preferred_element_type=F32)
                later_ref[n][...] += sums[n][:, ATT_BLOCK:]

        for d in reversed(range(ndiag)):
            consume(scores(i * ndiag + d), i * ndiag + d, d * ATT_BLOCK + col < row)

        n_off = i * ndiag
        for n, zn in enumerate(scores(jnp.maximum(n_off - 1, 0))):
            z_ref[n][...] = zn

        def off_diagonal(jj, c):
            z2 = [z_ref[n][...] for n in heads]
            ahead = scores(jnp.maximum(n_off - 2 - jj, 0))
            consume(z2, n_off - 1 - jj, None)
            for n in heads:
                z_ref[n][...] = ahead[n]
            return c

        lax.fori_loop(0, n_off, off_diagonal, 0)
        for p in range(ATT_PAIRS):
            o_ref[rows, p * LANES:(p + 1) * LANES] = jnp.where(
                head0, acc_ref[2 * p][...], acc_ref[2 * p + 1][...]).astype(o_ref.dtype)
        return 0

    lax.fori_loop(0, nq, q_body, 0)


def _suffix_sum_matrix():
    j = jnp.arange(2 * ATT_BLOCK)[:, None] % ATT_BLOCK
    s = jnp.arange(2 * ATT_BLOCK)[None, :]
    return jnp.where(s < ATT_BLOCK, j > s, True).astype(BF16)


def _attention_specs(seq):
    n_steps = ATT_HEADS // 2 // ATT_PAIRS
    blk = lambda off: pl.BlockSpec((seq, ATT_STEP_LANES), lambda b, p: (b, off + p))
    specs = [blk(0), blk(n_steps), blk(2 * n_steps)]
    return n_steps, specs, pl.BlockSpec((seq, ATT_STEP_LANES), lambda b, p: (b, p))


def sb_attention(qkv, batch, seq):
    n_steps, in_specs, out_spec = _attention_specs(seq)
    u = _suffix_sum_matrix()
    return pl.pallas_call(
        functools.partial(_sb_kernel, seq=seq),
        grid=(batch, n_steps),
        in_specs=in_specs + [pl.BlockSpec(u.shape, lambda b, p: (0, 0))],
        out_specs=out_spec,
        out_shape=jax.ShapeDtypeStruct((batch * seq, ATT_DIM), BF16),
        scratch_shapes=[pltpu.VMEM((ATT_Q, LANES), F32)] * (6 * ATT_PAIRS),
        compiler_params=_params("parallel", "parallel"),
        name="sb_attention",
    )(qkv, qkv, qkv, u)


def sb_mixer_deepnorm(x, w_qkv, w_out, g, b, batch, seq, also_row_tiled=False):
    qkv = matmul(x, w_qkv.astype(BF16), BF16)
    o = sb_attention(qkv, batch, seq)
    return matmul_deepnorm(o, w_out.astype(BF16), x, g, b, also_row_tiled=also_row_tiled)


CUMSUM_BLOCK = 256


def _fox_decay_kernel(f_ref, bf_ref, ccol_ref, crow_ref, *, seq):
    tri = _tril_f32(CUMSUM_BLOCK)
    carry = jnp.zeros((1, LANES), F32)
    for blk in range(seq // CUMSUM_BLOCK):
        rows = slice(blk * CUMSUM_BLOCK, (blk + 1) * CUMSUM_BLOCK)
        log_f = jax.nn.log_sigmoid(f_ref[rows, :] + bf_ref[...])
        c = jnp.dot(tri, log_f, preferred_element_type=F32, precision=HIGHEST) + carry
        carry = c[CUMSUM_BLOCK - 1:CUMSUM_BLOCK, :]
        ccol_ref[rows, :] = c
        c_t = c.T
        for p in range(ATT_HEADS // 2):
            crow_ref[0, p, :, rows] = c_t[2 * p:2 * p + 2, :]


def fox_decay(f_raw, b_f, batch, seq):
    return pl.pallas_call(
        functools.partial(_fox_decay_kernel, seq=seq),
        grid=(batch,),
        in_specs=[pl.BlockSpec((seq, LANES), lambda b: (b, 0)),
                  pl.BlockSpec((1, LANES), lambda b: (0, 0))],
        out_specs=[pl.BlockSpec((seq, LANES), lambda b: (b, 0)),
                   pl.BlockSpec((1, ATT_HEADS // 2, 2, seq), lambda b: (b, 0, 0, 0))],
        out_shape=[jax.ShapeDtypeStruct((batch * seq, LANES), F32),
                   jax.ShapeDtypeStruct((batch, ATT_HEADS // 2, 2, seq), F32)],
        compiler_params=_params("parallel"),
        name="fox_decay",
    )(f_raw, jnp.pad(b_f, (0, LANES - ATT_HEADS)).reshape(1, LANES))


def _fox_kernel(q_ref, k_ref, v_ref, ccol_ref, crow_ref, o_ref, crep_ref, vt_ref, *scratch, seq):
    n_heads = 2 * ATT_PAIRS
    m_ref, acc_ref, s_ref = scratch[:n_heads], scratch[n_heads:2 * n_heads], scratch[2 * n_heads:]
    nq = seq // ATT_Q
    ndiag = ATT_Q // ATT_BLOCK
    step = pl.program_id(1)
    heads = range(n_heads)
    head0 = lax.broadcasted_iota(jnp.int32, (ATT_Q, LANES), 1) < ATT_HEAD_DIM
    head0_rows = lax.broadcasted_iota(jnp.int32, (LANES, ATT_BLOCK), 0) < ATT_HEAD_DIM
    key = lax.broadcasted_iota(jnp.int32, (ATT_BLOCK, ATT_Q), 0)
    qry = lax.broadcasted_iota(jnp.int32, (ATT_BLOCK, ATT_Q), 1)

    pieces = jnp.concatenate(_split_bf16(ccol_ref[...], 3), axis=1)
    sel_row = lax.broadcasted_iota(jnp.int32, (3 * LANES, LANES), 0) & (LANES - 1)
    for n in heads:
        sel = jnp.where(sel_row == 2 * ATT_PAIRS * step + n, 1.0, 0.0).astype(BF16)
        crep_ref[n] = jnp.dot(pieces, sel, preferred_element_type=F32) * LOG2E
    for p in range(ATT_PAIRS):
        for blk in range(seq // ATT_BLOCK):
            rows = slice(blk * ATT_BLOCK, (blk + 1) * ATT_BLOCK)
            vt_ref[p, :, rows] = v_ref[rows, p * LANES:(p + 1) * LANES].astype(F32).T.astype(BF16)

    def q_body(i, _):
        q0 = pl.multiple_of(i * ATT_Q, ATT_Q)
        rows = pl.ds(q0, ATT_Q)
        qs = _split_heads(q_ref, rows, head0)
        c_q = [crow_ref[0, n // 2, n % 2:n % 2 + 1, rows] * LOG2E for n in heads]
        for n in heads:
            m_ref[n][...] = jnp.full((1, ATT_Q), NEG_BIG, F32)
            acc_ref[n][...] = jnp.zeros((LANES, ATT_Q), F32)

        def key_rows(j):
            return pl.ds(pl.multiple_of(j * ATT_BLOCK, ATT_BLOCK), ATT_BLOCK)

        def scores(j):
            k2 = [k_ref[key_rows(j), p * LANES:(p + 1) * LANES] for p in range(ATT_PAIRS)]
            return [lax.dot_general(k2[n // 2], qs[n], (((1,), (1,)), ((), ())),
                                    preferred_element_type=F32) * (ATT_SCALE * LOG2E) for n in heads]

        def consume(s, j, causal):
            keys = key_rows(j)
            vt = [vt_ref[p, :, keys] for p in range(ATT_PAIRS)]
            for n in heads:
                c_k = crep_ref[n, keys, :]
                sn = s[n] + (c_q[n] - jnp.concatenate([c_k] * (ATT_Q // LANES), axis=1))
                if causal is not None:
                    sn = jnp.where(causal, sn, NEG_BIG)
                m_old = m_ref[n][...]
                m_new = jnp.maximum(m_old, jnp.max(sn, axis=0, keepdims=True))
                prob = jnp.exp2(sn - m_new).astype(BF16)
                one = jnp.ones_like(vt[n // 2])
                v1t = jnp.where(head0_rows, vt[n // 2], one) if n % 2 == 0 else jnp.where(
                    head0_rows, one, vt[n // 2])
                acc_ref[n][...] = jnp.exp2(m_old - m_new) * acc_ref[n][...] + jnp.dot(
                    v1t, prob, preferred_element_type=F32)
                m_ref[n][...] = m_new

        for d in range(ndiag):
            consume(scores(i * ndiag + d), i * ndiag + d, d * ATT_BLOCK + key <= qry)

        n_off = i * ndiag
        for n, sn in enumerate(scores(0)):
            s_ref[n][...] = sn

        def off_diagonal(jj, c):
            s = [s_ref[n][...] for n in heads]
            ahead = scores(jnp.minimum(jj + 1, n_off - 1))
            consume(s, jj, None)
            for n in heads:
                s_ref[n][...] = ahead[n]
            return c

        lax.fori_loop(0, n_off, off_diagonal, 0)
        for p in range(ATT_PAIRS):
            a0, a1 = acc_ref[2 * p][...], acc_ref[2 * p + 1][...]
            o_t = jnp.concatenate([a0[:ATT_HEAD_DIM] / a0[ATT_HEAD_DIM:],
                                   a1[ATT_HEAD_DIM:] / a1[:ATT_HEAD_DIM]], axis=0)
            o_ref[rows, p * LANES:(p + 1) * LANES] = o_t.T.astype(o_ref.dtype)
        return 0

    lax.fori_loop(0, nq, q_body, 0)


def fox_attention(qkv, ccol, crow, batch, seq):
    n_steps, in_specs, out_spec = _attention_specs(seq)
    in_specs += [pl.BlockSpec((seq, LANES), lambda b, p: (b, 0)),
                 pl.BlockSpec((1, ATT_PAIRS, 2, seq), lambda b, p: (b, p, 0, 0))]
    return pl.pallas_call(
        functools.partial(_fox_kernel, seq=seq),
        grid=(batch, n_steps),
        in_specs=in_specs,
        out_specs=out_spec,
        out_shape=jax.ShapeDtypeStruct((batch * seq, ATT_DIM), BF16),
        scratch_shapes=([pltpu.VMEM((2 * ATT_PAIRS, seq, LANES), F32),
                         pltpu.VMEM((ATT_PAIRS, LANES, seq), BF16)]
                        + [pltpu.VMEM((1, ATT_Q), F32)] * (2 * ATT_PAIRS)
                        + [pltpu.VMEM((LANES, ATT_Q), F32)] * (2 * ATT_PAIRS)
                        + [pltpu.VMEM((ATT_BLOCK, ATT_Q), F32)] * (2 * ATT_PAIRS)),
        compiler_params=_params("parallel", "parallel"),
        name="fox_attention",
    )(qkv, qkv, qkv, ccol, crow)


def fox_mixer_deepnorm(x, w_qkvf, b_f, w_out, g, b, batch, seq):
    qkv = matmul(x, w_qkvf[:, :3 * ATT_DIM].astype(BF16), BF16)
    w_f = jnp.pad(w_qkvf[:, 3 * ATT_DIM:], ((0, 0), (0, LANES - ATT_HEADS)))
    ccol, crow = fox_decay(matmul_f32(x, w_f), b_f, batch, seq)
    o = fox_attention(qkv, ccol, crow, batch, seq)
    return matmul_deepnorm(o, w_out.astype(BF16), x, g, b)


MOE_TILE = 512
ROUTE_BLOCK = 512


def _router_kernel(x_ref, w_ref, idx_ref, gate_ref):
    logits = jnp.dot(x_ref[...], w_ref[...], preferred_element_type=F32, precision=HIGHEST)
    lane = lax.broadcasted_iota(jnp.int32, logits.shape, 1)
    logits = jnp.where(lane < N_EXPERTS, logits, NEG_BIG)
    m1 = jnp.max(logits, axis=-1, keepdims=True)
    i1 = jnp.min(jnp.where(logits == m1, lane, LANES), axis=-1, keepdims=True)
    rest = jnp.where(lane == i1, NEG_BIG, logits)
    m2 = jnp.max(rest, axis=-1, keepdims=True)
    i2 = jnp.min(jnp.where(rest == m2, lane, LANES), axis=-1, keepdims=True)
    e2 = jnp.exp(m2 - m1)
    denom = 1.0 + e2
    idx_ref[...] = jnp.where(lane == 0, i1, jnp.where(lane == 1, i2, 0))
    gate_ref[...] = jnp.where(lane == 0, 1.0 / denom, jnp.where(lane == 1, e2 / denom, 0.0))


def moe_router(x, w_router):
    t, d = x.shape
    tb = min(ROUTE_BLOCK, t)
    w = jnp.pad(w_router, ((0, 0), (0, LANES - N_EXPERTS)))
    return pl.pallas_call(
        _router_kernel,
        grid=(t // tb,),
        in_specs=[pl.BlockSpec((tb, d), lambda i: (i, 0)), pl.BlockSpec((d, LANES), lambda i: (0, 0))],
        out_specs=[pl.BlockSpec((tb, LANES), lambda i: (i, 0))] * 2,
        out_shape=[jax.ShapeDtypeStruct((t, LANES), jnp.int32), jax.ShapeDtypeStruct((t, LANES), F32)],
        compiler_params=_params("parallel"),
        name="moe_router",
    )(x, w)


def _rank_kernel(idx_ref, rank_ref, count_ref, run_ref):
    @pl.when(pl.program_id(0) == 0)
    def _():
        run_ref[...] = jnp.zeros_like(run_ref)

    idx = idx_ref[...]
    tb = idx.shape[0]
    lane = lax.broadcasted_iota(jnp.int32, idx.shape, 1)
    oh0 = lane == idx[:, 0:1]
    oh1 = lane == idx[:, 1:2]
    both = jnp.where(oh0 | oh1, 1.0, 0.0)
    before = jnp.dot(_tril_f32(tb, strict=True).astype(BF16), both.astype(BF16),
                     preferred_element_type=F32) + run_ref[...]
    r0 = jnp.sum(jnp.where(oh0, before, 0.0), axis=-1, keepdims=True)
    r1 = jnp.sum(jnp.where(oh1, before, 0.0), axis=-1, keepdims=True)
    rank_ref[...] = jnp.where(lane == 0, r0, jnp.where(lane == 1, r1, 0.0)).astype(jnp.int32)
    run_ref[...] += jnp.sum(both, axis=0, keepdims=True)
    count_ref[...] = run_ref[...]


def moe_rank(idx):
    t = idx.shape[0]
    tb = min(ROUTE_BLOCK, t)
    return pl.pallas_call(
        _rank_kernel,
        grid=(t // tb,),
        in_specs=[pl.BlockSpec((tb, LANES), lambda i: (i, 0))],
        out_specs=[pl.BlockSpec((tb, LANES), lambda i: (i, 0)), pl.BlockSpec((1, LANES), lambda i: (0, 0))],
        out_shape=[jax.ShapeDtypeStruct((t, LANES), jnp.int32), jax.ShapeDtypeStruct((1, LANES), F32)],
        scratch_shapes=[pltpu.VMEM((1, LANES), F32)],
        compiler_params=_params("arbitrary"),
        name="moe_rank",
    )(idx)


def _pos_kernel(idx_ref, rank_ref, off_ref, pos_ref):
    idx = idx_ref[...]
    lane = lax.broadcasted_iota(jnp.int32, idx.shape, 1)
    off = off_ref[...]
    p0 = jnp.sum(jnp.where(lane == idx[:, 0:1], off, 0), axis=-1, keepdims=True)
    p1 = jnp.sum(jnp.where(lane == idx[:, 1:2], off, 0), axis=-1, keepdims=True)
    pos_ref[...] = rank_ref[...] + jnp.where(lane == 0, p0, jnp.where(lane == 1, p1, 0))


def moe_positions(idx, rank, offsets):
    t = idx.shape[0]
    tb = min(ROUTE_BLOCK, t)
    off = jnp.pad(offsets, (0, LANES - N_EXPERTS)).reshape(1, LANES)
    blk = pl.BlockSpec((tb, LANES), lambda i: (i, 0))
    return pl.pallas_call(
        _pos_kernel,
        grid=(t // tb,),
        in_specs=[blk, blk, pl.BlockSpec((1, LANES), lambda i: (0, 0))],
        out_specs=blk,
        out_shape=jax.ShapeDtypeStruct((t, LANES), jnp.int32),
        compiler_params=_params("parallel"),
        name="moe_positions",
    )(idx, rank, off)


DMA_ISSUE_UNROLL = 8


def _tile_rows(row):
    return pl.ds(pl.multiple_of(row * ROW_CHUNKS, ROW_CHUNKS), ROW_CHUNKS)


def _dispatch_kernel(pos_ref, cnt_ref, off_ref, x_ref, xs_ref, zero_ref, sem, pad_sem, *, tb):
    base = pl.program_id(0) * (TOP_K * tb)

    def issue(t, _):
        for k in range(TOP_K):
            pltpu.make_async_copy(x_ref.at[_tile_rows(t)],
                                  xs_ref.at[_tile_rows(pos_ref[base + TOP_K * t + k])], sem).start()
        return 0

    lax.fori_loop(0, tb, issue, 0, unroll=DMA_ISSUE_UNROLL)

    @pl.when(pl.program_id(0) == 0)
    def _():
        zero_ref[...] = jnp.zeros_like(zero_ref)

        def fill_range(first, count):
            def copy(r):
                return pltpu.make_async_copy(zero_ref, xs_ref.at[_tile_rows(first + r)], pad_sem)

            def fill(r, _):
                copy(r).start()
                return 0

            def drain(r, _):
                copy(r).wait()
                return 0

            lax.fori_loop(0, count, fill, 0)
            lax.fori_loop(0, count, drain, 0)

        for e in range(N_EXPERTS):
            fill_range(off_ref[e] + cnt_ref[e], (-cnt_ref[e]) & (MOE_TILE - 1))
        last = N_EXPERTS - 1
        used = off_ref[last] + cnt_ref[last] + ((-cnt_ref[last]) & (MOE_TILE - 1))
        fill_range(used, xs_ref.shape[0] // ROW_CHUNKS - used)

    for _ in range(TOP_K):
        pltpu.make_async_copy(x_ref, xs_ref.at[pl.ds(0, tb * ROW_CHUNKS)], sem).wait()


def moe_dispatch(x_tiled, pos_flat, counts, offsets, n_rows):
    t = x_tiled.shape[0] // ROW_CHUNKS
    tb = min(ROUTE_BLOCK, t)
    grid_spec = pltpu.PrefetchScalarGridSpec(
        num_scalar_prefetch=3,
        grid=(t // tb,),
        in_specs=[pl.BlockSpec((tb * ROW_CHUNKS, LANES), lambda i, *_: (i, 0))],
        out_specs=pl.BlockSpec(memory_space=pl.ANY),
        scratch_shapes=[pltpu.VMEM((ROW_CHUNKS, LANES), F32), pltpu.SemaphoreType.DMA(()),
                        pltpu.SemaphoreType.DMA(())],
    )
    return pl.pallas_call(
        functools.partial(_dispatch_kernel, tb=tb),
        grid_spec=grid_spec,
        out_shape=jax.ShapeDtypeStruct((n_rows * ROW_CHUNKS, LANES), F32),
        compiler_params=_params("arbitrary"),
        name="moe_dispatch",
    )(pos_flat, counts, offsets, x_tiled)


def _moe_ffn_kernel(te_ref, nt_ref, x_ref, wg_ref, wu_ref, wd_ref, o_ref, xb_ref, acc_ref):
    i = pl.program_id(0)
    f = pl.program_id(1)

    @pl.when(i < nt_ref[0])
    def _():
        @pl.when(f == 0)
        def _():
            for j in range(ROW_CHUNKS):
                xb_ref[:, j * LANES:(j + 1) * LANES] = _load_row_tiled(x_ref, j, MOE_TILE).astype(BF16)
            acc_ref[...] = jnp.zeros_like(acc_ref)

        xb = xb_ref[...]
        gate = jnp.dot(xb, wg_ref[0], preferred_element_type=F32)
        up = jnp.dot(xb, wu_ref[0], preferred_element_type=F32)
        h = (_silu(gate) * up).astype(BF16)
        acc_ref[...] += jnp.dot(h, wd_ref[0], preferred_element_type=F32)

        @pl.when(f == pl.num_programs(1) - 1)
        def _():
            _store_row_tiled(o_ref, acc_ref[...])

    @pl.when((i >= nt_ref[0]) & (f == 0))
    def _():
        o_ref[...] = jnp.zeros_like(o_ref)


def moe_ffn(xs, tile_expert, n_tiles_used, wg, wu, wd, tf=1792):
    n_rows, d = xs.shape[0] // ROW_CHUNKS, D_MODEL
    n_tiles = n_rows // MOE_TILE
    tile_spec = lambda index: pl.BlockSpec((MOE_TILE * ROW_CHUNKS, LANES), index)
    fdim = wg.shape[2]
    nf = fdim // tf

    def live(i, nt):
        return jnp.minimum(i, nt[0] - 1)

    def fblk(i, f, nt):
        return jnp.where(i < nt[0], f, nf - 1)

    grid_spec = pltpu.PrefetchScalarGridSpec(
        num_scalar_prefetch=2,
        grid=(n_tiles, nf),
        in_specs=[tile_spec(lambda i, f, te, nt: (live(i, nt), 0)),
                  pl.BlockSpec((1, d, tf), lambda i, f, te, nt: (te[i], 0, fblk(i, f, nt))),
                  pl.BlockSpec((1, d, tf), lambda i, f, te, nt: (te[i], 0, fblk(i, f, nt))),
                  pl.BlockSpec((1, tf, d), lambda i, f, te, nt: (te[i], fblk(i, f, nt), 0))],
        out_specs=tile_spec(lambda i, f, te, nt: (i, 0)),
        scratch_shapes=[pltpu.VMEM((MOE_TILE, d), BF16), pltpu.VMEM((MOE_TILE, d), F32)],
    )
    return pl.pallas_call(
        _moe_ffn_kernel,
        grid_spec=grid_spec,
        out_shape=jax.ShapeDtypeStruct(xs.shape, F32),
        compiler_params=_params("arbitrary", "arbitrary"),
        name="moe_ffn",
    )(tile_expert, n_tiles_used, xs, wg, wu, wd)


def _combine_kernel(pos_ref, x_ref, gate_ref, g_ref, b_ref, ys_ref, o_ref, buf_ref, sem, *, tb):
    base = pl.program_id(0) * (TOP_K * tb)

    def issue(t, _):
        for k in range(TOP_K):
            pltpu.make_async_copy(ys_ref.at[_tile_rows(pos_ref[base + TOP_K * t + k])],
                                  buf_ref.at[k, _tile_rows(t)], sem).start()
        return 0

    lax.fori_loop(0, tb, issue, 0, unroll=DMA_ISSUE_UNROLL)
    for k in range(TOP_K):
        pltpu.make_async_copy(ys_ref.at[pl.ds(0, tb * ROW_CHUNKS)], buf_ref.at[k], sem).wait()
    gates = gate_ref[...]
    y = jnp.concatenate(
        [buf_ref[0, pl.ds(j, tb, stride=ROW_CHUNKS), :] * gates[:, 0:1]
         + buf_ref[1, pl.ds(j, tb, stride=ROW_CHUNKS), :] * gates[:, 1:2] for j in range(ROW_CHUNKS)],
        axis=1)
    o_ref[...] = _layer_norm_rows(DN_ALPHA * x_ref[...] + y, g_ref[...], b_ref[...])


def moe_combine(x, ys, pos_flat, gates, g, b):
    t, d = x.shape
    tb = min(ROUTE_BLOCK, t)
    grid_spec = pltpu.PrefetchScalarGridSpec(
        num_scalar_prefetch=1,
        grid=(t // tb,),
        in_specs=[pl.BlockSpec((tb, d), lambda i, *_: (i, 0)),
                  pl.BlockSpec((tb, LANES), lambda i, *_: (i, 0)),
                  pl.BlockSpec((1, d), lambda i, *_: (0, 0)),
                  pl.BlockSpec((1, d), lambda i, *_: (0, 0)),
                  pl.BlockSpec(memory_space=pl.ANY)],
        out_specs=pl.BlockSpec((tb, d), lambda i, *_: (i, 0)),
        scratch_shapes=[pltpu.VMEM((TOP_K, tb * ROW_CHUNKS, LANES), F32), pltpu.SemaphoreType.DMA(())],
    )
    return pl.pallas_call(
        functools.partial(_combine_kernel, tb=tb),
        grid_spec=grid_spec,
        out_shape=jax.ShapeDtypeStruct((t, d), F32),
        compiler_params=_params("arbitrary"),
        name="moe_combine",
    )(pos_flat, x, gates, g.reshape(1, d), b.reshape(1, d), ys)


def moe_deepnorm(x, x_tiled, w_router, wg, wu, wd, g, b):
    t = x.shape[0]
    idx, gates = moe_router(x, w_router)
    rank, counts_f = moe_rank(idx)
    counts = counts_f[0, :N_EXPERTS].astype(jnp.int32)
    padded = (counts + MOE_TILE - 1) // MOE_TILE * MOE_TILE
    ends = jnp.cumsum(padded)
    offsets = ends - padded
    n_tiles = (TOP_K * t) // MOE_TILE + N_EXPERTS
    n_used = (ends[-1] // MOE_TILE).astype(jnp.int32)
    tile_start = jnp.arange(n_tiles, dtype=jnp.int32) * MOE_TILE
    tile_start = jnp.minimum(tile_start, ends[-1] - MOE_TILE)
    tile_expert = jnp.sum(tile_start[:, None] >= ends[None, :], axis=1).astype(jnp.int32)
    pos = moe_positions(idx, rank, offsets)
    pos_flat = pos[:, :TOP_K].reshape(-1)
    xs = moe_dispatch(x_tiled, pos_flat, counts, offsets, n_tiles * MOE_TILE)
    ys = moe_ffn(xs, tile_expert, n_used.reshape(1), wg.astype(BF16), wu.astype(BF16), wd.astype(BF16))
    return moe_combine(x, ys, pos_flat, gates, g, b)


def kernel(x, l0_ssd_w_in, l0_ssd_conv_w, l0_ssd_conv_b, l0_ssd_dt_bias, l0_ssd_a_log, l0_ssd_d_skip, l0_ssd_norm_w, l0_ssd_w_out, l0_ln_mix_g, l0_ln_mix_b, l0_ffn_w_gate, l0_ffn_w_up, l0_ffn_w_down, l0_ln_ffn_g, l0_ln_ffn_b, l1_sb_w_qkv, l1_sb_w_out, l1_ln_mix_g, l1_ln_mix_b, l1_moe_w_router, l1_moe_w_gate, l1_moe_w_up, l1_moe_w_down, l1_ln_ffn_g, l1_ln_ffn_b, l2_fox_w_qkvf, l2_fox_b_f, l2_fox_w_out, l2_ln_mix_g, l2_ln_mix_b, l2_ffn_w_gate, l2_ffn_w_up, l2_ffn_w_down, l2_ln_ffn_g, l2_ln_ffn_b, l3_ssd_w_in, l3_ssd_conv_w, l3_ssd_conv_b, l3_ssd_dt_bias, l3_ssd_a_log, l3_ssd_d_skip, l3_ssd_norm_w, l3_ssd_w_out, l3_ln_mix_g, l3_ln_mix_b, l3_moe_w_router, l3_moe_w_gate, l3_moe_w_up, l3_moe_w_down, l3_ln_ffn_g, l3_ln_ffn_b):
    batch, seq, d = x.shape
    h = x.reshape(batch * seq, d)
    bf = lambda w: w.astype(BF16)
    h = ssd_mixer_deepnorm(h, l0_ssd_w_in, l0_ssd_conv_w, l0_ssd_conv_b, l0_ssd_dt_bias, l0_ssd_a_log,
                           l0_ssd_d_skip, l0_ssd_norm_w, l0_ssd_w_out, l0_ln_mix_g, l0_ln_mix_b, batch, seq)
    h = ffn_deepnorm(h, bf(l0_ffn_w_gate), bf(l0_ffn_w_up), bf(l0_ffn_w_down), l0_ln_ffn_g, l0_ln_ffn_b)
    h, h_tiled = sb_mixer_deepnorm(h, l1_sb_w_qkv, l1_sb_w_out, l1_ln_mix_g, l1_ln_mix_b, batch, seq,
                                   also_row_tiled=True)
    h = moe_deepnorm(h, h_tiled, l1_moe_w_router, l1_moe_w_gate, l1_moe_w_up, l1_moe_w_down, l1_ln_ffn_g, l1_ln_ffn_b)
    h = fox_mixer_deepnorm(h, l2_fox_w_qkvf, l2_fox_b_f, l2_fox_w_out, l2_ln_mix_g, l2_ln_mix_b, batch, seq)
    h = ffn_deepnorm(h, bf(l2_ffn_w_gate), bf(l2_ffn_w_up), bf(l2_ffn_w_down), l2_ln_ffn_g, l2_ln_ffn_b)
    h, h_tiled = ssd_mixer_deepnorm(h, l3_ssd_w_in, l3_ssd_conv_w, l3_ssd_conv_b, l3_ssd_dt_bias, l3_ssd_a_log,
                                    l3_ssd_d_skip, l3_ssd_norm_w, l3_ssd_w_out, l3_ln_mix_g, l3_ln_mix_b,
                                    batch, seq, also_row_tiled=True)
    h = moe_deepnorm(h, h_tiled, l3_moe_w_router, l3_moe_w_gate, l3_moe_w_up, l3_moe_w_down, l3_ln_ffn_g, l3_ln_ffn_b)
    return h.reshape(batch, seq, d)
```

```python
import functools
import math

import jax
import jax.numpy as jnp
from jax import lax
from jax.experimental import pallas as pl
from jax.experimental.pallas import tpu as pltpu

F32 = jnp.float32
BF16 = jnp.bfloat16
HIGHEST = lax.Precision.HIGHEST

LANES = 128
SUBLANES = 8
VMEM_LIMIT_BYTES = 56 * 1024 * 1024

D_MODEL = 1024
DEPTH = 4
SSD_D_INNER = 2048
SSD_HEAD_DIM = 64
SSD_HEADS = 32
SSD_GROUPS = 4
SSD_STATE = 128
SSD_CONV = 4
SSD_GROUP_DIM = SSD_D_INNER // SSD_GROUPS
SSD_BC_DIM = 2 * SSD_GROUPS * SSD_STATE
ATT_HEAD_DIM = 64
ATT_HEADS = 16
ATT_DIM = 1024
N_EXPERTS = 8
TOP_K = 2
DN_ALPHA = (2.0 * DEPTH) ** 0.25
LN_EPS = 1e-5
RMS_EPS = 1e-5
NEG_BIG = -1e30


def _params(*semantics):
    return pltpu.CompilerParams(dimension_semantics=semantics, vmem_limit_bytes=VMEM_LIMIT_BYTES)


def _layer_norm_rows(h, g, b):
    mu = jnp.mean(h, axis=-1, keepdims=True)
    d = h - mu
    var = jnp.mean(d * d, axis=-1, keepdims=True)
    return d * lax.rsqrt(var + LN_EPS) * g + b


def _silu(x):
    half = 0.5 * x
    return half * (1.0 + jnp.tanh(half))


def _mm_kernel(x_ref, w_ref, o_ref, xb_ref):
    @pl.when(pl.program_id(1) == 0)
    def _():
        xb_ref[...] = x_ref[...].astype(BF16)

    o_ref[...] = jnp.dot(xb_ref[...], w_ref[...], preferred_element_type=F32).astype(o_ref.dtype)


def matmul(x, w, out_dtype, tm=1024, tn=1024):
    m, k = x.shape
    tm = min(tm, m)
    n = w.shape[1]
    return pl.pallas_call(
        _mm_kernel,
        grid=(m // tm, n // tn),
        in_specs=[pl.BlockSpec((tm, k), lambda i, j: (i, 0)),
                  pl.BlockSpec((k, tn), lambda i, j: (0, j))],
        out_specs=pl.BlockSpec((tm, tn), lambda i, j: (i, j)),
        out_shape=jax.ShapeDtypeStruct((m, n), out_dtype),
        scratch_shapes=[pltpu.VMEM((tm, k), BF16)],
        compiler_params=_params("parallel", "arbitrary"),
        name="matmul",
    )(x, w)


def _dot_split(x, w):
    xh = x.astype(BF16)
    xl = (x - xh.astype(F32)).astype(BF16)
    wh = w.astype(BF16)
    wl = (w - wh.astype(F32)).astype(BF16)
    dot = functools.partial(jnp.dot, preferred_element_type=F32)
    return dot(xh, wh) + (dot(xh, wl) + dot(xl, wh))


def _mm_f32_kernel(x_ref, w_ref, o_ref):
    o_ref[...] = _dot_split(x_ref[...], w_ref[...])


def matmul_f32(x, w, tm=1024):
    m, k = x.shape
    tm = min(tm, m)
    n = w.shape[1]
    return pl.pallas_call(
        _mm_f32_kernel,
        grid=(m // tm,),
        in_specs=[pl.BlockSpec((tm, k), lambda i: (i, 0)),
                  pl.BlockSpec((k, n), lambda i: (0, 0))],
        out_specs=pl.BlockSpec((tm, n), lambda i: (i, 0)),
        out_shape=jax.ShapeDtypeStruct((m, n), F32),
        compiler_params=_params("parallel"),
        name="matmul_f32",
    )(x, w)


ROW_CHUNKS = D_MODEL // LANES


def _store_row_tiled(ref, value):
    n = value.shape[0]
    for j in range(ROW_CHUNKS):
        ref[pl.ds(j, n, stride=ROW_CHUNKS), :] = value[:, j * LANES:(j + 1) * LANES]


def _load_row_tiled(ref, j, n):
    return ref[pl.ds(j, n, stride=ROW_CHUNKS), :]


def _mm_ln_kernel(x_ref, w_ref, r_ref, g_ref, b_ref, o_ref, *maybe_tiled_ref):
    y = jnp.dot(x_ref[...], w_ref[...], preferred_element_type=F32)
    o = _layer_norm_rows(DN_ALPHA * r_ref[...] + y, g_ref[...], b_ref[...])
    o_ref[...] = o
    for t_ref in maybe_tiled_ref:
        _store_row_tiled(t_ref, o)


def matmul_deepnorm(x, w, resid, g, b, tm=512, also_row_tiled=False):
    m, k = x.shape
    tm = min(tm, m)
    d = w.shape[1]
    out_specs = [pl.BlockSpec((tm, d), lambda i: (i, 0))]
    out_shape = [jax.ShapeDtypeStruct((m, d), F32)]
    if also_row_tiled:
        out_specs.append(pl.BlockSpec((tm * ROW_CHUNKS, LANES), lambda i: (i, 0)))
        out_shape.append(jax.ShapeDtypeStruct((m * ROW_CHUNKS, LANES), F32))
    out = pl.pallas_call(
        _mm_ln_kernel,
        grid=(m // tm,),
        in_specs=[pl.BlockSpec((tm, k), lambda i: (i, 0)),
                  pl.BlockSpec((k, d), lambda i: (0, 0)),
                  pl.BlockSpec((tm, d), lambda i: (i, 0)),
                  pl.BlockSpec((1, d), lambda i: (0, 0)),
                  pl.BlockSpec((1, d), lambda i: (0, 0))],
        out_specs=out_specs,
        out_shape=out_shape,
        compiler_params=_params("parallel"),
        name="matmul_deepnorm",
    )(x, w, resid, g.reshape(1, d), b.reshape(1, d))
    return tuple(out) if also_row_tiled else out[0]


def _ffn_kernel(x_ref, wg_ref, wu_ref, wd_ref, g_ref, b_ref, o_ref):
    x = x_ref[...]
    xb = x.astype(BF16)
    gate = jnp.dot(xb, wg_ref[...], preferred_element_type=F32)
    up = jnp.dot(xb, wu_ref[...], preferred_element_type=F32)
    h = (_silu(gate) * up).astype(BF16)
    y = jnp.dot(h, wd_ref[...], preferred_element_type=F32)
    o_ref[...] = _layer_norm_rows(DN_ALPHA * x + y, g_ref[...], b_ref[...])


def ffn_deepnorm(x, wg, wu, wd, g, b, tm=512):
    m, d = x.shape
    tm = min(tm, m)
    resident = lambda a: pl.BlockSpec(a.shape, lambda i: (0, 0), pipeline_mode=pl.Buffered(1))
    return pl.pallas_call(
        _ffn_kernel,
        grid=(m // tm,),
        in_specs=[pl.BlockSpec((tm, d), lambda i: (i, 0)),
                  resident(wg), resident(wu), resident(wd),
                  pl.BlockSpec((1, d), lambda i: (0, 0)),
                  pl.BlockSpec((1, d), lambda i: (0, 0))],
        out_specs=pl.BlockSpec((tm, d), lambda i: (i, 0)),
        out_shape=jax.ShapeDtypeStruct((m, d), F32),
        compiler_params=_params("parallel"),
        name="ffn_deepnorm",
    )(x, wg, wu, wd, g.reshape(1, d), b.reshape(1, d))


def _split_bf16(v, pieces):
    out = []
    r = v
    for _ in range(pieces - 1):
        p = r.astype(BF16)
        out.append(p)
        r = r - p.astype(F32)
    out.append(r.astype(BF16))
    return out


def _expand(v, e_ref, pieces):
    stacked = jnp.concatenate(_split_bf16(v, pieces), axis=1)
    return jnp.dot(stacked, e_ref[...], preferred_element_type=F32)


def _tril_f32(n, strict=False):
    r = lax.broadcasted_iota(jnp.int32, (n, n), 0)
    c = lax.broadcasted_iota(jnp.int32, (n, n), 1)
    return ((r > c) if strict else (r >= c)).astype(F32)


def _ssd_kernel(z_ref, xs_ref, bc_ref, dt_ref,
                dtb_ref, alog_ref, dskip_ref, normw_ref, e64_ref, e128_ref, o_ref,
                state_ref, xdt_ref, acol_ref, arow_ref, cb_ref, ydiag_ref,
                *, chunk):
    L = chunk
    c = pl.program_id(1)

    @pl.when(c == 0)
    def _():
        state_ref[...] = jnp.zeros_like(state_ref)

    xs = xs_ref[...]
    bcv = bc_ref[...]

    dt = jax.nn.softplus(dt_ref[...] + dtb_ref[...])
    da = dt * (-jnp.exp(alog_ref[...]))
    a_cs = jnp.dot(_tril_f32(L), da, preferred_element_type=F32, precision=HIGHEST)
    ea = jnp.exp(a_cs)
    dte = jnp.exp(a_cs[L - 1:L, :] - a_cs)

    dt_x = _expand(dt, e64_ref, 2)
    ea_x = _expand(ea, e64_ref, 2)
    dte_x = _expand(dte, e64_ref, 2)
    acol_ref[...] = _expand(a_cs, e128_ref, 3)
    a_t = a_cs.T
    for h in range(SSD_HEADS):
        arow_ref[h] = jnp.broadcast_to(a_t[h:h + 1, :], (SUBLANES, L))

    xdt = xs * dt_x
    xdt_ref[...] = xdt.astype(BF16)
    xdte = (xdt * dte_x).astype(BF16)

    y_off = []
    for g in range(SSD_GROUPS):
        bm = bcv[:, g * SSD_STATE:(g + 1) * SSD_STATE]
        cm = bcv[:, (SSD_GROUPS + g) * SSD_STATE:(SSD_GROUPS + g + 1) * SSD_STATE].astype(BF16)
        cb_ref[g] = lax.dot_general(cm, bm.astype(BF16), (((1,), (1,)), ((), ())),
                                    preferred_element_type=F32)
        gs = slice(g * SSD_GROUP_DIM, (g + 1) * SSD_GROUP_DIM)
        st = state_ref[g]
        y_off.append(jnp.dot(cm, st.astype(BF16), preferred_element_type=F32) * ea_x[:, gs])
        state_ref[g] = st * ea_x[L - 1:L, gs] + jnp.dot(
            bm.T.astype(BF16), xdte[:, gs], preferred_element_type=F32)

    row = lax.broadcasted_iota(jnp.int32, (L, L), 0)
    col = lax.broadcasted_iota(jnp.int32, (L, L), 1)
    causal = row >= col
    head0 = lax.broadcasted_iota(jnp.int32, (L, LANES), 1) < SSD_HEAD_DIM

    heads_per_group = SSD_HEADS // SSD_GROUPS

    def group_body(g, carry):
        cbg = cb_ref[g]
        scores = []
        for j in range(heads_per_group):
            h = g * heads_per_group + j
            a_l = acol_ref[:, pl.ds(pl.multiple_of(h * LANES, LANES), LANES)]
            a_l = jnp.concatenate([a_l] * (L // LANES), axis=1)
            a_s = arow_ref[h][0:1, :]
            scores.append((cbg * jnp.exp(jnp.where(causal, a_l - a_s, NEG_BIG))).astype(BF16))
        for pp in range(heads_per_group // 2):
            lanes = pl.ds(pl.multiple_of((g * (heads_per_group // 2) + pp) * LANES, LANES), LANES)
            xpair = xdt_ref[:, lanes]
            ys = [jnp.dot(scores[2 * pp + j], xpair, preferred_element_type=F32) for j in range(2)]
            ydiag_ref[:, lanes] = jnp.where(head0, ys[0], ys[1])
        return carry

    lax.fori_loop(0, SSD_GROUPS, group_body, 0)

    y = ydiag_ref[...] + jnp.concatenate(y_off, axis=1) + dskip_ref[...] * xs
    y = y * _silu(z_ref[...])
    parts = []
    for g in range(SSD_GROUPS):
        yg = y[:, g * SSD_GROUP_DIM:(g + 1) * SSD_GROUP_DIM]
        parts.append(yg * lax.rsqrt(jnp.mean(yg * yg, axis=-1, keepdims=True) + RMS_EPS))
    o_ref[...] = (jnp.concatenate(parts, axis=1) * normw_ref[...]).astype(o_ref.dtype)


def _expansion_matrix(width, pieces):
    h = jnp.arange(LANES)[:, None]
    lane = jnp.arange(SSD_HEADS * width)[None, :]
    e = (lane // width == h).astype(BF16)
    return jnp.concatenate([e] * pieces, axis=0)


def _mm_conv_kernel(x_ref, halo_ref, w_ref, cw_ref, cb_ref, o_ref, ext_ref, xb_ref, *, tiles_per_seq):
    tm = x_ref.shape[0]

    @pl.when(pl.program_id(1) == 0)
    def _():
        xb_ref[...] = x_ref[...].astype(BF16)

    w = w_ref[...]
    halo = jnp.dot(halo_ref[...].astype(BF16), w, preferred_element_type=F32)
    first = pl.program_id(0) % tiles_per_seq == 0
    ext_ref[0:SUBLANES, :] = jnp.where(first, 0.0, halo)
    ext_ref[SUBLANES:SUBLANES + tm, :] = jnp.dot(xb_ref[...], w, preferred_element_type=F32)
    acc = cb_ref[...]
    for k in range(SSD_CONV):
        start = SUBLANES - (SSD_CONV - 1) + k
        acc = acc + cw_ref[k:k + 1, :] * ext_ref[start:start + tm, :]
    o_ref[...] = _silu(acc)


def matmul_conv_silu(x, w, conv_w, conv_b, seq, tm=1024, tn=1024):
    m, k = x.shape
    tm = min(tm, seq)
    n = w.shape[1]
    halo_blocks = tm // SUBLANES
    return pl.pallas_call(
        functools.partial(_mm_conv_kernel, tiles_per_seq=seq // tm),
        grid=(m // tm, n // tn),
        in_specs=[pl.BlockSpec((tm, k), lambda i, j: (i, 0)),
                  pl.BlockSpec((SUBLANES, k), lambda i, j: (jnp.maximum(i * halo_blocks - 1, 0), 0)),
                  pl.BlockSpec((k, tn), lambda i, j: (0, j)),
                  pl.BlockSpec((SSD_CONV, tn), lambda i, j: (0, j)),
                  pl.BlockSpec((1, tn), lambda i, j: (0, j))],
        out_specs=pl.BlockSpec((tm, tn), lambda i, j: (i, j)),
        out_shape=jax.ShapeDtypeStruct((m, n), F32),
        scratch_shapes=[pltpu.VMEM((tm + SUBLANES, tn), F32), pltpu.VMEM((tm, k), BF16)],
        compiler_params=_params("parallel", "arbitrary"),
        name="matmul_conv_silu",
    )(x, x, w, conv_w, conv_b.reshape(1, n))


def ssd_core(z, xbc, dt_raw, dt_bias, a_log, d_skip, norm_w, batch, seq, chunk=128):
    t = z.shape[0]
    nc = seq // chunk
    pad = LANES - SSD_HEADS
    row = lambda v: v.reshape(1, -1).astype(F32)
    args = (
        z, xbc, xbc, dt_raw,
        row(jnp.pad(dt_bias, (0, pad))), row(jnp.pad(a_log, (0, pad))),
        row(jnp.repeat(d_skip, SSD_HEAD_DIM)), row(norm_w),
        _expansion_matrix(SSD_HEAD_DIM, 2), _expansion_matrix(LANES, 3),
    )
    blk = lambda b, c: (b * nc + c, 0)
    const = lambda b, c: (0, 0)
    full = lambda a: pl.BlockSpec(a.shape, const)
    in_specs = [
        pl.BlockSpec((chunk, SSD_D_INNER), blk),
        pl.BlockSpec((chunk, SSD_D_INNER), blk),
        pl.BlockSpec((chunk, SSD_BC_DIM), lambda b, c: (b * nc + c, 2)),
        pl.BlockSpec((chunk, LANES), blk),
    ] + [full(a) for a in args[4:]]
    return pl.pallas_call(
        functools.partial(_ssd_kernel, chunk=chunk),
        grid=(batch, nc),
        in_specs=in_specs,
        out_specs=pl.BlockSpec((chunk, SSD_D_INNER), blk),
        out_shape=jax.ShapeDtypeStruct((t, SSD_D_INNER), BF16),
        scratch_shapes=[
            pltpu.VMEM((SSD_GROUPS, SSD_STATE, SSD_GROUP_DIM), F32),
            pltpu.VMEM((chunk, SSD_D_INNER), BF16),
            pltpu.VMEM((chunk, SSD_HEADS * LANES), F32),
            pltpu.VMEM((SSD_HEADS, SUBLANES, chunk), F32),
            pltpu.VMEM((SSD_GROUPS, chunk, chunk), F32),
            pltpu.VMEM((chunk, SSD_D_INNER), F32),
        ],
        compiler_params=_params("parallel", "arbitrary"),
        name="ssd_core",
    )(*args)


def ssd_mixer_deepnorm(x, w_in, conv_w, conv_b, dt_bias, a_log, d_skip, norm_w, w_out, g, b,
                       batch, seq, also_row_tiled=False):
    n_zx = SSD_D_INNER + SSD_D_INNER + SSD_BC_DIM
    z = matmul(x, w_in[:, :SSD_D_INNER].astype(BF16), F32)
    xbc = matmul_conv_silu(x, w_in[:, SSD_D_INNER:n_zx].astype(BF16), conv_w, conv_b, seq)
    w_dt = jnp.pad(w_in[:, n_zx:], ((0, 0), (0, LANES - SSD_HEADS)))
    dt_raw = matmul_f32(x, w_dt)
    y = ssd_core(z, xbc, dt_raw, dt_bias, a_log, d_skip, norm_w, batch, seq)
    return matmul_deepnorm(y, w_out.astype(BF16), x, g, b, also_row_tiled=also_row_tiled)


ATT_BLOCK = 128
ATT_Q = 256
ATT_PAIRS = 4
ATT_STEP_LANES = ATT_PAIRS * LANES
ATT_SCALE = ATT_HEAD_DIM ** -0.5
LOG2E = 1.4426950408889634


def _split_heads(q_ref, rows, head0):
    qs = []
    for p in range(ATT_PAIRS):
        q2 = q_ref[rows, p * LANES:(p + 1) * LANES]
        zero = jnp.zeros_like(q2)
        qs += [jnp.where(head0, q2, zero), jnp.where(head0, zero, q2)]
    return qs


def _sb_kernel(q_ref, k_ref, v_ref, u_ref, o_ref, *scratch, seq):
    n_heads = 2 * ATT_PAIRS
    later_ref, acc_ref, z_ref = scratch[:n_heads], scratch[n_heads:2 * n_heads], scratch[2 * n_heads:]
    nq = seq // ATT_Q
    ndiag = ATT_Q // ATT_BLOCK
    head0 = lax.broadcasted_iota(jnp.int32, (ATT_Q, LANES), 1) < ATT_HEAD_DIM
    row = lax.broadcasted_iota(jnp.int32, (ATT_Q, ATT_BLOCK), 0)
    col = lax.broadcasted_iota(jnp.int32, (ATT_Q, ATT_BLOCK), 1)

    def q_body(i, _):
        q0 = pl.multiple_of(i * ATT_Q, ATT_Q)
        rows = pl.ds(q0, ATT_Q)
        qs = _split_heads(q_ref, rows, head0)
        for n in range(2 * ATT_PAIRS):
            later_ref[n][...] = jnp.zeros((ATT_Q, LANES), F32)
            acc_ref[n][...] = jnp.zeros((ATT_Q, LANES), F32)

        heads = range(2 * ATT_PAIRS)

        def key_rows(j):
            return pl.ds(pl.multiple_of(j * ATT_BLOCK, ATT_BLOCK), ATT_BLOCK)

        def scores(j):
            k2 = [k_ref[key_rows(j), p * LANES:(p + 1) * LANES] for p in range(ATT_PAIRS)]
            return [lax.dot_general(qs[n], k2[n // 2], (((1,), (1,)), ((), ())),
                                    preferred_element_type=F32) * (ATT_SCALE * LOG2E) for n in heads]

        def consume(z2, j, strict):
            keys = key_rows(j)
            v2 = [v_ref[keys, p * LANES:(p + 1) * LANES] for p in range(ATT_PAIRS)]
            log_beta, sums = [], []
            for n in heads:
                lb = jnp.minimum(z2[n], 0.0) - jnp.log2(1.0 + jnp.exp2(-jnp.abs(z2[n])))
                log_keep = lb - z2[n]
                if strict is not None:
                    log_keep = jnp.where(strict, log_keep, 0.0)
                hi = log_keep.astype(BF16)
                lo = (log_keep - hi.astype(F32)).astype(BF16)
                log_beta.append(lb)
                sums.append(jnp.dot(jnp.concatenate([hi, lo], axis=1), u_ref[...],
                                    preferred_element_type=F32))
            for n in heads:
                w = jnp.exp2(log_beta[n] + sums[n][:, :ATT_BLOCK] + later_ref[n][...])
                if strict is not None:
                    w = jnp.where(strict, w, 0.0)
                acc_ref[n][...] += jnp.dot(w.astype(BF16), v2[n // 2], preferred_element_type=F32)
                later_ref[n][...] += sums[n][:, ATT_BLOCK:]

        for d in reversed(range(ndiag)):
            consume(scores(i * ndiag + d), i * ndiag + d, d * ATT_BLOCK + col < row)

        n_off = i * ndiag
        for n, zn in enumerate(scores(jnp.maximum(n_off - 1, 0))):
            z_ref[n][...] = zn

        def off_diagonal(jj, c):
            z2 = [z_ref[n][...] for n in heads]
            ahead = scores(jnp.maximum(n_off - 2 - jj, 0))
            consume(z2, n_off - 1 - jj, None)
            for n in heads:
                z_ref[n][...] = ahead[n]
            return c

        lax.fori_loop(0, n_off, off_diagonal, 0)
        for p in range(ATT_PAIRS):
            o_ref[rows, p * LANES:(p + 1) * LANES] = jnp.where(
                head0, acc_ref[2 * p][...], acc_ref[2 * p + 1][...]).astype(o_ref.dtype)
        return 0

    lax.fori_loop(0, nq, q_body, 0)


def _suffix_sum_matrix():
    j = jnp.arange(2 * ATT_BLOCK)[:, None] % ATT_BLOCK
    s = jnp.arange(2 * ATT_BLOCK)[None, :]
    return jnp.where(s < ATT_BLOCK, j > s, True).astype(BF16)


def _attention_specs(seq):
    n_steps = ATT_HEADS // 2 // ATT_PAIRS
    blk = lambda off: pl.BlockSpec((seq, ATT_STEP_LANES), lambda b, p: (b, off + p))
    specs = [blk(0), blk(n_steps), blk(2 * n_steps)]
    return n_steps, specs, pl.BlockSpec((seq, ATT_STEP_LANES), lambda b, p: (b, p))


def sb_attention(qkv, batch, seq):
    n_steps, in_specs, out_spec = _attention_specs(seq)
    u = _suffix_sum_matrix()
    return pl.pallas_call(
        functools.partial(_sb_kernel, seq=seq),
        grid=(batch, n_steps),
        in_specs=in_specs + [pl.BlockSpec(u.shape, lambda b, p: (0, 0))],
        out_specs=out_spec,
        out_shape=jax.ShapeDtypeStruct((batch * seq, ATT_DIM), BF16),
        scratch_shapes=[pltpu.VMEM((ATT_Q, LANES), F32)] * (6 * ATT_PAIRS),
        compiler_params=_params("parallel", "parallel"),
        name="sb_attention",
    )(qkv, qkv, qkv, u)


def sb_mixer_deepnorm(x, w_qkv, w_out, g, b, batch, seq, also_row_tiled=False):
    qkv = matmul(x, w_qkv.astype(BF16), BF16)
    o = sb_attention(qkv, batch, seq)
    return matmul_deepnorm(o, w_out.astype(BF16), x, g, b, also_row_tiled=also_row_tiled)


CUMSUM_BLOCK = 256


def _fox_decay_kernel(f_ref, bf_ref, ccol_ref, crow_ref, *, seq):
    tri = _tril_f32(CUMSUM_BLOCK)
    carry = jnp.zeros((1, LANES), F32)
    for blk in range(seq // CUMSUM_BLOCK):
        rows = slice(blk * CUMSUM_BLOCK, (blk + 1) * CUMSUM_BLOCK)
        log_f = jax.nn.log_sigmoid(f_ref[rows, :] + bf_ref[...])
        c = jnp.dot(tri, log_f, preferred_element_type=F32, precision=HIGHEST) + carry
        carry = c[CUMSUM_BLOCK - 1:CUMSUM_BLOCK, :]
        ccol_ref[rows, :] = c
        c_t = c.T
        for p in range(ATT_HEADS // 2):
            crow_ref[0, p, :, rows] = c_t[2 * p:2 * p + 2, :]


def fox_decay(f_raw, b_f, batch, seq):
    return pl.pallas_call(
        functools.partial(_fox_decay_kernel, seq=seq),
        grid=(batch,),
        in_specs=[pl.BlockSpec((seq, LANES), lambda b: (b, 0)),
                  pl.BlockSpec((1, LANES), lambda b: (0, 0))],
        out_specs=[pl.BlockSpec((seq, LANES), lambda b: (b, 0)),
                   pl.BlockSpec((1, ATT_HEADS // 2, 2, seq), lambda b: (b, 0, 0, 0))],
        out_shape=[jax.ShapeDtypeStruct((batch * seq, LANES), F32),
                   jax.ShapeDtypeStruct((batch, ATT_HEADS // 2, 2, seq), F32)],
        compiler_params=_params("parallel"),
        name="fox_decay",
    )(f_raw, jnp.pad(b_f, (0, LANES - ATT_HEADS)).reshape(1, LANES))


def _fox_kernel(q_ref, k_ref, v_ref, ccol_ref, crow_ref, o_ref, crep_ref, vt_ref, *scratch, seq):
    n_heads = 2 * ATT_PAIRS
    m_ref, acc_ref, s_ref = scratch[:n_heads], scratch[n_heads:2 * n_heads], scratch[2 * n_heads:]
    nq = seq // ATT_Q
    ndiag = ATT_Q // ATT_BLOCK
    step = pl.program_id(1)
    heads = range(n_heads)
    head0 = lax.broadcasted_iota(jnp.int32, (ATT_Q, LANES), 1) < ATT_HEAD_DIM
    head0_rows = lax.broadcasted_iota(jnp.int32, (LANES, ATT_BLOCK), 0) < ATT_HEAD_DIM
    key = lax.broadcasted_iota(jnp.int32, (ATT_BLOCK, ATT_Q), 0)
    qry = lax.broadcasted_iota(jnp.int32, (ATT_BLOCK, ATT_Q), 1)

    pieces = jnp.concatenate(_split_bf16(ccol_ref[...], 3), axis=1)
    sel_row = lax.broadcasted_iota(jnp.int32, (3 * LANES, LANES), 0) & (LANES - 1)
    for n in heads:
        sel = jnp.where(sel_row == 2 * ATT_PAIRS * step + n, 1.0, 0.0).astype(BF16)
        crep_ref[n] = jnp.dot(pieces, sel, preferred_element_type=F32) * LOG2E
    for p in range(ATT_PAIRS):
        for blk in range(seq // ATT_BLOCK):
            rows = slice(blk * ATT_BLOCK, (blk + 1) * ATT_BLOCK)
            vt_ref[p, :, rows] = v_ref[rows, p * LANES:(p + 1) * LANES].astype(F32).T.astype(BF16)

    def q_body(i, _):
        q0 = pl.multiple_of(i * ATT_Q, ATT_Q)
        rows = pl.ds(q0, ATT_Q)
        qs = _split_heads(q_ref, rows, head0)
        c_q = [crow_ref[0, n // 2, n % 2:n % 2 + 1, rows] * LOG2E for n in heads]
        for n in heads:
            m_ref[n][...] = jnp.full((1, ATT_Q), NEG_BIG, F32)
            acc_ref[n][...] = jnp.zeros((LANES, ATT_Q), F32)

        def key_rows(j):
            return pl.ds(pl.multiple_of(j * ATT_BLOCK, ATT_BLOCK), ATT_BLOCK)

        def scores(j):
            k2 = [k_ref[key_rows(j), p * LANES:(p + 1) * LANES] for p in range(ATT_PAIRS)]
            return [lax.dot_general(k2[n // 2], qs[n], (((1,), (1,)), ((), ())),
                                    preferred_element_type=F32) * (ATT_SCALE * LOG2E) for n in heads]

        def consume(s, j, causal):
            keys = key_rows(j)
            vt = [vt_ref[p, :, keys] for p in range(ATT_PAIRS)]
            for n in heads:
                c_k = crep_ref[n, keys, :]
                sn = s[n] + (c_q[n] - jnp.concatenate([c_k] * (ATT_Q // LANES), axis=1))
                if causal is not None:
                    sn = jnp.where(causal, sn, NEG_BIG)
                m_old = m_ref[n][...]
                m_new = jnp.maximum(m_old, jnp.max(sn, axis=0, keepdims=True))
                prob = jnp.exp2(sn - m_new).astype(BF16)
                one = jnp.ones_like(vt[n // 2])
                v1t = jnp.where(head0_rows, vt[n // 2], one) if n % 2 == 0 else jnp.where(
                    head0_rows, one, vt[n // 2])
                acc_ref[n][...] = jnp.exp2(m_old - m_new) * acc_ref[n][...] + jnp.dot(
                    v1t, prob, preferred_element_type=F32)
                m_ref[n][...] = m_new

        for d in range(ndiag):
            consume(scores(i * ndiag + d), i * ndiag + d, d * ATT_BLOCK + key <= qry)

        n_off = i * ndiag
        for n, sn in enumerate(scores(0)):
            s_ref[n][...] = sn

        def off_diagonal(jj, c):
            s = [s_ref[n][...] for n in heads]
            ahead = scores(jnp.minimum(jj + 1, n_off - 1))
            consume(s, jj, None)
            for n in heads:
                s_ref[n][...] = ahead[n]
            return c

        lax.fori_loop(0, n_off, off_diagonal, 0)
        for p in range(ATT_PAIRS):
            a0, a1 = acc_ref[2 * p][...], acc_ref[2 * p + 1][...]
            o_t = jnp.concatenate([a0[:ATT_HEAD_DIM] / a0[ATT_HEAD_DIM:],
                                   a1[ATT_HEAD_DIM:] / a1[:ATT_HEAD_DIM]], axis=0)
            o_ref[rows, p * LANES:(p + 1) * LANES] = o_t.T.astype(o_ref.dtype)
        return 0

    lax.fori_loop(0, nq, q_body, 0)


def fox_attention(qkv, ccol, crow, batch, seq):
    n_steps, in_specs, out_spec = _attention_specs(seq)
    in_specs += [pl.BlockSpec((seq, LANES), lambda b, p: (b, 0)),
                 pl.BlockSpec((1, ATT_PAIRS, 2, seq), lambda b, p: (b, p, 0, 0))]
    return pl.pallas_call(
        functools.partial(_fox_kernel, seq=seq),
        grid=(batch, n_steps),
        in_specs=in_specs,
        out_specs=out_spec,
        out_shape=jax.ShapeDtypeStruct((batch * seq, ATT_DIM), BF16),
        scratch_shapes=([pltpu.VMEM((2 * ATT_PAIRS, seq, LANES), F32),
                         pltpu.VMEM((ATT_PAIRS, LANES, seq), BF16)]
                        + [pltpu.VMEM((1, ATT_Q), F32)] * (2 * ATT_PAIRS)
                        + [pltpu.VMEM((LANES, ATT_Q), F32)] * (2 * ATT_PAIRS)
                        + [pltpu.VMEM((ATT_BLOCK, ATT_Q), F32)] * (2 * ATT_PAIRS)),
        compiler_params=_params("parallel", "parallel"),
        name="fox_attention",
    )(qkv, qkv, qkv, ccol, crow)


def fox_mixer_deepnorm(x, w_qkvf, b_f, w_out, g, b, batch, seq):
    qkv = matmul(x, w_qkvf[:, :3 * ATT_DIM].astype(BF16), BF16)
    w_f = jnp.pad(w_qkvf[:, 3 * ATT_DIM:], ((0, 0), (0, LANES - ATT_HEADS)))
    ccol, crow = fox_decay(matmul_f32(x, w_f), b_f, batch, seq)
    o = fox_attention(qkv, ccol, crow, batch, seq)
    return matmul_deepnorm(o, w_out.astype(BF16), x, g, b)


MOE_TILE = 512
ROUTE_BLOCK = 512


def _router_kernel(x_ref, w_ref, idx_ref, gate_ref):
    logits = _dot_split(x_ref[...], w_ref[...])
    lane = lax.broadcasted_iota(jnp.int32, logits.shape, 1)
    logits = jnp.where(lane < N_EXPERTS, logits, NEG_BIG)
    m1 = jnp.max(logits, axis=-1, keepdims=True)
    i1 = jnp.min(jnp.where(logits == m1, lane, LANES), axis=-1, keepdims=True)
    rest = jnp.where(lane == i1, NEG_BIG, logits)
    m2 = jnp.max(rest, axis=-1, keepdims=True)
    i2 = jnp.min(jnp.where(rest == m2, lane, LANES), axis=-1, keepdims=True)
    e2 = jnp.exp(m2 - m1)
    denom = 1.0 + e2
    idx_ref[...] = jnp.where(lane == 0, i1, jnp.where(lane == 1, i2, 0))
    gate_ref[...] = jnp.where(lane == 0, 1.0 / denom, jnp.where(lane == 1, e2 / denom, 0.0))


def moe_router(x, w_router):
    t, d = x.shape
    tb = min(ROUTE_BLOCK, t)
    w = jnp.pad(w_router, ((0, 0), (0, LANES - N_EXPERTS)))
    return pl.pallas_call(
        _router_kernel,
        grid=(t // tb,),
        in_specs=[pl.BlockSpec((tb, d), lambda i: (i, 0)), pl.BlockSpec((d, LANES), lambda i: (0, 0))],
        out_specs=[pl.BlockSpec((tb, LANES), lambda i: (i, 0))] * 2,
        out_shape=[jax.ShapeDtypeStruct((t, LANES), jnp.int32), jax.ShapeDtypeStruct((t, LANES), F32)],
        compiler_params=_params("parallel"),
        name="moe_router",
    )(x, w)


def _rank_kernel(idx_ref, rank_ref, count_ref, run_ref):
    @pl.when(pl.program_id(0) == 0)
    def _():
        run_ref[...] = jnp.zeros_like(run_ref)

    idx = idx_ref[...]
    tb = idx.shape[0]
    lane = lax.broadcasted_iota(jnp.int32, idx.shape, 1)
    oh0 = lane == idx[:, 0:1]
    oh1 = lane == idx[:, 1:2]
    both = jnp.where(oh0 | oh1, 1.0, 0.0)
    before = jnp.dot(_tril_f32(tb, strict=True).astype(BF16), both.astype(BF16),
                     preferred_element_type=F32) + run_ref[...]
    r0 = jnp.sum(jnp.where(oh0, before, 0.0), axis=-1, keepdims=True)
    r1 = jnp.sum(jnp.where(oh1, before, 0.0), axis=-1, keepdims=True)
    rank_ref[...] = jnp.where(lane == 0, r0, jnp.where(lane == 1, r1, 0.0)).astype(jnp.int32)
    run_ref[...] += jnp.sum(both, axis=0, keepdims=True)
    count_ref[...] = run_ref[...]


def moe_rank(idx):
    t = idx.shape[0]
    tb = min(ROUTE_BLOCK, t)
    return pl.pallas_call(
        _rank_kernel,
        grid=(t // tb,),
        in_specs=[pl.BlockSpec((tb, LANES), lambda i: (i, 0))],
        out_specs=[pl.BlockSpec((tb, LANES), lambda i: (i, 0)), pl.BlockSpec((1, LANES), lambda i: (0, 0))],
        out_shape=[jax.ShapeDtypeStruct((t, LANES), jnp.int32), jax.ShapeDtypeStruct((1, LANES), F32)],
        scratch_shapes=[pltpu.VMEM((1, LANES), F32)],
        compiler_params=_params("arbitrary"),
        name="moe_rank",
    )(idx)


def _pos_kernel(idx_ref, rank_ref, off_ref, pos_ref):
    idx = idx_ref[...]
    lane = lax.broadcasted_iota(jnp.int32, idx.shape, 1)
    off = off_ref[...]
    p0 = jnp.sum(jnp.where(lane == idx[:, 0:1], off, 0), axis=-1, keepdims=True)
    p1 = jnp.sum(jnp.where(lane == idx[:, 1:2], off, 0), axis=-1, keepdims=True)
    pos_ref[...] = rank_ref[...] + jnp.where(lane == 0, p0, jnp.where(lane == 1, p1, 0))


def moe_positions(idx, rank, offsets):
    t = idx.shape[0]
    tb = min(ROUTE_BLOCK, t)
    off = jnp.pad(offsets, (0, LANES - N_EXPERTS)).reshape(1, LANES)
    blk = pl.BlockSpec((tb, LANES), lambda i: (i, 0))
    return pl.pallas_call(
        _pos_kernel,
        grid=(t // tb,),
        in_specs=[blk, blk, pl.BlockSpec((1, LANES), lambda i: (0, 0))],
        out_specs=blk,
        out_shape=jax.ShapeDtypeStruct((t, LANES), jnp.int32),
        compiler_params=_params("parallel"),
        name="moe_positions",
    )(idx, rank, off)


DMA_ISSUE_UNROLL = 8


def _tile_rows(row):
    return pl.ds(pl.multiple_of(row * ROW_CHUNKS, ROW_CHUNKS), ROW_CHUNKS)


def _dispatch_kernel(pos_ref, cnt_ref, off_ref, x_ref, xs_ref, zero_ref, sem, pad_sem, *, tb):
    base = pl.program_id(0) * (TOP_K * tb)

    def issue(t, _):
        for k in range(TOP_K):
            pltpu.make_async_copy(x_ref.at[_tile_rows(t)],
                                  xs_ref.at[_tile_rows(pos_ref[base + TOP_K * t + k])], sem).start()
        return 0

    lax.fori_loop(0, tb, issue, 0, unroll=DMA_ISSUE_UNROLL)

    @pl.when(pl.program_id(0) == 0)
    def _():
        zero_ref[...] = jnp.zeros_like(zero_ref)

        def fill_range(first, count):
            def copy(r):
                return pltpu.make_async_copy(zero_ref, xs_ref.at[_tile_rows(first + r)], pad_sem)

            def fill(r, _):
                copy(r).start()
                return 0

            def drain(r, _):
                copy(r).wait()
                return 0

            lax.fori_loop(0, count, fill, 0)
            lax.fori_loop(0, count, drain, 0)

        for e in range(N_EXPERTS):
            fill_range(off_ref[e] + cnt_ref[e], (-cnt_ref[e]) & (MOE_TILE - 1))
        last = N_EXPERTS - 1
        used = off_ref[last] + cnt_ref[last] + ((-cnt_ref[last]) & (MOE_TILE - 1))
        fill_range(used, xs_ref.shape[0] // ROW_CHUNKS - used)

    for _ in range(TOP_K):
        pltpu.make_async_copy(x_ref, xs_ref.at[pl.ds(0, tb * ROW_CHUNKS)], sem).wait()


def moe_dispatch(x_tiled, pos_flat, counts, offsets, n_rows):
    t = x_tiled.shape[0] // ROW_CHUNKS
    tb = min(ROUTE_BLOCK, t)
    grid_spec = pltpu.PrefetchScalarGridSpec(
        num_scalar_prefetch=3,
        grid=(t // tb,),
        in_specs=[pl.BlockSpec((tb * ROW_CHUNKS, LANES), lambda i, *_: (i, 0))],
        out_specs=pl.BlockSpec(memory_space=pl.ANY),
        scratch_shapes=[pltpu.VMEM((ROW_CHUNKS, LANES), F32), pltpu.SemaphoreType.DMA(()),
                        pltpu.SemaphoreType.DMA(())],
    )
    return pl.pallas_call(
        functools.partial(_dispatch_kernel, tb=tb),
        grid_spec=grid_spec,
        out_shape=jax.ShapeDtypeStruct((n_rows * ROW_CHUNKS, LANES), F32),
        compiler_params=_params("arbitrary"),
        name="moe_dispatch",
    )(pos_flat, counts, offsets, x_tiled)


def _moe_ffn_kernel(te_ref, nt_ref, x_ref, wg_ref, wu_ref, wd_ref, o_ref, xb_ref, acc_ref):
    i = pl.program_id(0)
    f = pl.program_id(1)

    @pl.when(i < nt_ref[0])
    def _():
        @pl.when(f == 0)
        def _():
            for j in range(ROW_CHUNKS):
                xb_ref[:, j * LANES:(j + 1) * LANES] = _load_row_tiled(x_ref, j, MOE_TILE).astype(BF16)
            acc_ref[...] = jnp.zeros_like(acc_ref)

        xb = xb_ref[...]
        gate = jnp.dot(xb, wg_ref[0], preferred_element_type=F32)
        up = jnp.dot(xb, wu_ref[0], preferred_element_type=F32)
        h = (_silu(gate) * up).astype(BF16)
        acc_ref[...] += jnp.dot(h, wd_ref[0], preferred_element_type=F32)

        @pl.when(f == pl.num_programs(1) - 1)
        def _():
            _store_row_tiled(o_ref, acc_ref[...])

    @pl.when((i >= nt_ref[0]) & (f == 0))
    def _():
        o_ref[...] = jnp.zeros_like(o_ref)


def moe_ffn(xs, tile_expert, n_tiles_used, wg, wu, wd, tf=1792):
    n_rows, d = xs.shape[0] // ROW_CHUNKS, D_MODEL
    n_tiles = n_rows // MOE_TILE
    tile_spec = lambda index: pl.BlockSpec((MOE_TILE * ROW_CHUNKS, LANES), index)
    fdim = wg.shape[2]
    nf = fdim // tf

    def live(i, nt):
        return jnp.minimum(i, nt[0] - 1)

    def fblk(i, f, nt):
        return jnp.where(i < nt[0], f, nf - 1)

    grid_spec = pltpu.PrefetchScalarGridSpec(
        num_scalar_prefetch=2,
        grid=(n_tiles, nf),
        in_specs=[tile_spec(lambda i, f, te, nt: (live(i, nt), 0)),
                  pl.BlockSpec((1, d, tf), lambda i, f, te, nt: (te[i], 0, fblk(i, f, nt))),
                  pl.BlockSpec((1, d, tf), lambda i, f, te, nt: (te[i], 0, fblk(i, f, nt))),
                  pl.BlockSpec((1, tf, d), lambda i, f, te, nt: (te[i], fblk(i, f, nt), 0))],
        out_specs=tile_spec(lambda i, f, te, nt: (i, 0)),
        scratch_shapes=[pltpu.VMEM((MOE_TILE, d), BF16), pltpu.VMEM((MOE_TILE, d), F32)],
    )
    return pl.pallas_call(
        _moe_ffn_kernel,
        grid_spec=grid_spec,
        out_shape=jax.ShapeDtypeStruct(xs.shape, F32),
        compiler_params=_params("arbitrary", "arbitrary"),
        name="moe_ffn",
    )(tile_expert, n_tiles_used, xs, wg, wu, wd)


def _combine_kernel(pos_ref, x_ref, gate_ref, g_ref, b_ref, ys_ref, o_ref, buf_ref, sem, *, tb):
    i = pl.program_id(0)
    slot = i % 2

    def start_gathers(block, into):
        base = block * (TOP_K * tb)

        def issue(t, _):
            for k in range(TOP_K):
                pltpu.make_async_copy(ys_ref.at[_tile_rows(pos_ref[base + TOP_K * t + k])],
                                      buf_ref.at[into, k, _tile_rows(t)], sem.at[into]).start()
            return 0

        lax.fori_loop(0, tb, issue, 0, unroll=DMA_ISSUE_UNROLL)

    @pl.when(i == 0)
    def _():
        start_gathers(0, 0)

    @pl.when(i + 1 < pl.num_programs(0))
    def _():
        start_gathers(i + 1, 1 - slot)

    for k in range(TOP_K):
        pltpu.make_async_copy(ys_ref.at[pl.ds(0, tb * ROW_CHUNKS)], buf_ref.at[slot, k],
                              sem.at[slot]).wait()
    gates = gate_ref[...]
    y = jnp.concatenate(
        [buf_ref[slot, 0, pl.ds(j, tb, stride=ROW_CHUNKS), :] * gates[:, 0:1]
         + buf_ref[slot, 1, pl.ds(j, tb, stride=ROW_CHUNKS), :] * gates[:, 1:2]
         for j in range(ROW_CHUNKS)], axis=1)
    o_ref[...] = _layer_norm_rows(DN_ALPHA * x_ref[...] + y, g_ref[...], b_ref[...])


def moe_combine(x, ys, pos_flat, gates, g, b):
    t, d = x.shape
    tb = min(ROUTE_BLOCK, t)
    grid_spec = pltpu.PrefetchScalarGridSpec(
        num_scalar_prefetch=1,
        grid=(t // tb,),
        in_specs=[pl.BlockSpec((tb, d), lambda i, *_: (i, 0)),
                  pl.BlockSpec((tb, LANES), lambda i, *_: (i, 0)),
                  pl.BlockSpec((1, d), lambda i, *_: (0, 0)),
                  pl.BlockSpec((1, d), lambda i, *_: (0, 0)),
                  pl.BlockSpec(memory_space=pl.ANY)],
        out_specs=pl.BlockSpec((tb, d), lambda i, *_: (i, 0)),
        scratch_shapes=[pltpu.VMEM((2, TOP_K, tb * ROW_CHUNKS, LANES), F32),
                        pltpu.SemaphoreType.DMA((2,))],
    )
    return pl.pallas_call(
        functools.partial(_combine_kernel, tb=tb),
        grid_spec=grid_spec,
        out_shape=jax.ShapeDtypeStruct((t, d), F32),
        compiler_params=_params("arbitrary"),
        name="moe_combine",
    )(pos_flat, x, gates, g.reshape(1, d), b.reshape(1, d), ys)


def moe_deepnorm(x, x_tiled, w_router, wg, wu, wd, g, b):
    t = x.shape[0]
    idx, gates = moe_router(x, w_router)
    rank, counts_f = moe_rank(idx)
    counts = counts_f[0, :N_EXPERTS].astype(jnp.int32)
    padded = (counts + MOE_TILE - 1) // MOE_TILE * MOE_TILE
    ends = jnp.cumsum(padded)
    offsets = ends - padded
    n_tiles = (TOP_K * t) // MOE_TILE + N_EXPERTS
    n_used = (ends[-1] // MOE_TILE).astype(jnp.int32)
    tile_start = jnp.arange(n_tiles, dtype=jnp.int32) * MOE_TILE
    tile_start = jnp.minimum(tile_start, ends[-1] - MOE_TILE)
    tile_expert = jnp.sum(tile_start[:, None] >= ends[None, :], axis=1).astype(jnp.int32)
    pos = moe_positions(idx, rank, offsets)
    pos_flat = pos[:, :TOP_K].reshape(-1)
    xs = moe_dispatch(x_tiled, pos_flat, counts, offsets, n_tiles * MOE_TILE)
    ys = moe_ffn(xs, tile_expert, n_used.reshape(1), wg.astype(BF16), wu.astype(BF16), wd.astype(BF16))
    return moe_combine(x, ys, pos_flat, gates, g, b)


def kernel(x, l0_ssd_w_in, l0_ssd_conv_w, l0_ssd_conv_b, l0_ssd_dt_bias, l0_ssd_a_log, l0_ssd_d_skip, l0_ssd_norm_w, l0_ssd_w_out, l0_ln_mix_g, l0_ln_mix_b, l0_ffn_w_gate, l0_ffn_w_up, l0_ffn_w_down, l0_ln_ffn_g, l0_ln_ffn_b, l1_sb_w_qkv, l1_sb_w_out, l1_ln_mix_g, l1_ln_mix_b, l1_moe_w_router, l1_moe_w_gate, l1_moe_w_up, l1_moe_w_down, l1_ln_ffn_g, l1_ln_ffn_b, l2_fox_w_qkvf, l2_fox_b_f, l2_fox_w_out, l2_ln_mix_g, l2_ln_mix_b, l2_ffn_w_gate, l2_ffn_w_up, l2_ffn_w_down, l2_ln_ffn_g, l2_ln_ffn_b, l3_ssd_w_in, l3_ssd_conv_w, l3_ssd_conv_b, l3_ssd_dt_bias, l3_ssd_a_log, l3_ssd_d_skip, l3_ssd_norm_w, l3_ssd_w_out, l3_ln_mix_g, l3_ln_mix_b, l3_moe_w_router, l3_moe_w_gate, l3_moe_w_up, l3_moe_w_down, l3_ln_ffn_g, l3_ln_ffn_b):
    batch, seq, d = x.shape
    h = x.reshape(batch * seq, d)
    bf = lambda w: w.astype(BF16)
    h = ssd_mixer_deepnorm(h, l0_ssd_w_in, l0_ssd_conv_w, l0_ssd_conv_b, l0_ssd_dt_bias, l0_ssd_a_log,
                           l0_ssd_d_skip, l0_ssd_norm_w, l0_ssd_w_out, l0_ln_mix_g, l0_ln_mix_b, batch, seq)
    h = ffn_deepnorm(h, bf(l0_ffn_w_gate), bf(l0_ffn_w_up), bf(l0_ffn_w_down), l0_ln_ffn_g, l0_ln_ffn_b)
    h, h_tiled = sb_mixer_deepnorm(h, l1_sb_w_qkv, l1_sb_w_out, l1_ln_mix_g, l1_ln_mix_b, batch, seq,
                                   also_row_tiled=True)
    h = moe_deepnorm(h, h_tiled, l1_moe_w_router, l1_moe_w_gate, l1_moe_w_up, l1_moe_w_down, l1_ln_ffn_g, l1_ln_ffn_b)
    h = fox_mixer_deepnorm(h, l2_fox_w_qkvf, l2_fox_b_f, l2_fox_w_out, l2_ln_mix_g, l2_ln_mix_b, batch, seq)
    h = ffn_deepnorm(h, bf(l2_ffn_w_gate), bf(l2_ffn_w_up), bf(l2_ffn_w_down), l2_ln_ffn_g, l2_ln_ffn_b)
    h, h_tiled = ssd_mixer_deepnorm(h, l3_ssd_w_in, l3_ssd_conv_w, l3_ssd_conv_b, l3_ssd_dt_bias, l3_ssd_a_log,
                                    l3_ssd_d_skip, l3_ssd_norm_w, l3_ssd_w_out, l3_ln_mix_g, l3_ln_mix_b,
                                    batch, seq, also_row_tiled=True)
    h = moe_deepnorm(h, h_tiled, l3_moe_w_router, l3_moe_w_gate, l3_moe_w_up, l3_moe_w_down, l3_ln_ffn_g, l3_ln_ffn_b)
    return h.reshape(batch, seq, d)
```

```python
import functools
import math

import jax
import jax.numpy as jnp
from jax import lax
from jax.experimental import pallas as pl
from jax.experimental.pallas import tpu as pltpu

F32 = jnp.float32
BF16 = jnp.bfloat16
HIGHEST = lax.Precision.HIGHEST

LANES = 128
SUBLANES = 8
VMEM_LIMIT_BYTES = 56 * 1024 * 1024

D_MODEL = 1024
DEPTH = 4
SSD_D_INNER = 2048
SSD_HEAD_DIM = 64
SSD_HEADS = 32
SSD_GROUPS = 4
SSD_STATE = 128
SSD_CONV = 4
SSD_GROUP_DIM = SSD_D_INNER // SSD_GROUPS
SSD_BC_DIM = 2 * SSD_GROUPS * SSD_STATE
ATT_HEAD_DIM = 64
ATT_HEADS = 16
ATT_DIM = 1024
N_EXPERTS = 8
TOP_K = 2
DN_ALPHA = (2.0 * DEPTH) ** 0.25
LN_EPS = 1e-5
RMS_EPS = 1e-5
NEG_BIG = -1e30


def _params(*semantics):
    return pltpu.CompilerParams(dimension_semantics=semantics, vmem_limit_bytes=VMEM_LIMIT_BYTES)


def _layer_norm_rows(h, g, b):
    mu = jnp.mean(h, axis=-1, keepdims=True)
    d = h - mu
    var = jnp.mean(d * d, axis=-1, keepdims=True)
    return d * lax.rsqrt(var + LN_EPS) * g + b


def _silu(x):
    half = 0.5 * x
    return half * (1.0 + jnp.tanh(half))


def _mm_kernel(x_ref, w_ref, o_ref, xb_ref):
    @pl.when(pl.program_id(1) == 0)
    def _():
        xb_ref[...] = x_ref[...].astype(BF16)

    o_ref[...] = jnp.dot(xb_ref[...], w_ref[...], preferred_element_type=F32).astype(o_ref.dtype)


def matmul(x, w, out_dtype, tm=1024, tn=1024):
    m, k = x.shape
    tm = min(tm, m)
    n = w.shape[1]
    return pl.pallas_call(
        _mm_kernel,
        grid=(m // tm, n // tn),
        in_specs=[pl.BlockSpec((tm, k), lambda i, j: (i, 0)),
                  pl.BlockSpec((k, tn), lambda i, j: (0, j))],
        out_specs=pl.BlockSpec((tm, tn), lambda i, j: (i, j)),
        out_shape=jax.ShapeDtypeStruct((m, n), out_dtype),
        scratch_shapes=[pltpu.VMEM((tm, k), BF16)],
        compiler_params=_params("parallel", "arbitrary"),
        name="matmul",
    )(x, w)


def _dot_split(x, w):
    xh = x.astype(BF16)
    xl = (x - xh.astype(F32)).astype(BF16)
    wh = w.astype(BF16)
    wl = (w - wh.astype(F32)).astype(BF16)
    dot = functools.partial(jnp.dot, preferred_element_type=F32)
    return dot(xh, wh) + (dot(xh, wl) + dot(xl, wh))


def _mm_f32_kernel(x_ref, w_ref, o_ref):
    o_ref[...] = _dot_split(x_ref[...], w_ref[...])


def matmul_f32(x, w, tm=1024):
    m, k = x.shape
    tm = min(tm, m)
    n = w.shape[1]
    return pl.pallas_call(
        _mm_f32_kernel,
        grid=(m // tm,),
        in_specs=[pl.BlockSpec((tm, k), lambda i: (i, 0)),
                  pl.BlockSpec((k, n), lambda i: (0, 0))],
        out_specs=pl.BlockSpec((tm, n), lambda i: (i, 0)),
        out_shape=jax.ShapeDtypeStruct((m, n), F32),
        compiler_params=_params("parallel"),
        name="matmul_f32",
    )(x, w)


ROW_CHUNKS = D_MODEL // LANES


def _store_row_tiled(ref, value):
    n = value.shape[0]
    for j in range(ROW_CHUNKS):
        ref[pl.ds(j, n, stride=ROW_CHUNKS), :] = value[:, j * LANES:(j + 1) * LANES]


def _load_row_tiled(ref, j, n):
    return ref[pl.ds(j, n, stride=ROW_CHUNKS), :]


def _mm_ln_route_kernel(x_ref, w_ref, r_ref, g_ref, b_ref, wr_ref,
                        o_ref, tiled_ref, idx_ref, gate_ref, rank_ref, count_ref, run_ref):
    @pl.when(pl.program_id(0) == 0)
    def _():
        run_ref[...] = jnp.zeros_like(run_ref)

    y = jnp.dot(x_ref[...], w_ref[...], preferred_element_type=F32)
    o = _layer_norm_rows(DN_ALPHA * r_ref[...] + y, g_ref[...], b_ref[...])
    o_ref[...] = o
    _store_row_tiled(tiled_ref, o)
    idx, gates = _top2(_dot_split(o, wr_ref[...]))
    idx_ref[...] = idx
    gate_ref[...] = gates
    rank_ref[...] = _rank_block(idx, run_ref)
    count_ref[...] = run_ref[...]


def matmul_deepnorm_route(x, w, resid, g, b, w_router):
    m, k = x.shape
    tm = min(ROUTE_BLOCK, m)
    d = w.shape[1]
    rows = lambda width: pl.BlockSpec((tm, width), lambda i: (i, 0))
    fixed = lambda a: pl.BlockSpec(a.shape, lambda i: (0, 0))
    args = [x, w, resid, g.reshape(1, d), b.reshape(1, d),
            jnp.pad(w_router, ((0, 0), (0, LANES - N_EXPERTS)))]
    return pl.pallas_call(
        _mm_ln_route_kernel,
        grid=(m // tm,),
        in_specs=[rows(k), fixed(w), rows(d), fixed(args[3]), fixed(args[4]), fixed(args[5])],
        out_specs=[rows(d), pl.BlockSpec((tm * ROW_CHUNKS, LANES), lambda i: (i, 0)),
                   rows(LANES), rows(LANES), rows(LANES), pl.BlockSpec((1, LANES), lambda i: (0, 0))],
        out_shape=[jax.ShapeDtypeStruct((m, d), F32),
                   jax.ShapeDtypeStruct((m * ROW_CHUNKS, LANES), F32),
                   jax.ShapeDtypeStruct((m, LANES), jnp.int32),
                   jax.ShapeDtypeStruct((m, LANES), F32),
                   jax.ShapeDtypeStruct((m, LANES), jnp.int32),
                   jax.ShapeDtypeStruct((1, LANES), F32)],
        scratch_shapes=[pltpu.VMEM((1, LANES), F32)],
        compiler_params=_params("arbitrary"),
        name="matmul_deepnorm_route",
    )(*args)


def _mix_ffn_kernel(o_ref, wo_ref, r_ref, g1_ref, b1_ref, wg_ref, wu_ref, wd_ref, g2_ref, b2_ref, out_ref):
    x = _layer_norm_rows(
        DN_ALPHA * r_ref[...] + jnp.dot(o_ref[...], wo_ref[...], preferred_element_type=F32),
        g1_ref[...], b1_ref[...])
    xb = x.astype(BF16)
    gate = jnp.dot(xb, wg_ref[...], preferred_element_type=F32)
    up = jnp.dot(xb, wu_ref[...], preferred_element_type=F32)
    h = (_silu(gate) * up).astype(BF16)
    y = jnp.dot(h, wd_ref[...], preferred_element_type=F32)
    out_ref[...] = _layer_norm_rows(DN_ALPHA * x + y, g2_ref[...], b2_ref[...])


def mixer_out_ffn_deepnorm(o, w_out, resid, g1, b1, wg, wu, wd, g2, b2, tm=512):
    m, k = o.shape
    d = w_out.shape[1]
    tm = min(tm, m)
    resident = lambda a: pl.BlockSpec(a.shape, lambda i: (0, 0), pipeline_mode=pl.Buffered(1))
    vec = lambda v: v.reshape(1, d)
    args = (o, w_out, resid, vec(g1), vec(b1), wg, wu, wd, vec(g2), vec(b2))
    return pl.pallas_call(
        _mix_ffn_kernel,
        grid=(m // tm,),
        in_specs=[pl.BlockSpec((tm, k), lambda i: (i, 0)), resident(w_out),
                  pl.BlockSpec((tm, d), lambda i: (i, 0)), resident(args[3]), resident(args[4]),
                  resident(wg), resident(wu), resident(wd), resident(args[8]), resident(args[9])],
        out_specs=pl.BlockSpec((tm, d), lambda i: (i, 0)),
        out_shape=jax.ShapeDtypeStruct((m, d), F32),
        compiler_params=_params("parallel"),
        name="mixer_out_ffn_deepnorm",
    )(*args)


def _split_bf16(v, pieces):
    out = []
    r = v
    for _ in range(pieces - 1):
        p = r.astype(BF16)
        out.append(p)
        r = r - p.astype(F32)
    out.append(r.astype(BF16))
    return out


def _expand(v, e_ref, pieces):
    stacked = jnp.concatenate(_split_bf16(v, pieces), axis=1)
    return jnp.dot(stacked, e_ref[...], preferred_element_type=F32)


def _tril_f32(n, strict=False):
    r = lax.broadcasted_iota(jnp.int32, (n, n), 0)
    c = lax.broadcasted_iota(jnp.int32, (n, n), 1)
    return ((r > c) if strict else (r >= c)).astype(F32)


def _ssd_kernel(z_ref, xs_ref, bc_ref, dt_ref,
                dtb_ref, alog_ref, dskip_ref, normw_ref, e64_ref, e128_ref, o_ref,
                state_ref, xdt_ref, acol_ref, arow_ref, cb_ref, ydiag_ref,
                *, chunk):
    L = chunk
    c = pl.program_id(1)

    @pl.when(c == 0)
    def _():
        state_ref[...] = jnp.zeros_like(state_ref)

    xs = xs_ref[...]
    bcv = bc_ref[...]

    dt = jax.nn.softplus(dt_ref[...] + dtb_ref[...])
    da = dt * (-jnp.exp(alog_ref[...]))
    a_cs = jnp.dot(_tril_f32(L), da, preferred_element_type=F32, precision=HIGHEST)
    ea = jnp.exp(a_cs)
    dte = jnp.exp(a_cs[L - 1:L, :] - a_cs)

    dt_x = _expand(dt, e64_ref, 2)
    ea_x = _expand(ea, e64_ref, 2)
    dte_x = _expand(dte, e64_ref, 2)
    acol_ref[...] = _expand(a_cs, e128_ref, 3)
    a_t = a_cs.T
    for h in range(SSD_HEADS):
        arow_ref[h] = jnp.broadcast_to(a_t[h:h + 1, :], (SUBLANES, L))

    xdt = xs * dt_x
    xdt_ref[...] = xdt.astype(BF16)
    xdte = (xdt * dte_x).astype(BF16)

    y_off = []
    for g in range(SSD_GROUPS):
        bm = bcv[:, g * SSD_STATE:(g + 1) * SSD_STATE]
        cm = bcv[:, (SSD_GROUPS + g) * SSD_STATE:(SSD_GROUPS + g + 1) * SSD_STATE].astype(BF16)
        cb_ref[g] = lax.dot_general(cm, bm.astype(BF16), (((1,), (1,)), ((), ())),
                                    preferred_element_type=F32)
        gs = slice(g * SSD_GROUP_DIM, (g + 1) * SSD_GROUP_DIM)
        st = state_ref[g]
        y_off.append(jnp.dot(cm, st.astype(BF16), preferred_element_type=F32) * ea_x[:, gs])
        state_ref[g] = st * ea_x[L - 1:L, gs] + jnp.dot(
            bm.T.astype(BF16), xdte[:, gs], preferred_element_type=F32)

    row = lax.broadcasted_iota(jnp.int32, (L, L), 0)
    col = lax.broadcasted_iota(jnp.int32, (L, L), 1)
    causal = row >= col
    head0 = lax.broadcasted_iota(jnp.int32, (L, LANES), 1) < SSD_HEAD_DIM

    heads_per_group = SSD_HEADS // SSD_GROUPS

    def group_body(g, carry):
        cbg = cb_ref[g]
        scores = []
        for j in range(heads_per_group):
            h = g * heads_per_group + j
            a_l = acol_ref[:, pl.ds(pl.multiple_of(h * LANES, LANES), LANES)]
            a_l = jnp.concatenate([a_l] * (L // LANES), axis=1)
            a_s = arow_ref[h][0:1, :]
            scores.append((cbg * jnp.exp(jnp.where(causal, a_l - a_s, NEG_BIG))).astype(BF16))
        for pp in range(heads_per_group // 2):
            lanes = pl.ds(pl.multiple_of((g * (heads_per_group // 2) + pp) * LANES, LANES), LANES)
            xpair = xdt_ref[:, lanes]
            ys = [jnp.dot(scores[2 * pp + j], xpair, preferred_element_type=F32) for j in range(2)]
            ydiag_ref[:, lanes] = jnp.where(head0, ys[0], ys[1])
        return carry

    lax.fori_loop(0, SSD_GROUPS, group_body, 0)

    y = ydiag_ref[...] + jnp.concatenate(y_off, axis=1) + dskip_ref[...] * xs
    y = y * _silu(z_ref[...])
    parts = []
    for g in range(SSD_GROUPS):
        yg = y[:, g * SSD_GROUP_DIM:(g + 1) * SSD_GROUP_DIM]
        parts.append(yg * lax.rsqrt(jnp.mean(yg * yg, axis=-1, keepdims=True) + RMS_EPS))
    o_ref[...] = (jnp.concatenate(parts, axis=1) * normw_ref[...]).astype(o_ref.dtype)


def _expansion_matrix(width, pieces):
    h = jnp.arange(LANES)[:, None]
    lane = jnp.arange(SSD_HEADS * width)[None, :]
    e = (lane // width == h).astype(BF16)
    return jnp.concatenate([e] * pieces, axis=0)


def _mm_conv_kernel(x_ref, halo_ref, w_ref, cw_ref, cb_ref, o_ref, ext_ref, xb_ref, *, tiles_per_seq):
    tm = x_ref.shape[0]

    @pl.when(pl.program_id(1) == 0)
    def _():
        xb_ref[...] = x_ref[...].astype(BF16)

    w = w_ref[...]
    halo = jnp.dot(halo_ref[...].astype(BF16), w, preferred_element_type=F32)
    first = pl.program_id(0) % tiles_per_seq == 0
    ext_ref[0:SUBLANES, :] = jnp.where(first, 0.0, halo)
    ext_ref[SUBLANES:SUBLANES + tm, :] = jnp.dot(xb_ref[...], w, preferred_element_type=F32)
    acc = cb_ref[...]
    for k in range(SSD_CONV):
        start = SUBLANES - (SSD_CONV - 1) + k
        acc = acc + cw_ref[k:k + 1, :] * ext_ref[start:start + tm, :]
    o_ref[...] = _silu(acc)


def matmul_conv_silu(x, w, conv_w, conv_b, seq, tm=1024, tn=1024):
    m, k = x.shape
    tm = min(tm, seq)
    n = w.shape[1]
    halo_blocks = tm // SUBLANES
    return pl.pallas_call(
        functools.partial(_mm_conv_kernel, tiles_per_seq=seq // tm),
        grid=(m // tm, n // tn),
        in_specs=[pl.BlockSpec((tm, k), lambda i, j: (i, 0)),
                  pl.BlockSpec((SUBLANES, k), lambda i, j: (jnp.maximum(i * halo_blocks - 1, 0), 0)),
                  pl.BlockSpec((k, tn), lambda i, j: (0, j)),
                  pl.BlockSpec((SSD_CONV, tn), lambda i, j: (0, j)),
                  pl.BlockSpec((1, tn), lambda i, j: (0, j))],
        out_specs=pl.BlockSpec((tm, tn), lambda i, j: (i, j)),
        out_shape=jax.ShapeDtypeStruct((m, n), F32),
        scratch_shapes=[pltpu.VMEM((tm + SUBLANES, tn), F32), pltpu.VMEM((tm, k), BF16)],
        compiler_params=_params("parallel", "arbitrary"),
        name="matmul_conv_silu",
    )(x, x, w, conv_w, conv_b.reshape(1, n))


def ssd_core(z, xbc, dt_raw, dt_bias, a_log, d_skip, norm_w, batch, seq, chunk=128):
    t = z.shape[0]
    nc = seq // chunk
    pad = LANES - SSD_HEADS
    row = lambda v: v.reshape(1, -1).astype(F32)
    args = (
        z, xbc, xbc, dt_raw,
        row(jnp.pad(dt_bias, (0, pad))), row(jnp.pad(a_log, (0, pad))),
        row(jnp.repeat(d_skip, SSD_HEAD_DIM)), row(norm_w),
        _expansion_matrix(SSD_HEAD_DIM, 2), _expansion_matrix(LANES, 3),
    )
    blk = lambda b, c: (b * nc + c, 0)
    const = lambda b, c: (0, 0)
    full = lambda a: pl.BlockSpec(a.shape, const)
    in_specs = [
        pl.BlockSpec((chunk, SSD_D_INNER), blk),
        pl.BlockSpec((chunk, SSD_D_INNER), blk),
        pl.BlockSpec((chunk, SSD_BC_DIM), lambda b, c: (b * nc + c, 2)),
        pl.BlockSpec((chunk, LANES), blk),
    ] + [full(a) for a in args[4:]]
    return pl.pallas_call(
        functools.partial(_ssd_kernel, chunk=chunk),
        grid=(batch, nc),
        in_specs=in_specs,
        out_specs=pl.BlockSpec((chunk, SSD_D_INNER), blk),
        out_shape=jax.ShapeDtypeStruct((t, SSD_D_INNER), BF16),
        scratch_shapes=[
            pltpu.VMEM((SSD_GROUPS, SSD_STATE, SSD_GROUP_DIM), F32),
            pltpu.VMEM((chunk, SSD_D_INNER), BF16),
            pltpu.VMEM((chunk, SSD_HEADS * LANES), F32),
            pltpu.VMEM((SSD_HEADS, SUBLANES, chunk), F32),
            pltpu.VMEM((SSD_GROUPS, chunk, chunk), F32),
            pltpu.VMEM((chunk, SSD_D_INNER), F32),
        ],
        compiler_params=_params("parallel", "arbitrary"),
        name="ssd_core",
    )(*args)


def ssd_mixer(x, w_in, conv_w, conv_b, dt_bias, a_log, d_skip, norm_w, batch, seq):
    n_zx = SSD_D_INNER + SSD_D_INNER + SSD_BC_DIM
    z = matmul(x, w_in[:, :SSD_D_INNER].astype(BF16), F32)
    xbc = matmul_conv_silu(x, w_in[:, SSD_D_INNER:n_zx].astype(BF16), conv_w, conv_b, seq)
    w_dt = jnp.pad(w_in[:, n_zx:], ((0, 0), (0, LANES - SSD_HEADS)))
    dt_raw = matmul_f32(x, w_dt)
    return ssd_core(z, xbc, dt_raw, dt_bias, a_log, d_skip, norm_w, batch, seq)


ATT_BLOCK = 128
ATT_Q = 256
ATT_PAIRS = 4
ATT_STEP_LANES = ATT_PAIRS * LANES
ATT_SCALE = ATT_HEAD_DIM ** -0.5
LOG2E = 1.4426950408889634


def _split_heads(q_ref, rows, head0):
    qs = []
    for p in range(ATT_PAIRS):
        q2 = q_ref[rows, p * LANES:(p + 1) * LANES]
        zero = jnp.zeros_like(q2)
        qs += [jnp.where(head0, q2, zero), jnp.where(head0, zero, q2)]
    return qs


def _sb_kernel(q_ref, k_ref, v_ref, u_ref, o_ref, *scratch, seq):
    n_heads = 2 * ATT_PAIRS
    later_ref, acc_ref, z_ref = scratch[:n_heads], scratch[n_heads:2 * n_heads], scratch[2 * n_heads:]
    nq = seq // ATT_Q
    ndiag = ATT_Q // ATT_BLOCK
    head0 = lax.broadcasted_iota(jnp.int32, (ATT_Q, LANES), 1) < ATT_HEAD_DIM
    row = lax.broadcasted_iota(jnp.int32, (ATT_Q, ATT_BLOCK), 0)
    col = lax.broadcasted_iota(jnp.int32, (ATT_Q, ATT_BLOCK), 1)

    def q_body(i, _):
        q0 = pl.multiple_of(i * ATT_Q, ATT_Q)
        rows = pl.ds(q0, ATT_Q)
        qs = _split_heads(q_ref, rows, head0)
        for n in range(2 * ATT_PAIRS):
            later_ref[n][...] = jnp.zeros((ATT_Q, LANES), F32)
            acc_ref[n][...] = jnp.zeros((ATT_Q, LANES), F32)

        heads = range(2 * ATT_PAIRS)

        def key_rows(j):
            return pl.ds(pl.multiple_of(j * ATT_BLOCK, ATT_BLOCK), ATT_BLOCK)

        def scores(j):
            k2 = [k_ref[key_rows(j), p * LANES:(p + 1) * LANES] for p in range(ATT_PAIRS)]
            return [lax.dot_general(qs[n], k2[n // 2], (((1,), (1,)), ((), ())),
                                    preferred_element_type=F32) * (ATT_SCALE * LOG2E) for n in heads]

        def consume(z2, j, strict):
            keys = key_rows(j)
            v2 = [v_ref[keys, p * LANES:(p + 1) * LANES] for p in range(ATT_PAIRS)]
            log_beta, sums = [], []
            for n in heads:
                lb = jnp.minimum(z2[n], 0.0) - jnp.log2(1.0 + jnp.exp2(-jnp.abs(z2[n])))
                log_keep = lb - z2[n]
                if strict is not None:
                    log_keep = jnp.where(strict, log_keep, 0.0)
                hi = log_keep.astype(BF16)
                lo = (log_keep - hi.astype(F32)).astype(BF16)
                log_beta.append(lb)
                sums.append(jnp.dot(jnp.concatenate([hi, lo], axis=1), u_ref[...],
                                    preferred_element_type=F32))
            for n in heads:
                w = jnp.exp2(log_beta[n] + sums[n][:, :ATT_BLOCK] + later_ref[n][...])
                if strict is not None:
                    w = jnp.where(strict, w, 0.0)
                acc_ref[n][...] += jnp.dot(w.astype(BF16), v2[n // 2], preferred_element_type=F32)
                later_ref[n][...] += sums[n][:, ATT_BLOCK:]

        for d in reversed(range(ndiag)):
            consume(scores(i * ndiag + d), i * ndiag + d, d * ATT_BLOCK + col < row)

        n_off = i * ndiag
        for n, zn in enumerate(scores(jnp.maximum(n_off - 1, 0))):
            z_ref[n][...] = zn

        def off_diagonal(jj, c):
            z2 = [z_ref[n][...] for n in heads]
            ahead = scores(jnp.maximum(n_off - 2 - jj, 0))
            consume(z2, n_off - 1 - jj, None)
            for n in heads:
                z_ref[n][...] = ahead[n]
            return c

        lax.fori_loop(0, n_off, off_diagonal, 0)
        for p in range(ATT_PAIRS):
            o_ref[rows, p * LANES:(p + 1) * LANES] = jnp.where(
                head0, acc_ref[2 * p][...], acc_ref[2 * p + 1][...]).astype(o_ref.dtype)
        return 0

    lax.fori_loop(0, nq, q_body, 0)


def _suffix_sum_matrix():
    j = jnp.arange(2 * ATT_BLOCK)[:, None] % ATT_BLOCK
    s = jnp.arange(2 * ATT_BLOCK)[None, :]
    return jnp.where(s < ATT_BLOCK, j > s, True).astype(BF16)


def _attention_specs(seq):
    n_steps = ATT_HEADS // 2 // ATT_PAIRS
    blk = lambda off: pl.BlockSpec((seq, ATT_STEP_LANES), lambda b, p: (b, off + p))
    specs = [blk(0), blk(n_steps), blk(2 * n_steps)]
    return n_steps, specs, pl.BlockSpec((seq, ATT_STEP_LANES), lambda b, p: (b, p))


def sb_attention(qkv, batch, seq):
    n_steps, in_specs, out_spec = _attention_specs(seq)
    u = _suffix_sum_matrix()
    return pl.pallas_call(
        functools.partial(_sb_kernel, seq=seq),
        grid=(batch, n_steps),
        in_specs=in_specs + [pl.BlockSpec(u.shape, lambda b, p: (0, 0))],
        out_specs=out_spec,
        out_shape=jax.ShapeDtypeStruct((batch * seq, ATT_DIM), BF16),
        scratch_shapes=[pltpu.VMEM((ATT_Q, LANES), F32)] * (6 * ATT_PAIRS),
        compiler_params=_params("parallel", "parallel"),
        name="sb_attention",
    )(qkv, qkv, qkv, u)


def sb_mixer(x, w_qkv, batch, seq):
    return sb_attention(matmul(x, w_qkv.astype(BF16), BF16), batch, seq)


CUMSUM_BLOCK = 256


def _fox_decay_kernel(f_ref, bf_ref, ccol_ref, crow_ref, *, seq):
    tri = _tril_f32(CUMSUM_BLOCK)
    carry = jnp.zeros((1, LANES), F32)
    for blk in range(seq // CUMSUM_BLOCK):
        rows = slice(blk * CUMSUM_BLOCK, (blk + 1) * CUMSUM_BLOCK)
        log_f = jax.nn.log_sigmoid(f_ref[rows, :] + bf_ref[...])
        c = jnp.dot(tri, log_f, preferred_element_type=F32, precision=HIGHEST) + carry
        carry = c[CUMSUM_BLOCK - 1:CUMSUM_BLOCK, :]
        ccol_ref[rows, :] = c
        c_t = c.T
        for p in range(ATT_HEADS // 2):
            crow_ref[0, p, :, rows] = c_t[2 * p:2 * p + 2, :]


def fox_decay(f_raw, b_f, batch, seq):
    return pl.pallas_call(
        functools.partial(_fox_decay_kernel, seq=seq),
        grid=(batch,),
        in_specs=[pl.BlockSpec((seq, LANES), lambda b: (b, 0)),
                  pl.BlockSpec((1, LANES), lambda b: (0, 0))],
        out_specs=[pl.BlockSpec((seq, LANES), lambda b: (b, 0)),
                   pl.BlockSpec((1, ATT_HEADS // 2, 2, seq), lambda b: (b, 0, 0, 0))],
        out_shape=[jax.ShapeDtypeStruct((batch * seq, LANES), F32),
                   jax.ShapeDtypeStruct((batch, ATT_HEADS // 2, 2, seq), F32)],
        compiler_params=_params("parallel"),
        name="fox_decay",
    )(f_raw, jnp.pad(b_f, (0, LANES - ATT_HEADS)).reshape(1, LANES))


def _fox_kernel(q_ref, k_ref, v_ref, ccol_ref, crow_ref, o_ref, crep_ref, vt_ref, *scratch, seq):
    n_heads = 2 * ATT_PAIRS
    m_ref, acc_ref, s_ref = scratch[:n_heads], scratch[n_heads:2 * n_heads], scratch[2 * n_heads:]
    nq = seq // ATT_Q
    ndiag = ATT_Q // ATT_BLOCK
    step = pl.program_id(1)
    heads = range(n_heads)
    head0 = lax.broadcasted_iota(jnp.int32, (ATT_Q, LANES), 1) < ATT_HEAD_DIM
    head0_rows = lax.broadcasted_iota(jnp.int32, (LANES, ATT_BLOCK), 0) < ATT_HEAD_DIM
    key = lax.broadcasted_iota(jnp.int32, (ATT_BLOCK, ATT_Q), 0)
    qry = lax.broadcasted_iota(jnp.int32, (ATT_BLOCK, ATT_Q), 1)

    pieces = jnp.concatenate(_split_bf16(ccol_ref[...], 3), axis=1)
    sel_row = lax.broadcasted_iota(jnp.int32, (3 * LANES, LANES), 0) & (LANES - 1)
    for n in heads:
        sel = jnp.where(sel_row == 2 * ATT_PAIRS * step + n, 1.0, 0.0).astype(BF16)
        crep_ref[n] = jnp.dot(pieces, sel, preferred_element_type=F32) * LOG2E
    for p in range(ATT_PAIRS):
        for blk in range(seq // ATT_BLOCK):
            rows = slice(blk * ATT_BLOCK, (blk + 1) * ATT_BLOCK)
            vt_ref[p, :, rows] = v_ref[rows, p * LANES:(p + 1) * LANES].astype(F32).T.astype(BF16)

    def q_body(i, _):
        q0 = pl.multiple_of(i * ATT_Q, ATT_Q)
        rows = pl.ds(q0, ATT_Q)
        qs = _split_heads(q_ref, rows, head0)
        c_q = [crow_ref[0, n // 2, n % 2:n % 2 + 1, rows] * LOG2E for n in heads]
        for n in heads:
            m_ref[n][...] = jnp.full((1, ATT_Q), NEG_BIG, F32)
            acc_ref[n][...] = jnp.zeros((LANES, ATT_Q), F32)

        def key_rows(j):
            return pl.ds(pl.multiple_of(j * ATT_BLOCK, ATT_BLOCK), ATT_BLOCK)

        def scores(j):
            k2 = [k_ref[key_rows(j), p * LANES:(p + 1) * LANES] for p in range(ATT_PAIRS)]
            return [lax.dot_general(k2[n // 2], qs[n], (((1,), (1,)), ((), ())),
                                    preferred_element_type=F32) * (ATT_SCALE * LOG2E) for n in heads]

        def consume(s, j, causal):
            keys = key_rows(j)
            vt = [vt_ref[p, :, keys] for p in range(ATT_PAIRS)]
            for n in heads:
                c_k = crep_ref[n, keys, :]
                sn = s[n] + (c_q[n] - jnp.concatenate([c_k] * (ATT_Q // LANES), axis=1))
                if causal is not None:
                    sn = jnp.where(causal, sn, NEG_BIG)
                m_old = m_ref[n][...]
                m_new = jnp.maximum(m_old, jnp.max(sn, axis=0, keepdims=True))
                prob = jnp.exp2(sn - m_new).astype(BF16)
                one = jnp.ones_like(vt[n // 2])
                v1t = jnp.where(head0_rows, vt[n // 2], one) if n % 2 == 0 else jnp.where(
                    head0_rows, one, vt[n // 2])
                acc_ref[n][...] = jnp.exp2(m_old - m_new) * acc_ref[n][...] + jnp.dot(
                    v1t, prob, preferred_element_type=F32)
                m_ref[n][...] = m_new

        for d in range(ndiag):
            consume(scores(i * ndiag + d), i * ndiag + d, d * ATT_BLOCK + key <= qry)

        n_off = i * ndiag
        for n, sn in enumerate(scores(0)):
            s_ref[n][...] = sn

        def off_diagonal(jj, c):
            s = [s_ref[n][...] for n in heads]
            ahead = scores(jnp.minimum(jj + 1, n_off - 1))
            consume(s, jj, None)
            for n in heads:
                s_ref[n][...] = ahead[n]
            return c

        lax.fori_loop(0, n_off, off_diagonal, 0)
        for p in range(ATT_PAIRS):
            a0, a1 = acc_ref[2 * p][...], acc_ref[2 * p + 1][...]
            o_t = jnp.concatenate([a0[:ATT_HEAD_DIM] / a0[ATT_HEAD_DIM:],
                                   a1[ATT_HEAD_DIM:] / a1[:ATT_HEAD_DIM]], axis=0)
            o_ref[rows, p * LANES:(p + 1) * LANES] = o_t.T.astype(o_ref.dtype)
        return 0

    lax.fori_loop(0, nq, q_body, 0)


def fox_attention(qkv, ccol, crow, batch, seq):
    n_steps, in_specs, out_spec = _attention_specs(seq)
    in_specs += [pl.BlockSpec((seq, LANES), lambda b, p: (b, 0)),
                 pl.BlockSpec((1, ATT_PAIRS, 2, seq), lambda b, p: (b, p, 0, 0))]
    return pl.pallas_call(
        functools.partial(_fox_kernel, seq=seq),
        grid=(batch, n_steps),
        in_specs=in_specs,
        out_specs=out_spec,
        out_shape=jax.ShapeDtypeStruct((batch * seq, ATT_DIM), BF16),
        scratch_shapes=([pltpu.VMEM((2 * ATT_PAIRS, seq, LANES), F32),
                         pltpu.VMEM((ATT_PAIRS, LANES, seq), BF16)]
                        + [pltpu.VMEM((1, ATT_Q), F32)] * (2 * ATT_PAIRS)
                        + [pltpu.VMEM((LANES, ATT_Q), F32)] * (2 * ATT_PAIRS)
                        + [pltpu.VMEM((ATT_BLOCK, ATT_Q), F32)] * (2 * ATT_PAIRS)),
        compiler_params=_params("parallel", "parallel"),
        name="fox_attention",
    )(qkv, qkv, qkv, ccol, crow)


def fox_mixer(x, w_qkvf, b_f, batch, seq):
    qkv = matmul(x, w_qkvf[:, :3 * ATT_DIM].astype(BF16), BF16)
    w_f = jnp.pad(w_qkvf[:, 3 * ATT_DIM:], ((0, 0), (0, LANES - ATT_HEADS)))
    ccol, crow = fox_decay(matmul_f32(x, w_f), b_f, batch, seq)
    return fox_attention(qkv, ccol, crow, batch, seq)


MOE_TILE = 512
ROUTE_BLOCK = 512


def _top2(logits):
    lane = lax.broadcasted_iota(jnp.int32, logits.shape, 1)
    logits = jnp.where(lane < N_EXPERTS, logits, NEG_BIG)
    m1 = jnp.max(logits, axis=-1, keepdims=True)
    i1 = jnp.min(jnp.where(logits == m1, lane, LANES), axis=-1, keepdims=True)
    rest = jnp.where(lane == i1, NEG_BIG, logits)
    m2 = jnp.max(rest, axis=-1, keepdims=True)
    i2 = jnp.min(jnp.where(rest == m2, lane, LANES), axis=-1, keepdims=True)
    e2 = jnp.exp(m2 - m1)
    denom = 1.0 + e2
    idx = jnp.where(lane == 0, i1, jnp.where(lane == 1, i2, 0))
    gates = jnp.where(lane == 0, 1.0 / denom, jnp.where(lane == 1, e2 / denom, 0.0))
    return idx, gates


def _rank_block(idx, run_ref):
    tb = idx.shape[0]
    lane = lax.broadcasted_iota(jnp.int32, idx.shape, 1)
    oh0 = lane == idx[:, 0:1]
    oh1 = lane == idx[:, 1:2]
    both = jnp.where(oh0 | oh1, 1.0, 0.0)
    before = jnp.dot(_tril_f32(tb, strict=True).astype(BF16), both.astype(BF16),
                     preferred_element_type=F32) + run_ref[...]
    r0 = jnp.sum(jnp.where(oh0, before, 0.0), axis=-1, keepdims=True)
    r1 = jnp.sum(jnp.where(oh1, before, 0.0), axis=-1, keepdims=True)
    run_ref[...] += jnp.sum(both, axis=0, keepdims=True)
    return jnp.where(lane == 0, r0, jnp.where(lane == 1, r1, 0.0)).astype(jnp.int32)


def _pos_kernel(idx_ref, rank_ref, off_ref, pos_ref):
    idx = idx_ref[...]
    lane = lax.broadcasted_iota(jnp.int32, idx.shape, 1)
    off = off_ref[...]
    p0 = jnp.sum(jnp.where(lane == idx[:, 0:1], off, 0), axis=-1, keepdims=True)
    p1 = jnp.sum(jnp.where(lane == idx[:, 1:2], off, 0), axis=-1, keepdims=True)
    pos_ref[...] = rank_ref[...] + jnp.where(lane == 0, p0, jnp.where(lane == 1, p1, 0))


def moe_positions(idx, rank, offsets):
    t = idx.shape[0]
    tb = min(4 * ROUTE_BLOCK, t)
    off = jnp.pad(offsets, (0, LANES - N_EXPERTS)).reshape(1, LANES)
    blk = pl.BlockSpec((tb, LANES), lambda i: (i, 0))
    return pl.pallas_call(
        _pos_kernel,
        grid=(t // tb,),
        in_specs=[blk, blk, pl.BlockSpec((1, LANES), lambda i: (0, 0))],
        out_specs=blk,
        out_shape=jax.ShapeDtypeStruct((t, LANES), jnp.int32),
        compiler_params=_params("parallel"),
        name="moe_positions",
    )(idx, rank, off)


DMA_ISSUE_UNROLL = 8


def _tile_rows(row):
    return pl.ds(pl.multiple_of(row * ROW_CHUNKS, ROW_CHUNKS), ROW_CHUNKS)


def _dispatch_kernel(pos_ref, cnt_ref, off_ref, x_ref, xs_ref, zero_ref, sem, pad_sem, *, tb):
    base = pl.program_id(0) * (TOP_K * tb)

    def issue(t, _):
        for k in range(TOP_K):
            pltpu.make_async_copy(x_ref.at[_tile_rows(t)],
                                  xs_ref.at[_tile_rows(pos_ref[base + TOP_K * t + k])], sem).start()
        return 0

    lax.fori_loop(0, tb, issue, 0, unroll=DMA_ISSUE_UNROLL)

    @pl.when(pl.program_id(0) == 0)
    def _():
        zero_ref[...] = jnp.zeros_like(zero_ref)

        def fill_range(first, count):
            def copy(r):
                return pltpu.make_async_copy(zero_ref, xs_ref.at[_tile_rows(first + r)], pad_sem)

            def fill(r, _):
                copy(r).start()
                return 0

            def drain(r, _):
                copy(r).wait()
                return 0

            lax.fori_loop(0, count, fill, 0)
            lax.fori_loop(0, count, drain, 0)

        for e in range(N_EXPERTS):
            fill_range(off_ref[e] + cnt_ref[e], (-cnt_ref[e]) & (MOE_TILE - 1))
        last = N_EXPERTS - 1
        used = off_ref[last] + cnt_ref[last] + ((-cnt_ref[last]) & (MOE_TILE - 1))
        fill_range(used, xs_ref.shape[0] // ROW_CHUNKS - used)

    for _ in range(TOP_K):
        pltpu.make_async_copy(x_ref, xs_ref.at[pl.ds(0, tb * ROW_CHUNKS)], sem).wait()


def moe_dispatch(x_tiled, pos_flat, counts, offsets, n_rows):
    t = x_tiled.shape[0] // ROW_CHUNKS
    tb = min(ROUTE_BLOCK, t)
    grid_spec = pltpu.PrefetchScalarGridSpec(
        num_scalar_prefetch=3,
        grid=(t // tb,),
        in_specs=[pl.BlockSpec((tb * ROW_CHUNKS, LANES), lambda i, *_: (i, 0))],
        out_specs=pl.BlockSpec(memory_space=pl.ANY),
        scratch_shapes=[pltpu.VMEM((ROW_CHUNKS, LANES), F32), pltpu.SemaphoreType.DMA(()),
                        pltpu.SemaphoreType.DMA(())],
    )
    return pl.pallas_call(
        functools.partial(_dispatch_kernel, tb=tb),
        grid_spec=grid_spec,
        out_shape=jax.ShapeDtypeStruct((n_rows * ROW_CHUNKS, LANES), F32),
        compiler_params=_params("arbitrary"),
        name="moe_dispatch",
    )(pos_flat, counts, offsets, x_tiled)


def _moe_ffn_kernel(te_ref, nt_ref, x_ref, wg_ref, wu_ref, wd_ref, o_ref, xb_ref, acc_ref):
    i = pl.program_id(0)
    f = pl.program_id(1)

    @pl.when(i < nt_ref[0])
    def _():
        @pl.when(f == 0)
        def _():
            for j in range(ROW_CHUNKS):
                xb_ref[:, j * LANES:(j + 1) * LANES] = _load_row_tiled(x_ref, j, MOE_TILE).astype(BF16)
            acc_ref[...] = jnp.zeros_like(acc_ref)

        xb = xb_ref[...]
        gate = jnp.dot(xb, wg_ref[0], preferred_element_type=F32)
        up = jnp.dot(xb, wu_ref[0], preferred_element_type=F32)
        h = (_silu(gate) * up).astype(BF16)
        acc_ref[...] += jnp.dot(h, wd_ref[0], preferred_element_type=F32)

        @pl.when(f == pl.num_programs(1) - 1)
        def _():
            _store_row_tiled(o_ref, acc_ref[...])

    @pl.when((i >= nt_ref[0]) & (f == 0))
    def _():
        o_ref[...] = jnp.zeros_like(o_ref)


def moe_ffn(xs, tile_expert, n_tiles_used, wg, wu, wd, tf=1792):
    n_rows, d = xs.shape[0] // ROW_CHUNKS, D_MODEL
    n_tiles = n_rows // MOE_TILE
    tile_spec = lambda index: pl.BlockSpec((MOE_TILE * ROW_CHUNKS, LANES), index)
    fdim = wg.shape[2]
    nf = fdim // tf

    def live(i, nt):
        return jnp.minimum(i, nt[0] - 1)

    def fblk(i, f, nt):
        return jnp.where(i < nt[0], f, nf - 1)

    grid_spec = pltpu.PrefetchScalarGridSpec(
        num_scalar_prefetch=2,
        grid=(n_tiles, nf),
        in_specs=[tile_spec(lambda i, f, te, nt: (live(i, nt), 0)),
                  pl.BlockSpec((1, d, tf), lambda i, f, te, nt: (te[i], 0, fblk(i, f, nt))),
                  pl.BlockSpec((1, d, tf), lambda i, f, te, nt: (te[i], 0, fblk(i, f, nt))),
                  pl.BlockSpec((1, tf, d), lambda i, f, te, nt: (te[i], fblk(i, f, nt), 0))],
        out_specs=tile_spec(lambda i, f, te, nt: (i, 0)),
        scratch_shapes=[pltpu.VMEM((MOE_TILE, d), BF16), pltpu.VMEM((MOE_TILE, d), F32)],
    )
    return pl.pallas_call(
        _moe_ffn_kernel,
        grid_spec=grid_spec,
        out_shape=jax.ShapeDtypeStruct(xs.shape, F32),
        compiler_params=_params("arbitrary", "arbitrary"),
        name="moe_ffn",
    )(tile_expert, n_tiles_used, xs, wg, wu, wd)


def _combine_kernel(pos_ref, x_ref, gate_ref, g_ref, b_ref, ys_ref, o_ref, buf_ref, sem, *, tb):
    i = pl.program_id(0)
    slot = i % 2

    def start_gathers(block, into):
        base = block * (TOP_K * tb)

        def issue(t, _):
            for k in range(TOP_K):
                pltpu.make_async_copy(ys_ref.at[_tile_rows(pos_ref[base + TOP_K * t + k])],
                                      buf_ref.at[into, k, _tile_rows(t)], sem.at[into]).start()
            return 0

        lax.fori_loop(0, tb, issue, 0, unroll=DMA_ISSUE_UNROLL)

    @pl.when(i == 0)
    def _():
        start_gathers(0, 0)

    @pl.when(i + 1 < pl.num_programs(0))
    def _():
        start_gathers(i + 1, 1 - slot)

    for k in range(TOP_K):
        pltpu.make_async_copy(ys_ref.at[pl.ds(0, tb * ROW_CHUNKS)], buf_ref.at[slot, k],
                              sem.at[slot]).wait()
    gates = gate_ref[...]
    y = jnp.concatenate(
        [buf_ref[slot, 0, pl.ds(j, tb, stride=ROW_CHUNKS), :] * gates[:, 0:1]
         + buf_ref[slot, 1, pl.ds(j, tb, stride=ROW_CHUNKS), :] * gates[:, 1:2]
         for j in range(ROW_CHUNKS)], axis=1)
    o_ref[...] = _layer_norm_rows(DN_ALPHA * x_ref[...] + y, g_ref[...], b_ref[...])


def moe_combine(x, ys, pos_flat, gates, g, b):
    t, d = x.shape
    tb = min(ROUTE_BLOCK, t)
    grid_spec = pltpu.PrefetchScalarGridSpec(
        num_scalar_prefetch=1,
        grid=(t // tb,),
        in_specs=[pl.BlockSpec((tb, d), lambda i, *_: (i, 0)),
                  pl.BlockSpec((tb, LANES), lambda i, *_: (i, 0)),
                  pl.BlockSpec((1, d), lambda i, *_: (0, 0)),
                  pl.BlockSpec((1, d), lambda i, *_: (0, 0)),
                  pl.BlockSpec(memory_space=pl.ANY)],
        out_specs=pl.BlockSpec((tb, d), lambda i, *_: (i, 0)),
        scratch_shapes=[pltpu.VMEM((2, TOP_K, tb * ROW_CHUNKS, LANES), F32),
                        pltpu.SemaphoreType.DMA((2,))],
    )
    return pl.pallas_call(
        functools.partial(_combine_kernel, tb=tb),
        grid_spec=grid_spec,
        out_shape=jax.ShapeDtypeStruct((t, d), F32),
        compiler_params=_params("arbitrary"),
        name="moe_combine",
    )(pos_flat, x, gates, g.reshape(1, d), b.reshape(1, d), ys)


def moe_deepnorm(routed, wg, wu, wd, g, b):
    x, x_tiled, idx, gates, rank, counts_f = routed
    t = x.shape[0]
    counts = counts_f[0, :N_EXPERTS].astype(jnp.int32)
    padded = (counts + MOE_TILE - 1) // MOE_TILE * MOE_TILE
    ends = jnp.cumsum(padded)
    offsets = ends - padded
    n_tiles = (TOP_K * t) // MOE_TILE + N_EXPERTS
    n_used = (ends[-1] // MOE_TILE).astype(jnp.int32)
    tile_start = jnp.arange(n_tiles, dtype=jnp.int32) * MOE_TILE
    tile_start = jnp.minimum(tile_start, ends[-1] - MOE_TILE)
    tile_expert = jnp.sum(tile_start[:, None] >= ends[None, :], axis=1).astype(jnp.int32)
    pos = moe_positions(idx, rank, offsets)
    pos_flat = pos[:, :TOP_K].reshape(-1)
    xs = moe_dispatch(x_tiled, pos_flat, counts, offsets, n_tiles * MOE_TILE)
    ys = moe_ffn(xs, tile_expert, n_used.reshape(1), wg.astype(BF16), wu.astype(BF16), wd.astype(BF16))
    return moe_combine(x, ys, pos_flat, gates, g, b)


def kernel(x, l0_ssd_w_in, l0_ssd_conv_w, l0_ssd_conv_b, l0_ssd_dt_bias, l0_ssd_a_log, l0_ssd_d_skip, l0_ssd_norm_w, l0_ssd_w_out, l0_ln_mix_g, l0_ln_mix_b, l0_ffn_w_gate, l0_ffn_w_up, l0_ffn_w_down, l0_ln_ffn_g, l0_ln_ffn_b, l1_sb_w_qkv, l1_sb_w_out, l1_ln_mix_g, l1_ln_mix_b, l1_moe_w_router, l1_moe_w_gate, l1_moe_w_up, l1_moe_w_down, l1_ln_ffn_g, l1_ln_ffn_b, l2_fox_w_qkvf, l2_fox_b_f, l2_fox_w_out, l2_ln_mix_g, l2_ln_mix_b, l2_ffn_w_gate, l2_ffn_w_up, l2_ffn_w_down, l2_ln_ffn_g, l2_ln_ffn_b, l3_ssd_w_in, l3_ssd_conv_w, l3_ssd_conv_b, l3_ssd_dt_bias, l3_ssd_a_log, l3_ssd_d_skip, l3_ssd_norm_w, l3_ssd_w_out, l3_ln_mix_g, l3_ln_mix_b, l3_moe_w_router, l3_moe_w_gate, l3_moe_w_up, l3_moe_w_down, l3_ln_ffn_g, l3_ln_ffn_b):
    batch, seq, d = x.shape
    h = x.reshape(batch * seq, d)
    bf = lambda w: w.astype(BF16)
    y = ssd_mixer(h, l0_ssd_w_in, l0_ssd_conv_w, l0_ssd_conv_b, l0_ssd_dt_bias, l0_ssd_a_log,
                  l0_ssd_d_skip, l0_ssd_norm_w, batch, seq)
    h = mixer_out_ffn_deepnorm(y, bf(l0_ssd_w_out), h, l0_ln_mix_g, l0_ln_mix_b,
                               bf(l0_ffn_w_gate), bf(l0_ffn_w_up), bf(l0_ffn_w_down), l0_ln_ffn_g, l0_ln_ffn_b)
    o = sb_mixer(h, l1_sb_w_qkv, batch, seq)
    routed = matmul_deepnorm_route(o, bf(l1_sb_w_out), h, l1_ln_mix_g, l1_ln_mix_b, l1_moe_w_router)
    h = moe_deepnorm(routed, l1_moe_w_gate, l1_moe_w_up, l1_moe_w_down, l1_ln_ffn_g, l1_ln_ffn_b)
    o = fox_mixer(h, l2_fox_w_qkvf, l2_fox_b_f, batch, seq)
    h = mixer_out_ffn_deepnorm(o, bf(l2_fox_w_out), h, l2_ln_mix_g, l2_ln_mix_b,
                               bf(l2_ffn_w_gate), bf(l2_ffn_w_up), bf(l2_ffn_w_down), l2_ln_ffn_g, l2_ln_ffn_b)
    y = ssd_mixer(h, l3_ssd_w_in, l3_ssd_conv_w, l3_ssd_conv_b, l3_ssd_dt_bias, l3_ssd_a_log,
                  l3_ssd_d_skip, l3_ssd_norm_w, batch, seq)
    routed = matmul_deepnorm_route(y, bf(l3_ssd_w_out), h, l3_ln_mix_g, l3_ln_mix_b, l3_moe_w_router)
    h = moe_deepnorm(routed, l3_moe_w_gate, l3_moe_w_up, l3_moe_w_down, l3_ln_ffn_g, l3_ln_ffn_b)
    return h.reshape(batch, seq, d)
```

```python
import functools
import math

import jax
import jax.numpy as jnp
from jax import lax
from jax.experimental import pallas as pl
from jax.experimental.pallas import tpu as pltpu

F32 = jnp.float32
BF16 = jnp.bfloat16
HIGHEST = lax.Precision.HIGHEST

LANES = 128
SUBLANES = 8
VMEM_LIMIT_BYTES = 56 * 1024 * 1024

D_MODEL = 1024
DEPTH = 4
SSD_D_INNER = 2048
SSD_HEAD_DIM = 64
SSD_HEADS = 32
SSD_GROUPS = 4
SSD_STATE = 128
SSD_CONV = 4
SSD_GROUP_DIM = SSD_D_INNER // SSD_GROUPS
SSD_BC_DIM = 2 * SSD_GROUPS * SSD_STATE
ATT_HEAD_DIM = 64
ATT_HEADS = 16
ATT_DIM = 1024
N_EXPERTS = 8
TOP_K = 2
DN_ALPHA = (2.0 * DEPTH) ** 0.25
LN_EPS = 1e-5
RMS_EPS = 1e-5
NEG_BIG = -1e30


def _params(*semantics):
    return pltpu.CompilerParams(dimension_semantics=semantics, vmem_limit_bytes=VMEM_LIMIT_BYTES)


def _layer_norm_rows(h, g, b):
    mu = jnp.mean(h, axis=-1, keepdims=True)
    d = h - mu
    var = jnp.mean(d * d, axis=-1, keepdims=True)
    return d * lax.rsqrt(var + LN_EPS) * g + b


def _silu(x):
    half = 0.5 * x
    return half * (1.0 + jnp.tanh(half))


def _mm_kernel(x_ref, w_ref, o_ref, xb_ref):
    @pl.when(pl.program_id(1) == 0)
    def _():
        xb_ref[...] = x_ref[...].astype(BF16)

    o_ref[...] = jnp.dot(xb_ref[...], w_ref[...], preferred_element_type=F32).astype(o_ref.dtype)


def matmul(x, w, out_dtype, tm=1024, tn=1024):
    m, k = x.shape
    tm = min(tm, m)
    n = w.shape[1]
    return pl.pallas_call(
        _mm_kernel,
        grid=(m // tm, n // tn),
        in_specs=[pl.BlockSpec((tm, k), lambda i, j: (i, 0)),
                  pl.BlockSpec((k, tn), lambda i, j: (0, j))],
        out_specs=pl.BlockSpec((tm, tn), lambda i, j: (i, j)),
        out_shape=jax.ShapeDtypeStruct((m, n), out_dtype),
        scratch_shapes=[pltpu.VMEM((tm, k), BF16)],
        compiler_params=_params("parallel", "arbitrary"),
        name="matmul",
    )(x, w)


def _dot_split(x, w):
    xh = x.astype(BF16)
    xl = (x - xh.astype(F32)).astype(BF16)
    wh = w.astype(BF16)
    wl = (w - wh.astype(F32)).astype(BF16)
    dot = functools.partial(jnp.dot, preferred_element_type=F32)
    return dot(xh, wh) + (dot(xh, wl) + dot(xl, wh))


def _mm_f32_kernel(x_ref, w_ref, o_ref):
    o_ref[...] = _dot_split(x_ref[...], w_ref[...])


def matmul_f32(x, w, tm=1024):
    m, k = x.shape
    tm = min(tm, m)
    n = w.shape[1]
    return pl.pallas_call(
        _mm_f32_kernel,
        grid=(m // tm,),
        in_specs=[pl.BlockSpec((tm, k), lambda i: (i, 0)),
                  pl.BlockSpec((k, n), lambda i: (0, 0))],
        out_specs=pl.BlockSpec((tm, n), lambda i: (i, 0)),
        out_shape=jax.ShapeDtypeStruct((m, n), F32),
        compiler_params=_params("parallel"),
        name="matmul_f32",
    )(x, w)


ROW_CHUNKS = D_MODEL // LANES


def _store_row_tiled(ref, value):
    n = value.shape[0]
    for j in range(ROW_CHUNKS):
        ref[pl.ds(j, n, stride=ROW_CHUNKS), :] = value[:, j * LANES:(j + 1) * LANES]


def _load_row_tiled(ref, j, n):
    return ref[pl.ds(j, n, stride=ROW_CHUNKS), :]


def _mm_ln_route_kernel(x_ref, w_ref, r_ref, g_ref, b_ref, wr_ref,
                        o_ref, tiled_ref, idx_ref, gate_ref, rank_ref, count_ref, run_ref):
    @pl.when(pl.program_id(0) == 0)
    def _():
        run_ref[...] = jnp.zeros_like(run_ref)

    y = jnp.dot(x_ref[...], w_ref[...], preferred_element_type=F32)
    o = _layer_norm_rows(DN_ALPHA * r_ref[...] + y, g_ref[...], b_ref[...])
    o_ref[...] = o
    _store_row_tiled(tiled_ref, o)
    idx, gates = _top2(_dot_split(o, wr_ref[...]))
    idx_ref[...] = idx
    gate_ref[...] = gates
    rank_ref[...] = _rank_block(idx, run_ref)
    count_ref[...] = run_ref[...]


def matmul_deepnorm_route(x, w, resid, g, b, w_router):
    m, k = x.shape
    tm = min(ROUTE_BLOCK, m)
    d = w.shape[1]
    rows = lambda width: pl.BlockSpec((tm, width), lambda i: (i, 0))
    fixed = lambda a: pl.BlockSpec(a.shape, lambda i: (0, 0))
    args = [x, w, resid, g.reshape(1, d), b.reshape(1, d),
            jnp.pad(w_router, ((0, 0), (0, LANES - N_EXPERTS)))]
    return pl.pallas_call(
        _mm_ln_route_kernel,
        grid=(m // tm,),
        in_specs=[rows(k), fixed(w), rows(d), fixed(args[3]), fixed(args[4]), fixed(args[5])],
        out_specs=[rows(d), pl.BlockSpec((tm * ROW_CHUNKS, LANES), lambda i: (i, 0)),
                   rows(LANES), rows(LANES), rows(LANES), pl.BlockSpec((1, LANES), lambda i: (0, 0))],
        out_shape=[jax.ShapeDtypeStruct((m, d), F32),
                   jax.ShapeDtypeStruct((m * ROW_CHUNKS, LANES), F32),
                   jax.ShapeDtypeStruct((m, LANES), jnp.int32),
                   jax.ShapeDtypeStruct((m, LANES), F32),
                   jax.ShapeDtypeStruct((m, LANES), jnp.int32),
                   jax.ShapeDtypeStruct((1, LANES), F32)],
        scratch_shapes=[pltpu.VMEM((1, LANES), F32)],
        compiler_params=_params("arbitrary"),
        name="matmul_deepnorm_route",
    )(*args)


def _mix_ffn_kernel(o_ref, wo_ref, r_ref, g1_ref, b1_ref, wg_ref, wu_ref, wd_ref, g2_ref, b2_ref, out_ref):
    x = _layer_norm_rows(
        DN_ALPHA * r_ref[...] + jnp.dot(o_ref[...], wo_ref[...], preferred_element_type=F32),
        g1_ref[...], b1_ref[...])
    xb = x.astype(BF16)
    gate = jnp.dot(xb, wg_ref[...], preferred_element_type=F32)
    up = jnp.dot(xb, wu_ref[...], preferred_element_type=F32)
    h = (_silu(gate) * up).astype(BF16)
    y = jnp.dot(h, wd_ref[...], preferred_element_type=F32)
    out_ref[...] = _layer_norm_rows(DN_ALPHA * x + y, g2_ref[...], b2_ref[...])


def mixer_out_ffn_deepnorm(o, w_out, resid, g1, b1, wg, wu, wd, g2, b2, tm=512):
    m, k = o.shape
    d = w_out.shape[1]
    tm = min(tm, m)
    resident = lambda a: pl.BlockSpec(a.shape, lambda i: (0, 0), pipeline_mode=pl.Buffered(1))
    vec = lambda v: v.reshape(1, d)
    args = (o, w_out, resid, vec(g1), vec(b1), wg, wu, wd, vec(g2), vec(b2))
    return pl.pallas_call(
        _mix_ffn_kernel,
        grid=(m // tm,),
        in_specs=[pl.BlockSpec((tm, k), lambda i: (i, 0)), resident(w_out),
                  pl.BlockSpec((tm, d), lambda i: (i, 0)), resident(args[3]), resident(args[4]),
                  resident(wg), resident(wu), resident(wd), resident(args[8]), resident(args[9])],
        out_specs=pl.BlockSpec((tm, d), lambda i: (i, 0)),
        out_shape=jax.ShapeDtypeStruct((m, d), F32),
        compiler_params=_params("parallel"),
        name="mixer_out_ffn_deepnorm",
    )(*args)


def _split_bf16(v, pieces):
    out = []
    r = v
    for _ in range(pieces - 1):
        p = r.astype(BF16)
        out.append(p)
        r = r - p.astype(F32)
    out.append(r.astype(BF16))
    return out


def _expand(v, e_ref, pieces):
    stacked = jnp.concatenate(_split_bf16(v, pieces), axis=1)
    return jnp.dot(stacked, e_ref[...], preferred_element_type=F32)


def _tril_f32(n, strict=False):
    r = lax.broadcasted_iota(jnp.int32, (n, n), 0)
    c = lax.broadcasted_iota(jnp.int32, (n, n), 1)
    return ((r > c) if strict else (r >= c)).astype(F32)


def _ssd_kernel(z_ref, xs_ref, bc_ref, dt_ref,
                dtb_ref, alog_ref, dskip_ref, normw_ref, e64_ref, e128_ref, o_ref,
                state_ref, xdt_ref, acol_ref, arow_ref, cb_ref, ydiag_ref,
                *, chunk):
    L = chunk
    c = pl.program_id(1)

    @pl.when(c == 0)
    def _():
        state_ref[...] = jnp.zeros_like(state_ref)

    xs = xs_ref[...]
    bcv = bc_ref[...]

    dt = jax.nn.softplus(dt_ref[...] + dtb_ref[...])
    da = dt * (-jnp.exp(alog_ref[...]))
    a_cs = jnp.dot(_tril_f32(L), da, preferred_element_type=F32, precision=HIGHEST)
    ea = jnp.exp(a_cs)
    dte = jnp.exp(a_cs[L - 1:L, :] - a_cs)

    dt_x = _expand(dt, e64_ref, 2)
    ea_x = _expand(ea, e64_ref, 2)
    dte_x = _expand(dte, e64_ref, 2)
    acol_ref[...] = _expand(a_cs, e128_ref, 3)
    a_t = a_cs.T
    for h in range(SSD_HEADS):
        arow_ref[h] = jnp.broadcast_to(a_t[h:h + 1, :], (SUBLANES, L))

    xdt = xs * dt_x
    xdt_ref[...] = xdt.astype(BF16)
    xdte = (xdt * dte_x).astype(BF16)

    y_off = []
    for g in range(SSD_GROUPS):
        bm = bcv[:, g * SSD_STATE:(g + 1) * SSD_STATE]
        cm = bcv[:, (SSD_GROUPS + g) * SSD_STATE:(SSD_GROUPS + g + 1) * SSD_STATE].astype(BF16)
        cb_ref[g] = lax.dot_general(cm, bm.astype(BF16), (((1,), (1,)), ((), ())),
                                    preferred_element_type=F32)
        gs = slice(g * SSD_GROUP_DIM, (g + 1) * SSD_GROUP_DIM)
        st = state_ref[g]
        y_off.append(jnp.dot(cm, st.astype(BF16), preferred_element_type=F32) * ea_x[:, gs])
        state_ref[g] = st * ea_x[L - 1:L, gs] + jnp.dot(
            bm.T.astype(BF16), xdte[:, gs], preferred_element_type=F32)

    row = lax.broadcasted_iota(jnp.int32, (L, L), 0)
    col = lax.broadcasted_iota(jnp.int32, (L, L), 1)
    causal = row >= col
    head0 = lax.broadcasted_iota(jnp.int32, (L, LANES), 1) < SSD_HEAD_DIM

    heads_per_group = SSD_HEADS // SSD_GROUPS

    def group_body(g, carry):
        cbg = cb_ref[g]
        scores = []
        for j in range(heads_per_group):
            h = g * heads_per_group + j
            a_l = acol_ref[:, pl.ds(pl.multiple_of(h * LANES, LANES), LANES)]
            a_l = jnp.concatenate([a_l] * (L // LANES), axis=1)
            a_s = arow_ref[h][0:1, :]
            scores.append((cbg * jnp.exp(jnp.where(causal, a_l - a_s, NEG_BIG))).astype(BF16))
        for pp in range(heads_per_group // 2):
            lanes = pl.ds(pl.multiple_of((g * (heads_per_group // 2) + pp) * LANES, LANES), LANES)
            xpair = xdt_ref[:, lanes]
            ys = [jnp.dot(scores[2 * pp + j], xpair, preferred_element_type=F32) for j in range(2)]
            ydiag_ref[:, lanes] = jnp.where(head0, ys[0], ys[1])
        return carry

    lax.fori_loop(0, SSD_GROUPS, group_body, 0)

    y = ydiag_ref[...] + jnp.concatenate(y_off, axis=1) + dskip_ref[...] * xs
    y = y * _silu(z_ref[...])
    parts = []
    for g in range(SSD_GROUPS):
        yg = y[:, g * SSD_GROUP_DIM:(g + 1) * SSD_GROUP_DIM]
        parts.append(yg * lax.rsqrt(jnp.mean(yg * yg, axis=-1, keepdims=True) + RMS_EPS))
    o_ref[...] = (jnp.concatenate(parts, axis=1) * normw_ref[...]).astype(o_ref.dtype)


def _expansion_matrix(width, pieces):
    h = jnp.arange(LANES)[:, None]
    lane = jnp.arange(SSD_HEADS * width)[None, :]
    e = (lane // width == h).astype(BF16)
    return jnp.concatenate([e] * pieces, axis=0)


def _mm_conv_kernel(x_ref, halo_ref, w_ref, cw_ref, cb_ref, o_ref, ext_ref, xb_ref, *, tiles_per_seq):
    tm = x_ref.shape[0]

    @pl.when(pl.program_id(1) == 0)
    def _():
        xb_ref[...] = x_ref[...].astype(BF16)

    w = w_ref[...]
    halo = jnp.dot(halo_ref[...].astype(BF16), w, preferred_element_type=F32)
    first = pl.program_id(0) % tiles_per_seq == 0
    ext_ref[0:SUBLANES, :] = jnp.where(first, 0.0, halo)
    ext_ref[SUBLANES:SUBLANES + tm, :] = jnp.dot(xb_ref[...], w, preferred_element_type=F32)
    ext = ext_ref[...]
    acc = cw_ref[0:1, :] * ext
    for k in range(1, SSD_CONV):
        acc = pltpu.roll(acc, 1, axis=0) + cw_ref[k:k + 1, :] * ext
    o_ref[...] = _silu(acc[SUBLANES:SUBLANES + tm, :] + cb_ref[...])


def matmul_conv_silu(x, w, conv_w, conv_b, seq, tm=1024, tn=1024):
    m, k = x.shape
    tm = min(tm, seq)
    n = w.shape[1]
    halo_blocks = tm // SUBLANES
    return pl.pallas_call(
        functools.partial(_mm_conv_kernel, tiles_per_seq=seq // tm),
        grid=(m // tm, n // tn),
        in_specs=[pl.BlockSpec((tm, k), lambda i, j: (i, 0)),
                  pl.BlockSpec((SUBLANES, k), lambda i, j: (jnp.maximum(i * halo_blocks - 1, 0), 0)),
                  pl.BlockSpec((k, tn), lambda i, j: (0, j)),
                  pl.BlockSpec((SSD_CONV, tn), lambda i, j: (0, j)),
                  pl.BlockSpec((1, tn), lambda i, j: (0, j))],
        out_specs=pl.BlockSpec((tm, tn), lambda i, j: (i, j)),
        out_shape=jax.ShapeDtypeStruct((m, n), F32),
        scratch_shapes=[pltpu.VMEM((tm + SUBLANES, tn), F32), pltpu.VMEM((tm, k), BF16)],
        compiler_params=_params("parallel", "arbitrary"),
        name="matmul_conv_silu",
    )(x, x, w, conv_w, conv_b.reshape(1, n))


def ssd_core(z, xbc, dt_raw, dt_bias, a_log, d_skip, norm_w, batch, seq, chunk=128):
    t = z.shape[0]
    nc = seq // chunk
    pad = LANES - SSD_HEADS
    row = lambda v: v.reshape(1, -1).astype(F32)
    args = (
        z, xbc, xbc, dt_raw,
        row(jnp.pad(dt_bias, (0, pad))), row(jnp.pad(a_log, (0, pad))),
        row(jnp.repeat(d_skip, SSD_HEAD_DIM)), row(norm_w),
        _expansion_matrix(SSD_HEAD_DIM, 2), _expansion_matrix(LANES, 3),
    )
    blk = lambda b, c: (b * nc + c, 0)
    const = lambda b, c: (0, 0)
    full = lambda a: pl.BlockSpec(a.shape, const)
    in_specs = [
        pl.BlockSpec((chunk, SSD_D_INNER), blk),
        pl.BlockSpec((chunk, SSD_D_INNER), blk),
        pl.BlockSpec((chunk, SSD_BC_DIM), lambda b, c: (b * nc + c, 2)),
        pl.BlockSpec((chunk, LANES), blk),
    ] + [full(a) for a in args[4:]]
    return pl.pallas_call(
        functools.partial(_ssd_kernel, chunk=chunk),
        grid=(batch, nc),
        in_specs=in_specs,
        out_specs=pl.BlockSpec((chunk, SSD_D_INNER), blk),
        out_shape=jax.ShapeDtypeStruct((t, SSD_D_INNER), BF16),
        scratch_shapes=[
            pltpu.VMEM((SSD_GROUPS, SSD_STATE, SSD_GROUP_DIM), F32),
            pltpu.VMEM((chunk, SSD_D_INNER), BF16),
            pltpu.VMEM((chunk, SSD_HEADS * LANES), F32),
            pltpu.VMEM((SSD_HEADS, SUBLANES, chunk), F32),
            pltpu.VMEM((SSD_GROUPS, chunk, chunk), F32),
            pltpu.VMEM((chunk, SSD_D_INNER), F32),
        ],
        compiler_params=_params("parallel", "arbitrary"),
        name="ssd_core",
    )(*args)


def ssd_mixer(x, w_in, conv_w, conv_b, dt_bias, a_log, d_skip, norm_w, batch, seq):
    n_zx = SSD_D_INNER + SSD_D_INNER + SSD_BC_DIM
    z = matmul(x, w_in[:, :SSD_D_INNER].astype(BF16), F32)
    xbc = matmul_conv_silu(x, w_in[:, SSD_D_INNER:n_zx].astype(BF16), conv_w, conv_b, seq)
    w_dt = jnp.pad(w_in[:, n_zx:], ((0, 0), (0, LANES - SSD_HEADS)))
    dt_raw = matmul_f32(x, w_dt)
    return ssd_core(z, xbc, dt_raw, dt_bias, a_log, d_skip, norm_w, batch, seq)


ATT_BLOCK = 128
ATT_Q = 256
ATT_PAIRS = 4
ATT_STEP_LANES = ATT_PAIRS * LANES
ATT_SCALE = ATT_HEAD_DIM ** -0.5
LOG2E = 1.4426950408889634
EXP2_UNDERFLOW = -160.0


def _split_heads(q_ref, rows, head0):
    qs = []
    for p in range(ATT_PAIRS):
        q2 = q_ref[rows, p * LANES:(p + 1) * LANES]
        zero = jnp.zeros_like(q2)
        qs += [jnp.where(head0, q2, zero), jnp.where(head0, zero, q2)]
    return qs


def _sb_kernel(q_ref, k_ref, v_ref, u_ref, o_ref, *scratch, seq):
    n_heads = 2 * ATT_PAIRS
    later_ref, acc_ref, z_ref = scratch[:n_heads], scratch[n_heads:2 * n_heads], scratch[2 * n_heads:]
    nq = seq // ATT_Q
    ndiag = ATT_Q // ATT_BLOCK
    head0 = lax.broadcasted_iota(jnp.int32, (ATT_Q, LANES), 1) < ATT_HEAD_DIM
    row = lax.broadcasted_iota(jnp.int32, (ATT_Q, ATT_BLOCK), 0)
    col = lax.broadcasted_iota(jnp.int32, (ATT_Q, ATT_BLOCK), 1)

    def q_body(i, _):
        q0 = pl.multiple_of(i * ATT_Q, ATT_Q)
        rows = pl.ds(q0, ATT_Q)
        qs = _split_heads(q_ref, rows, head0)
        for n in range(2 * ATT_PAIRS):
            later_ref[n][...] = jnp.zeros((ATT_Q, LANES), F32)
            acc_ref[n][...] = jnp.zeros((ATT_Q, LANES), F32)

        heads = range(2 * ATT_PAIRS)

        def key_rows(j):
            return pl.ds(pl.multiple_of(j * ATT_BLOCK, ATT_BLOCK), ATT_BLOCK)

        def scores(j):
            k2 = [k_ref[key_rows(j), p * LANES:(p + 1) * LANES] for p in range(ATT_PAIRS)]
            return [lax.dot_general(qs[n], k2[n // 2], (((1,), (1,)), ((), ())),
                                    preferred_element_type=F32) * (ATT_SCALE * LOG2E) for n in heads]

        def consume(z2, j, strict):
            keys = key_rows(j)
            v2 = [v_ref[keys, p * LANES:(p + 1) * LANES] for p in range(ATT_PAIRS)]
            log_beta, sums = [], []
            for n in heads:
                lb = jnp.minimum(z2[n], 0.0) - jnp.log2(1.0 + jnp.exp2(-jnp.abs(z2[n])))
                log_keep = lb - z2[n]
                if strict is not None:
                    log_keep = jnp.where(strict, log_keep, 0.0)
                hi = log_keep.astype(BF16)
                lo = (log_keep - hi.astype(F32)).astype(BF16)
                log_beta.append(lb)
                sums.append(jnp.dot(jnp.concatenate([hi, lo], axis=1), u_ref[...],
                                    preferred_element_type=F32))
            for n in heads:
                w = jnp.exp2(log_beta[n] + sums[n][:, :ATT_BLOCK] + later_ref[n][...])
                if strict is not None:
                    w = jnp.where(strict, w, 0.0)
                acc_ref[n][...] += jnp.dot(w.astype(BF16), v2[n // 2], preferred_element_type=F32)
                later_ref[n][...] += sums[n][:, ATT_BLOCK:]

        for d in reversed(range(ndiag)):
            consume(scores(i * ndiag + d), i * ndiag + d, d * ATT_BLOCK + col < row)

        n_off = i * ndiag
        for n, zn in enumerate(scores(jnp.maximum(n_off - 1, 0))):
            z_ref[n][...] = zn

        def off_diagonal(carry):
            jj, _ = carry
            z2 = [z_ref[n][...] for n in heads]
            ahead = scores(jnp.maximum(n_off - 2 - jj, 0))
            consume(z2, n_off - 1 - jj, None)
            for n in heads:
                z_ref[n][...] = ahead[n]
            top = later_ref[0][...]
            for n in range(1, 2 * ATT_PAIRS):
                top = jnp.maximum(top, later_ref[n][...])
            return jj + 1, jnp.max(top) > EXP2_UNDERFLOW

        lax.while_loop(lambda c: (c[0] < n_off) & c[1], off_diagonal, (jnp.int32(0), n_off > 0))
        for p in range(ATT_PAIRS):
            o_ref[rows, p * LANES:(p + 1) * LANES] = jnp.where(
                head0, acc_ref[2 * p][...], acc_ref[2 * p + 1][...]).astype(o_ref.dtype)
        return 0

    lax.fori_loop(0, nq, q_body, 0)


def _suffix_sum_matrix():
    j = jnp.arange(2 * ATT_BLOCK)[:, None] % ATT_BLOCK
    s = jnp.arange(2 * ATT_BLOCK)[None, :]
    return jnp.where(s < ATT_BLOCK, j > s, True).astype(BF16)


def _attention_specs(seq):
    n_steps = ATT_HEADS // 2 // ATT_PAIRS
    blk = lambda off: pl.BlockSpec((seq, ATT_STEP_LANES), lambda b, p: (b, off + p))
    specs = [blk(0), blk(n_steps), blk(2 * n_steps)]
    return n_steps, specs, pl.BlockSpec((seq, ATT_STEP_LANES), lambda b, p: (b, p))


def sb_attention(qkv, batch, seq):
    n_steps, in_specs, out_spec = _attention_specs(seq)
    u = _suffix_sum_matrix()
    return pl.pallas_call(
        functools.partial(_sb_kernel, seq=seq),
        grid=(batch, n_steps),
        in_specs=in_specs + [pl.BlockSpec(u.shape, lambda b, p: (0, 0))],
        out_specs=out_spec,
        out_shape=jax.ShapeDtypeStruct((batch * seq, ATT_DIM), BF16),
        scratch_shapes=[pltpu.VMEM((ATT_Q, LANES), F32)] * (6 * ATT_PAIRS),
        compiler_params=_params("parallel", "parallel"),
        name="sb_attention",
    )(qkv, qkv, qkv, u)


def sb_mixer(x, w_qkv, batch, seq):
    return sb_attention(matmul(x, w_qkv.astype(BF16), BF16), batch, seq)


CUMSUM_BLOCK = 256


def _fox_decay_kernel(f_ref, bf_ref, ccol_ref, crow_ref, *, seq):
    tri = _tril_f32(CUMSUM_BLOCK)
    carry = jnp.zeros((1, LANES), F32)
    for blk in range(seq // CUMSUM_BLOCK):
        rows = slice(blk * CUMSUM_BLOCK, (blk + 1) * CUMSUM_BLOCK)
        log_f = jax.nn.log_sigmoid(f_ref[rows, :] + bf_ref[...])
        c = jnp.dot(tri, log_f, preferred_element_type=F32, precision=HIGHEST) + carry
        carry = c[CUMSUM_BLOCK - 1:CUMSUM_BLOCK, :]
        ccol_ref[rows, :] = c
        c_t = c.T
        for p in range(ATT_HEADS // 2):
            crow_ref[0, p, :, rows] = c_t[2 * p:2 * p + 2, :]


def fox_decay(f_raw, b_f, batch, seq):
    return pl.pallas_call(
        functools.partial(_fox_decay_kernel, seq=seq),
        grid=(batch,),
        in_specs=[pl.BlockSpec((seq, LANES), lambda b: (b, 0)),
                  pl.BlockSpec((1, LANES), lambda b: (0, 0))],
        out_specs=[pl.BlockSpec((seq, LANES), lambda b: (b, 0)),
                   pl.BlockSpec((1, ATT_HEADS // 2, 2, seq), lambda b: (b, 0, 0, 0))],
        out_shape=[jax.ShapeDtypeStruct((batch * seq, LANES), F32),
                   jax.ShapeDtypeStruct((batch, ATT_HEADS // 2, 2, seq), F32)],
        compiler_params=_params("parallel"),
        name="fox_decay",
    )(f_raw, jnp.pad(b_f, (0, LANES - ATT_HEADS)).reshape(1, LANES))


def _fox_kernel(q_ref, k_ref, v_ref, ccol_ref, crow_ref, o_ref, crep_ref, vt_ref, *scratch, seq):
    n_heads = 2 * ATT_PAIRS
    m_ref, acc_ref, s_ref = scratch[:n_heads], scratch[n_heads:2 * n_heads], scratch[2 * n_heads:]
    nq = seq // ATT_Q
    ndiag = ATT_Q // ATT_BLOCK
    step = pl.program_id(1)
    heads = range(n_heads)
    head0 = lax.broadcasted_iota(jnp.int32, (ATT_Q, LANES), 1) < ATT_HEAD_DIM
    head0_rows = lax.broadcasted_iota(jnp.int32, (LANES, ATT_BLOCK), 0) < ATT_HEAD_DIM
    key = lax.broadcasted_iota(jnp.int32, (ATT_BLOCK, ATT_Q), 0)
    qry = lax.broadcasted_iota(jnp.int32, (ATT_BLOCK, ATT_Q), 1)

    pieces = jnp.concatenate(_split_bf16(ccol_ref[...], 3), axis=1)
    sel_row = lax.broadcasted_iota(jnp.int32, (3 * LANES, LANES), 0) & (LANES - 1)
    for n in heads:
        sel = jnp.where(sel_row == 2 * ATT_PAIRS * step + n, 1.0, 0.0).astype(BF16)
        crep_ref[n] = jnp.dot(pieces, sel, preferred_element_type=F32) * LOG2E
    for p in range(ATT_PAIRS):
        for blk in range(seq // ATT_BLOCK):
            rows = slice(blk * ATT_BLOCK, (blk + 1) * ATT_BLOCK)
            vt_ref[p, :, rows] = v_ref[rows, p * LANES:(p + 1) * LANES].astype(F32).T.astype(BF16)

    def q_body(i, _):
        q0 = pl.multiple_of(i * ATT_Q, ATT_Q)
        rows = pl.ds(q0, ATT_Q)
        qs = _split_heads(q_ref, rows, head0)
        c_q = [crow_ref[0, n // 2, n % 2:n % 2 + 1, rows] * LOG2E for n in heads]
        for n in heads:
            m_ref[n][...] = jnp.full((1, ATT_Q), NEG_BIG, F32)
            acc_ref[n][...] = jnp.zeros((LANES, ATT_Q), F32)

        def key_rows(j):
            return pl.ds(pl.multiple_of(j * ATT_BLOCK, ATT_BLOCK), ATT_BLOCK)

        def scores(j):
            k2 = [k_ref[key_rows(j), p * LANES:(p + 1) * LANES] for p in range(ATT_PAIRS)]
            return [lax.dot_general(k2[n // 2], qs[n], (((1,), (1,)), ((), ())),
                                    preferred_element_type=F32) * (ATT_SCALE * LOG2E) for n in heads]

        def consume(s, j, causal):
            keys = key_rows(j)
            vt = [vt_ref[p, :, keys] for p in range(ATT_PAIRS)]
            for n in heads:
                c_k = crep_ref[n, keys, :]
                sn = s[n] + (c_q[n] - jnp.concatenate([c_k] * (ATT_Q // LANES), axis=1))
                if causal is not None:
                    sn = jnp.where(causal, sn, NEG_BIG)
                m_old = m_ref[n][...]
                m_new = jnp.maximum(m_old, jnp.max(sn, axis=0, keepdims=True))
                prob = jnp.exp2(sn - m_new).astype(BF16)
                one = jnp.ones_like(vt[n // 2])
                v1t = jnp.where(head0_rows, vt[n // 2], one) if n % 2 == 0 else jnp.where(
                    head0_rows, one, vt[n // 2])
                acc_ref[n][...] = jnp.exp2(m_old - m_new) * acc_ref[n][...] + jnp.dot(
                    v1t, prob, preferred_element_type=F32)
                m_ref[n][...] = m_new

        for d in range(ndiag):
            consume(scores(i * ndiag + d), i * ndiag + d, d * ATT_BLOCK + key <= qry)

        n_off = i * ndiag
        for n, sn in enumerate(scores(0)):
            s_ref[n][...] = sn

        def off_diagonal(jj, c):
            s = [s_ref[n][...] for n in heads]
            ahead = scores(jnp.minimum(jj + 1, n_off - 1))
            consume(s, jj, None)
            for n in heads:
                s_ref[n][...] = ahead[n]
            return c

        lax.fori_loop(0, n_off, off_diagonal, 0)
        for p in range(ATT_PAIRS):
            a0, a1 = acc_ref[2 * p][...], acc_ref[2 * p + 1][...]
            o_t = jnp.concatenate([a0[:ATT_HEAD_DIM] / a0[ATT_HEAD_DIM:],
                                   a1[ATT_HEAD_DIM:] / a1[:ATT_HEAD_DIM]], axis=0)
            o_ref[rows, p * LANES:(p + 1) * LANES] = o_t.T.astype(o_ref.dtype)
        return 0

    lax.fori_loop(0, nq, q_body, 0)


def fox_attention(qkv, ccol, crow, batch, seq):
    n_steps, in_specs, out_spec = _attention_specs(seq)
    in_specs += [pl.BlockSpec((seq, LANES), lambda b, p: (b, 0)),
                 pl.BlockSpec((1, ATT_PAIRS, 2, seq), lambda b, p: (b, p, 0, 0))]
    return pl.pallas_call(
        functools.partial(_fox_kernel, seq=seq),
        grid=(batch, n_steps),
        in_specs=in_specs,
        out_specs=out_spec,
        out_shape=jax.ShapeDtypeStruct((batch * seq, ATT_DIM), BF16),
        scratch_shapes=([pltpu.VMEM((2 * ATT_PAIRS, seq, LANES), F32),
                         pltpu.VMEM((ATT_PAIRS, LANES, seq), BF16)]
                        + [pltpu.VMEM((1, ATT_Q), F32)] * (2 * ATT_PAIRS)
                        + [pltpu.VMEM((LANES, ATT_Q), F32)] * (2 * ATT_PAIRS)
                        + [pltpu.VMEM((ATT_BLOCK, ATT_Q), F32)] * (2 * ATT_PAIRS)),
        compiler_params=_params("parallel", "parallel"),
        name="fox_attention",
    )(qkv, qkv, qkv, ccol, crow)


def fox_mixer(x, w_qkvf, b_f, batch, seq):
    qkv = matmul(x, w_qkvf[:, :3 * ATT_DIM].astype(BF16), BF16)
    w_f = jnp.pad(w_qkvf[:, 3 * ATT_DIM:], ((0, 0), (0, LANES - ATT_HEADS)))
    ccol, crow = fox_decay(matmul_f32(x, w_f), b_f, batch, seq)
    return fox_attention(qkv, ccol, crow, batch, seq)


MOE_TILE = 512
ROUTE_BLOCK = 512


def _top2(logits):
    lane = lax.broadcasted_iota(jnp.int32, logits.shape, 1)
    logits = jnp.where(lane < N_EXPERTS, logits, NEG_BIG)
    m1 = jnp.max(logits, axis=-1, keepdims=True)
    i1 = jnp.min(jnp.where(logits == m1, lane, LANES), axis=-1, keepdims=True)
    rest = jnp.where(lane == i1, NEG_BIG, logits)
    m2 = jnp.max(rest, axis=-1, keepdims=True)
    i2 = jnp.min(jnp.where(rest == m2, lane, LANES), axis=-1, keepdims=True)
    e2 = jnp.exp(m2 - m1)
    denom = 1.0 + e2
    idx = jnp.where(lane == 0, i1, jnp.where(lane == 1, i2, 0))
    gates = jnp.where(lane == 0, 1.0 / denom, jnp.where(lane == 1, e2 / denom, 0.0))
    return idx, gates


def _rank_block(idx, run_ref):
    tb = idx.shape[0]
    lane = lax.broadcasted_iota(jnp.int32, idx.shape, 1)
    oh0 = lane == idx[:, 0:1]
    oh1 = lane == idx[:, 1:2]
    both = jnp.where(oh0 | oh1, 1.0, 0.0)
    before = jnp.dot(_tril_f32(tb, strict=True).astype(BF16), both.astype(BF16),
                     preferred_element_type=F32) + run_ref[...]
    r0 = jnp.sum(jnp.where(oh0, before, 0.0), axis=-1, keepdims=True)
    r1 = jnp.sum(jnp.where(oh1, before, 0.0), axis=-1, keepdims=True)
    run_ref[...] += jnp.sum(both, axis=0, keepdims=True)
    return jnp.where(lane == 0, r0, jnp.where(lane == 1, r1, 0.0)).astype(jnp.int32)


def _pos_kernel(idx_ref, rank_ref, off_ref, pos_ref):
    idx = idx_ref[...]
    lane = lax.broadcasted_iota(jnp.int32, idx.shape, 1)
    off = off_ref[...]
    p0 = jnp.sum(jnp.where(lane == idx[:, 0:1], off, 0), axis=-1, keepdims=True)
    p1 = jnp.sum(jnp.where(lane == idx[:, 1:2], off, 0), axis=-1, keepdims=True)
    pos_ref[...] = rank_ref[...] + jnp.where(lane == 0, p0, jnp.where(lane == 1, p1, 0))


def moe_positions(idx, rank, offsets):
    t = idx.shape[0]
    tb = min(4 * ROUTE_BLOCK, t)
    off = jnp.pad(offsets, (0, LANES - N_EXPERTS)).reshape(1, LANES)
    blk = pl.BlockSpec((tb, LANES), lambda i: (i, 0))
    return pl.pallas_call(
        _pos_kernel,
        grid=(t // tb,),
        in_specs=[blk, blk, pl.BlockSpec((1, LANES), lambda i: (0, 0))],
        out_specs=blk,
        out_shape=jax.ShapeDtypeStruct((t, LANES), jnp.int32),
        compiler_params=_params("parallel"),
        name="moe_positions",
    )(idx, rank, off)


DMA_ISSUE_UNROLL = 8


def _tile_rows(row):
    return pl.ds(pl.multiple_of(row * ROW_CHUNKS, ROW_CHUNKS), ROW_CHUNKS)


def _dispatch_kernel(pos_ref, cnt_ref, off_ref, x_ref, xs_ref, zero_ref, sem, pad_sem, *, tb):
    base = pl.program_id(0) * (TOP_K * tb)

    def issue(t, _):
        for k in range(TOP_K):
            pltpu.make_async_copy(x_ref.at[_tile_rows(t)],
                                  xs_ref.at[_tile_rows(pos_ref[base + TOP_K * t + k])], sem).start()
        return 0

    lax.fori_loop(0, tb, issue, 0, unroll=DMA_ISSUE_UNROLL)

    @pl.when(pl.program_id(0) == 0)
    def _():
        zero_ref[...] = jnp.zeros_like(zero_ref)

        def fill_range(first, count):
            def copy(r):
                return pltpu.make_async_copy(zero_ref, xs_ref.at[_tile_rows(first + r)], pad_sem)

            def fill(r, _):
                copy(r).start()
                return 0

            def drain(r, _):
                copy(r).wait()
                return 0

            lax.fori_loop(0, count, fill, 0)
            lax.fori_loop(0, count, drain, 0)

        for e in range(N_EXPERTS):
            fill_range(off_ref[e] + cnt_ref[e], (-cnt_ref[e]) & (MOE_TILE - 1))
        last = N_EXPERTS - 1
        used = off_ref[last] + cnt_ref[last] + ((-cnt_ref[last]) & (MOE_TILE - 1))
        fill_range(used, xs_ref.shape[0] // ROW_CHUNKS - used)

    for _ in range(TOP_K):
        pltpu.make_async_copy(x_ref, xs_ref.at[pl.ds(0, tb * ROW_CHUNKS)], sem).wait()


def moe_dispatch(x_tiled, pos_flat, counts, offsets, n_rows):
    t = x_tiled.shape[0] // ROW_CHUNKS
    tb = min(ROUTE_BLOCK, t)
    grid_spec = pltpu.PrefetchScalarGridSpec(
        num_scalar_prefetch=3,
        grid=(t // tb,),
        in_specs=[pl.BlockSpec((tb * ROW_CHUNKS, LANES), lambda i, *_: (i, 0))],
        out_specs=pl.BlockSpec(memory_space=pl.ANY),
        scratch_shapes=[pltpu.VMEM((ROW_CHUNKS, LANES), F32), pltpu.SemaphoreType.DMA(()),
                        pltpu.SemaphoreType.DMA(())],
    )
    return pl.pallas_call(
        functools.partial(_dispatch_kernel, tb=tb),
        grid_spec=grid_spec,
        out_shape=jax.ShapeDtypeStruct((n_rows * ROW_CHUNKS, LANES), F32),
        compiler_params=_params("arbitrary"),
        name="moe_dispatch",
    )(pos_flat, counts, offsets, x_tiled)


def _moe_ffn_kernel(te_ref, nt_ref, x_ref, wg_ref, wu_ref, wd_ref, o_ref, xb_ref, acc_ref):
    i = pl.program_id(0)
    f = pl.program_id(1)

    @pl.when(i < nt_ref[0])
    def _():
        @pl.when(f == 0)
        def _():
            for j in range(ROW_CHUNKS):
                xb_ref[:, j * LANES:(j + 1) * LANES] = _load_row_tiled(x_ref, j, MOE_TILE).astype(BF16)
            acc_ref[...] = jnp.zeros_like(acc_ref)

        xb = xb_ref[...]
        gate = jnp.dot(xb, wg_ref[0], preferred_element_type=F32)
        up = jnp.dot(xb, wu_ref[0], preferred_element_type=F32)
        h = (_silu(gate) * up).astype(BF16)
        acc_ref[...] += jnp.dot(h, wd_ref[0], preferred_element_type=F32)

        @pl.when(f == pl.num_programs(1) - 1)
        def _():
            _store_row_tiled(o_ref, acc_ref[...])

    @pl.when((i >= nt_ref[0]) & (f == 0))
    def _():
        o_ref[...] = jnp.zeros_like(o_ref)


def moe_ffn(xs, tile_expert, n_tiles_used, wg, wu, wd, tf=1792):
    n_rows, d = xs.shape[0] // ROW_CHUNKS, D_MODEL
    n_tiles = n_rows // MOE_TILE
    tile_spec = lambda index: pl.BlockSpec((MOE_TILE * ROW_CHUNKS, LANES), index)
    fdim = wg.shape[2]
    nf = fdim // tf

    def live(i, nt):
        return jnp.minimum(i, nt[0] - 1)

    def fblk(i, f, nt):
        return jnp.where(i < nt[0], f, nf - 1)

    grid_spec = pltpu.PrefetchScalarGridSpec(
        num_scalar_prefetch=2,
        grid=(n_tiles, nf),
        in_specs=[tile_spec(lambda i, f, te, nt: (live(i, nt), 0)),
                  pl.BlockSpec((1, d, tf), lambda i, f, te, nt: (te[i], 0, fblk(i, f, nt))),
                  pl.BlockSpec((1, d, tf), lambda i, f, te, nt: (te[i], 0, fblk(i, f, nt))),
                  pl.BlockSpec((1, tf, d), lambda i, f, te, nt: (te[i], fblk(i, f, nt), 0))],
        out_specs=tile_spec(lambda i, f, te, nt: (i, 0)),
        scratch_shapes=[pltpu.VMEM((MOE_TILE, d), BF16), pltpu.VMEM((MOE_TILE, d), F32)],
    )
    return pl.pallas_call(
        _moe_ffn_kernel,
        grid_spec=grid_spec,
        out_shape=jax.ShapeDtypeStruct(xs.shape, F32),
        compiler_params=_params("arbitrary", "arbitrary"),
        name="moe_ffn",
    )(tile_expert, n_tiles_used, xs, wg, wu, wd)


def _combine_kernel(pos_ref, x_ref, gate_ref, g_ref, b_ref, ys_ref, o_ref, buf_ref, sem, *, tb):
    i = pl.program_id(0)
    slot = i % 2

    def start_gathers(block, into):
        base = block * (TOP_K * tb)

        def issue(t, _):
            for k in range(TOP_K):
                pltpu.make_async_copy(ys_ref.at[_tile_rows(pos_ref[base + TOP_K * t + k])],
                                      buf_ref.at[into, k, _tile_rows(t)], sem.at[into]).start()
            return 0

        lax.fori_loop(0, tb, issue, 0, unroll=DMA_ISSUE_UNROLL)

    @pl.when(i == 0)
    def _():
        start_gathers(0, 0)

    @pl.when(i + 1 < pl.num_programs(0))
    def _():
        start_gathers(i + 1, 1 - slot)

    for k in range(TOP_K):
        pltpu.make_async_copy(ys_ref.at[pl.ds(0, tb * ROW_CHUNKS)], buf_ref.at[slot, k],
                              sem.at[slot]).wait()
    gates = gate_ref[...]
    y = jnp.concatenate(
        [buf_ref[slot, 0, pl.ds(j, tb, stride=ROW_CHUNKS), :] * gates[:, 0:1]
         + buf_ref[slot, 1, pl.ds(j, tb, stride=ROW_CHUNKS), :] * gates[:, 1:2]
         for j in range(ROW_CHUNKS)], axis=1)
    o_ref[...] = _layer_norm_rows(DN_ALPHA * x_ref[...] + y, g_ref[...], b_ref[...])


def moe_combine(x, ys, pos_flat, gates, g, b):
    t, d = x.shape
    tb = min(ROUTE_BLOCK, t)
    grid_spec = pltpu.PrefetchScalarGridSpec(
        num_scalar_prefetch=1,
        grid=(t // tb,),
        in_specs=[pl.BlockSpec((tb, d), lambda i, *_: (i, 0)),
                  pl.BlockSpec((tb, LANES), lambda i, *_: (i, 0)),
                  pl.BlockSpec((1, d), lambda i, *_: (0, 0)),
                  pl.BlockSpec((1, d), lambda i, *_: (0, 0)),
                  pl.BlockSpec(memory_space=pl.ANY)],
        out_specs=pl.BlockSpec((tb, d), lambda i, *_: (i, 0)),
        scratch_shapes=[pltpu.VMEM((2, TOP_K, tb * ROW_CHUNKS, LANES), F32),
                        pltpu.SemaphoreType.DMA((2,))],
    )
    return pl.pallas_call(
        functools.partial(_combine_kernel, tb=tb),
        grid_spec=grid_spec,
        out_shape=jax.ShapeDtypeStruct((t, d), F32),
        compiler_params=_params("arbitrary"),
        name="moe_combine",
    )(pos_flat, x, gates, g.reshape(1, d), b.reshape(1, d), ys)


def moe_deepnorm(routed, wg, wu, wd, g, b):
    x, x_tiled, idx, gates, rank, counts_f = routed
    t = x.shape[0]
    counts = counts_f[0, :N_EXPERTS].astype(jnp.int32)
    padded = (counts + MOE_TILE - 1) // MOE_TILE * MOE_TILE
    ends = jnp.cumsum(padded)
    offsets = ends - padded
    n_tiles = (TOP_K * t) // MOE_TILE + N_EXPERTS
    n_used = (ends[-1] // MOE_TILE).astype(jnp.int32)
    tile_start = jnp.arange(n_tiles, dtype=jnp.int32) * MOE_TILE
    tile_start = jnp.minimum(tile_start, ends[-1] - MOE_TILE)
    tile_expert = jnp.sum(tile_start[:, None] >= ends[None, :], axis=1).astype(jnp.int32)
    pos = moe_positions(idx, rank, offsets)
    pos_flat = pos[:, :TOP_K].reshape(-1)
    xs = moe_dispatch(x_tiled, pos_flat, counts, offsets, n_tiles * MOE_TILE)
    ys = moe_ffn(xs, tile_expert, n_used.reshape(1), wg.astype(BF16), wu.astype(BF16), wd.astype(BF16))
    return moe_combine(x, ys, pos_flat, gates, g, b)


def kernel(x, l0_ssd_w_in, l0_ssd_conv_w, l0_ssd_conv_b, l0_ssd_dt_bias, l0_ssd_a_log, l0_ssd_d_skip, l0_ssd_norm_w, l0_ssd_w_out, l0_ln_mix_g, l0_ln_mix_b, l0_ffn_w_gate, l0_ffn_w_up, l0_ffn_w_down, l0_ln_ffn_g, l0_ln_ffn_b, l1_sb_w_qkv, l1_sb_w_out, l1_ln_mix_g, l1_ln_mix_b, l1_moe_w_router, l1_moe_w_gate, l1_moe_w_up, l1_moe_w_down, l1_ln_ffn_g, l1_ln_ffn_b, l2_fox_w_qkvf, l2_fox_b_f, l2_fox_w_out, l2_ln_mix_g, l2_ln_mix_b, l2_ffn_w_gate, l2_ffn_w_up, l2_ffn_w_down, l2_ln_ffn_g, l2_ln_ffn_b, l3_ssd_w_in, l3_ssd_conv_w, l3_ssd_conv_b, l3_ssd_dt_bias, l3_ssd_a_log, l3_ssd_d_skip, l3_ssd_norm_w, l3_ssd_w_out, l3_ln_mix_g, l3_ln_mix_b, l3_moe_w_router, l3_moe_w_gate, l3_moe_w_up, l3_moe_w_down, l3_ln_ffn_g, l3_ln_ffn_b):
    batch, seq, d = x.shape
    h = x.reshape(batch * seq, d)
    bf = lambda w: w.astype(BF16)
    y = ssd_mixer(h, l0_ssd_w_in, l0_ssd_conv_w, l0_ssd_conv_b, l0_ssd_dt_bias, l0_ssd_a_log,
                  l0_ssd_d_skip, l0_ssd_norm_w, batch, seq)
    h = mixer_out_ffn_deepnorm(y, bf(l0_ssd_w_out), h, l0_ln_mix_g, l0_ln_mix_b,
                               bf(l0_ffn_w_gate), bf(l0_ffn_w_up), bf(l0_ffn_w_down), l0_ln_ffn_g, l0_ln_ffn_b)
    o = sb_mixer(h, l1_sb_w_qkv, batch, seq)
    routed = matmul_deepnorm_route(o, bf(l1_sb_w_out), h, l1_ln_mix_g, l1_ln_mix_b, l1_moe_w_router)
    h = moe_deepnorm(routed, l1_moe_w_gate, l1_moe_w_up, l1_moe_w_down, l1_ln_ffn_g, l1_ln_ffn_b)
    o = fox_mixer(h, l2_fox_w_qkvf, l2_fox_b_f, batch, seq)
    h = mixer_out_ffn_deepnorm(o, bf(l2_fox_w_out), h, l2_ln_mix_g, l2_ln_mix_b,
                               bf(l2_ffn_w_gate), bf(l2_ffn_w_up), bf(l2_ffn_w_down), l2_ln_ffn_g, l2_ln_ffn_b)
    y = ssd_mixer(h, l3_ssd_w_in, l3_ssd_conv_w, l3_ssd_conv_b, l3_ssd_dt_bias, l3_ssd_a_log,
                  l3_ssd_d_skip, l3_ssd_norm_w, batch, seq)
    routed = matmul_deepnorm_route(y, bf(l3_ssd_w_out), h, l3_ln_mix_g, l3_ln_mix_b, l3_moe_w_router)
    h = moe_deepnorm(routed, l3_moe_w_gate, l3_moe_w_up, l3_moe_w_down, l3_ln_ffn_g, l3_ln_ffn_b)
    return h.reshape(batch, seq, d)
```

```python
import functools
import math

import jax
import jax.numpy as jnp
from jax import lax
from jax.experimental import pallas as pl
from jax.experimental.pallas import tpu as pltpu

F32 = jnp.float32
BF16 = jnp.bfloat16
HIGHEST = lax.Precision.HIGHEST

LANES = 128
SUBLANES = 8
VMEM_LIMIT_BYTES = 56 * 1024 * 1024

D_MODEL = 1024
DEPTH = 4
SSD_D_INNER = 2048
SSD_HEAD_DIM = 64
SSD_HEADS = 32
SSD_GROUPS = 4
SSD_STATE = 128
SSD_CONV = 4
SSD_GROUP_DIM = SSD_D_INNER // SSD_GROUPS
SSD_BC_DIM = 2 * SSD_GROUPS * SSD_STATE
ATT_HEAD_DIM = 64
ATT_HEADS = 16
ATT_DIM = 1024
N_EXPERTS = 8
TOP_K = 2
DN_ALPHA = (2.0 * DEPTH) ** 0.25
LN_EPS = 1e-5
RMS_EPS = 1e-5
NEG_BIG = -1e30


def _params(*semantics):
    return pltpu.CompilerParams(dimension_semantics=semantics, vmem_limit_bytes=VMEM_LIMIT_BYTES)


def _layer_norm_rows(h, g, b):
    mu = jnp.mean(h, axis=-1, keepdims=True)
    d = h - mu
    var = jnp.mean(d * d, axis=-1, keepdims=True)
    return d * lax.rsqrt(var + LN_EPS) * g + b


def _silu(x):
    half = 0.5 * x
    return half * (1.0 + jnp.tanh(half))


def _mm_kernel(x_ref, w_ref, o_ref, xb_ref):
    @pl.when(pl.program_id(1) == 0)
    def _():
        xb_ref[...] = x_ref[...].astype(BF16)

    o_ref[...] = jnp.dot(xb_ref[...], w_ref[...], preferred_element_type=F32).astype(o_ref.dtype)


def matmul(x, w, out_dtype, tm=1024, tn=1024):
    m, k = x.shape
    tm = min(tm, m)
    n = w.shape[1]
    return pl.pallas_call(
        _mm_kernel,
        grid=(m // tm, n // tn),
        in_specs=[pl.BlockSpec((tm, k), lambda i, j: (i, 0)),
                  pl.BlockSpec((k, tn), lambda i, j: (0, j))],
        out_specs=pl.BlockSpec((tm, tn), lambda i, j: (i, j)),
        out_shape=jax.ShapeDtypeStruct((m, n), out_dtype),
        scratch_shapes=[pltpu.VMEM((tm, k), BF16)],
        compiler_params=_params("parallel", "arbitrary"),
        name="matmul",
    )(x, w)


def _dot_split(x, w):
    xh = x.astype(BF16)
    xl = (x - xh.astype(F32)).astype(BF16)
    wh = w.astype(BF16)
    wl = (w - wh.astype(F32)).astype(BF16)
    dot = functools.partial(jnp.dot, preferred_element_type=F32)
    return dot(xh, wh) + (dot(xh, wl) + dot(xl, wh))


def _mm_f32_kernel(x_ref, w_ref, o_ref):
    o_ref[...] = _dot_split(x_ref[...], w_ref[...])


def matmul_f32(x, w, tm=1024):
    m, k = x.shape
    tm = min(tm, m)
    n = w.shape[1]
    return pl.pallas_call(
        _mm_f32_kernel,
        grid=(m // tm,),
        in_specs=[pl.BlockSpec((tm, k), lambda i: (i, 0)),
                  pl.BlockSpec((k, n), lambda i: (0, 0))],
        out_specs=pl.BlockSpec((tm, n), lambda i: (i, 0)),
        out_shape=jax.ShapeDtypeStruct((m, n), F32),
        compiler_params=_params("parallel"),
        name="matmul_f32",
    )(x, w)


ROW_CHUNKS = D_MODEL // LANES


def _store_row_tiled(ref, value):
    n = value.shape[0]
    for j in range(ROW_CHUNKS):
        ref[pl.ds(j, n, stride=ROW_CHUNKS), :] = value[:, j * LANES:(j + 1) * LANES]


def _load_row_tiled(ref, j, n):
    return ref[pl.ds(j, n, stride=ROW_CHUNKS), :]


def _mm_ln_route_kernel(x_ref, w_ref, r_ref, g_ref, b_ref, wr_ref,
                        o_ref, tiled_ref, idx_ref, gate_ref, rank_ref, count_ref, run_ref):
    @pl.when(pl.program_id(0) == 0)
    def _():
        run_ref[...] = jnp.zeros_like(run_ref)

    y = jnp.dot(x_ref[...], w_ref[...], preferred_element_type=F32)
    o = _layer_norm_rows(DN_ALPHA * r_ref[...] + y, g_ref[...], b_ref[...])
    o_ref[...] = o
    _store_row_tiled(tiled_ref, o)
    idx, gates = _top2(_dot_split(o, wr_ref[...]))
    idx_ref[...] = idx
    gate_ref[...] = gates
    rank_ref[...] = _rank_block(idx, run_ref)
    count_ref[...] = run_ref[...]


def matmul_deepnorm_route(x, w, resid, g, b, w_router):
    m, k = x.shape
    tm = min(ROUTE_BLOCK, m)
    d = w.shape[1]
    rows = lambda width: pl.BlockSpec((tm, width), lambda i: (i, 0))
    fixed = lambda a: pl.BlockSpec(a.shape, lambda i: (0, 0))
    args = [x, w, resid, g.reshape(1, d), b.reshape(1, d),
            jnp.pad(w_router, ((0, 0), (0, LANES - N_EXPERTS)))]
    return pl.pallas_call(
        _mm_ln_route_kernel,
        grid=(m // tm,),
        in_specs=[rows(k), fixed(w), rows(d), fixed(args[3]), fixed(args[4]), fixed(args[5])],
        out_specs=[rows(d), pl.BlockSpec((tm * ROW_CHUNKS, LANES), lambda i: (i, 0)),
                   rows(LANES), rows(LANES), rows(LANES), pl.BlockSpec((1, LANES), lambda i: (0, 0))],
        out_shape=[jax.ShapeDtypeStruct((m, d), F32),
                   jax.ShapeDtypeStruct((m * ROW_CHUNKS, LANES), F32),
                   jax.ShapeDtypeStruct((m, LANES), jnp.int32),
                   jax.ShapeDtypeStruct((m, LANES), F32),
                   jax.ShapeDtypeStruct((m, LANES), jnp.int32),
                   jax.ShapeDtypeStruct((1, LANES), F32)],
        scratch_shapes=[pltpu.VMEM((1, LANES), F32)],
        compiler_params=_params("arbitrary"),
        name="matmul_deepnorm_route",
    )(*args)


def _mix_ffn_kernel(o_ref, wo_ref, r_ref, g1_ref, b1_ref, wg_ref, wu_ref, wd_ref, g2_ref, b2_ref, out_ref):
    x = _layer_norm_rows(
        DN_ALPHA * r_ref[...] + jnp.dot(o_ref[...], wo_ref[...], preferred_element_type=F32),
        g1_ref[...], b1_ref[...])
    xb = x.astype(BF16)
    gate = jnp.dot(xb, wg_ref[...], preferred_element_type=F32)
    up = jnp.dot(xb, wu_ref[...], preferred_element_type=F32)
    h = (_silu(gate) * up).astype(BF16)
    y = jnp.dot(h, wd_ref[...], preferred_element_type=F32)
    out_ref[...] = _layer_norm_rows(DN_ALPHA * x + y, g2_ref[...], b2_ref[...])


def mixer_out_ffn_deepnorm(o, w_out, resid, g1, b1, wg, wu, wd, g2, b2, tm=512):
    m, k = o.shape
    d = w_out.shape[1]
    tm = min(tm, m)
    resident = lambda a: pl.BlockSpec(a.shape, lambda i: (0, 0), pipeline_mode=pl.Buffered(1))
    vec = lambda v: v.reshape(1, d)
    args = (o, w_out, resid, vec(g1), vec(b1), wg, wu, wd, vec(g2), vec(b2))
    return pl.pallas_call(
        _mix_ffn_kernel,
        grid=(m // tm,),
        in_specs=[pl.BlockSpec((tm, k), lambda i: (i, 0)), resident(w_out),
                  pl.BlockSpec((tm, d), lambda i: (i, 0)), resident(args[3]), resident(args[4]),
                  resident(wg), resident(wu), resident(wd), resident(args[8]), resident(args[9])],
        out_specs=pl.BlockSpec((tm, d), lambda i: (i, 0)),
        out_shape=jax.ShapeDtypeStruct((m, d), F32),
        compiler_params=_params("parallel"),
        name="mixer_out_ffn_deepnorm",
    )(*args)


def _split_bf16(v, pieces):
    out = []
    r = v
    for _ in range(pieces - 1):
        p = r.astype(BF16)
        out.append(p)
        r = r - p.astype(F32)
    out.append(r.astype(BF16))
    return out


def _expand(v, e_ref, pieces):
    stacked = jnp.concatenate(_split_bf16(v, pieces), axis=1)
    return jnp.dot(stacked, e_ref[...], preferred_element_type=F32)


def _tril_f32(n, strict=False):
    r = lax.broadcasted_iota(jnp.int32, (n, n), 0)
    c = lax.broadcasted_iota(jnp.int32, (n, n), 1)
    return ((r > c) if strict else (r >= c)).astype(F32)


def _ssd_kernel(z_ref, xs_ref, bc_ref, dt_ref,
                dtb_ref, alog_ref, dskip_ref, normw_ref, e64_ref, e128_ref, o_ref,
                state_ref, xdt_ref, acol_ref, arow_ref, cb_ref, ydiag_ref,
                *, chunk):
    L = chunk
    c = pl.program_id(1)

    @pl.when(c == 0)
    def _():
        state_ref[...] = jnp.zeros_like(state_ref)

    xs = xs_ref[...]
    bcv = bc_ref[...]

    dt = jax.nn.softplus(dt_ref[...] + dtb_ref[...])
    da = dt * (-jnp.exp(alog_ref[...]))
    a_cs = jnp.dot(_tril_f32(L), da, preferred_element_type=F32, precision=HIGHEST)
    ea = jnp.exp(a_cs)
    dte = jnp.exp(a_cs[L - 1:L, :] - a_cs)

    dt_x = _expand(dt, e64_ref, 2)
    ea_x = _expand(ea, e64_ref, 2)
    dte_x = _expand(dte, e64_ref, 2)
    acol_ref[...] = _expand(a_cs, e128_ref, 3)
    a_t = a_cs.T
    for h in range(SSD_HEADS):
        arow_ref[h] = jnp.broadcast_to(a_t[h:h + 1, :], (SUBLANES, L))

    xdt = xs * dt_x
    xdt_ref[...] = xdt.astype(BF16)
    xdte = (xdt * dte_x).astype(BF16)

    y_off = []
    for g in range(SSD_GROUPS):
        bm = bcv[:, g * SSD_STATE:(g + 1) * SSD_STATE]
        cm = bcv[:, (SSD_GROUPS + g) * SSD_STATE:(SSD_GROUPS + g + 1) * SSD_STATE].astype(BF16)
        cb_ref[g] = lax.dot_general(cm, bm.astype(BF16), (((1,), (1,)), ((), ())),
                                    preferred_element_type=F32)
        gs = slice(g * SSD_GROUP_DIM, (g + 1) * SSD_GROUP_DIM)
        st = state_ref[g]
        y_off.append(jnp.dot(cm, st.astype(BF16), preferred_element_type=F32) * ea_x[:, gs])
        state_ref[g] = st * ea_x[L - 1:L, gs] + jnp.dot(
            bm.T.astype(BF16), xdte[:, gs], preferred_element_type=F32)

    row = lax.broadcasted_iota(jnp.int32, (L, L), 0)
    col = lax.broadcasted_iota(jnp.int32, (L, L), 1)
    causal = row >= col
    head0 = lax.broadcasted_iota(jnp.int32, (L, LANES), 1) < SSD_HEAD_DIM

    heads_per_group = SSD_HEADS // SSD_GROUPS

    def group_body(g, carry):
        cbg = cb_ref[g]
        scores = []
        for j in range(heads_per_group):
            h = g * heads_per_group + j
            a_l = acol_ref[:, pl.ds(pl.multiple_of(h * LANES, LANES), LANES)]
            a_l = jnp.concatenate([a_l] * (L // LANES), axis=1)
            a_s = arow_ref[h][0:1, :]
            scores.append((cbg * jnp.exp(jnp.where(causal, a_l - a_s, NEG_BIG))).astype(BF16))
        for pp in range(heads_per_group // 2):
            lanes = pl.ds(pl.multiple_of((g * (heads_per_group // 2) + pp) * LANES, LANES), LANES)
            xpair = xdt_ref[:, lanes]
            ys = [jnp.dot(scores[2 * pp + j], xpair, preferred_element_type=F32) for j in range(2)]
            ydiag_ref[:, lanes] = jnp.where(head0, ys[0], ys[1])
        return carry

    lax.fori_loop(0, SSD_GROUPS, group_body, 0)

    y = ydiag_ref[...] + jnp.concatenate(y_off, axis=1) + dskip_ref[...] * xs
    y = y * _silu(z_ref[...])
    parts = []
    for g in range(SSD_GROUPS):
        yg = y[:, g * SSD_GROUP_DIM:(g + 1) * SSD_GROUP_DIM]
        parts.append(yg * lax.rsqrt(jnp.mean(yg * yg, axis=-1, keepdims=True) + RMS_EPS))
    o_ref[...] = (jnp.concatenate(parts, axis=1) * normw_ref[...]).astype(o_ref.dtype)


def _expansion_matrix(width, pieces):
    h = jnp.arange(LANES)[:, None]
    lane = jnp.arange(SSD_HEADS * width)[None, :]
    e = (lane // width == h).astype(BF16)
    return jnp.concatenate([e] * pieces, axis=0)


def _mm_conv_kernel(x_ref, halo_ref, w_ref, cw_ref, cb_ref, o_ref, ext_ref, xb_ref, *, tiles_per_seq):
    tm = x_ref.shape[0]

    @pl.when(pl.program_id(1) == 0)
    def _():
        xb_ref[...] = x_ref[...].astype(BF16)

    w = w_ref[...]
    halo = jnp.dot(halo_ref[...].astype(BF16), w, preferred_element_type=F32)
    first = pl.program_id(0) % tiles_per_seq == 0
    ext_ref[0:SUBLANES, :] = jnp.where(first, 0.0, halo)
    ext_ref[SUBLANES:SUBLANES + tm, :] = jnp.dot(xb_ref[...], w, preferred_element_type=F32)
    ext = ext_ref[...]
    acc = cw_ref[0:1, :] * ext
    for k in range(1, SSD_CONV):
        acc = pltpu.roll(acc, 1, axis=0) + cw_ref[k:k + 1, :] * ext
    o_ref[...] = _silu(acc[SUBLANES:SUBLANES + tm, :] + cb_ref[...])


def matmul_conv_silu(x, w, conv_w, conv_b, seq, tm=1024, tn=1024):
    m, k = x.shape
    tm = min(tm, seq)
    n = w.shape[1]
    halo_blocks = tm // SUBLANES
    return pl.pallas_call(
        functools.partial(_mm_conv_kernel, tiles_per_seq=seq // tm),
        grid=(m // tm, n // tn),
        in_specs=[pl.BlockSpec((tm, k), lambda i, j: (i, 0)),
                  pl.BlockSpec((SUBLANES, k), lambda i, j: (jnp.maximum(i * halo_blocks - 1, 0), 0)),
                  pl.BlockSpec((k, tn), lambda i, j: (0, j)),
                  pl.BlockSpec((SSD_CONV, tn), lambda i, j: (0, j)),
                  pl.BlockSpec((1, tn), lambda i, j: (0, j))],
        out_specs=pl.BlockSpec((tm, tn), lambda i, j: (i, j)),
        out_shape=jax.ShapeDtypeStruct((m, n), F32),
        scratch_shapes=[pltpu.VMEM((tm + SUBLANES, tn), F32), pltpu.VMEM((tm, k), BF16)],
        compiler_params=_params("parallel", "arbitrary"),
        name="matmul_conv_silu",
    )(x, x, w, conv_w, conv_b.reshape(1, n))


def ssd_core(z, xbc, dt_raw, dt_bias, a_log, d_skip, norm_w, batch, seq, chunk=128):
    t = z.shape[0]
    nc = seq // chunk
    pad = LANES - SSD_HEADS
    row = lambda v: v.reshape(1, -1).astype(F32)
    args = (
        z, xbc, xbc, dt_raw,
        row(jnp.pad(dt_bias, (0, pad))), row(jnp.pad(a_log, (0, pad))),
        row(jnp.repeat(d_skip, SSD_HEAD_DIM)), row(norm_w),
        _expansion_matrix(SSD_HEAD_DIM, 2), _expansion_matrix(LANES, 3),
    )
    blk = lambda b, c: (b * nc + c, 0)
    const = lambda b, c: (0, 0)
    full = lambda a: pl.BlockSpec(a.shape, const)
    in_specs = [
        pl.BlockSpec((chunk, SSD_D_INNER), blk),
        pl.BlockSpec((chunk, SSD_D_INNER), blk),
        pl.BlockSpec((chunk, SSD_BC_DIM), lambda b, c: (b * nc + c, 2)),
        pl.BlockSpec((chunk, LANES), blk),
    ] + [full(a) for a in args[4:]]
    return pl.pallas_call(
        functools.partial(_ssd_kernel, chunk=chunk),
        grid=(batch, nc),
        in_specs=in_specs,
        out_specs=pl.BlockSpec((chunk, SSD_D_INNER), blk),
        out_shape=jax.ShapeDtypeStruct((t, SSD_D_INNER), BF16),
        scratch_shapes=[
            pltpu.VMEM((SSD_GROUPS, SSD_STATE, SSD_GROUP_DIM), F32),
            pltpu.VMEM((chunk, SSD_D_INNER), BF16),
            pltpu.VMEM((chunk, SSD_HEADS * LANES), F32),
            pltpu.VMEM((SSD_HEADS, SUBLANES, chunk), F32),
            pltpu.VMEM((SSD_GROUPS, chunk, chunk), F32),
            pltpu.VMEM((chunk, SSD_D_INNER), F32),
        ],
        compiler_params=_params("parallel", "arbitrary"),
        name="ssd_core",
    )(*args)


def ssd_mixer(x, w_in, conv_w, conv_b, dt_bias, a_log, d_skip, norm_w, batch, seq):
    n_zx = SSD_D_INNER + SSD_D_INNER + SSD_BC_DIM
    z = matmul(x, w_in[:, :SSD_D_INNER].astype(BF16), F32)
    xbc = matmul_conv_silu(x, w_in[:, SSD_D_INNER:n_zx].astype(BF16), conv_w, conv_b, seq)
    w_dt = jnp.pad(w_in[:, n_zx:], ((0, 0), (0, LANES - SSD_HEADS)))
    dt_raw = matmul_f32(x, w_dt)
    return ssd_core(z, xbc, dt_raw, dt_bias, a_log, d_skip, norm_w, batch, seq)


ATT_BLOCK = 128
ATT_Q = 256
ATT_PAIRS = 4
ATT_STEP_LANES = ATT_PAIRS * LANES
ATT_SCALE = ATT_HEAD_DIM ** -0.5
LOG2E = 1.4426950408889634
EXP2_UNDERFLOW = -160.0
FOX_BOUND_SLACK = 1.05


def _split_heads(q_ref, rows, head0):
    qs = []
    for p in range(ATT_PAIRS):
        q2 = q_ref[rows, p * LANES:(p + 1) * LANES]
        zero = jnp.zeros_like(q2)
        qs += [jnp.where(head0, q2, zero), jnp.where(head0, zero, q2)]
    return qs


def _sb_kernel(q_ref, k_ref, v_ref, u_ref, o_ref, *scratch, seq):
    n_heads = 2 * ATT_PAIRS
    later_ref, acc_ref, z_ref = scratch[:n_heads], scratch[n_heads:2 * n_heads], scratch[2 * n_heads:]
    nq = seq // ATT_Q
    ndiag = ATT_Q // ATT_BLOCK
    head0 = lax.broadcasted_iota(jnp.int32, (ATT_Q, LANES), 1) < ATT_HEAD_DIM
    row = lax.broadcasted_iota(jnp.int32, (ATT_Q, ATT_BLOCK), 0)
    col = lax.broadcasted_iota(jnp.int32, (ATT_Q, ATT_BLOCK), 1)

    def q_body(i, _):
        q0 = pl.multiple_of(i * ATT_Q, ATT_Q)
        rows = pl.ds(q0, ATT_Q)
        qs = _split_heads(q_ref, rows, head0)
        for n in range(2 * ATT_PAIRS):
            later_ref[n][...] = jnp.zeros((ATT_Q, LANES), F32)
            acc_ref[n][...] = jnp.zeros((ATT_Q, LANES), F32)

        heads = range(2 * ATT_PAIRS)

        def key_rows(j):
            return pl.ds(pl.multiple_of(j * ATT_BLOCK, ATT_BLOCK), ATT_BLOCK)

        def scores(j):
            k2 = [k_ref[key_rows(j), p * LANES:(p + 1) * LANES] for p in range(ATT_PAIRS)]
            return [lax.dot_general(qs[n], k2[n // 2], (((1,), (1,)), ((), ())),
                                    preferred_element_type=F32) * (ATT_SCALE * LOG2E) for n in heads]

        def consume(z2, j, strict):
            keys = key_rows(j)
            v2 = [v_ref[keys, p * LANES:(p + 1) * LANES] for p in range(ATT_PAIRS)]
            log_beta, sums = [], []
            for n in heads:
                lb = jnp.minimum(z2[n], 0.0) - jnp.log2(1.0 + jnp.exp2(-jnp.abs(z2[n])))
                log_keep = lb - z2[n]
                if strict is not None:
                    log_keep = jnp.where(strict, log_keep, 0.0)
                hi = log_keep.astype(BF16)
                lo = (log_keep - hi.astype(F32)).astype(BF16)
                log_beta.append(lb)
                sums.append(jnp.dot(jnp.concatenate([hi, lo], axis=1), u_ref[...],
                                    preferred_element_type=F32))
            for n in heads:
                w = jnp.exp2(log_beta[n] + sums[n][:, :ATT_BLOCK] + later_ref[n][...])
                if strict is not None:
                    w = jnp.where(strict, w, 0.0)
                acc_ref[n][...] += jnp.dot(w.astype(BF16), v2[n // 2], preferred_element_type=F32)
                later_ref[n][...] += sums[n][:, ATT_BLOCK:]

        for d in reversed(range(ndiag)):
            consume(scores(i * ndiag + d), i * ndiag + d, d * ATT_BLOCK + col < row)

        n_off = i * ndiag
        for n, zn in enumerate(scores(jnp.maximum(n_off - 1, 0))):
            z_ref[n][...] = zn

        def off_diagonal(carry):
            jj, _ = carry
            z2 = [z_ref[n][...] for n in heads]
            ahead = scores(jnp.maximum(n_off - 2 - jj, 0))
            consume(z2, n_off - 1 - jj, None)
            for n in heads:
                z_ref[n][...] = ahead[n]
            top = later_ref[0][...]
            for n in range(1, 2 * ATT_PAIRS):
                top = jnp.maximum(top, later_ref[n][...])
            return jj + 1, jnp.max(top) > EXP2_UNDERFLOW

        lax.while_loop(lambda c: (c[0] < n_off) & c[1], off_diagonal, (jnp.int32(0), n_off > 0))
        for p in range(ATT_PAIRS):
            o_ref[rows, p * LANES:(p + 1) * LANES] = jnp.where(
                head0, acc_ref[2 * p][...], acc_ref[2 * p + 1][...]).astype(o_ref.dtype)
        return 0

    lax.fori_loop(0, nq, q_body, 0)


def _suffix_sum_matrix():
    j = jnp.arange(2 * ATT_BLOCK)[:, None] % ATT_BLOCK
    s = jnp.arange(2 * ATT_BLOCK)[None, :]
    return jnp.where(s < ATT_BLOCK, j > s, True).astype(BF16)


def _attention_specs(seq):
    n_steps = ATT_HEADS // 2 // ATT_PAIRS
    blk = lambda off: pl.BlockSpec((seq, ATT_STEP_LANES), lambda b, p: (b, off + p))
    specs = [blk(0), blk(n_steps), blk(2 * n_steps)]
    return n_steps, specs, pl.BlockSpec((seq, ATT_STEP_LANES), lambda b, p: (b, p))


def sb_attention(qkv, batch, seq):
    n_steps, in_specs, out_spec = _attention_specs(seq)
    u = _suffix_sum_matrix()
    return pl.pallas_call(
        functools.partial(_sb_kernel, seq=seq),
        grid=(batch, n_steps),
        in_specs=in_specs + [pl.BlockSpec(u.shape, lambda b, p: (0, 0))],
        out_specs=out_spec,
        out_shape=jax.ShapeDtypeStruct((batch * seq, ATT_DIM), BF16),
        scratch_shapes=[pltpu.VMEM((ATT_Q, LANES), F32)] * (6 * ATT_PAIRS),
        compiler_params=_params("parallel", "parallel"),
        name="sb_attention",
    )(qkv, qkv, qkv, u)


def sb_mixer(x, w_qkv, batch, seq):
    return sb_attention(matmul(x, w_qkv.astype(BF16), BF16), batch, seq)


CUMSUM_BLOCK = 256


def _fox_decay_kernel(f_ref, bf_ref, ccol_ref, crow_ref, *, seq):
    tri = _tril_f32(CUMSUM_BLOCK)
    carry = jnp.zeros((1, LANES), F32)
    for blk in range(seq // CUMSUM_BLOCK):
        rows = slice(blk * CUMSUM_BLOCK, (blk + 1) * CUMSUM_BLOCK)
        log_f = jax.nn.log_sigmoid(f_ref[rows, :] + bf_ref[...])
        c = jnp.dot(tri, log_f, preferred_element_type=F32, precision=HIGHEST) + carry
        carry = c[CUMSUM_BLOCK - 1:CUMSUM_BLOCK, :]
        ccol_ref[rows, :] = c
        c_t = c.T
        for p in range(ATT_HEADS // 2):
            crow_ref[0, p, :, rows] = c_t[2 * p:2 * p + 2, :]


def fox_decay(f_raw, b_f, batch, seq):
    return pl.pallas_call(
        functools.partial(_fox_decay_kernel, seq=seq),
        grid=(batch,),
        in_specs=[pl.BlockSpec((seq, LANES), lambda b: (b, 0)),
                  pl.BlockSpec((1, LANES), lambda b: (0, 0))],
        out_specs=[pl.BlockSpec((seq, LANES), lambda b: (b, 0)),
                   pl.BlockSpec((1, ATT_HEADS // 2, 2, seq), lambda b: (b, 0, 0, 0))],
        out_shape=[jax.ShapeDtypeStruct((batch * seq, LANES), F32),
                   jax.ShapeDtypeStruct((batch, ATT_HEADS // 2, 2, seq), F32)],
        compiler_params=_params("parallel"),
        name="fox_decay",
    )(f_raw, jnp.pad(b_f, (0, LANES - ATT_HEADS)).reshape(1, LANES))


def _fox_kernel(q_ref, k_ref, v_ref, ccol_ref, crow_ref, o_ref, crep_ref, vt_ref, *scratch, seq):
    n_heads = 2 * ATT_PAIRS
    m_ref, acc_ref, s_ref = scratch[:n_heads], scratch[n_heads:2 * n_heads], scratch[2 * n_heads:]
    nq = seq // ATT_Q
    ndiag = ATT_Q // ATT_BLOCK
    step = pl.program_id(1)
    heads = range(n_heads)
    head0 = lax.broadcasted_iota(jnp.int32, (ATT_Q, LANES), 1) < ATT_HEAD_DIM
    head0_rows = lax.broadcasted_iota(jnp.int32, (LANES, ATT_BLOCK), 0) < ATT_HEAD_DIM
    key = lax.broadcasted_iota(jnp.int32, (ATT_BLOCK, ATT_Q), 0)
    qry = lax.broadcasted_iota(jnp.int32, (ATT_BLOCK, ATT_Q), 1)

    pieces = jnp.concatenate(_split_bf16(ccol_ref[...], 3), axis=1)
    sel_row = lax.broadcasted_iota(jnp.int32, (3 * LANES, LANES), 0) & (LANES - 1)
    for n in heads:
        sel = jnp.where(sel_row == 2 * ATT_PAIRS * step + n, 1.0, 0.0).astype(BF16)
        crep_ref[n] = jnp.dot(pieces, sel, preferred_element_type=F32) * LOG2E
    for p in range(ATT_PAIRS):
        for blk in range(seq // ATT_BLOCK):
            rows = slice(blk * ATT_BLOCK, (blk + 1) * ATT_BLOCK)
            vt_ref[p, :, rows] = v_ref[rows, p * LANES:(p + 1) * LANES].astype(F32).T.astype(BF16)
    lane_r = lax.broadcasted_iota(jnp.int32, (LANES, LANES), 0) < ATT_HEAD_DIM
    lane_c = lax.broadcasted_iota(jnp.int32, (LANES, LANES), 1) < ATT_HEAD_DIM
    same_head = jnp.where(lane_r == lane_c, 1.0, 0.0).astype(BF16)
    k_norm = []
    for p in range(ATT_PAIRS):
        kf = k_ref[:, p * LANES:(p + 1) * LANES].astype(F32)
        sq = jnp.dot((kf * kf).astype(BF16), same_head, preferred_element_type=F32)
        top = jnp.sqrt(jnp.max(sq, axis=0, keepdims=True))
        k_norm += [top[:, 0:1], top[:, ATT_HEAD_DIM:ATT_HEAD_DIM + 1]]
    ones_rows = jnp.ones((SUBLANES, LANES), BF16)

    def q_body(i, _):
        q0 = pl.multiple_of(i * ATT_Q, ATT_Q)
        rows = pl.ds(q0, ATT_Q)
        qs = _split_heads(q_ref, rows, head0)
        c_q = [crow_ref[0, n // 2, n % 2:n % 2 + 1, rows] * LOG2E for n in heads]
        qk_bound = []
        for n in heads:
            qf = qs[n].astype(F32)
            q_sq = lax.dot_general(ones_rows, (qf * qf).astype(BF16), (((1,), (1,)), ((), ())),
                                   preferred_element_type=F32)[0:1, :]
            qk_bound.append(jnp.sqrt(q_sq) * k_norm[n] * (ATT_SCALE * LOG2E * FOX_BOUND_SLACK))
        for n in heads:
            m_ref[n][...] = jnp.full((1, ATT_Q), NEG_BIG, F32)
            acc_ref[n][...] = jnp.zeros((LANES, ATT_Q), F32)

        def key_rows(j):
            return pl.ds(pl.multiple_of(j * ATT_BLOCK, ATT_BLOCK), ATT_BLOCK)

        def scores(j):
            k2 = [k_ref[key_rows(j), p * LANES:(p + 1) * LANES] for p in range(ATT_PAIRS)]
            return [lax.dot_general(k2[n // 2], qs[n], (((1,), (1,)), ((), ())),
                                    preferred_element_type=F32) * (ATT_SCALE * LOG2E) for n in heads]

        def consume(s, j, causal):
            keys = key_rows(j)
            vt = [vt_ref[p, :, keys] for p in range(ATT_PAIRS)]
            for n in heads:
                c_k = crep_ref[n, keys, :]
                sn = s[n] + (c_q[n] - jnp.concatenate([c_k] * (ATT_Q // LANES), axis=1))
                if causal is not None:
                    sn = jnp.where(causal, sn, NEG_BIG)
                m_old = m_ref[n][...]
                m_new = jnp.maximum(m_old, jnp.max(sn, axis=0, keepdims=True))
                prob = jnp.exp2(sn - m_new).astype(BF16)
                one = jnp.ones_like(vt[n // 2])
                v1t = jnp.where(head0_rows, vt[n // 2], one) if n % 2 == 0 else jnp.where(
                    head0_rows, one, vt[n // 2])
                acc_ref[n][...] = jnp.exp2(m_old - m_new) * acc_ref[n][...] + jnp.dot(
                    v1t, prob, preferred_element_type=F32)
                m_ref[n][...] = m_new

        for d in range(ndiag):
            consume(scores(i * ndiag + d), i * ndiag + d, d * ATT_BLOCK + key <= qry)

        n_off = i * ndiag
        for n, sn in enumerate(scores(jnp.maximum(n_off - 1, 0))):
            s_ref[n][...] = sn

        def off_diagonal(carry):
            jj, _ = carry
            j = n_off - 1 - jj
            s = [s_ref[n][...] for n in heads]
            ahead = scores(jnp.maximum(j - 1, 0))
            consume(s, j, None)
            for n in heads:
                s_ref[n][...] = ahead[n]
            prev_keys = key_rows(jnp.maximum(j - 1, 0))
            worst = None
            for n in heads:
                c_last = crow_ref[0, n // 2, n % 2:n % 2 + 1, prev_keys][:, LANES - 1:LANES] * LOG2E
                gap = qk_bound[n] + c_q[n] - c_last - m_ref[n][...]
                worst = gap if worst is None else jnp.maximum(worst, gap)
            return jj + 1, jnp.max(worst) > EXP2_UNDERFLOW

        lax.while_loop(lambda c: (c[0] < n_off) & c[1], off_diagonal, (jnp.int32(0), n_off > 0))
        for p in range(ATT_PAIRS):
            a0, a1 = acc_ref[2 * p][...], acc_ref[2 * p + 1][...]
            o_t = jnp.concatenate([a0[:ATT_HEAD_DIM] / a0[ATT_HEAD_DIM:],
                                   a1[ATT_HEAD_DIM:] / a1[:ATT_HEAD_DIM]], axis=0)
            o_ref[rows, p * LANES:(p + 1) * LANES] = o_t.T.astype(o_ref.dtype)
        return 0

    lax.fori_loop(0, nq, q_body, 0)


def fox_attention(qkv, ccol, crow, batch, seq):
    n_steps, in_specs, out_spec = _attention_specs(seq)
    in_specs += [pl.BlockSpec((seq, LANES), lambda b, p: (b, 0)),
                 pl.BlockSpec((1, ATT_PAIRS, 2, seq), lambda b, p: (b, p, 0, 0))]
    return pl.pallas_call(
        functools.partial(_fox_kernel, seq=seq),
        grid=(batch, n_steps),
        in_specs=in_specs,
        out_specs=out_spec,
        out_shape=jax.ShapeDtypeStruct((batch * seq, ATT_DIM), BF16),
        scratch_shapes=([pltpu.VMEM((2 * ATT_PAIRS, seq, LANES), F32),
                         pltpu.VMEM((ATT_PAIRS, LANES, seq), BF16)]
                        + [pltpu.VMEM((1, ATT_Q), F32)] * (2 * ATT_PAIRS)
                        + [pltpu.VMEM((LANES, ATT_Q), F32)] * (2 * ATT_PAIRS)
                        + [pltpu.VMEM((ATT_BLOCK, ATT_Q), F32)] * (2 * ATT_PAIRS)),
        compiler_params=_params("parallel", "parallel"),
        name="fox_attention",
    )(qkv, qkv, qkv, ccol, crow)


def fox_mixer(x, w_qkvf, b_f, batch, seq):
    qkv = matmul(x, w_qkvf[:, :3 * ATT_DIM].astype(BF16), BF16)
    w_f = jnp.pad(w_qkvf[:, 3 * ATT_DIM:], ((0, 0), (0, LANES - ATT_HEADS)))
    ccol, crow = fox_decay(matmul_f32(x, w_f), b_f, batch, seq)
    return fox_attention(qkv, ccol, crow, batch, seq)


MOE_TILE = 512
ROUTE_BLOCK = 512


def _top2(logits):
    lane = lax.broadcasted_iota(jnp.int32, logits.shape, 1)
    logits = jnp.where(lane < N_EXPERTS, logits, NEG_BIG)
    m1 = jnp.max(logits, axis=-1, keepdims=True)
    i1 = jnp.min(jnp.where(logits == m1, lane, LANES), axis=-1, keepdims=True)
    rest = jnp.where(lane == i1, NEG_BIG, logits)
    m2 = jnp.max(rest, axis=-1, keepdims=True)
    i2 = jnp.min(jnp.where(rest == m2, lane, LANES), axis=-1, keepdims=True)
    e2 = jnp.exp(m2 - m1)
    denom = 1.0 + e2
    idx = jnp.where(lane == 0, i1, jnp.where(lane == 1, i2, 0))
    gates = jnp.where(lane == 0, 1.0 / denom, jnp.where(lane == 1, e2 / denom, 0.0))
    return idx, gates


def _rank_block(idx, run_ref):
    tb = idx.shape[0]
    lane = lax.broadcasted_iota(jnp.int32, idx.shape, 1)
    oh0 = lane == idx[:, 0:1]
    oh1 = lane == idx[:, 1:2]
    both = jnp.where(oh0 | oh1, 1.0, 0.0)
    before = jnp.dot(_tril_f32(tb, strict=True).astype(BF16), both.astype(BF16),
                     preferred_element_type=F32) + run_ref[...]
    r0 = jnp.sum(jnp.where(oh0, before, 0.0), axis=-1, keepdims=True)
    r1 = jnp.sum(jnp.where(oh1, before, 0.0), axis=-1, keepdims=True)
    run_ref[...] += jnp.sum(both, axis=0, keepdims=True)
    return jnp.where(lane == 0, r0, jnp.where(lane == 1, r1, 0.0)).astype(jnp.int32)


def _pos_kernel(idx_ref, rank_ref, off_ref, pos_ref):
    idx = idx_ref[...]
    lane = lax.broadcasted_iota(jnp.int32, idx.shape, 1)
    off = off_ref[...]
    p0 = jnp.sum(jnp.where(lane == idx[:, 0:1], off, 0), axis=-1, keepdims=True)
    p1 = jnp.sum(jnp.where(lane == idx[:, 1:2], off, 0), axis=-1, keepdims=True)
    pos_ref[...] = rank_ref[...] + jnp.where(lane == 0, p0, jnp.where(lane == 1, p1, 0))


def moe_positions(idx, rank, offsets):
    t = idx.shape[0]
    tb = min(4 * ROUTE_BLOCK, t)
    off = jnp.pad(offsets, (0, LANES - N_EXPERTS)).reshape(1, LANES)
    blk = pl.BlockSpec((tb, LANES), lambda i: (i, 0))
    return pl.pallas_call(
        _pos_kernel,
        grid=(t // tb,),
        in_specs=[blk, blk, pl.BlockSpec((1, LANES), lambda i: (0, 0))],
        out_specs=blk,
        out_shape=jax.ShapeDtypeStruct((t, LANES), jnp.int32),
        compiler_params=_params("parallel"),
        name="moe_positions",
    )(idx, rank, off)


DMA_ISSUE_UNROLL = 8


def _tile_rows(row):
    return pl.ds(pl.multiple_of(row * ROW_CHUNKS, ROW_CHUNKS), ROW_CHUNKS)


def _dispatch_kernel(pos_ref, cnt_ref, off_ref, x_ref, xs_ref, zero_ref, sem, pad_sem, *, tb):
    base = pl.program_id(0) * (TOP_K * tb)

    def issue(t, _):
        for k in range(TOP_K):
            pltpu.make_async_copy(x_ref.at[_tile_rows(t)],
                                  xs_ref.at[_tile_rows(pos_ref[base + TOP_K * t + k])], sem).start()
        return 0

    lax.fori_loop(0, tb, issue, 0, unroll=DMA_ISSUE_UNROLL)

    @pl.when(pl.program_id(0) == 0)
    def _():
        zero_ref[...] = jnp.zeros_like(zero_ref)

        def fill_range(first, count):
            def copy(r):
                return pltpu.make_async_copy(zero_ref, xs_ref.at[_tile_rows(first + r)], pad_sem)

            def fill(r, _):
                copy(r).start()
                return 0

            def drain(r, _):
                copy(r).wait()
                return 0

            lax.fori_loop(0, count, fill, 0)
            lax.fori_loop(0, count, drain, 0)

        for e in range(N_EXPERTS):
            fill_range(off_ref[e] + cnt_ref[e], (-cnt_ref[e]) & (MOE_TILE - 1))
        last = N_EXPERTS - 1
        used = off_ref[last] + cnt_ref[last] + ((-cnt_ref[last]) & (MOE_TILE - 1))
        fill_range(used, xs_ref.shape[0] // ROW_CHUNKS - used)

    for _ in range(TOP_K):
        pltpu.make_async_copy(x_ref, xs_ref.at[pl.ds(0, tb * ROW_CHUNKS)], sem).wait()


def moe_dispatch(x_tiled, pos_flat, counts, offsets, n_rows):
    t = x_tiled.shape[0] // ROW_CHUNKS
    tb = min(ROUTE_BLOCK, t)
    grid_spec = pltpu.PrefetchScalarGridSpec(
        num_scalar_prefetch=3,
        grid=(t // tb,),
        in_specs=[pl.BlockSpec((tb * ROW_CHUNKS, LANES), lambda i, *_: (i, 0))],
        out_specs=pl.BlockSpec(memory_space=pl.ANY),
        scratch_shapes=[pltpu.VMEM((ROW_CHUNKS, LANES), F32), pltpu.SemaphoreType.DMA(()),
                        pltpu.SemaphoreType.DMA(())],
    )
    return pl.pallas_call(
        functools.partial(_dispatch_kernel, tb=tb),
        grid_spec=grid_spec,
        out_shape=jax.ShapeDtypeStruct((n_rows * ROW_CHUNKS, LANES), F32),
        compiler_params=_params("arbitrary"),
        name="moe_dispatch",
    )(pos_flat, counts, offsets, x_tiled)


def _moe_ffn_kernel(te_ref, nt_ref, x_ref, wg_ref, wu_ref, wd_ref, o_ref, xb_ref, acc_ref):
    i = pl.program_id(0)
    f = pl.program_id(1)

    @pl.when(i < nt_ref[0])
    def _():
        @pl.when(f == 0)
        def _():
            for j in range(ROW_CHUNKS):
                xb_ref[:, j * LANES:(j + 1) * LANES] = _load_row_tiled(x_ref, j, MOE_TILE).astype(BF16)
            acc_ref[...] = jnp.zeros_like(acc_ref)

        xb = xb_ref[...]
        gate = jnp.dot(xb, wg_ref[0], preferred_element_type=F32)
        up = jnp.dot(xb, wu_ref[0], preferred_element_type=F32)
        h = (_silu(gate) * up).astype(BF16)
        acc_ref[...] += jnp.dot(h, wd_ref[0], preferred_element_type=F32)

        @pl.when(f == pl.num_programs(1) - 1)
        def _():
            _store_row_tiled(o_ref, acc_ref[...])

    @pl.when((i >= nt_ref[0]) & (f == 0))
    def _():
        o_ref[...] = jnp.zeros_like(o_ref)


def moe_ffn(xs, tile_expert, n_tiles_used, wg, wu, wd, tf=1792):
    n_rows, d = xs.shape[0] // ROW_CHUNKS, D_MODEL
    n_tiles = n_rows // MOE_TILE
    tile_spec = lambda index: pl.BlockSpec((MOE_TILE * ROW_CHUNKS, LANES), index)
    fdim = wg.shape[2]
    nf = fdim // tf

    def live(i, nt):
        return jnp.minimum(i, nt[0] - 1)

    def fblk(i, f, nt):
        return jnp.where(i < nt[0], f, nf - 1)

    grid_spec = pltpu.PrefetchScalarGridSpec(
        num_scalar_prefetch=2,
        grid=(n_tiles, nf),
        in_specs=[tile_spec(lambda i, f, te, nt: (live(i, nt), 0)),
                  pl.BlockSpec((1, d, tf), lambda i, f, te, nt: (te[i], 0, fblk(i, f, nt))),
                  pl.BlockSpec((1, d, tf), lambda i, f, te, nt: (te[i], 0, fblk(i, f, nt))),
                  pl.BlockSpec((1, tf, d), lambda i, f, te, nt: (te[i], fblk(i, f, nt), 0))],
        out_specs=tile_spec(lambda i, f, te, nt: (i, 0)),
        scratch_shapes=[pltpu.VMEM((MOE_TILE, d), BF16), pltpu.VMEM((MOE_TILE, d), F32)],
    )
    return pl.pallas_call(
        _moe_ffn_kernel,
        grid_spec=grid_spec,
        out_shape=jax.ShapeDtypeStruct(xs.shape, F32),
        compiler_params=_params("arbitrary", "arbitrary"),
        name="moe_ffn",
    )(tile_expert, n_tiles_used, xs, wg, wu, wd)


def _combine_kernel(pos_ref, x_ref, gate_ref, g_ref, b_ref, ys_ref, o_ref, buf_ref, sem, *, tb):
    i = pl.program_id(0)
    slot = i % 2

    def start_gathers(block, into):
        base = block * (TOP_K * tb)

        def issue(t, _):
            for k in range(TOP_K):
                pltpu.make_async_copy(ys_ref.at[_tile_rows(pos_ref[base + TOP_K * t + k])],
                                      buf_ref.at[into, k, _tile_rows(t)], sem.at[into]).start()
            return 0

        lax.fori_loop(0, tb, issue, 0, unroll=DMA_ISSUE_UNROLL)

    @pl.when(i == 0)
    def _():
        start_gathers(0, 0)

    @pl.when(i + 1 < pl.num_programs(0))
    def _():
        start_gathers(i + 1, 1 - slot)

    for k in range(TOP_K):
        pltpu.make_async_copy(ys_ref.at[pl.ds(0, tb * ROW_CHUNKS)], buf_ref.at[slot, k],
                              sem.at[slot]).wait()
    gates = gate_ref[...]
    y = jnp.concatenate(
        [buf_ref[slot, 0, pl.ds(j, tb, stride=ROW_CHUNKS), :] * gates[:, 0:1]
         + buf_ref[slot, 1, pl.ds(j, tb, stride=ROW_CHUNKS), :] * gates[:, 1:2]
         for j in range(ROW_CHUNKS)], axis=1)
    o_ref[...] = _layer_norm_rows(DN_ALPHA * x_ref[...] + y, g_ref[...], b_ref[...])


def moe_combine(x, ys, pos_flat, gates, g, b):
    t, d = x.shape
    tb = min(ROUTE_BLOCK, t)
    grid_spec = pltpu.PrefetchScalarGridSpec(
        num_scalar_prefetch=1,
        grid=(t // tb,),
        in_specs=[pl.BlockSpec((tb, d), lambda i, *_: (i, 0)),
                  pl.BlockSpec((tb, LANES), lambda i, *_: (i, 0)),
                  pl.BlockSpec((1, d), lambda i, *_: (0, 0)),
                  pl.BlockSpec((1, d), lambda i, *_: (0, 0)),
                  pl.BlockSpec(memory_space=pl.ANY)],
        out_specs=pl.BlockSpec((tb, d), lambda i, *_: (i, 0)),
        scratch_shapes=[pltpu.VMEM((2, TOP_K, tb * ROW_CHUNKS, LANES), F32),
                        pltpu.SemaphoreType.DMA((2,))],
    )
    return pl.pallas_call(
        functools.partial(_combine_kernel, tb=tb),
        grid_spec=grid_spec,
        out_shape=jax.ShapeDtypeStruct((t, d), F32),
        compiler_params=_params("arbitrary"),
        name="moe_combine",
    )(pos_flat, x, gates, g.reshape(1, d), b.reshape(1, d), ys)


def moe_deepnorm(routed, wg, wu, wd, g, b):
    x, x_tiled, idx, gates, rank, counts_f = routed
    t = x.shape[0]
    counts = counts_f[0, :N_EXPERTS].astype(jnp.int32)
    padded = (counts + MOE_TILE - 1) // MOE_TILE * MOE_TILE
    ends = jnp.cumsum(padded)
    offsets = ends - padded
    n_tiles = (TOP_K * t) // MOE_TILE + N_EXPERTS
    n_used = (ends[-1] // MOE_TILE).astype(jnp.int32)
    tile_start = jnp.arange(n_tiles, dtype=jnp.int32) * MOE_TILE
    tile_start = jnp.minimum(tile_start, ends[-1] - MOE_TILE)
    tile_expert = jnp.sum(tile_start[:, None] >= ends[None, :], axis=1).astype(jnp.int32)
    pos = moe_positions(idx, rank, offsets)
    pos_flat = pos[:, :TOP_K].reshape(-1)
    xs = moe_dispatch(x_tiled, pos_flat, counts, offsets, n_tiles * MOE_TILE)
    ys = moe_ffn(xs, tile_expert, n_used.reshape(1), wg.astype(BF16), wu.astype(BF16), wd.astype(BF16))
    return moe_combine(x, ys, pos_flat, gates, g, b)


def kernel(x, l0_ssd_w_in, l0_ssd_conv_w, l0_ssd_conv_b, l0_ssd_dt_bias, l0_ssd_a_log, l0_ssd_d_skip, l0_ssd_norm_w, l0_ssd_w_out, l0_ln_mix_g, l0_ln_mix_b, l0_ffn_w_gate, l0_ffn_w_up, l0_ffn_w_down, l0_ln_ffn_g, l0_ln_ffn_b, l1_sb_w_qkv, l1_sb_w_out, l1_ln_mix_g, l1_ln_mix_b, l1_moe_w_router, l1_moe_w_gate, l1_moe_w_up, l1_moe_w_down, l1_ln_ffn_g, l1_ln_ffn_b, l2_fox_w_qkvf, l2_fox_b_f, l2_fox_w_out, l2_ln_mix_g, l2_ln_mix_b, l2_ffn_w_gate, l2_ffn_w_up, l2_ffn_w_down, l2_ln_ffn_g, l2_ln_ffn_b, l3_ssd_w_in, l3_ssd_conv_w, l3_ssd_conv_b, l3_ssd_dt_bias, l3_ssd_a_log, l3_ssd_d_skip, l3_ssd_norm_w, l3_ssd_w_out, l3_ln_mix_g, l3_ln_mix_b, l3_moe_w_router, l3_moe_w_gate, l3_moe_w_up, l3_moe_w_down, l3_ln_ffn_g, l3_ln_ffn_b):
    batch, seq, d = x.shape
    h = x.reshape(batch * seq, d)
    bf = lambda w: w.astype(BF16)
    y = ssd_mixer(h, l0_ssd_w_in, l0_ssd_conv_w, l0_ssd_conv_b, l0_ssd_dt_bias, l0_ssd_a_log,
                  l0_ssd_d_skip, l0_ssd_norm_w, batch, seq)
    h = mixer_out_ffn_deepnorm(y, bf(l0_ssd_w_out), h, l0_ln_mix_g, l0_ln_mix_b,
                               bf(l0_ffn_w_gate), bf(l0_ffn_w_up), bf(l0_ffn_w_down), l0_ln_ffn_g, l0_ln_ffn_b)
    o = sb_mixer(h, l1_sb_w_qkv, batch, seq)
    routed = matmul_deepnorm_route(o, bf(l1_sb_w_out), h, l1_ln_mix_g, l1_ln_mix_b, l1_moe_w_router)
    h = moe_deepnorm(routed, l1_moe_w_gate, l1_moe_w_up, l1_moe_w_down, l1_ln_ffn_g, l1_ln_ffn_b)
    o = fox_mixer(h, l2_fox_w_qkvf, l2_fox_b_f, batch, seq)
    h = mixer_out_ffn_deepnorm(o, bf(l2_fox_w_out), h, l2_ln_mix_g, l2_ln_mix_b,
                               bf(l2_ffn_w_gate), bf(l2_ffn_w_up), bf(l2_ffn_w_down), l2_ln_ffn_g, l2_ln_ffn_b)
    y = ssd_mixer(h, l3_ssd_w_in, l3_ssd_conv_w, l3_ssd_conv_b, l3_ssd_dt_bias, l3_ssd_a_log,
                  l3_ssd_d_skip, l3_ssd_norm_w, batch, seq)
    routed = matmul_deepnorm_route(y, bf(l3_ssd_w_out), h, l3_ln_mix_g, l3_ln_mix_b, l3_moe_w_router)
    h = moe_deepnorm(routed, l3_moe_w_gate, l3_moe_w_up, l3_moe_w_down, l3_ln_ffn_g, l3_ln_ffn_b)
    return h.reshape(batch, seq, d)
```

```python
import functools
import math

import jax
import jax.numpy as jnp
from jax import lax
from jax.experimental import pallas as pl
from jax.experimental.pallas import tpu as pltpu

F32 = jnp.float32
BF16 = jnp.bfloat16
HIGHEST = lax.Precision.HIGHEST

LANES = 128
SUBLANES = 8
VMEM_LIMIT_BYTES = 56 * 1024 * 1024

D_MODEL = 1024
DEPTH = 4
SSD_D_INNER = 2048
SSD_HEAD_DIM = 64
SSD_HEADS = 32
SSD_GROUPS = 4
SSD_STATE = 128
SSD_CONV = 4
SSD_GROUP_DIM = SSD_D_INNER // SSD_GROUPS
SSD_BC_DIM = 2 * SSD_GROUPS * SSD_STATE
ATT_HEAD_DIM = 64
ATT_HEADS = 16
ATT_DIM = 1024
N_EXPERTS = 8
TOP_K = 2
DN_ALPHA = (2.0 * DEPTH) ** 0.25
LN_EPS = 1e-5
RMS_EPS = 1e-5
NEG_BIG = -1e30


def _params(*semantics):
    return pltpu.CompilerParams(dimension_semantics=semantics, vmem_limit_bytes=VMEM_LIMIT_BYTES)


def _layer_norm_rows(h, g, b):
    mu = jnp.mean(h, axis=-1, keepdims=True)
    d = h - mu
    var = jnp.mean(d * d, axis=-1, keepdims=True)
    return d * lax.rsqrt(var + LN_EPS) * g + b


def _silu(x):
    half = 0.5 * x
    return half * (1.0 + jnp.tanh(half))


def _mm_kernel(x_ref, w_ref, o_ref, xb_ref):
    @pl.when(pl.program_id(1) == 0)
    def _():
        xb_ref[...] = x_ref[...].astype(BF16)

    o_ref[...] = jnp.dot(xb_ref[...], w_ref[...], preferred_element_type=F32).astype(o_ref.dtype)


def matmul(x, w, out_dtype, tm=1024, tn=1024):
    m, k = x.shape
    tm = min(tm, m)
    n = w.shape[1]
    return pl.pallas_call(
        _mm_kernel,
        grid=(m // tm, n // tn),
        in_specs=[pl.BlockSpec((tm, k), lambda i, j: (i, 0)),
                  pl.BlockSpec((k, tn), lambda i, j: (0, j))],
        out_specs=pl.BlockSpec((tm, tn), lambda i, j: (i, j)),
        out_shape=jax.ShapeDtypeStruct((m, n), out_dtype),
        scratch_shapes=[pltpu.VMEM((tm, k), BF16)],
        compiler_params=_params("parallel", "arbitrary"),
        name="matmul",
    )(x, w)


def _dot_split(x, w):
    xh = x.astype(BF16)
    xl = (x - xh.astype(F32)).astype(BF16)
    wh = w.astype(BF16)
    wl = (w - wh.astype(F32)).astype(BF16)
    dot = functools.partial(jnp.dot, preferred_element_type=F32)
    return dot(xh, wh) + (dot(xh, wl) + dot(xl, wh))


def _mm_f32_kernel(x_ref, w_ref, o_ref):
    o_ref[...] = _dot_split(x_ref[...], w_ref[...])


def matmul_f32(x, w, tm=1024):
    m, k = x.shape
    tm = min(tm, m)
    n = w.shape[1]
    return pl.pallas_call(
        _mm_f32_kernel,
        grid=(m // tm,),
        in_specs=[pl.BlockSpec((tm, k), lambda i: (i, 0)),
                  pl.BlockSpec((k, n), lambda i: (0, 0))],
        out_specs=pl.BlockSpec((tm, n), lambda i: (i, 0)),
        out_shape=jax.ShapeDtypeStruct((m, n), F32),
        compiler_params=_params("parallel"),
        name="matmul_f32",
    )(x, w)


ROW_CHUNKS = D_MODEL // LANES


def _store_row_tiled(ref, value):
    n = value.shape[0]
    for j in range(ROW_CHUNKS):
        ref[pl.ds(j, n, stride=ROW_CHUNKS), :] = value[:, j * LANES:(j + 1) * LANES]


def _load_row_tiled(ref, j, n):
    return ref[pl.ds(j, n, stride=ROW_CHUNKS), :]


def _mm_ln_route_kernel(x_ref, w_ref, r_ref, g_ref, b_ref, wr_ref,
                        o_ref, tiled_ref, idx_ref, gate_ref, rank_ref, count_ref, run_ref):
    @pl.when(pl.program_id(0) == 0)
    def _():
        run_ref[...] = jnp.zeros_like(run_ref)

    y = jnp.dot(x_ref[...], w_ref[...], preferred_element_type=F32)
    o = _layer_norm_rows(DN_ALPHA * r_ref[...] + y, g_ref[...], b_ref[...])
    o_ref[...] = o
    _store_row_tiled(tiled_ref, o)
    idx, gates = _top2(_dot_split(o, wr_ref[...]))
    idx_ref[...] = idx
    gate_ref[...] = gates
    rank_ref[...] = _rank_block(idx, run_ref)
    count_ref[...] = run_ref[...]


def matmul_deepnorm_route(x, w, resid, g, b, w_router):
    m, k = x.shape
    tm = min(ROUTE_BLOCK, m)
    d = w.shape[1]
    rows = lambda width: pl.BlockSpec((tm, width), lambda i: (i, 0))
    fixed = lambda a: pl.BlockSpec(a.shape, lambda i: (0, 0))
    args = [x, w, resid, g.reshape(1, d), b.reshape(1, d),
            jnp.pad(w_router, ((0, 0), (0, LANES - N_EXPERTS)))]
    return pl.pallas_call(
        _mm_ln_route_kernel,
        grid=(m // tm,),
        in_specs=[rows(k), fixed(w), rows(d), fixed(args[3]), fixed(args[4]), fixed(args[5])],
        out_specs=[rows(d), pl.BlockSpec((tm * ROW_CHUNKS, LANES), lambda i: (i, 0)),
                   rows(LANES), rows(LANES), rows(LANES), pl.BlockSpec((1, LANES), lambda i: (0, 0))],
        out_shape=[jax.ShapeDtypeStruct((m, d), F32),
                   jax.ShapeDtypeStruct((m * ROW_CHUNKS, LANES), F32),
                   jax.ShapeDtypeStruct((m, LANES), jnp.int32),
                   jax.ShapeDtypeStruct((m, LANES), F32),
                   jax.ShapeDtypeStruct((m, LANES), jnp.int32),
                   jax.ShapeDtypeStruct((1, LANES), F32)],
        scratch_shapes=[pltpu.VMEM((1, LANES), F32)],
        compiler_params=_params("arbitrary"),
        name="matmul_deepnorm_route",
    )(*args)


def _mix_ffn_kernel(o_ref, wo_ref, r_ref, g1_ref, b1_ref, wg_ref, wu_ref, wd_ref, g2_ref, b2_ref, out_ref):
    x = _layer_norm_rows(
        DN_ALPHA * r_ref[...] + jnp.dot(o_ref[...], wo_ref[...], preferred_element_type=F32),
        g1_ref[...], b1_ref[...])
    xb = x.astype(BF16)
    gate = jnp.dot(xb, wg_ref[...], preferred_element_type=F32)
    up = jnp.dot(xb, wu_ref[...], preferred_element_type=F32)
    h = (_silu(gate) * up).astype(BF16)
    y = jnp.dot(h, wd_ref[...], preferred_element_type=F32)
    out_ref[...] = _layer_norm_rows(DN_ALPHA * x + y, g2_ref[...], b2_ref[...])


def mixer_out_ffn_deepnorm(o, w_out, resid, g1, b1, wg, wu, wd, g2, b2, tm=512):
    m, k = o.shape
    d = w_out.shape[1]
    tm = min(tm, m)
    resident = lambda a: pl.BlockSpec(a.shape, lambda i: (0, 0), pipeline_mode=pl.Buffered(1))
    vec = lambda v: v.reshape(1, d)
    args = (o, w_out, resid, vec(g1), vec(b1), wg, wu, wd, vec(g2), vec(b2))
    return pl.pallas_call(
        _mix_ffn_kernel,
        grid=(m // tm,),
        in_specs=[pl.BlockSpec((tm, k), lambda i: (i, 0)), resident(w_out),
                  pl.BlockSpec((tm, d), lambda i: (i, 0)), resident(args[3]), resident(args[4]),
                  resident(wg), resident(wu), resident(wd), resident(args[8]), resident(args[9])],
        out_specs=pl.BlockSpec((tm, d), lambda i: (i, 0)),
        out_shape=jax.ShapeDtypeStruct((m, d), F32),
        compiler_params=_params("parallel"),
        name="mixer_out_ffn_deepnorm",
    )(*args)


def _split_bf16(v, pieces):
    out = []
    r = v
    for _ in range(pieces - 1):
        p = r.astype(BF16)
        out.append(p)
        r = r - p.astype(F32)
    out.append(r.astype(BF16))
    return out


def _expand(v, e_ref, pieces):
    stacked = jnp.concatenate(_split_bf16(v, pieces), axis=1)
    return jnp.dot(stacked, e_ref[...], preferred_element_type=F32)


def _tril_f32(n, strict=False):
    r = lax.broadcasted_iota(jnp.int32, (n, n), 0)
    c = lax.broadcasted_iota(jnp.int32, (n, n), 1)
    return ((r > c) if strict else (r >= c)).astype(F32)


def _ssd_kernel(z_ref, xs_ref, bc_ref, dt_ref,
                dtb_ref, alog_ref, dskip_ref, normw_ref, e64_ref, o_ref,
                state_ref, xdt_ref, cb_ref, ydiag_ref,
                *, chunk):
    L = chunk
    c = pl.program_id(1)

    @pl.when(c == 0)
    def _():
        state_ref[...] = jnp.zeros_like(state_ref)

    xs = xs_ref[...]
    bcv = bc_ref[...]

    dt = jax.nn.softplus(dt_ref[...] + dtb_ref[...])
    da = dt * (-jnp.exp(alog_ref[...]))
    a_cs = jnp.dot(_tril_f32(L), da, preferred_element_type=F32, precision=HIGHEST)
    ea = jnp.exp(a_cs)
    dte = jnp.exp(a_cs[L - 1:L, :] - a_cs)

    spread = _expand(jnp.concatenate([dt, ea, dte], axis=0), e64_ref, 2)
    dt_x, ea_x, dte_x = spread[0:L], spread[L:2 * L], spread[2 * L:3 * L]
    a_t = a_cs.T

    xdt = xs * dt_x
    xdt_ref[...] = xdt.astype(BF16)
    xdte = (xdt * dte_x).astype(BF16)

    y_off = []
    for g in range(SSD_GROUPS):
        bm = bcv[:, g * SSD_STATE:(g + 1) * SSD_STATE]
        cm = bcv[:, (SSD_GROUPS + g) * SSD_STATE:(SSD_GROUPS + g + 1) * SSD_STATE].astype(BF16)
        cb_ref[g] = lax.dot_general(cm, bm.astype(BF16), (((1,), (1,)), ((), ())),
                                    preferred_element_type=F32)
        gs = slice(g * SSD_GROUP_DIM, (g + 1) * SSD_GROUP_DIM)
        st = state_ref[g]
        y_off.append(jnp.dot(cm, st.astype(BF16), preferred_element_type=F32) * ea_x[:, gs])
        state_ref[g] = st * ea_x[L - 1:L, gs] + jnp.dot(
            bm.T.astype(BF16), xdte[:, gs], preferred_element_type=F32)

    row = lax.broadcasted_iota(jnp.int32, (L, L), 0)
    col = lax.broadcasted_iota(jnp.int32, (L, L), 1)
    causal = row >= col
    head0 = lax.broadcasted_iota(jnp.int32, (L, LANES), 1) < SSD_HEAD_DIM

    heads_per_group = SSD_HEADS // SSD_GROUPS

    for g in range(SSD_GROUPS):
        cbg = cb_ref[g]
        scores = []
        for j in range(heads_per_group):
            h = g * heads_per_group + j
            a_l = a_cs[:, h:h + 1]
            a_s = a_t[h:h + 1, :]
            scores.append((cbg * jnp.exp(jnp.where(causal, a_l - a_s, NEG_BIG))).astype(BF16))
        for pp in range(heads_per_group // 2):
            lanes = slice((g * (heads_per_group // 2) + pp) * LANES,
                          (g * (heads_per_group // 2) + pp + 1) * LANES)
            xpair = xdt_ref[:, lanes]
            ys = [jnp.dot(scores[2 * pp + j], xpair, preferred_element_type=F32) for j in range(2)]
            ydiag_ref[:, lanes] = jnp.where(head0, ys[0], ys[1])

    y = ydiag_ref[...] + jnp.concatenate(y_off, axis=1) + dskip_ref[...] * xs
    y = y * _silu(z_ref[...])
    parts = []
    for g in range(SSD_GROUPS):
        yg = y[:, g * SSD_GROUP_DIM:(g + 1) * SSD_GROUP_DIM]
        parts.append(yg * lax.rsqrt(jnp.mean(yg * yg, axis=-1, keepdims=True) + RMS_EPS))
    o_ref[...] = (jnp.concatenate(parts, axis=1) * normw_ref[...]).astype(o_ref.dtype)


def _expansion_matrix(width, pieces):
    h = jnp.arange(LANES)[:, None]
    lane = jnp.arange(SSD_HEADS * width)[None, :]
    e = (lane // width == h).astype(BF16)
    return jnp.concatenate([e] * pieces, axis=0)


def _mm_conv_kernel(x_ref, halo_ref, w_ref, cw_ref, cb_ref, o_ref, ext_ref, xb_ref, *, tiles_per_seq):
    tm = x_ref.shape[0]

    @pl.when(pl.program_id(1) == 0)
    def _():
        xb_ref[...] = x_ref[...].astype(BF16)

    w = w_ref[...]
    halo = jnp.dot(halo_ref[...].astype(BF16), w, preferred_element_type=F32)
    first = pl.program_id(0) % tiles_per_seq == 0
    ext_ref[0:SUBLANES, :] = jnp.where(first, 0.0, halo)
    ext_ref[SUBLANES:SUBLANES + tm, :] = jnp.dot(xb_ref[...], w, preferred_element_type=F32)
    ext = ext_ref[...]
    acc = cw_ref[0:1, :] * ext
    for k in range(1, SSD_CONV):
        acc = pltpu.roll(acc, 1, axis=0) + cw_ref[k:k + 1, :] * ext
    o_ref[...] = _silu(acc[SUBLANES:SUBLANES + tm, :] + cb_ref[...])


def matmul_conv_silu(x, w, conv_w, conv_b, seq, tm=1024, tn=1024):
    m, k = x.shape
    tm = min(tm, seq)
    n = w.shape[1]
    halo_blocks = tm // SUBLANES
    return pl.pallas_call(
        functools.partial(_mm_conv_kernel, tiles_per_seq=seq // tm),
        grid=(m // tm, n // tn),
        in_specs=[pl.BlockSpec((tm, k), lambda i, j: (i, 0)),
                  pl.BlockSpec((SUBLANES, k), lambda i, j: (jnp.maximum(i * halo_blocks - 1, 0), 0)),
                  pl.BlockSpec((k, tn), lambda i, j: (0, j)),
                  pl.BlockSpec((SSD_CONV, tn), lambda i, j: (0, j)),
                  pl.BlockSpec((1, tn), lambda i, j: (0, j))],
        out_specs=pl.BlockSpec((tm, tn), lambda i, j: (i, j)),
        out_shape=jax.ShapeDtypeStruct((m, n), F32),
        scratch_shapes=[pltpu.VMEM((tm + SUBLANES, tn), F32), pltpu.VMEM((tm, k), BF16)],
        compiler_params=_params("parallel", "arbitrary"),
        name="matmul_conv_silu",
    )(x, x, w, conv_w, conv_b.reshape(1, n))


def ssd_core(z, xbc, dt_raw, dt_bias, a_log, d_skip, norm_w, batch, seq, chunk=128):
    t = z.shape[0]
    nc = seq // chunk
    pad = LANES - SSD_HEADS
    row = lambda v: v.reshape(1, -1).astype(F32)
    args = (
        z, xbc, xbc, dt_raw,
        row(jnp.pad(dt_bias, (0, pad))), row(jnp.pad(a_log, (0, pad))),
        row(jnp.repeat(d_skip, SSD_HEAD_DIM)), row(norm_w),
        _expansion_matrix(SSD_HEAD_DIM, 2),
    )
    blk = lambda b, c: (b * nc + c, 0)
    const = lambda b, c: (0, 0)
    full = lambda a: pl.BlockSpec(a.shape, const)
    in_specs = [
        pl.BlockSpec((chunk, SSD_D_INNER), blk),
        pl.BlockSpec((chunk, SSD_D_INNER), blk),
        pl.BlockSpec((chunk, SSD_BC_DIM), lambda b, c: (b * nc + c, 2)),
        pl.BlockSpec((chunk, LANES), blk),
    ] + [full(a) for a in args[4:]]
    return pl.pallas_call(
        functools.partial(_ssd_kernel, chunk=chunk),
        grid=(batch, nc),
        in_specs=in_specs,
        out_specs=pl.BlockSpec((chunk, SSD_D_INNER), blk),
        out_shape=jax.ShapeDtypeStruct((t, SSD_D_INNER), BF16),
        scratch_shapes=[
            pltpu.VMEM((SSD_GROUPS, SSD_STATE, SSD_GROUP_DIM), F32),
            pltpu.VMEM((chunk, SSD_D_INNER), BF16),
            pltpu.VMEM((SSD_GROUPS, chunk, chunk), F32),
            pltpu.VMEM((chunk, SSD_D_INNER), F32),
        ],
        compiler_params=_params("parallel", "arbitrary"),
        name="ssd_core",
    )(*args)


def ssd_mixer(x, w_in, conv_w, conv_b, dt_bias, a_log, d_skip, norm_w, batch, seq):
    n_zx = SSD_D_INNER + SSD_D_INNER + SSD_BC_DIM
    z = matmul(x, w_in[:, :SSD_D_INNER].astype(BF16), F32)
    xbc = matmul_conv_silu(x, w_in[:, SSD_D_INNER:n_zx].astype(BF16), conv_w, conv_b, seq)
    w_dt = jnp.pad(w_in[:, n_zx:], ((0, 0), (0, LANES - SSD_HEADS)))
    dt_raw = matmul_f32(x, w_dt)
    return ssd_core(z, xbc, dt_raw, dt_bias, a_log, d_skip, norm_w, batch, seq)


ATT_BLOCK = 128
ATT_Q = 256
ATT_PAIRS = 4
ATT_STEP_LANES = ATT_PAIRS * LANES
ATT_SCALE = ATT_HEAD_DIM ** -0.5
LOG2E = 1.4426950408889634
EXP2_UNDERFLOW = -160.0
FOX_BOUND_SLACK = 1.05


def _split_heads(q_ref, rows, head0):
    qs = []
    for p in range(ATT_PAIRS):
        q2 = q_ref[rows, p * LANES:(p + 1) * LANES]
        zero = jnp.zeros_like(q2)
        qs += [jnp.where(head0, q2, zero), jnp.where(head0, zero, q2)]
    return qs


def _sb_kernel(q_ref, k_ref, v_ref, u_ref, o_ref, *scratch, seq):
    n_heads = 2 * ATT_PAIRS
    later_ref, acc_ref, z_ref = scratch[:n_heads], scratch[n_heads:2 * n_heads], scratch[2 * n_heads:]
    nq = seq // ATT_Q
    ndiag = ATT_Q // ATT_BLOCK
    head0 = lax.broadcasted_iota(jnp.int32, (ATT_Q, LANES), 1) < ATT_HEAD_DIM
    row = lax.broadcasted_iota(jnp.int32, (ATT_Q, ATT_BLOCK), 0)
    col = lax.broadcasted_iota(jnp.int32, (ATT_Q, ATT_BLOCK), 1)

    def q_body(i, _):
        q0 = pl.multiple_of(i * ATT_Q, ATT_Q)
        rows = pl.ds(q0, ATT_Q)
        qs = _split_heads(q_ref, rows, head0)
        for n in range(2 * ATT_PAIRS):
            later_ref[n][...] = jnp.zeros((ATT_Q, LANES), F32)
            acc_ref[n][...] = jnp.zeros((ATT_Q, LANES), F32)

        heads = range(2 * ATT_PAIRS)

        def key_rows(j):
            return pl.ds(pl.multiple_of(j * ATT_BLOCK, ATT_BLOCK), ATT_BLOCK)

        def scores(j):
            k2 = [k_ref[key_rows(j), p * LANES:(p + 1) * LANES] for p in range(ATT_PAIRS)]
            return [lax.dot_general(qs[n], k2[n // 2], (((1,), (1,)), ((), ())),
                                    preferred_element_type=F32) * (ATT_SCALE * LOG2E) for n in heads]

        def consume(z2, j, strict):
            keys = key_rows(j)
            v2 = [v_ref[keys, p * LANES:(p + 1) * LANES] for p in range(ATT_PAIRS)]
            log_beta, sums = [], []
            for n in heads:
                lb = jnp.minimum(z2[n], 0.0) - jnp.log2(1.0 + jnp.exp2(-jnp.abs(z2[n])))
                log_keep = lb - z2[n]
                if strict is not None:
                    log_keep = jnp.where(strict, log_keep, 0.0)
                hi = log_keep.astype(BF16)
                lo = (log_keep - hi.astype(F32)).astype(BF16)
                log_beta.append(lb)
                sums.append(jnp.dot(jnp.concatenate([hi, lo], axis=1), u_ref[...],
                                    preferred_element_type=F32))
            for n in heads:
                w = jnp.exp2(log_beta[n] + sums[n][:, :ATT_BLOCK] + later_ref[n][...])
                if strict is not None:
                    w = jnp.where(strict, w, 0.0)
                acc_ref[n][...] += jnp.dot(w.astype(BF16), v2[n // 2], preferred_element_type=F32)
                later_ref[n][...] += sums[n][:, ATT_BLOCK:]

        for d in reversed(range(ndiag)):
            consume(scores(i * ndiag + d), i * ndiag + d, d * ATT_BLOCK + col < row)

        n_off = i * ndiag
        for n, zn in enumerate(scores(jnp.maximum(n_off - 1, 0))):
            z_ref[n][...] = zn

        def off_diagonal(carry):
            jj, _ = carry
            z2 = [z_ref[n][...] for n in heads]
            ahead = scores(jnp.maximum(n_off - 2 - jj, 0))
            consume(z2, n_off - 1 - jj, None)
            for n in heads:
                z_ref[n][...] = ahead[n]
            top = later_ref[0][...]
            for n in range(1, 2 * ATT_PAIRS):
                top = jnp.maximum(top, later_ref[n][...])
            return jj + 1, jnp.max(top) > EXP2_UNDERFLOW

        lax.while_loop(lambda c: (c[0] < n_off) & c[1], off_diagonal, (jnp.int32(0), n_off > 0))
        for p in range(ATT_PAIRS):
            o_ref[rows, p * LANES:(p + 1) * LANES] = jnp.where(
                head0, acc_ref[2 * p][...], acc_ref[2 * p + 1][...]).astype(o_ref.dtype)
        return 0

    lax.fori_loop(0, nq, q_body, 0)


def _suffix_sum_matrix():
    j = jnp.arange(2 * ATT_BLOCK)[:, None] % ATT_BLOCK
    s = jnp.arange(2 * ATT_BLOCK)[None, :]
    return jnp.where(s < ATT_BLOCK, j > s, True).astype(BF16)


def _attention_specs(seq):
    n_steps = ATT_HEADS // 2 // ATT_PAIRS
    blk = lambda off: pl.BlockSpec((seq, ATT_STEP_LANES), lambda b, p: (b, off + p))
    specs = [blk(0), blk(n_steps), blk(2 * n_steps)]
    return n_steps, specs, pl.BlockSpec((seq, ATT_STEP_LANES), lambda b, p: (b, p))


def sb_attention(qkv, batch, seq):
    n_steps, in_specs, out_spec = _attention_specs(seq)
    u = _suffix_sum_matrix()
    return pl.pallas_call(
        functools.partial(_sb_kernel, seq=seq),
        grid=(batch, n_steps),
        in_specs=in_specs + [pl.BlockSpec(u.shape, lambda b, p: (0, 0))],
        out_specs=out_spec,
        out_shape=jax.ShapeDtypeStruct((batch * seq, ATT_DIM), BF16),
        scratch_shapes=[pltpu.VMEM((ATT_Q, LANES), F32)] * (6 * ATT_PAIRS),
        compiler_params=_params("parallel", "parallel"),
        name="sb_attention",
    )(qkv, qkv, qkv, u)


def sb_mixer(x, w_qkv, batch, seq):
    return sb_attention(matmul(x, w_qkv.astype(BF16), BF16), batch, seq)


CUMSUM_BLOCK = 256


def _fox_decay_kernel(f_ref, bf_ref, ccol_ref, crow_ref, *, seq):
    tri = _tril_f32(CUMSUM_BLOCK)
    carry = jnp.zeros((1, LANES), F32)
    for blk in range(seq // CUMSUM_BLOCK):
        rows = slice(blk * CUMSUM_BLOCK, (blk + 1) * CUMSUM_BLOCK)
        log_f = jax.nn.log_sigmoid(f_ref[rows, :] + bf_ref[...])
        c = jnp.dot(tri, log_f, preferred_element_type=F32, precision=HIGHEST) + carry
        carry = c[CUMSUM_BLOCK - 1:CUMSUM_BLOCK, :]
        ccol_ref[rows, :] = c
        c_t = c.T
        for p in range(ATT_HEADS // 2):
            crow_ref[0, p, :, rows] = c_t[2 * p:2 * p + 2, :]


def fox_decay(f_raw, b_f, batch, seq):
    return pl.pallas_call(
        functools.partial(_fox_decay_kernel, seq=seq),
        grid=(batch,),
        in_specs=[pl.BlockSpec((seq, LANES), lambda b: (b, 0)),
                  pl.BlockSpec((1, LANES), lambda b: (0, 0))],
        out_specs=[pl.BlockSpec((seq, LANES), lambda b: (b, 0)),
                   pl.BlockSpec((1, ATT_HEADS // 2, 2, seq), lambda b: (b, 0, 0, 0))],
        out_shape=[jax.ShapeDtypeStruct((batch * seq, LANES), F32),
                   jax.ShapeDtypeStruct((batch, ATT_HEADS // 2, 2, seq), F32)],
        compiler_params=_params("parallel"),
        name="fox_decay",
    )(f_raw, jnp.pad(b_f, (0, LANES - ATT_HEADS)).reshape(1, LANES))


def _fox_kernel(q_ref, k_ref, v_ref, ccol_ref, crow_ref, o_ref, crep_ref, vt_ref, *scratch, seq):
    n_heads = 2 * ATT_PAIRS
    m_ref, acc_ref, s_ref = scratch[:n_heads], scratch[n_heads:2 * n_heads], scratch[2 * n_heads:]
    nq = seq // ATT_Q
    ndiag = ATT_Q // ATT_BLOCK
    step = pl.program_id(1)
    heads = range(n_heads)
    head0 = lax.broadcasted_iota(jnp.int32, (ATT_Q, LANES), 1) < ATT_HEAD_DIM
    head0_rows = lax.broadcasted_iota(jnp.int32, (LANES, ATT_BLOCK), 0) < ATT_HEAD_DIM
    key = lax.broadcasted_iota(jnp.int32, (ATT_BLOCK, ATT_Q), 0)
    qry = lax.broadcasted_iota(jnp.int32, (ATT_BLOCK, ATT_Q), 1)

    pieces = jnp.concatenate(_split_bf16(ccol_ref[...], 3), axis=1)
    sel_row = lax.broadcasted_iota(jnp.int32, (3 * LANES, LANES), 0) & (LANES - 1)
    for n in heads:
        sel = jnp.where(sel_row == 2 * ATT_PAIRS * step + n, 1.0, 0.0).astype(BF16)
        crep_ref[n] = jnp.dot(pieces, sel, preferred_element_type=F32) * LOG2E
    for p in range(ATT_PAIRS):
        for blk in range(seq // ATT_BLOCK):
            rows = slice(blk * ATT_BLOCK, (blk + 1) * ATT_BLOCK)
            vt_ref[p, :, rows] = v_ref[rows, p * LANES:(p + 1) * LANES].astype(F32).T.astype(BF16)
    lane_r = lax.broadcasted_iota(jnp.int32, (LANES, LANES), 0) < ATT_HEAD_DIM
    lane_c = lax.broadcasted_iota(jnp.int32, (LANES, LANES), 1) < ATT_HEAD_DIM
    same_head = jnp.where(lane_r == lane_c, 1.0, 0.0).astype(BF16)
    k_norm = []
    for p in range(ATT_PAIRS):
        kf = k_ref[:, p * LANES:(p + 1) * LANES].astype(F32)
        sq = jnp.dot((kf * kf).astype(BF16), same_head, preferred_element_type=F32)
        top = jnp.sqrt(jnp.max(sq, axis=0, keepdims=True))
        k_norm += [top[:, 0:1], top[:, ATT_HEAD_DIM:ATT_HEAD_DIM + 1]]
    ones_rows = jnp.ones((SUBLANES, LANES), BF16)

    def q_body(i, _):
        q0 = pl.multiple_of(i * ATT_Q, ATT_Q)
        rows = pl.ds(q0, ATT_Q)
        qs = _split_heads(q_ref, rows, head0)
        c_q = [crow_ref[0, n // 2, n % 2:n % 2 + 1, rows] * LOG2E for n in heads]
        qk_bound = []
        for n in heads:
            qf = qs[n].astype(F32)
            q_sq = lax.dot_general(ones_rows, (qf * qf).astype(BF16), (((1,), (1,)), ((), ())),
                                   preferred_element_type=F32)[0:1, :]
            qk_bound.append(jnp.sqrt(q_sq) * k_norm[n] * (ATT_SCALE * LOG2E * FOX_BOUND_SLACK))
        for n in heads:
            m_ref[n][...] = jnp.full((1, ATT_Q), NEG_BIG, F32)
            acc_ref[n][...] = jnp.zeros((LANES, ATT_Q), F32)

        def key_rows(j):
            return pl.ds(pl.multiple_of(j * ATT_BLOCK, ATT_BLOCK), ATT_BLOCK)

        def scores(j):
            k2 = [k_ref[key_rows(j), p * LANES:(p + 1) * LANES] for p in range(ATT_PAIRS)]
            return [lax.dot_general(k2[n // 2], qs[n], (((1,), (1,)), ((), ())),
                                    preferred_element_type=F32) * (ATT_SCALE * LOG2E) for n in heads]

        def consume(s, j, causal):
            keys = key_rows(j)
            vt = [vt_ref[p, :, keys] for p in range(ATT_PAIRS)]
            for n in heads:
                c_k = crep_ref[n, keys, :]
                sn = s[n] + (c_q[n] - jnp.concatenate([c_k] * (ATT_Q // LANES), axis=1))
                if causal is not None:
                    sn = jnp.where(causal, sn, NEG_BIG)
                m_old = m_ref[n][...]
                m_new = jnp.maximum(m_old, jnp.max(sn, axis=0, keepdims=True))
                prob = jnp.exp2(sn - m_new).astype(BF16)
                one = jnp.ones_like(vt[n // 2])
                v1t = jnp.where(head0_rows, vt[n // 2], one) if n % 2 == 0 else jnp.where(
                    head0_rows, one, vt[n // 2])
                acc_ref[n][...] = jnp.exp2(m_old - m_new) * acc_ref[n][...] + jnp.dot(
                    v1t, prob, preferred_element_type=F32)
                m_ref[n][...] = m_new

        for d in range(ndiag):
            consume(scores(i * ndiag + d), i * ndiag + d, d * ATT_BLOCK + key <= qry)

        n_off = i * ndiag
        for n, sn in enumerate(scores(jnp.maximum(n_off - 1, 0))):
            s_ref[n][...] = sn

        def off_diagonal(carry):
            jj, _ = carry
            j = n_off - 1 - jj
            s = [s_ref[n][...] for n in heads]
            ahead = scores(jnp.maximum(j - 1, 0))
            consume(s, j, None)
            for n in heads:
                s_ref[n][...] = ahead[n]
            prev_keys = key_rows(jnp.maximum(j - 1, 0))
            worst = None
            for n in heads:
                c_last = crow_ref[0, n // 2, n % 2:n % 2 + 1, prev_keys][:, LANES - 1:LANES] * LOG2E
                gap = qk_bound[n] + c_q[n] - c_last - m_ref[n][...]
                worst = gap if worst is None else jnp.maximum(worst, gap)
            return jj + 1, jnp.max(worst) > EXP2_UNDERFLOW

        lax.while_loop(lambda c: (c[0] < n_off) & c[1], off_diagonal, (jnp.int32(0), n_off > 0))
        for p in range(ATT_PAIRS):
            a0, a1 = acc_ref[2 * p][...], acc_ref[2 * p + 1][...]
            o_t = jnp.concatenate([a0[:ATT_HEAD_DIM] / a0[ATT_HEAD_DIM:],
                                   a1[ATT_HEAD_DIM:] / a1[:ATT_HEAD_DIM]], axis=0)
            o_ref[rows, p * LANES:(p + 1) * LANES] = o_t.T.astype(o_ref.dtype)
        return 0

    lax.fori_loop(0, nq, q_body, 0)


def fox_attention(qkv, ccol, crow, batch, seq):
    n_steps, in_specs, out_spec = _attention_specs(seq)
    in_specs += [pl.BlockSpec((seq, LANES), lambda b, p: (b, 0)),
                 pl.BlockSpec((1, ATT_PAIRS, 2, seq), lambda b, p: (b, p, 0, 0))]
    return pl.pallas_call(
        functools.partial(_fox_kernel, seq=seq),
        grid=(batch, n_steps),
        in_specs=in_specs,
        out_specs=out_spec,
        out_shape=jax.ShapeDtypeStruct((batch * seq, ATT_DIM), BF16),
        scratch_shapes=([pltpu.VMEM((2 * ATT_PAIRS, seq, LANES), F32),
                         pltpu.VMEM((ATT_PAIRS, LANES, seq), BF16)]
                        + [pltpu.VMEM((1, ATT_Q), F32)] * (2 * ATT_PAIRS)
                        + [pltpu.VMEM((LANES, ATT_Q), F32)] * (2 * ATT_PAIRS)
                        + [pltpu.VMEM((ATT_BLOCK, ATT_Q), F32)] * (2 * ATT_PAIRS)),
        compiler_params=_params("parallel", "parallel"),
        name="fox_attention",
    )(qkv, qkv, qkv, ccol, crow)


def fox_mixer(x, w_qkvf, b_f, batch, seq):
    qkv = matmul(x, w_qkvf[:, :3 * ATT_DIM].astype(BF16), BF16)
    w_f = jnp.pad(w_qkvf[:, 3 * ATT_DIM:], ((0, 0), (0, LANES - ATT_HEADS)))
    ccol, crow = fox_decay(matmul_f32(x, w_f), b_f, batch, seq)
    return fox_attention(qkv, ccol, crow, batch, seq)


MOE_TILE = 512
ROUTE_BLOCK = 512


def _top2(logits):
    lane = lax.broadcasted_iota(jnp.int32, logits.shape, 1)
    logits = jnp.where(lane < N_EXPERTS, logits, NEG_BIG)
    m1 = jnp.max(logits, axis=-1, keepdims=True)
    i1 = jnp.min(jnp.where(logits == m1, lane, LANES), axis=-1, keepdims=True)
    rest = jnp.where(lane == i1, NEG_BIG, logits)
    m2 = jnp.max(rest, axis=-1, keepdims=True)
    i2 = jnp.min(jnp.where(rest == m2, lane, LANES), axis=-1, keepdims=True)
    e2 = jnp.exp(m2 - m1)
    denom = 1.0 + e2
    idx = jnp.where(lane == 0, i1, jnp.where(lane == 1, i2, 0))
    gates = jnp.where(lane == 0, 1.0 / denom, jnp.where(lane == 1, e2 / denom, 0.0))
    return idx, gates


def _rank_block(idx, run_ref):
    tb = idx.shape[0]
    lane = lax.broadcasted_iota(jnp.int32, idx.shape, 1)
    oh0 = lane == idx[:, 0:1]
    oh1 = lane == idx[:, 1:2]
    both = jnp.where(oh0 | oh1, 1.0, 0.0)
    before = jnp.dot(_tril_f32(tb, strict=True).astype(BF16), both.astype(BF16),
                     preferred_element_type=F32) + run_ref[...]
    r0 = jnp.sum(jnp.where(oh0, before, 0.0), axis=-1, keepdims=True)
    r1 = jnp.sum(jnp.where(oh1, before, 0.0), axis=-1, keepdims=True)
    run_ref[...] += jnp.sum(both, axis=0, keepdims=True)
    return jnp.where(lane == 0, r0, jnp.where(lane == 1, r1, 0.0)).astype(jnp.int32)


def _pos_kernel(idx_ref, rank_ref, off_ref, pos_ref):
    idx = idx_ref[...]
    lane = lax.broadcasted_iota(jnp.int32, idx.shape, 1)
    off = off_ref[...]
    p0 = jnp.sum(jnp.where(lane == idx[:, 0:1], off, 0), axis=-1, keepdims=True)
    p1 = jnp.sum(jnp.where(lane == idx[:, 1:2], off, 0), axis=-1, keepdims=True)
    pos_ref[...] = rank_ref[...] + jnp.where(lane == 0, p0, jnp.where(lane == 1, p1, 0))


def moe_positions(idx, rank, offsets):
    t = idx.shape[0]
    tb = min(4 * ROUTE_BLOCK, t)
    off = jnp.pad(offsets, (0, LANES - N_EXPERTS)).reshape(1, LANES)
    blk = pl.BlockSpec((tb, LANES), lambda i: (i, 0))
    return pl.pallas_call(
        _pos_kernel,
        grid=(t // tb,),
        in_specs=[blk, blk, pl.BlockSpec((1, LANES), lambda i: (0, 0))],
        out_specs=blk,
        out_shape=jax.ShapeDtypeStruct((t, LANES), jnp.int32),
        compiler_params=_params("parallel"),
        name="moe_positions",
    )(idx, rank, off)


DMA_ISSUE_UNROLL = 8


def _tile_rows(row):
    return pl.ds(pl.multiple_of(row * ROW_CHUNKS, ROW_CHUNKS), ROW_CHUNKS)


def _dispatch_kernel(pos_ref, cnt_ref, off_ref, x_ref, xs_ref, zero_ref, sem, pad_sem, *, tb):
    base = pl.program_id(0) * (TOP_K * tb)

    def issue(t, _):
        for k in range(TOP_K):
            pltpu.make_async_copy(x_ref.at[_tile_rows(t)],
                                  xs_ref.at[_tile_rows(pos_ref[base + TOP_K * t + k])], sem).start()
        return 0

    lax.fori_loop(0, tb, issue, 0, unroll=DMA_ISSUE_UNROLL)

    @pl.when(pl.program_id(0) == 0)
    def _():
        zero_ref[...] = jnp.zeros_like(zero_ref)

        def fill_range(first, count):
            def copy(r):
                return pltpu.make_async_copy(zero_ref, xs_ref.at[_tile_rows(first + r)], pad_sem)

            def fill(r, _):
                copy(r).start()
                return 0

            def drain(r, _):
                copy(r).wait()
                return 0

            lax.fori_loop(0, count, fill, 0)
            lax.fori_loop(0, count, drain, 0)

        for e in range(N_EXPERTS):
            fill_range(off_ref[e] + cnt_ref[e], (-cnt_ref[e]) & (MOE_TILE - 1))
        last = N_EXPERTS - 1
        used = off_ref[last] + cnt_ref[last] + ((-cnt_ref[last]) & (MOE_TILE - 1))
        fill_range(used, xs_ref.shape[0] // ROW_CHUNKS - used)

    for _ in range(TOP_K):
        pltpu.make_async_copy(x_ref, xs_ref.at[pl.ds(0, tb * ROW_CHUNKS)], sem).wait()


def moe_dispatch(x_tiled, pos_flat, counts, offsets, n_rows):
    t = x_tiled.shape[0] // ROW_CHUNKS
    tb = min(ROUTE_BLOCK, t)
    grid_spec = pltpu.PrefetchScalarGridSpec(
        num_scalar_prefetch=3,
        grid=(t // tb,),
        in_specs=[pl.BlockSpec((tb * ROW_CHUNKS, LANES), lambda i, *_: (i, 0))],
        out_specs=pl.BlockSpec(memory_space=pl.ANY),
        scratch_shapes=[pltpu.VMEM((ROW_CHUNKS, LANES), F32), pltpu.SemaphoreType.DMA(()),
                        pltpu.SemaphoreType.DMA(())],
    )
    return pl.pallas_call(
        functools.partial(_dispatch_kernel, tb=tb),
        grid_spec=grid_spec,
        out_shape=jax.ShapeDtypeStruct((n_rows * ROW_CHUNKS, LANES), F32),
        compiler_params=_params("arbitrary"),
        name="moe_dispatch",
    )(pos_flat, counts, offsets, x_tiled)


def _moe_ffn_kernel(te_ref, nt_ref, x_ref, wg_ref, wu_ref, wd_ref, o_ref, xb_ref, acc_ref):
    i = pl.program_id(0)
    f = pl.program_id(1)

    @pl.when(i < nt_ref[0])
    def _():
        @pl.when(f == 0)
        def _():
            for j in range(ROW_CHUNKS):
                xb_ref[:, j * LANES:(j + 1) * LANES] = _load_row_tiled(x_ref, j, MOE_TILE).astype(BF16)
            acc_ref[...] = jnp.zeros_like(acc_ref)

        xb = xb_ref[...]
        gate = jnp.dot(xb, wg_ref[0], preferred_element_type=F32)
        up = jnp.dot(xb, wu_ref[0], preferred_element_type=F32)
        h = (_silu(gate) * up).astype(BF16)
        acc_ref[...] += jnp.dot(h, wd_ref[0], preferred_element_type=F32)

        @pl.when(f == pl.num_programs(1) - 1)
        def _():
            _store_row_tiled(o_ref, acc_ref[...])

    @pl.when((i >= nt_ref[0]) & (f == 0))
    def _():
        o_ref[...] = jnp.zeros_like(o_ref)


def moe_ffn(xs, tile_expert, n_tiles_used, wg, wu, wd, tf=1792):
    n_rows, d = xs.shape[0] // ROW_CHUNKS, D_MODEL
    n_tiles = n_rows // MOE_TILE
    tile_spec = lambda index: pl.BlockSpec((MOE_TILE * ROW_CHUNKS, LANES), index)
    fdim = wg.shape[2]
    nf = fdim // tf

    def live(i, nt):
        return jnp.minimum(i, nt[0] - 1)

    def fblk(i, f, nt):
        return jnp.where(i < nt[0], f, nf - 1)

    grid_spec = pltpu.PrefetchScalarGridSpec(
        num_scalar_prefetch=2,
        grid=(n_tiles, nf),
        in_specs=[tile_spec(lambda i, f, te, nt: (live(i, nt), 0)),
                  pl.BlockSpec((1, d, tf), lambda i, f, te, nt: (te[i], 0, fblk(i, f, nt))),
                  pl.BlockSpec((1, d, tf), lambda i, f, te, nt: (te[i], 0, fblk(i, f, nt))),
                  pl.BlockSpec((1, tf, d), lambda i, f, te, nt: (te[i], fblk(i, f, nt), 0))],
        out_specs=tile_spec(lambda i, f, te, nt: (i, 0)),
        scratch_shapes=[pltpu.VMEM((MOE_TILE, d), BF16), pltpu.VMEM((MOE_TILE, d), F32)],
    )
    return pl.pallas_call(
        _moe_ffn_kernel,
        grid_spec=grid_spec,
        out_shape=jax.ShapeDtypeStruct(xs.shape, F32),
        compiler_params=_params("arbitrary", "arbitrary"),
        name="moe_ffn",
    )(tile_expert, n_tiles_used, xs, wg, wu, wd)


def _combine_kernel(pos_ref, x_ref, gate_ref, g_ref, b_ref, ys_ref, o_ref, buf_ref, sem, *, tb):
    i = pl.program_id(0)
    slot = i % 2

    def start_gathers(block, into):
        base = block * (TOP_K * tb)

        def issue(t, _):
            for k in range(TOP_K):
                pltpu.make_async_copy(ys_ref.at[_tile_rows(pos_ref[base + TOP_K * t + k])],
                                      buf_ref.at[into, k, _tile_rows(t)], sem.at[into]).start()
            return 0

        lax.fori_loop(0, tb, issue, 0, unroll=DMA_ISSUE_UNROLL)

    @pl.when(i == 0)
    def _():
        start_gathers(0, 0)

    @pl.when(i + 1 < pl.num_programs(0))
    def _():
        start_gathers(i + 1, 1 - slot)

    for k in range(TOP_K):
        pltpu.make_async_copy(ys_ref.at[pl.ds(0, tb * ROW_CHUNKS)], buf_ref.at[slot, k],
                              sem.at[slot]).wait()
    gates = gate_ref[...]
    y = jnp.concatenate(
        [buf_ref[slot, 0, pl.ds(j, tb, stride=ROW_CHUNKS), :] * gates[:, 0:1]
         + buf_ref[slot, 1, pl.ds(j, tb, stride=ROW_CHUNKS), :] * gates[:, 1:2]
         for j in range(ROW_CHUNKS)], axis=1)
    o_ref[...] = _layer_norm_rows(DN_ALPHA * x_ref[...] + y, g_ref[...], b_ref[...])


def moe_combine(x, ys, pos_flat, gates, g, b):
    t, d = x.shape
    tb = min(ROUTE_BLOCK, t)
    grid_spec = pltpu.PrefetchScalarGridSpec(
        num_scalar_prefetch=1,
        grid=(t // tb,),
        in_specs=[pl.BlockSpec((tb, d), lambda i, *_: (i, 0)),
                  pl.BlockSpec((tb, LANES), lambda i, *_: (i, 0)),
                  pl.BlockSpec((1, d), lambda i, *_: (0, 0)),
                  pl.BlockSpec((1, d), lambda i, *_: (0, 0)),
                  pl.BlockSpec(memory_space=pl.ANY)],
        out_specs=pl.BlockSpec((tb, d), lambda i, *_: (i, 0)),
        scratch_shapes=[pltpu.VMEM((2, TOP_K, tb * ROW_CHUNKS, LANES), F32),
                        pltpu.SemaphoreType.DMA((2,))],
    )
    return pl.pallas_call(
        functools.partial(_combine_kernel, tb=tb),
        grid_spec=grid_spec,
        out_shape=jax.ShapeDtypeStruct((t, d), F32),
        compiler_params=_params("arbitrary"),
        name="moe_combine",
    )(pos_flat, x, gates, g.reshape(1, d), b.reshape(1, d), ys)


def moe_deepnorm(routed, wg, wu, wd, g, b):
    x, x_tiled, idx, gates, rank, counts_f = routed
    t = x.shape[0]
    counts = counts_f[0, :N_EXPERTS].astype(jnp.int32)
    padded = (counts + MOE_TILE - 1) // MOE_TILE * MOE_TILE
    ends = jnp.cumsum(padded)
    offsets = ends - padded
    n_tiles = (TOP_K * t) // MOE_TILE + N_EXPERTS
    n_used = (ends[-1] // MOE_TILE).astype(jnp.int32)
    tile_start = jnp.arange(n_tiles, dtype=jnp.int32) * MOE_TILE
    tile_start = jnp.minimum(tile_start, ends[-1] - MOE_TILE)
    tile_expert = jnp.sum(tile_start[:, None] >= ends[None, :], axis=1).astype(jnp.int32)
    pos = moe_positions(idx, rank, offsets)
    pos_flat = pos[:, :TOP_K].reshape(-1)
    xs = moe_dispatch(x_tiled, pos_flat, counts, offsets, n_tiles * MOE_TILE)
    ys = moe_ffn(xs, tile_expert, n_used.reshape(1), wg.astype(BF16), wu.astype(BF16), wd.astype(BF16))
    return moe_combine(x, ys, pos_flat, gates, g, b)


def kernel(x, l0_ssd_w_in, l0_ssd_conv_w, l0_ssd_conv_b, l0_ssd_dt_bias, l0_ssd_a_log, l0_ssd_d_skip, l0_ssd_norm_w, l0_ssd_w_out, l0_ln_mix_g, l0_ln_mix_b, l0_ffn_w_gate, l0_ffn_w_up, l0_ffn_w_down, l0_ln_ffn_g, l0_ln_ffn_b, l1_sb_w_qkv, l1_sb_w_out, l1_ln_mix_g, l1_ln_mix_b, l1_moe_w_router, l1_moe_w_gate, l1_moe_w_up, l1_moe_w_down, l1_ln_ffn_g, l1_ln_ffn_b, l2_fox_w_qkvf, l2_fox_b_f, l2_fox_w_out, l2_ln_mix_g, l2_ln_mix_b, l2_ffn_w_gate, l2_ffn_w_up, l2_ffn_w_down, l2_ln_ffn_g, l2_ln_ffn_b, l3_ssd_w_in, l3_ssd_conv_w, l3_ssd_conv_b, l3_ssd_dt_bias, l3_ssd_a_log, l3_ssd_d_skip, l3_ssd_norm_w, l3_ssd_w_out, l3_ln_mix_g, l3_ln_mix_b, l3_moe_w_router, l3_moe_w_gate, l3_moe_w_up, l3_moe_w_down, l3_ln_ffn_g, l3_ln_ffn_b):
    batch, seq, d = x.shape
    h = x.reshape(batch * seq, d)
    bf = lambda w: w.astype(BF16)
    y = ssd_mixer(h, l0_ssd_w_in, l0_ssd_conv_w, l0_ssd_conv_b, l0_ssd_dt_bias, l0_ssd_a_log,
                  l0_ssd_d_skip, l0_ssd_norm_w, batch, seq)
    h = mixer_out_ffn_deepnorm(y, bf(l0_ssd_w_out), h, l0_ln_mix_g, l0_ln_mix_b,
                               bf(l0_ffn_w_gate), bf(l0_ffn_w_up), bf(l0_ffn_w_down), l0_ln_ffn_g, l0_ln_ffn_b)
    o = sb_mixer(h, l1_sb_w_qkv, batch, seq)
    routed = matmul_deepnorm_route(o, bf(l1_sb_w_out), h, l1_ln_mix_g, l1_ln_mix_b, l1_moe_w_router)
    h = moe_deepnorm(routed, l1_moe_w_gate, l1_moe_w_up, l1_moe_w_down, l1_ln_ffn_g, l1_ln_ffn_b)
    o = fox_mixer(h, l2_fox_w_qkvf, l2_fox_b_f, batch, seq)
    h = mixer_out_ffn_deepnorm(o, bf(l2_fox_w_out), h, l2_ln_mix_g, l2_ln_mix_b,
                               bf(l2_ffn_w_gate), bf(l2_ffn_w_up), bf(l2_ffn_w_down), l2_ln_ffn_g, l2_ln_ffn_b)
    y = ssd_mixer(h, l3_ssd_w_in, l3_ssd_conv_w, l3_ssd_conv_b, l3_ssd_dt_bias, l3_ssd_a_log,
                  l3_ssd_d_skip, l3_ssd_norm_w, batch, seq)
    routed = matmul_deepnorm_route(y, bf(l3_ssd_w_out), h, l3_ln_mix_g, l3_ln_mix_b, l3_moe_w_router)
    h = moe_deepnorm(routed, l3_moe_w_gate, l3_moe_w_up, l3_moe_w_down, l3_ln_ffn_g, l3_ln_ffn_b)
    return h.reshape(batch, seq, d)
```

```python
import functools
import math

import jax
import jax.numpy as jnp
from jax import lax
from jax.experimental import pallas as pl
from jax.experimental.pallas import tpu as pltpu

F32 = jnp.float32
BF16 = jnp.bfloat16
HIGHEST = lax.Precision.HIGHEST

LANES = 128
SUBLANES = 8
VMEM_LIMIT_BYTES = 56 * 1024 * 1024

D_MODEL = 1024
DEPTH = 4
SSD_D_INNER = 2048
SSD_HEAD_DIM = 64
SSD_HEADS = 32
SSD_GROUPS = 4
SSD_STATE = 128
SSD_CONV = 4
SSD_GROUP_DIM = SSD_D_INNER // SSD_GROUPS
SSD_BC_DIM = 2 * SSD_GROUPS * SSD_STATE
ATT_HEAD_DIM = 64
ATT_HEADS = 16
ATT_DIM = 1024
N_EXPERTS = 8
TOP_K = 2
DN_ALPHA = (2.0 * DEPTH) ** 0.25
LN_EPS = 1e-5
RMS_EPS = 1e-5
NEG_BIG = -1e30


def _params(*semantics):
    return pltpu.CompilerParams(dimension_semantics=semantics, vmem_limit_bytes=VMEM_LIMIT_BYTES)


def _layer_norm_rows(h, g, b):
    mu = jnp.mean(h, axis=-1, keepdims=True)
    d = h - mu
    var = jnp.mean(d * d, axis=-1, keepdims=True)
    return d * lax.rsqrt(var + LN_EPS) * g + b


def _silu(x):
    half = 0.5 * x
    return half * (1.0 + jnp.tanh(half))


def _mm_kernel(x_ref, w_ref, o_ref, xb_ref):
    @pl.when(pl.program_id(1) == 0)
    def _():
        xb_ref[...] = x_ref[...].astype(BF16)

    o_ref[...] = jnp.dot(xb_ref[...], w_ref[...], preferred_element_type=F32).astype(o_ref.dtype)


def matmul(x, w, out_dtype, tm=1024, tn=1024):
    m, k = x.shape
    tm = min(tm, m)
    n = w.shape[1]
    return pl.pallas_call(
        _mm_kernel,
        grid=(m // tm, n // tn),
        in_specs=[pl.BlockSpec((tm, k), lambda i, j: (i, 0)),
                  pl.BlockSpec((k, tn), lambda i, j: (0, j))],
        out_specs=pl.BlockSpec((tm, tn), lambda i, j: (i, j)),
        out_shape=jax.ShapeDtypeStruct((m, n), out_dtype),
        scratch_shapes=[pltpu.VMEM((tm, k), BF16)],
        compiler_params=_params("parallel", "arbitrary"),
        name="matmul",
    )(x, w)


def _dot_split(x, w):
    xh = x.astype(BF16)
    xl = (x - xh.astype(F32)).astype(BF16)
    wh = w.astype(BF16)
    wl = (w - wh.astype(F32)).astype(BF16)
    dot = functools.partial(jnp.dot, preferred_element_type=F32)
    return dot(xh, wh) + (dot(xh, wl) + dot(xl, wh))


def _mm_f32_kernel(x_ref, w_ref, o_ref):
    o_ref[...] = _dot_split(x_ref[...], w_ref[...])


def matmul_f32(x, w, tm=1024):
    m, k = x.shape
    tm = min(tm, m)
    n = w.shape[1]
    return pl.pallas_call(
        _mm_f32_kernel,
        grid=(m // tm,),
        in_specs=[pl.BlockSpec((tm, k), lambda i: (i, 0)),
                  pl.BlockSpec((k, n), lambda i: (0, 0))],
        out_specs=pl.BlockSpec((tm, n), lambda i: (i, 0)),
        out_shape=jax.ShapeDtypeStruct((m, n), F32),
        compiler_params=_params("parallel"),
        name="matmul_f32",
    )(x, w)


ROW_CHUNKS = D_MODEL // LANES


def _store_row_tiled(ref, value):
    n = value.shape[0]
    for j in range(ROW_CHUNKS):
        ref[pl.ds(j, n, stride=ROW_CHUNKS), :] = value[:, j * LANES:(j + 1) * LANES]


def _load_row_tiled(ref, j, n):
    return ref[pl.ds(j, n, stride=ROW_CHUNKS), :]


def _mm_ln_route_kernel(x_ref, w_ref, r_ref, g_ref, b_ref, wr_ref,
                        o_ref, tiled_ref, idx_ref, gate_ref, rank_ref, count_ref, run_ref):
    @pl.when(pl.program_id(0) == 0)
    def _():
        run_ref[...] = jnp.zeros_like(run_ref)

    y = jnp.dot(x_ref[...], w_ref[...], preferred_element_type=F32)
    o = _layer_norm_rows(DN_ALPHA * r_ref[...] + y, g_ref[...], b_ref[...])
    o_ref[...] = o
    _store_row_tiled(tiled_ref, o)
    idx, gates = _top2(_dot_split(o, wr_ref[...]))
    idx_ref[...] = idx
    gate_ref[...] = gates
    rank_ref[...] = _rank_block(idx, run_ref)
    count_ref[...] = run_ref[...]


def matmul_deepnorm_route(x, w, resid, g, b, w_router):
    m, k = x.shape
    tm = min(ROUTE_BLOCK, m)
    d = w.shape[1]
    rows = lambda width: pl.BlockSpec((tm, width), lambda i: (i, 0))
    fixed = lambda a: pl.BlockSpec(a.shape, lambda i: (0, 0))
    args = [x, w, resid, g.reshape(1, d), b.reshape(1, d),
            jnp.pad(w_router, ((0, 0), (0, LANES - N_EXPERTS)))]
    return pl.pallas_call(
        _mm_ln_route_kernel,
        grid=(m // tm,),
        in_specs=[rows(k), fixed(w), rows(d), fixed(args[3]), fixed(args[4]), fixed(args[5])],
        out_specs=[rows(d), pl.BlockSpec((tm * ROW_CHUNKS, LANES), lambda i: (i, 0)),
                   rows(LANES), rows(LANES), rows(LANES), pl.BlockSpec((1, LANES), lambda i: (0, 0))],
        out_shape=[jax.ShapeDtypeStruct((m, d), F32),
                   jax.ShapeDtypeStruct((m * ROW_CHUNKS, LANES), F32),
                   jax.ShapeDtypeStruct((m, LANES), jnp.int32),
                   jax.ShapeDtypeStruct((m, LANES), F32),
                   jax.ShapeDtypeStruct((m, LANES), jnp.int32),
                   jax.ShapeDtypeStruct((1, LANES), F32)],
        scratch_shapes=[pltpu.VMEM((1, LANES), F32)],
        compiler_params=_params("arbitrary"),
        name="matmul_deepnorm_route",
    )(*args)


def _mix_ffn_kernel(o_ref, wo_ref, r_ref, g1_ref, b1_ref, wg_ref, wu_ref, wd_ref, g2_ref, b2_ref, out_ref):
    x = _layer_norm_rows(
        DN_ALPHA * r_ref[...] + jnp.dot(o_ref[...], wo_ref[...], preferred_element_type=F32),
        g1_ref[...], b1_ref[...])
    xb = x.astype(BF16)
    gate = jnp.dot(xb, wg_ref[...], preferred_element_type=F32)
    up = jnp.dot(xb, wu_ref[...], preferred_element_type=F32)
    h = (_silu(gate) * up).astype(BF16)
    y = jnp.dot(h, wd_ref[...], preferred_element_type=F32)
    out_ref[...] = _layer_norm_rows(DN_ALPHA * x + y, g2_ref[...], b2_ref[...])


def mixer_out_ffn_deepnorm(o, w_out, resid, g1, b1, wg, wu, wd, g2, b2, tm=512):
    m, k = o.shape
    d = w_out.shape[1]
    tm = min(tm, m)
    resident = lambda a: pl.BlockSpec(a.shape, lambda i: (0, 0), pipeline_mode=pl.Buffered(1))
    vec = lambda v: v.reshape(1, d)
    args = (o, w_out, resid, vec(g1), vec(b1), wg, wu, wd, vec(g2), vec(b2))
    return pl.pallas_call(
        _mix_ffn_kernel,
        grid=(m // tm,),
        in_specs=[pl.BlockSpec((tm, k), lambda i: (i, 0)), resident(w_out),
                  pl.BlockSpec((tm, d), lambda i: (i, 0)), resident(args[3]), resident(args[4]),
                  resident(wg), resident(wu), resident(wd), resident(args[8]), resident(args[9])],
        out_specs=pl.BlockSpec((tm, d), lambda i: (i, 0)),
        out_shape=jax.ShapeDtypeStruct((m, d), F32),
        compiler_params=_params("parallel"),
        name="mixer_out_ffn_deepnorm",
    )(*args)


def _split_bf16(v, pieces):
    out = []
    r = v
    for _ in range(pieces - 1):
        p = r.astype(BF16)
        out.append(p)
        r = r - p.astype(F32)
    out.append(r.astype(BF16))
    return out


def _expand(v, e_ref, pieces):
    stacked = jnp.concatenate(_split_bf16(v, pieces), axis=1)
    return jnp.dot(stacked, e_ref[...], preferred_element_type=F32)


def _tril_f32(n, strict=False):
    r = lax.broadcasted_iota(jnp.int32, (n, n), 0)
    c = lax.broadcasted_iota(jnp.int32, (n, n), 1)
    return ((r > c) if strict else (r >= c)).astype(F32)


def _ssd_kernel(z_ref, xs_ref, bc_ref, dt_ref,
                dtb_ref, alog_ref, dskip_ref, normw_ref, e64_ref, o_ref,
                state_ref, xdt_ref, cb_ref, ydiag_ref,
                *, chunk):
    L = chunk
    c = pl.program_id(1)

    @pl.when(c == 0)
    def _():
        state_ref[...] = jnp.zeros_like(state_ref)

    xs = xs_ref[...]
    bcv = bc_ref[...]

    dt = jax.nn.softplus(dt_ref[...] + dtb_ref[...])
    da = dt * (-jnp.exp(alog_ref[...]))
    a_cs = jnp.dot(_tril_f32(L), da, preferred_element_type=F32, precision=HIGHEST)
    ea = jnp.exp(a_cs)
    dte = jnp.exp(a_cs[L - 1:L, :] - a_cs)

    spread = _expand(jnp.concatenate([dt, ea, dte], axis=0), e64_ref, 2)
    dt_x, ea_x, dte_x = spread[0:L], spread[L:2 * L], spread[2 * L:3 * L]
    a_t = a_cs.T

    xdt = xs * dt_x
    xdt_ref[...] = xdt.astype(BF16)
    xdte = (xdt * dte_x).astype(BF16)

    y_off = []
    for g in range(SSD_GROUPS):
        bm = bcv[:, g * SSD_STATE:(g + 1) * SSD_STATE]
        cm = bcv[:, (SSD_GROUPS + g) * SSD_STATE:(SSD_GROUPS + g + 1) * SSD_STATE].astype(BF16)
        cb_ref[g] = lax.dot_general(cm, bm.astype(BF16), (((1,), (1,)), ((), ())),
                                    preferred_element_type=F32)
        gs = slice(g * SSD_GROUP_DIM, (g + 1) * SSD_GROUP_DIM)
        st = state_ref[g]
        y_off.append(jnp.dot(cm, st.astype(BF16), preferred_element_type=F32) * ea_x[:, gs])
        state_ref[g] = st * ea_x[L - 1:L, gs] + jnp.dot(
            bm.T.astype(BF16), xdte[:, gs], preferred_element_type=F32)

    row = lax.broadcasted_iota(jnp.int32, (L, L), 0)
    col = lax.broadcasted_iota(jnp.int32, (L, L), 1)
    causal = row >= col
    head0 = lax.broadcasted_iota(jnp.int32, (L, LANES), 1) < SSD_HEAD_DIM

    heads_per_group = SSD_HEADS // SSD_GROUPS

    for g in range(SSD_GROUPS):
        cbg = cb_ref[g]
        scores = []
        for j in range(heads_per_group):
            h = g * heads_per_group + j
            a_l = a_cs[:, h:h + 1]
            a_s = a_t[h:h + 1, :]
            scores.append((cbg * jnp.exp(jnp.where(causal, a_l - a_s, NEG_BIG))).astype(BF16))
        for pp in range(heads_per_group // 2):
            lanes = slice((g * (heads_per_group // 2) + pp) * LANES,
                          (g * (heads_per_group // 2) + pp + 1) * LANES)
            xpair = xdt_ref[:, lanes]
            ys = [jnp.dot(scores[2 * pp + j], xpair, preferred_element_type=F32) for j in range(2)]
            ydiag_ref[:, lanes] = jnp.where(head0, ys[0], ys[1])

    y = ydiag_ref[...] + jnp.concatenate(y_off, axis=1) + dskip_ref[...] * xs
    y = y * _silu(z_ref[...])
    parts = []
    for g in range(SSD_GROUPS):
        yg = y[:, g * SSD_GROUP_DIM:(g + 1) * SSD_GROUP_DIM]
        parts.append(yg * lax.rsqrt(jnp.mean(yg * yg, axis=-1, keepdims=True) + RMS_EPS))
    o_ref[...] = (jnp.concatenate(parts, axis=1) * normw_ref[...]).astype(o_ref.dtype)


def _expansion_matrix(width, pieces):
    h = jnp.arange(LANES)[:, None]
    lane = jnp.arange(SSD_HEADS * width)[None, :]
    e = (lane // width == h).astype(BF16)
    return jnp.concatenate([e] * pieces, axis=0)


def _mm_conv_kernel(x_ref, halo_ref, w_ref, cw_ref, cb_ref, o_ref, ext_ref, xb_ref, *, tiles_per_seq):
    tm = x_ref.shape[0]

    @pl.when(pl.program_id(1) == 0)
    def _():
        xb_ref[...] = x_ref[...].astype(BF16)

    w = w_ref[...]
    halo = jnp.dot(halo_ref[...].astype(BF16), w, preferred_element_type=F32)
    first = pl.program_id(0) % tiles_per_seq == 0
    ext_ref[0:SUBLANES, :] = jnp.where(first, 0.0, halo)
    ext_ref[SUBLANES:SUBLANES + tm, :] = jnp.dot(xb_ref[...], w, preferred_element_type=F32)
    ext = ext_ref[...]
    acc = cw_ref[0:1, :] * ext
    for k in range(1, SSD_CONV):
        acc = pltpu.roll(acc, 1, axis=0) + cw_ref[k:k + 1, :] * ext
    o_ref[...] = _silu(acc[SUBLANES:SUBLANES + tm, :] + cb_ref[...])


def matmul_conv_silu(x, w, conv_w, conv_b, seq, tm=1024, tn=1024):
    m, k = x.shape
    tm = min(tm, seq)
    n = w.shape[1]
    halo_blocks = tm // SUBLANES
    return pl.pallas_call(
        functools.partial(_mm_conv_kernel, tiles_per_seq=seq // tm),
        grid=(m // tm, n // tn),
        in_specs=[pl.BlockSpec((tm, k), lambda i, j: (i, 0)),
                  pl.BlockSpec((SUBLANES, k), lambda i, j: (jnp.maximum(i * halo_blocks - 1, 0), 0)),
                  pl.BlockSpec((k, tn), lambda i, j: (0, j)),
                  pl.BlockSpec((SSD_CONV, tn), lambda i, j: (0, j)),
                  pl.BlockSpec((1, tn), lambda i, j: (0, j))],
        out_specs=pl.BlockSpec((tm, tn), lambda i, j: (i, j)),
        out_shape=jax.ShapeDtypeStruct((m, n), F32),
        scratch_shapes=[pltpu.VMEM((tm + SUBLANES, tn), F32), pltpu.VMEM((tm, k), BF16)],
        compiler_params=_params("parallel", "arbitrary"),
        name="matmul_conv_silu",
    )(x, x, w, conv_w, conv_b.reshape(1, n))


def ssd_core(z, xbc, dt_raw, dt_bias, a_log, d_skip, norm_w, batch, seq, chunk=128):
    t = z.shape[0]
    nc = seq // chunk
    pad = LANES - SSD_HEADS
    row = lambda v: v.reshape(1, -1).astype(F32)
    args = (
        z, xbc, xbc, dt_raw,
        row(jnp.pad(dt_bias, (0, pad))), row(jnp.pad(a_log, (0, pad))),
        row(jnp.repeat(d_skip, SSD_HEAD_DIM)), row(norm_w),
        _expansion_matrix(SSD_HEAD_DIM, 2),
    )
    blk = lambda b, c: (b * nc + c, 0)
    const = lambda b, c: (0, 0)
    full = lambda a: pl.BlockSpec(a.shape, const)
    in_specs = [
        pl.BlockSpec((chunk, SSD_D_INNER), blk),
        pl.BlockSpec((chunk, SSD_D_INNER), blk),
        pl.BlockSpec((chunk, SSD_BC_DIM), lambda b, c: (b * nc + c, 2)),
        pl.BlockSpec((chunk, LANES), blk),
    ] + [full(a) for a in args[4:]]
    return pl.pallas_call(
        functools.partial(_ssd_kernel, chunk=chunk),
        grid=(batch, nc),
        in_specs=in_specs,
        out_specs=pl.BlockSpec((chunk, SSD_D_INNER), blk),
        out_shape=jax.ShapeDtypeStruct((t, SSD_D_INNER), BF16),
        scratch_shapes=[
            pltpu.VMEM((SSD_GROUPS, SSD_STATE, SSD_GROUP_DIM), F32),
            pltpu.VMEM((chunk, SSD_D_INNER), BF16),
            pltpu.VMEM((SSD_GROUPS, chunk, chunk), F32),
            pltpu.VMEM((chunk, SSD_D_INNER), F32),
        ],
        compiler_params=_params("parallel", "arbitrary"),
        name="ssd_core",
    )(*args)


def ssd_mixer(x, w_in, conv_w, conv_b, dt_bias, a_log, d_skip, norm_w, batch, seq):
    n_zx = SSD_D_INNER + SSD_D_INNER + SSD_BC_DIM
    z = matmul(x, w_in[:, :SSD_D_INNER].astype(BF16), F32)
    xbc = matmul_conv_silu(x, w_in[:, SSD_D_INNER:n_zx].astype(BF16), conv_w, conv_b, seq)
    w_dt = jnp.pad(w_in[:, n_zx:], ((0, 0), (0, LANES - SSD_HEADS)))
    dt_raw = matmul_f32(x, w_dt)
    return ssd_core(z, xbc, dt_raw, dt_bias, a_log, d_skip, norm_w, batch, seq)


ATT_BLOCK = 128
ATT_Q = 256
ATT_PAIRS = 4
ATT_STEP_LANES = ATT_PAIRS * LANES
ATT_SCALE = ATT_HEAD_DIM ** -0.5
LOG2E = 1.4426950408889634
EXP2_UNDERFLOW = -160.0
FOX_BOUND_SLACK = 1.05


def _split_heads(q_ref, rows, head0):
    qs = []
    for p in range(ATT_PAIRS):
        q2 = q_ref[rows, p * LANES:(p + 1) * LANES]
        zero = jnp.zeros_like(q2)
        qs += [jnp.where(head0, q2, zero), jnp.where(head0, zero, q2)]
    return qs


def _sb_kernel(q_ref, k_ref, v_ref, u_ref, o_ref, *scratch, seq):
    n_heads = 2 * ATT_PAIRS
    later_ref, acc_ref, z_ref = scratch[:n_heads], scratch[n_heads:2 * n_heads], scratch[2 * n_heads:]
    nq = seq // ATT_Q
    ndiag = ATT_Q // ATT_BLOCK
    head0 = lax.broadcasted_iota(jnp.int32, (ATT_Q, LANES), 1) < ATT_HEAD_DIM
    row = lax.broadcasted_iota(jnp.int32, (ATT_Q, ATT_BLOCK), 0)
    col = lax.broadcasted_iota(jnp.int32, (ATT_Q, ATT_BLOCK), 1)

    def q_body(i, _):
        q0 = pl.multiple_of(i * ATT_Q, ATT_Q)
        rows = pl.ds(q0, ATT_Q)
        qs = _split_heads(q_ref, rows, head0)
        for n in range(2 * ATT_PAIRS):
            later_ref[n][...] = jnp.zeros((ATT_Q, LANES), F32)
            acc_ref[n][...] = jnp.zeros((ATT_Q, LANES), F32)

        heads = range(2 * ATT_PAIRS)

        def key_rows(j):
            return pl.ds(pl.multiple_of(j * ATT_BLOCK, ATT_BLOCK), ATT_BLOCK)

        def scores(j):
            k2 = [k_ref[key_rows(j), p * LANES:(p + 1) * LANES] for p in range(ATT_PAIRS)]
            return [lax.dot_general(qs[n], k2[n // 2], (((1,), (1,)), ((), ())),
                                    preferred_element_type=F32) * (ATT_SCALE * LOG2E) for n in heads]

        def consume(z2, j, strict):
            keys = key_rows(j)
            v2 = [v_ref[keys, p * LANES:(p + 1) * LANES] for p in range(ATT_PAIRS)]
            log_beta, sums = [], []
            for n in heads:
                lb = jnp.minimum(z2[n], 0.0) - jnp.log2(1.0 + jnp.exp2(-jnp.abs(z2[n])))
                log_keep = lb - z2[n]
                if strict is not None:
                    log_keep = jnp.where(strict, log_keep, 0.0)
                hi = log_keep.astype(BF16)
                lo = (log_keep - hi.astype(F32)).astype(BF16)
                log_beta.append(lb)
                sums.append(jnp.dot(jnp.concatenate([hi, lo], axis=1), u_ref[...],
                                    preferred_element_type=F32))
            for n in heads:
                w = jnp.exp2(log_beta[n] + sums[n][:, :ATT_BLOCK] + later_ref[n][...])
                if strict is not None:
                    w = jnp.where(strict, w, 0.0)
                acc_ref[n][...] += jnp.dot(w.astype(BF16), v2[n // 2], preferred_element_type=F32)
                later_ref[n][...] += sums[n][:, ATT_BLOCK:]

        for d in reversed(range(ndiag)):
            consume(scores(i * ndiag + d), i * ndiag + d, d * ATT_BLOCK + col < row)

        n_off = i * ndiag
        for n, zn in enumerate(scores(jnp.maximum(n_off - 1, 0))):
            z_ref[n][...] = zn

        def off_diagonal(carry):
            jj, _ = carry
            z2 = [z_ref[n][...] for n in heads]
            ahead = scores(jnp.maximum(n_off - 2 - jj, 0))
            consume(z2, n_off - 1 - jj, None)
            for n in heads:
                z_ref[n][...] = ahead[n]
            top = later_ref[0][...]
            for n in range(1, 2 * ATT_PAIRS):
                top = jnp.maximum(top, later_ref[n][...])
            return jj + 1, jnp.max(top) > EXP2_UNDERFLOW

        lax.while_loop(lambda c: (c[0] < n_off) & c[1], off_diagonal, (jnp.int32(0), n_off > 0))
        for p in range(ATT_PAIRS):
            o_ref[rows, p * LANES:(p + 1) * LANES] = jnp.where(
                head0, acc_ref[2 * p][...], acc_ref[2 * p + 1][...]).astype(o_ref.dtype)
        return 0

    lax.fori_loop(0, nq, q_body, 0)


def _suffix_sum_matrix():
    j = jnp.arange(2 * ATT_BLOCK)[:, None] % ATT_BLOCK
    s = jnp.arange(2 * ATT_BLOCK)[None, :]
    return jnp.where(s < ATT_BLOCK, j > s, True).astype(BF16)


def _attention_specs(seq):
    n_steps = ATT_HEADS // 2 // ATT_PAIRS
    blk = lambda off: pl.BlockSpec((seq, ATT_STEP_LANES), lambda b, p: (b, off + p))
    specs = [blk(0), blk(n_steps), blk(2 * n_steps)]
    return n_steps, specs, pl.BlockSpec((seq, ATT_STEP_LANES), lambda b, p: (b, p))


def sb_attention(qkv, batch, seq):
    n_steps, in_specs, out_spec = _attention_specs(seq)
    u = _suffix_sum_matrix()
    return pl.pallas_call(
        functools.partial(_sb_kernel, seq=seq),
        grid=(batch, n_steps),
        in_specs=in_specs + [pl.BlockSpec(u.shape, lambda b, p: (0, 0))],
        out_specs=out_spec,
        out_shape=jax.ShapeDtypeStruct((batch * seq, ATT_DIM), BF16),
        scratch_shapes=[pltpu.VMEM((ATT_Q, LANES), F32)] * (6 * ATT_PAIRS),
        compiler_params=_params("parallel", "parallel"),
        name="sb_attention",
    )(qkv, qkv, qkv, u)


def sb_mixer(x, w_qkv, batch, seq):
    return sb_attention(matmul(x, w_qkv.astype(BF16), BF16), batch, seq)


CUMSUM_BLOCK = 256


def _fox_decay_kernel(f_ref, bf_ref, ccol_ref, crow_ref, *, seq):
    tri = _tril_f32(CUMSUM_BLOCK)
    carry = jnp.zeros((1, LANES), F32)
    for blk in range(seq // CUMSUM_BLOCK):
        rows = slice(blk * CUMSUM_BLOCK, (blk + 1) * CUMSUM_BLOCK)
        log_f = jax.nn.log_sigmoid(f_ref[rows, :] + bf_ref[...])
        c = jnp.dot(tri, log_f, preferred_element_type=F32, precision=HIGHEST) + carry
        carry = c[CUMSUM_BLOCK - 1:CUMSUM_BLOCK, :]
        ccol_ref[rows, :] = c
        c_t = c.T
        for p in range(ATT_HEADS // 2):
            crow_ref[0, p, :, rows] = c_t[2 * p:2 * p + 2, :]


def fox_decay(f_raw, b_f, batch, seq):
    return pl.pallas_call(
        functools.partial(_fox_decay_kernel, seq=seq),
        grid=(batch,),
        in_specs=[pl.BlockSpec((seq, LANES), lambda b: (b, 0)),
                  pl.BlockSpec((1, LANES), lambda b: (0, 0))],
        out_specs=[pl.BlockSpec((seq, LANES), lambda b: (b, 0)),
                   pl.BlockSpec((1, ATT_HEADS // 2, 2, seq), lambda b: (b, 0, 0, 0))],
        out_shape=[jax.ShapeDtypeStruct((batch * seq, LANES), F32),
                   jax.ShapeDtypeStruct((batch, ATT_HEADS // 2, 2, seq), F32)],
        compiler_params=_params("parallel"),
        name="fox_decay",
    )(f_raw, jnp.pad(b_f, (0, LANES - ATT_HEADS)).reshape(1, LANES))


def _fox_kernel(q_ref, k_ref, v_ref, ccol_ref, crow_ref, o_ref, crep_ref, vt_ref, *scratch, seq):
    n_heads = 2 * ATT_PAIRS
    m_ref, acc_ref, s_ref = scratch[:n_heads], scratch[n_heads:2 * n_heads], scratch[2 * n_heads:]
    nq = seq // ATT_Q
    ndiag = ATT_Q // ATT_BLOCK
    step = pl.program_id(1)
    heads = range(n_heads)
    head0 = lax.broadcasted_iota(jnp.int32, (ATT_Q, LANES), 1) < ATT_HEAD_DIM
    head0_rows = lax.broadcasted_iota(jnp.int32, (LANES, ATT_BLOCK), 0) < ATT_HEAD_DIM
    key = lax.broadcasted_iota(jnp.int32, (ATT_BLOCK, ATT_Q), 0)
    qry = lax.broadcasted_iota(jnp.int32, (ATT_BLOCK, ATT_Q), 1)

    pieces = jnp.concatenate(_split_bf16(ccol_ref[...], 3), axis=1)
    sel_row = lax.broadcasted_iota(jnp.int32, (3 * LANES, LANES), 0) & (LANES - 1)
    for n in heads:
        sel = jnp.where(sel_row == 2 * ATT_PAIRS * step + n, 1.0, 0.0).astype(BF16)
        crep_ref[n] = jnp.dot(pieces, sel, preferred_element_type=F32) * LOG2E
    for p in range(ATT_PAIRS):
        for blk in range(seq // ATT_BLOCK):
            rows = slice(blk * ATT_BLOCK, (blk + 1) * ATT_BLOCK)
            vt_ref[p, :, rows] = v_ref[rows, p * LANES:(p + 1) * LANES].astype(F32).T.astype(BF16)
    lane_r = lax.broadcasted_iota(jnp.int32, (LANES, LANES), 0) < ATT_HEAD_DIM
    lane_c = lax.broadcasted_iota(jnp.int32, (LANES, LANES), 1) < ATT_HEAD_DIM
    same_head = jnp.where(lane_r == lane_c, 1.0, 0.0).astype(BF16)
    k_norm = []
    for p in range(ATT_PAIRS):
        kf = k_ref[:, p * LANES:(p + 1) * LANES].astype(F32)
        sq = jnp.dot((kf * kf).astype(BF16), same_head, preferred_element_type=F32)
        top = jnp.sqrt(jnp.max(sq, axis=0, keepdims=True))
        k_norm += [top[:, 0:1], top[:, ATT_HEAD_DIM:ATT_HEAD_DIM + 1]]
    ones_rows = jnp.ones((SUBLANES, LANES), BF16)

    def q_body(i, _):
        q0 = pl.multiple_of(i * ATT_Q, ATT_Q)
        rows = pl.ds(q0, ATT_Q)
        qs = _split_heads(q_ref, rows, head0)
        c_q = [crow_ref[0, n // 2, n % 2:n % 2 + 1, rows] * LOG2E for n in heads]
        qk_bound = []
        for n in heads:
            qf = qs[n].astype(F32)
            q_sq = lax.dot_general(ones_rows, (qf * qf).astype(BF16), (((1,), (1,)), ((), ())),
                                   preferred_element_type=F32)[0:1, :]
            qk_bound.append(jnp.sqrt(q_sq) * k_norm[n] * (ATT_SCALE * LOG2E * FOX_BOUND_SLACK))
        for n in heads:
            m_ref[n][...] = jnp.full((1, ATT_Q), NEG_BIG, F32)
            acc_ref[n][...] = jnp.zeros((LANES, ATT_Q), F32)

        def key_rows(j):
            return pl.ds(pl.multiple_of(j * ATT_BLOCK, ATT_BLOCK), ATT_BLOCK)

        def scores(j):
            k2 = [k_ref[key_rows(j), p * LANES:(p + 1) * LANES] for p in range(ATT_PAIRS)]
            return [lax.dot_general(k2[n // 2], qs[n], (((1,), (1,)), ((), ())),
                                    preferred_element_type=F32) * (ATT_SCALE * LOG2E) for n in heads]

        def consume(s, j, causal):
            keys = key_rows(j)
            vt = [vt_ref[p, :, keys] for p in range(ATT_PAIRS)]
            for n in heads:
                c_k = crep_ref[n, keys, :]
                sn = s[n] + (c_q[n] - jnp.concatenate([c_k] * (ATT_Q // LANES), axis=1))
                if causal is not None:
                    sn = jnp.where(causal, sn, NEG_BIG)
                m_old = m_ref[n][...]
                m_new = jnp.maximum(m_old, jnp.max(sn, axis=0, keepdims=True))
                prob = jnp.exp2(sn - m_new).astype(BF16)
                one = jnp.ones_like(vt[n // 2])
                v1t = jnp.where(head0_rows, vt[n // 2], one) if n % 2 == 0 else jnp.where(
                    head0_rows, one, vt[n // 2])
                acc_ref[n][...] = jnp.exp2(m_old - m_new) * acc_ref[n][...] + jnp.dot(
                    v1t, prob, preferred_element_type=F32)
                m_ref[n][...] = m_new

        for d in range(ndiag):
            consume(scores(i * ndiag + d), i * ndiag + d, d * ATT_BLOCK + key <= qry)

        n_off = i * ndiag
        for n, sn in enumerate(scores(jnp.maximum(n_off - 1, 0))):
            s_ref[n][...] = sn

        def off_diagonal(carry):
            jj, _ = carry
            j = n_off - 1 - jj
            s = [s_ref[n][...] for n in heads]
            ahead = scores(jnp.maximum(j - 1, 0))
            consume(s, j, None)
            for n in heads:
                s_ref[n][...] = ahead[n]
            prev_keys = key_rows(jnp.maximum(j - 1, 0))
            worst = None
            for n in heads:
                c_last = crow_ref[0, n // 2, n % 2:n % 2 + 1, prev_keys][:, LANES - 1:LANES] * LOG2E
                gap = qk_bound[n] + c_q[n] - c_last - m_ref[n][...]
                worst = gap if worst is None else jnp.maximum(worst, gap)
            return jj + 1, jnp.max(worst) > EXP2_UNDERFLOW

        lax.while_loop(lambda c: (c[0] < n_off) & c[1], off_diagonal, (jnp.int32(0), n_off > 0))
        for p in range(ATT_PAIRS):
            a0, a1 = acc_ref[2 * p][...], acc_ref[2 * p + 1][...]
            o_t = jnp.concatenate([a0[:ATT_HEAD_DIM] / a0[ATT_HEAD_DIM:],
                                   a1[ATT_HEAD_DIM:] / a1[:ATT_HEAD_DIM]], axis=0)
            o_ref[rows, p * LANES:(p + 1) * LANES] = o_t.T.astype(o_ref.dtype)
        return 0

    lax.fori_loop(0, nq, q_body, 0)


def fox_attention(qkv, ccol, crow, batch, seq):
    n_steps, in_specs, out_spec = _attention_specs(seq)
    in_specs += [pl.BlockSpec((seq, LANES), lambda b, p: (b, 0)),
                 pl.BlockSpec((1, ATT_PAIRS, 2, seq), lambda b, p: (b, p, 0, 0))]
    return pl.pallas_call(
        functools.partial(_fox_kernel, seq=seq),
        grid=(batch, n_steps),
        in_specs=in_specs,
        out_specs=out_spec,
        out_shape=jax.ShapeDtypeStruct((batch * seq, ATT_DIM), BF16),
        scratch_shapes=([pltpu.VMEM((2 * ATT_PAIRS, seq, LANES), F32),
                         pltpu.VMEM((ATT_PAIRS, LANES, seq), BF16)]
                        + [pltpu.VMEM((1, ATT_Q), F32)] * (2 * ATT_PAIRS)
                        + [pltpu.VMEM((LANES, ATT_Q), F32)] * (2 * ATT_PAIRS)
                        + [pltpu.VMEM((ATT_BLOCK, ATT_Q), F32)] * (2 * ATT_PAIRS)),
        compiler_params=_params("parallel", "parallel"),
        name="fox_attention",
    )(qkv, qkv, qkv, ccol, crow)


def fox_mixer(x, w_qkvf, b_f, batch, seq):
    qkv = matmul(x, w_qkvf[:, :3 * ATT_DIM].astype(BF16), BF16)
    w_f = jnp.pad(w_qkvf[:, 3 * ATT_DIM:], ((0, 0), (0, LANES - ATT_HEADS)))
    ccol, crow = fox_decay(matmul_f32(x, w_f), b_f, batch, seq)
    return fox_attention(qkv, ccol, crow, batch, seq)


MOE_TILE = 512
ROUTE_BLOCK = 512


def _top2(logits):
    lane = lax.broadcasted_iota(jnp.int32, logits.shape, 1)
    logits = jnp.where(lane < N_EXPERTS, logits, NEG_BIG)
    m1 = jnp.max(logits, axis=-1, keepdims=True)
    i1 = jnp.min(jnp.where(logits == m1, lane, LANES), axis=-1, keepdims=True)
    rest = jnp.where(lane == i1, NEG_BIG, logits)
    m2 = jnp.max(rest, axis=-1, keepdims=True)
    i2 = jnp.min(jnp.where(rest == m2, lane, LANES), axis=-1, keepdims=True)
    e2 = jnp.exp(m2 - m1)
    denom = 1.0 + e2
    idx = jnp.where(lane == 0, i1, jnp.where(lane == 1, i2, 0))
    gates = jnp.where(lane == 0, 1.0 / denom, jnp.where(lane == 1, e2 / denom, 0.0))
    return idx, gates


def _rank_block(idx, run_ref):
    tb = idx.shape[0]
    lane = lax.broadcasted_iota(jnp.int32, idx.shape, 1)
    oh0 = lane == idx[:, 0:1]
    oh1 = lane == idx[:, 1:2]
    both = jnp.where(oh0 | oh1, 1.0, 0.0)
    before = jnp.dot(_tril_f32(tb, strict=True).astype(BF16), both.astype(BF16),
                     preferred_element_type=F32) + run_ref[...]
    r0 = jnp.sum(jnp.where(oh0, before, 0.0), axis=-1, keepdims=True)
    r1 = jnp.sum(jnp.where(oh1, before, 0.0), axis=-1, keepdims=True)
    run_ref[...] += jnp.sum(both, axis=0, keepdims=True)
    return jnp.where(lane == 0, r0, jnp.where(lane == 1, r1, 0.0)).astype(jnp.int32)


def _pos_kernel(idx_ref, rank_ref, off_ref, pos_ref):
    idx = idx_ref[...]
    lane = lax.broadcasted_iota(jnp.int32, idx.shape, 1)
    off = off_ref[...]
    p0 = jnp.sum(jnp.where(lane == idx[:, 0:1], off, 0), axis=-1, keepdims=True)
    p1 = jnp.sum(jnp.where(lane == idx[:, 1:2], off, 0), axis=-1, keepdims=True)
    pos_ref[...] = rank_ref[...] + jnp.where(lane == 0, p0, jnp.where(lane == 1, p1, 0))


def moe_positions(idx, rank, offsets):
    t = idx.shape[0]
    tb = min(4 * ROUTE_BLOCK, t)
    off = jnp.pad(offsets, (0, LANES - N_EXPERTS)).reshape(1, LANES)
    blk = pl.BlockSpec((tb, LANES), lambda i: (i, 0))
    return pl.pallas_call(
        _pos_kernel,
        grid=(t // tb,),
        in_specs=[blk, blk, pl.BlockSpec((1, LANES), lambda i: (0, 0))],
        out_specs=blk,
        out_shape=jax.ShapeDtypeStruct((t, LANES), jnp.int32),
        compiler_params=_params("parallel"),
        name="moe_positions",
    )(idx, rank, off)


DMA_ISSUE_UNROLL = 8


def _tile_rows(row):
    return pl.ds(pl.multiple_of(row * ROW_CHUNKS, ROW_CHUNKS), ROW_CHUNKS)


def _dispatch_kernel(pos_ref, cnt_ref, off_ref, x_ref, xs_ref, zero_ref, sem, pad_sem, *, tb):
    base = pl.program_id(0) * (TOP_K * tb)

    def issue(t, _):
        for k in range(TOP_K):
            pltpu.make_async_copy(x_ref.at[_tile_rows(t)],
                                  xs_ref.at[_tile_rows(pos_ref[base + TOP_K * t + k])], sem).start()
        return 0

    lax.fori_loop(0, tb, issue, 0, unroll=DMA_ISSUE_UNROLL)

    @pl.when(pl.program_id(0) == 0)
    def _():
        zero_ref[...] = jnp.zeros_like(zero_ref)

        def fill_range(first, count):
            def copy(r):
                return pltpu.make_async_copy(zero_ref, xs_ref.at[_tile_rows(first + r)], pad_sem)

            def fill(r, _):
                copy(r).start()
                return 0

            def drain(r, _):
                copy(r).wait()
                return 0

            lax.fori_loop(0, count, fill, 0)
            lax.fori_loop(0, count, drain, 0)

        for e in range(N_EXPERTS):
            fill_range(off_ref[e] + cnt_ref[e], (-cnt_ref[e]) & (MOE_TILE - 1))
        last = N_EXPERTS - 1
        used = off_ref[last] + cnt_ref[last] + ((-cnt_ref[last]) & (MOE_TILE - 1))
        fill_range(used, xs_ref.shape[0] // ROW_CHUNKS - used)

    for _ in range(TOP_K):
        pltpu.make_async_copy(x_ref, xs_ref.at[pl.ds(0, tb * ROW_CHUNKS)], sem).wait()


def moe_dispatch(x_tiled, pos_flat, counts, offsets, n_rows):
    t = x_tiled.shape[0] // ROW_CHUNKS
    tb = min(ROUTE_BLOCK, t)
    grid_spec = pltpu.PrefetchScalarGridSpec(
        num_scalar_prefetch=3,
        grid=(t // tb,),
        in_specs=[pl.BlockSpec((tb * ROW_CHUNKS, LANES), lambda i, *_: (i, 0))],
        out_specs=pl.BlockSpec(memory_space=pl.ANY),
        scratch_shapes=[pltpu.VMEM((ROW_CHUNKS, LANES), F32), pltpu.SemaphoreType.DMA(()),
                        pltpu.SemaphoreType.DMA(())],
    )
    return pl.pallas_call(
        functools.partial(_dispatch_kernel, tb=tb),
        grid_spec=grid_spec,
        out_shape=jax.ShapeDtypeStruct((n_rows * ROW_CHUNKS, LANES), F32),
        compiler_params=_params("arbitrary"),
        name="moe_dispatch",
    )(pos_flat, counts, offsets, x_tiled)


def _moe_ffn_kernel(te_ref, nt_ref, x_ref, wg_ref, wu_ref, wd_ref, o_ref):
    i = pl.program_id(0)

    @pl.when(i < nt_ref[0])
    def _():
        xb = jnp.concatenate(
            [_load_row_tiled(x_ref, j, MOE_TILE).astype(BF16) for j in range(ROW_CHUNKS)], axis=1)
        gate = jnp.dot(xb, wg_ref[0], preferred_element_type=F32)
        up = jnp.dot(xb, wu_ref[0], preferred_element_type=F32)
        h = (_silu(gate) * up).astype(BF16)
        _store_row_tiled(o_ref, jnp.dot(h, wd_ref[0], preferred_element_type=F32))

    @pl.when(i >= nt_ref[0])
    def _():
        o_ref[...] = jnp.zeros_like(o_ref)


def moe_ffn(xs, tile_expert, n_tiles_used, wg, wu, wd):
    n_rows, d = xs.shape[0] // ROW_CHUNKS, D_MODEL
    n_tiles = n_rows // MOE_TILE
    tile_spec = lambda index: pl.BlockSpec((MOE_TILE * ROW_CHUNKS, LANES), index)
    expert = lambda w, buffers: pl.BlockSpec((1,) + w.shape[1:], lambda i, te, nt: (te[i], 0, 0),
                                             pipeline_mode=pl.Buffered(buffers))
    grid_spec = pltpu.PrefetchScalarGridSpec(
        num_scalar_prefetch=2,
        grid=(n_tiles,),
        in_specs=[tile_spec(lambda i, te, nt: (jnp.minimum(i, nt[0] - 1), 0)),
                  expert(wg, 2), expert(wu, 2), expert(wd, 1)],
        out_specs=tile_spec(lambda i, te, nt: (i, 0)),
    )
    return pl.pallas_call(
        _moe_ffn_kernel,
        grid_spec=grid_spec,
        out_shape=jax.ShapeDtypeStruct(xs.shape, F32),
        compiler_params=_params("arbitrary"),
        name="moe_ffn",
    )(tile_expert, n_tiles_used, xs, wg, wu, wd)


def _combine_kernel(pos_ref, x_ref, gate_ref, g_ref, b_ref, ys_ref, o_ref, buf_ref, sem, *, tb):
    i = pl.program_id(0)
    slot = i % 2

    def start_gathers(block, into):
        base = block * (TOP_K * tb)

        def issue(t, _):
            for k in range(TOP_K):
                pltpu.make_async_copy(ys_ref.at[_tile_rows(pos_ref[base + TOP_K * t + k])],
                                      buf_ref.at[into, k, _tile_rows(t)], sem.at[into]).start()
            return 0

        lax.fori_loop(0, tb, issue, 0, unroll=DMA_ISSUE_UNROLL)

    @pl.when(i == 0)
    def _():
        start_gathers(0, 0)

    @pl.when(i + 1 < pl.num_programs(0))
    def _():
        start_gathers(i + 1, 1 - slot)

    for k in range(TOP_K):
        pltpu.make_async_copy(ys_ref.at[pl.ds(0, tb * ROW_CHUNKS)], buf_ref.at[slot, k],
                              sem.at[slot]).wait()
    gates = gate_ref[...]
    y = jnp.concatenate(
        [buf_ref[slot, 0, pl.ds(j, tb, stride=ROW_CHUNKS), :] * gates[:, 0:1]
         + buf_ref[slot, 1, pl.ds(j, tb, stride=ROW_CHUNKS), :] * gates[:, 1:2]
         for j in range(ROW_CHUNKS)], axis=1)
    o_ref[...] = _layer_norm_rows(DN_ALPHA * x_ref[...] + y, g_ref[...], b_ref[...])


def moe_combine(x, ys, pos_flat, gates, g, b):
    t, d = x.shape
    tb = min(ROUTE_BLOCK, t)
    grid_spec = pltpu.PrefetchScalarGridSpec(
        num_scalar_prefetch=1,
        grid=(t // tb,),
        in_specs=[pl.BlockSpec((tb, d), lambda i, *_: (i, 0)),
                  pl.BlockSpec((tb, LANES), lambda i, *_: (i, 0)),
                  pl.BlockSpec((1, d), lambda i, *_: (0, 0)),
                  pl.BlockSpec((1, d), lambda i, *_: (0, 0)),
                  pl.BlockSpec(memory_space=pl.ANY)],
        out_specs=pl.BlockSpec((tb, d), lambda i, *_: (i, 0)),
        scratch_shapes=[pltpu.VMEM((2, TOP_K, tb * ROW_CHUNKS, LANES), F32),
                        pltpu.SemaphoreType.DMA((2,))],
    )
    return pl.pallas_call(
        functools.partial(_combine_kernel, tb=tb),
        grid_spec=grid_spec,
        out_shape=jax.ShapeDtypeStruct((t, d), F32),
        compiler_params=_params("arbitrary"),
        name="moe_combine",
    )(pos_flat, x, gates, g.reshape(1, d), b.reshape(1, d), ys)


def moe_deepnorm(routed, wg, wu, wd, g, b):
    x, x_tiled, idx, gates, rank, counts_f = routed
    t = x.shape[0]
    counts = counts_f[0, :N_EXPERTS].astype(jnp.int32)
    padded = (counts + MOE_TILE - 1) // MOE_TILE * MOE_TILE
    ends = jnp.cumsum(padded)
    offsets = ends - padded
    n_tiles = (TOP_K * t) // MOE_TILE + N_EXPERTS
    n_used = (ends[-1] // MOE_TILE).astype(jnp.int32)
    tile_start = jnp.arange(n_tiles, dtype=jnp.int32) * MOE_TILE
    tile_start = jnp.minimum(tile_start, ends[-1] - MOE_TILE)
    tile_expert = jnp.sum(tile_start[:, None] >= ends[None, :], axis=1).astype(jnp.int32)
    pos = moe_positions(idx, rank, offsets)
    pos_flat = pos[:, :TOP_K].reshape(-1)
    xs = moe_dispatch(x_tiled, pos_flat, counts, offsets, n_tiles * MOE_TILE)
    ys = moe_ffn(xs, tile_expert, n_used.reshape(1), wg.astype(BF16), wu.astype(BF16), wd.astype(BF16))
    return moe_combine(x, ys, pos_flat, gates, g, b)


def kernel(x, l0_ssd_w_in, l0_ssd_conv_w, l0_ssd_conv_b, l0_ssd_dt_bias, l0_ssd_a_log, l0_ssd_d_skip, l0_ssd_norm_w, l0_ssd_w_out, l0_ln_mix_g, l0_ln_mix_b, l0_ffn_w_gate, l0_ffn_w_up, l0_ffn_w_down, l0_ln_ffn_g, l0_ln_ffn_b, l1_sb_w_qkv, l1_sb_w_out, l1_ln_mix_g, l1_ln_mix_b, l1_moe_w_router, l1_moe_w_gate, l1_moe_w_up, l1_moe_w_down, l1_ln_ffn_g, l1_ln_ffn_b, l2_fox_w_qkvf, l2_fox_b_f, l2_fox_w_out, l2_ln_mix_g, l2_ln_mix_b, l2_ffn_w_gate, l2_ffn_w_up, l2_ffn_w_down, l2_ln_ffn_g, l2_ln_ffn_b, l3_ssd_w_in, l3_ssd_conv_w, l3_ssd_conv_b, l3_ssd_dt_bias, l3_ssd_a_log, l3_ssd_d_skip, l3_ssd_norm_w, l3_ssd_w_out, l3_ln_mix_g, l3_ln_mix_b, l3_moe_w_router, l3_moe_w_gate, l3_moe_w_up, l3_moe_w_down, l3_ln_ffn_g, l3_ln_ffn_b):
    batch, seq, d = x.shape
    h = x.reshape(batch * seq, d)
    bf = lambda w: w.astype(BF16)
    y = ssd_mixer(h, l0_ssd_w_in, l0_ssd_conv_w, l0_ssd_conv_b, l0_ssd_dt_bias, l0_ssd_a_log,
                  l0_ssd_d_skip, l0_ssd_norm_w, batch, seq)
    h = mixer_out_ffn_deepnorm(y, bf(l0_ssd_w_out), h, l0_ln_mix_g, l0_ln_mix_b,
                               bf(l0_ffn_w_gate), bf(l0_ffn_w_up), bf(l0_ffn_w_down), l0_ln_ffn_g, l0_ln_ffn_b)
    o = sb_mixer(h, l1_sb_w_qkv, batch, seq)
    routed = matmul_deepnorm_route(o, bf(l1_sb_w_out), h, l1_ln_mix_g, l1_ln_mix_b, l1_moe_w_router)
    h = moe_deepnorm(routed, l1_moe_w_gate, l1_moe_w_up, l1_moe_w_down, l1_ln_ffn_g, l1_ln_ffn_b)
    o = fox_mixer(h, l2_fox_w_qkvf, l2_fox_b_f, batch, seq)
    h = mixer_out_ffn_deepnorm(o, bf(l2_fox_w_out), h, l2_ln_mix_g, l2_ln_mix_b,
                               bf(l2_ffn_w_gate), bf(l2_ffn_w_up), bf(l2_ffn_w_down), l2_ln_ffn_g, l2_ln_ffn_b)
    y = ssd_mixer(h, l3_ssd_w_in, l3_ssd_conv_w, l3_ssd_conv_b, l3_ssd_dt_bias, l3_ssd_a_log,
                  l3_ssd_d_skip, l3_ssd_norm_w, batch, seq)
    routed = matmul_deepnorm_route(y, bf(l3_ssd_w_out), h, l3_ln_mix_g, l3_ln_mix_b, l3_moe_w_router)
    h = moe_deepnorm(routed, l3_moe_w_gate, l3_moe_w_up, l3_moe_w_down, l3_ln_ffn_g, l3_ln_ffn_b)
    return h.reshape(batch, seq, d)
```

```python
import functools
import math

import jax
import jax.numpy as jnp
from jax import lax
from jax.experimental import pallas as pl
from jax.experimental.pallas import tpu as pltpu

F32 = jnp.float32
BF16 = jnp.bfloat16
HIGHEST = lax.Precision.HIGHEST

LANES = 128
SUBLANES = 8
VMEM_LIMIT_BYTES = 56 * 1024 * 1024

D_MODEL = 1024
DEPTH = 4
SSD_D_INNER = 2048
SSD_HEAD_DIM = 64
SSD_HEADS = 32
SSD_GROUPS = 4
SSD_STATE = 128
SSD_CONV = 4
SSD_GROUP_DIM = SSD_D_INNER // SSD_GROUPS
SSD_BC_DIM = 2 * SSD_GROUPS * SSD_STATE
ATT_HEAD_DIM = 64
ATT_HEADS = 16
ATT_DIM = 1024
N_EXPERTS = 8
TOP_K = 2
DN_ALPHA = (2.0 * DEPTH) ** 0.25
LN_EPS = 1e-5
RMS_EPS = 1e-5
NEG_BIG = -1e30


def _params(*semantics):
    return pltpu.CompilerParams(dimension_semantics=semantics, vmem_limit_bytes=VMEM_LIMIT_BYTES)


def _layer_norm_rows(h, g, b):
    mu = jnp.mean(h, axis=-1, keepdims=True)
    d = h - mu
    var = jnp.mean(d * d, axis=-1, keepdims=True)
    return d * lax.rsqrt(var + LN_EPS) * g + b


def _silu(x):
    half = 0.5 * x
    return half * (1.0 + jnp.tanh(half))


def _mm_kernel(x_ref, w_ref, o_ref, xb_ref):
    @pl.when(pl.program_id(1) == 0)
    def _():
        xb_ref[...] = x_ref[...].astype(BF16)

    o_ref[...] = jnp.dot(xb_ref[...], w_ref[...], preferred_element_type=F32).astype(o_ref.dtype)


def matmul(x, w, out_dtype, tm=1024, tn=1024):
    m, k = x.shape
    tm = min(tm, m)
    n = w.shape[1]
    return pl.pallas_call(
        _mm_kernel,
        grid=(m // tm, n // tn),
        in_specs=[pl.BlockSpec((tm, k), lambda i, j: (i, 0)),
                  pl.BlockSpec((k, tn), lambda i, j: (0, j))],
        out_specs=pl.BlockSpec((tm, tn), lambda i, j: (i, j)),
        out_shape=jax.ShapeDtypeStruct((m, n), out_dtype),
        scratch_shapes=[pltpu.VMEM((tm, k), BF16)],
        compiler_params=_params("parallel", "arbitrary"),
        name="matmul",
    )(x, w)


def _dot_split(x, w):
    xh = x.astype(BF16)
    xl = (x - xh.astype(F32)).astype(BF16)
    wh = w.astype(BF16)
    wl = (w - wh.astype(F32)).astype(BF16)
    dot = functools.partial(jnp.dot, preferred_element_type=F32)
    return dot(xh, wh) + (dot(xh, wl) + dot(xl, wh))


def _mm_f32_kernel(x_ref, w_ref, o_ref):
    o_ref[...] = _dot_split(x_ref[...], w_ref[...])


def matmul_f32(x, w, tm=1024):
    m, k = x.shape
    tm = min(tm, m)
    n = w.shape[1]
    return pl.pallas_call(
        _mm_f32_kernel,
        grid=(m // tm,),
        in_specs=[pl.BlockSpec((tm, k), lambda i: (i, 0)),
                  pl.BlockSpec((k, n), lambda i: (0, 0))],
        out_specs=pl.BlockSpec((tm, n), lambda i: (i, 0)),
        out_shape=jax.ShapeDtypeStruct((m, n), F32),
        compiler_params=_params("parallel"),
        name="matmul_f32",
    )(x, w)


ROW_CHUNKS = D_MODEL // LANES


def _store_row_tiled(ref, value):
    n = value.shape[0]
    for j in range(ROW_CHUNKS):
        ref[pl.ds(j, n, stride=ROW_CHUNKS), :] = value[:, j * LANES:(j + 1) * LANES]


def _load_row_tiled(ref, j, n):
    return ref[pl.ds(j, n, stride=ROW_CHUNKS), :]


def _mm_ln_route_kernel(x_ref, w_ref, r_ref, g_ref, b_ref, wr_ref,
                        o_ref, tiled_ref, idx_ref, gate_ref, rank_ref, count_ref, run_ref):
    @pl.when(pl.program_id(0) == 0)
    def _():
        run_ref[...] = jnp.zeros_like(run_ref)

    y = jnp.dot(x_ref[...], w_ref[...], preferred_element_type=F32)
    o = _layer_norm_rows(DN_ALPHA * r_ref[...] + y, g_ref[...], b_ref[...])
    o_ref[...] = o
    _store_row_tiled(tiled_ref, o)
    idx, gates = _top2(_dot_split(o, wr_ref[...]))
    idx_ref[...] = idx
    gate_ref[...] = gates
    rank_ref[...] = _rank_block(idx, run_ref)
    count_ref[...] = run_ref[...]


def matmul_deepnorm_route(x, w, resid, g, b, w_router):
    m, k = x.shape
    tm = min(ROUTE_BLOCK, m)
    d = w.shape[1]
    rows = lambda width: pl.BlockSpec((tm, width), lambda i: (i, 0))
    fixed = lambda a: pl.BlockSpec(a.shape, lambda i: (0, 0))
    args = [x, w, resid, g.reshape(1, d), b.reshape(1, d),
            jnp.pad(w_router, ((0, 0), (0, LANES - N_EXPERTS)))]
    return pl.pallas_call(
        _mm_ln_route_kernel,
        grid=(m // tm,),
        in_specs=[rows(k), fixed(w), rows(d), fixed(args[3]), fixed(args[4]), fixed(args[5])],
        out_specs=[rows(d), pl.BlockSpec((tm * ROW_CHUNKS, LANES), lambda i: (i, 0)),
                   rows(LANES), rows(LANES), rows(LANES), pl.BlockSpec((1, LANES), lambda i: (0, 0))],
        out_shape=[jax.ShapeDtypeStruct((m, d), F32),
                   jax.ShapeDtypeStruct((m * ROW_CHUNKS, LANES), F32),
                   jax.ShapeDtypeStruct((m, LANES), jnp.int32),
                   jax.ShapeDtypeStruct((m, LANES), F32),
                   jax.ShapeDtypeStruct((m, LANES), jnp.int32),
                   jax.ShapeDtypeStruct((1, LANES), F32)],
        scratch_shapes=[pltpu.VMEM((1, LANES), F32)],
        compiler_params=_params("arbitrary"),
        name="matmul_deepnorm_route",
    )(*args)


def _mix_ffn_kernel(o_ref, wo_ref, r_ref, g1_ref, b1_ref, wg_ref, wu_ref, wd_ref, g2_ref, b2_ref, out_ref):
    x = _layer_norm_rows(
        DN_ALPHA * r_ref[...] + jnp.dot(o_ref[...], wo_ref[...], preferred_element_type=F32),
        g1_ref[...], b1_ref[...])
    xb = x.astype(BF16)
    gate = jnp.dot(xb, wg_ref[...], preferred_element_type=F32)
    up = jnp.dot(xb, wu_ref[...], preferred_element_type=F32)
    h = (_silu(gate) * up).astype(BF16)
    y = jnp.dot(h, wd_ref[...], preferred_element_type=F32)
    out_ref[...] = _layer_norm_rows(DN_ALPHA * x + y, g2_ref[...], b2_ref[...])


def mixer_out_ffn_deepnorm(o, w_out, resid, g1, b1, wg, wu, wd, g2, b2, tm=512):
    m, k = o.shape
    d = w_out.shape[1]
    tm = min(tm, m)
    resident = lambda a: pl.BlockSpec(a.shape, lambda i: (0, 0), pipeline_mode=pl.Buffered(1))
    vec = lambda v: v.reshape(1, d)
    args = (o, w_out, resid, vec(g1), vec(b1), wg, wu, wd, vec(g2), vec(b2))
    return pl.pallas_call(
        _mix_ffn_kernel,
        grid=(m // tm,),
        in_specs=[pl.BlockSpec((tm, k), lambda i: (i, 0)), resident(w_out),
                  pl.BlockSpec((tm, d), lambda i: (i, 0)), resident(args[3]), resident(args[4]),
                  resident(wg), resident(wu), resident(wd), resident(args[8]), resident(args[9])],
        out_specs=pl.BlockSpec((tm, d), lambda i: (i, 0)),
        out_shape=jax.ShapeDtypeStruct((m, d), F32),
        compiler_params=_params("parallel"),
        name="mixer_out_ffn_deepnorm",
    )(*args)


def _split_bf16(v, pieces):
    out = []
    r = v
    for _ in range(pieces - 1):
        p = r.astype(BF16)
        out.append(p)
        r = r - p.astype(F32)
    out.append(r.astype(BF16))
    return out


def _expand(v, e_ref, pieces):
    stacked = jnp.concatenate(_split_bf16(v, pieces), axis=1)
    return jnp.dot(stacked, e_ref[...], preferred_element_type=F32)


def _tril_f32(n, strict=False):
    r = lax.broadcasted_iota(jnp.int32, (n, n), 0)
    c = lax.broadcasted_iota(jnp.int32, (n, n), 1)
    return ((r > c) if strict else (r >= c)).astype(F32)


def _ssd_kernel(z_ref, xs_ref, bc_ref, dt_ref,
                dtb_ref, alog_ref, dskip_ref, normw_ref, e64_ref, o_ref,
                state_ref, xdt_ref, cb_ref, ydiag_ref,
                *, chunk):
    L = chunk
    c = pl.program_id(1)

    @pl.when(c == 0)
    def _():
        state_ref[...] = jnp.zeros_like(state_ref)

    xs = xs_ref[...]
    bcv = bc_ref[...]

    dt = jax.nn.softplus(dt_ref[...] + dtb_ref[...])
    da = dt * (-jnp.exp(alog_ref[...]))
    a_cs = jnp.dot(_tril_f32(L), da, preferred_element_type=F32, precision=HIGHEST)
    ea = jnp.exp(a_cs)
    dte = jnp.exp(a_cs[L - 1:L, :] - a_cs)

    spread = _expand(jnp.concatenate([dt, ea, dte], axis=0), e64_ref, 2)
    dt_x, ea_x, dte_x = spread[0:L], spread[L:2 * L], spread[2 * L:3 * L]
    a_t = a_cs.T

    xdt = xs * dt_x
    xdt_ref[...] = xdt.astype(BF16)
    xdte = (xdt * dte_x).astype(BF16)

    y_off = []
    for g in range(SSD_GROUPS):
        bm = bcv[:, g * SSD_STATE:(g + 1) * SSD_STATE]
        cm = bcv[:, (SSD_GROUPS + g) * SSD_STATE:(SSD_GROUPS + g + 1) * SSD_STATE].astype(BF16)
        cb_ref[g] = lax.dot_general(cm, bm.astype(BF16), (((1,), (1,)), ((), ())),
                                    preferred_element_type=F32)
        gs = slice(g * SSD_GROUP_DIM, (g + 1) * SSD_GROUP_DIM)
        st = state_ref[g]
        y_off.append(jnp.dot(cm, st.astype(BF16), preferred_element_type=F32) * ea_x[:, gs])
        state_ref[g] = st * ea_x[L - 1:L, gs] + jnp.dot(
            bm.T.astype(BF16), xdte[:, gs], preferred_element_type=F32)

    row = lax.broadcasted_iota(jnp.int32, (L, L), 0)
    col = lax.broadcasted_iota(jnp.int32, (L, L), 1)
    causal = row >= col
    head0 = lax.broadcasted_iota(jnp.int32, (L, LANES), 1) < SSD_HEAD_DIM

    heads_per_group = SSD_HEADS // SSD_GROUPS

    for g in range(SSD_GROUPS):
        cbg = cb_ref[g]
        scores = []
        for j in range(heads_per_group):
            h = g * heads_per_group + j
            a_l = a_cs[:, h:h + 1]
            a_s = a_t[h:h + 1, :]
            scores.append((cbg * jnp.exp(jnp.where(causal, a_l - a_s, NEG_BIG))).astype(BF16))
        for pp in range(heads_per_group // 2):
            lanes = slice((g * (heads_per_group // 2) + pp) * LANES,
                          (g * (heads_per_group // 2) + pp + 1) * LANES)
            xpair = xdt_ref[:, lanes]
            ys = [jnp.dot(scores[2 * pp + j], xpair, preferred_element_type=F32) for j in range(2)]
            ydiag_ref[:, lanes] = jnp.where(head0, ys[0], ys[1])

    y = ydiag_ref[...] + jnp.concatenate(y_off, axis=1) + dskip_ref[...] * xs
    y = y * _silu(z_ref[...])
    parts = []
    for g in range(SSD_GROUPS):
        yg = y[:, g * SSD_GROUP_DIM:(g + 1) * SSD_GROUP_DIM]
        parts.append(yg * lax.rsqrt(jnp.mean(yg * yg, axis=-1, keepdims=True) + RMS_EPS))
    o_ref[...] = (jnp.concatenate(parts, axis=1) * normw_ref[...]).astype(o_ref.dtype)


def _expansion_matrix(width, pieces):
    h = jnp.arange(LANES)[:, None]
    lane = jnp.arange(SSD_HEADS * width)[None, :]
    e = (lane // width == h).astype(BF16)
    return jnp.concatenate([e] * pieces, axis=0)


def _mm_conv_kernel(x_ref, halo_ref, w_ref, cw_ref, cb_ref, o_ref, ext_ref, xb_ref, *, tiles_per_seq):
    tm = x_ref.shape[0]

    @pl.when(pl.program_id(1) == 0)
    def _():
        xb_ref[...] = x_ref[...].astype(BF16)

    w = w_ref[...]
    halo = jnp.dot(halo_ref[...].astype(BF16), w, preferred_element_type=F32)
    first = pl.program_id(0) % tiles_per_seq == 0
    ext_ref[0:SUBLANES, :] = jnp.where(first, 0.0, halo)
    ext_ref[SUBLANES:SUBLANES + tm, :] = jnp.dot(xb_ref[...], w, preferred_element_type=F32)
    ext = ext_ref[...]
    acc = cw_ref[0:1, :] * ext
    for k in range(1, SSD_CONV):
        acc = pltpu.roll(acc, 1, axis=0) + cw_ref[k:k + 1, :] * ext
    o_ref[...] = _silu(acc[SUBLANES:SUBLANES + tm, :] + cb_ref[...])


def matmul_conv_silu(x, w, conv_w, conv_b, seq, tm=1024, tn=1024):
    m, k = x.shape
    tm = min(tm, seq)
    n = w.shape[1]
    halo_blocks = tm // SUBLANES
    return pl.pallas_call(
        functools.partial(_mm_conv_kernel, tiles_per_seq=seq // tm),
        grid=(m // tm, n // tn),
        in_specs=[pl.BlockSpec((tm, k), lambda i, j: (i, 0)),
                  pl.BlockSpec((SUBLANES, k), lambda i, j: (jnp.maximum(i * halo_blocks - 1, 0), 0)),
                  pl.BlockSpec((k, tn), lambda i, j: (0, j)),
                  pl.BlockSpec((SSD_CONV, tn), lambda i, j: (0, j)),
                  pl.BlockSpec((1, tn), lambda i, j: (0, j))],
        out_specs=pl.BlockSpec((tm, tn), lambda i, j: (i, j)),
        out_shape=jax.ShapeDtypeStruct((m, n), F32),
        scratch_shapes=[pltpu.VMEM((tm + SUBLANES, tn), F32), pltpu.VMEM((tm, k), BF16)],
        compiler_params=_params("parallel", "arbitrary"),
        name="matmul_conv_silu",
    )(x, x, w, conv_w, conv_b.reshape(1, n))


def ssd_core(z, xbc, dt_raw, dt_bias, a_log, d_skip, norm_w, batch, seq, chunk=128):
    t = z.shape[0]
    nc = seq // chunk
    pad = LANES - SSD_HEADS
    row = lambda v: v.reshape(1, -1).astype(F32)
    args = (
        z, xbc, xbc, dt_raw,
        row(jnp.pad(dt_bias, (0, pad))), row(jnp.pad(a_log, (0, pad))),
        row(jnp.repeat(d_skip, SSD_HEAD_DIM)), row(norm_w),
        _expansion_matrix(SSD_HEAD_DIM, 2),
    )
    blk = lambda b, c: (b * nc + c, 0)
    const = lambda b, c: (0, 0)
    full = lambda a: pl.BlockSpec(a.shape, const)
    in_specs = [
        pl.BlockSpec((chunk, SSD_D_INNER), blk),
        pl.BlockSpec((chunk, SSD_D_INNER), blk),
        pl.BlockSpec((chunk, SSD_BC_DIM), lambda b, c: (b * nc + c, 2)),
        pl.BlockSpec((chunk, LANES), blk),
    ] + [full(a) for a in args[4:]]
    return pl.pallas_call(
        functools.partial(_ssd_kernel, chunk=chunk),
        grid=(batch, nc),
        in_specs=in_specs,
        out_specs=pl.BlockSpec((chunk, SSD_D_INNER), blk),
        out_shape=jax.ShapeDtypeStruct((t, SSD_D_INNER), BF16),
        scratch_shapes=[
            pltpu.VMEM((SSD_GROUPS, SSD_STATE, SSD_GROUP_DIM), F32),
            pltpu.VMEM((chunk, SSD_D_INNER), BF16),
            pltpu.VMEM((SSD_GROUPS, chunk, chunk), F32),
            pltpu.VMEM((chunk, SSD_D_INNER), F32),
        ],
        compiler_params=_params("parallel", "arbitrary"),
        name="ssd_core",
    )(*args)


def ssd_mixer(x, w_in, conv_w, conv_b, dt_bias, a_log, d_skip, norm_w, batch, seq):
    n_zx = SSD_D_INNER + SSD_D_INNER + SSD_BC_DIM
    z = matmul(x, w_in[:, :SSD_D_INNER].astype(BF16), F32)
    xbc = matmul_conv_silu(x, w_in[:, SSD_D_INNER:n_zx].astype(BF16), conv_w, conv_b, seq)
    w_dt = jnp.pad(w_in[:, n_zx:], ((0, 0), (0, LANES - SSD_HEADS)))
    dt_raw = matmul_f32(x, w_dt)
    return ssd_core(z, xbc, dt_raw, dt_bias, a_log, d_skip, norm_w, batch, seq)


ATT_BLOCK = 128
ATT_Q = 256
ATT_PAIRS = 4
ATT_STEP_LANES = ATT_PAIRS * LANES
ATT_SCALE = ATT_HEAD_DIM ** -0.5
LOG2E = 1.4426950408889634
EXP2_UNDERFLOW = -160.0
FOX_BOUND_SLACK = 1.05


def _split_heads(q_ref, rows, head0):
    qs = []
    for p in range(ATT_PAIRS):
        q2 = q_ref[rows, p * LANES:(p + 1) * LANES]
        zero = jnp.zeros_like(q2)
        qs += [jnp.where(head0, q2, zero), jnp.where(head0, zero, q2)]
    return qs


def _sb_kernel(q_ref, k_ref, v_ref, u_ref, o_ref, *scratch, seq):
    n_heads = 2 * ATT_PAIRS
    later_ref, acc_ref, z_ref = scratch[:n_heads], scratch[n_heads:2 * n_heads], scratch[2 * n_heads:]
    nq = seq // ATT_Q
    ndiag = ATT_Q // ATT_BLOCK
    head0 = lax.broadcasted_iota(jnp.int32, (ATT_Q, LANES), 1) < ATT_HEAD_DIM
    row = lax.broadcasted_iota(jnp.int32, (ATT_Q, ATT_BLOCK), 0)
    col = lax.broadcasted_iota(jnp.int32, (ATT_Q, ATT_BLOCK), 1)

    def q_body(i, _):
        q0 = pl.multiple_of(i * ATT_Q, ATT_Q)
        rows = pl.ds(q0, ATT_Q)
        qs = _split_heads(q_ref, rows, head0)
        for n in range(2 * ATT_PAIRS):
            later_ref[n][...] = jnp.zeros((ATT_Q, LANES), F32)
            acc_ref[n][...] = jnp.zeros((ATT_Q, LANES), F32)

        heads = range(2 * ATT_PAIRS)

        def key_rows(j):
            return pl.ds(pl.multiple_of(j * ATT_BLOCK, ATT_BLOCK), ATT_BLOCK)

        def scores(j):
            k2 = [k_ref[key_rows(j), p * LANES:(p + 1) * LANES] for p in range(ATT_PAIRS)]
            return [lax.dot_general(qs[n], k2[n // 2], (((1,), (1,)), ((), ())),
                                    preferred_element_type=F32) * (ATT_SCALE * LOG2E) for n in heads]

        def consume(z2, j, strict):
            keys = key_rows(j)
            v2 = [v_ref[keys, p * LANES:(p + 1) * LANES] for p in range(ATT_PAIRS)]
            log_beta, sums = [], []
            for n in heads:
                lb = jnp.minimum(z2[n], 0.0) - jnp.log2(1.0 + jnp.exp2(-jnp.abs(z2[n])))
                log_keep = lb - z2[n]
                if strict is not None:
                    log_keep = jnp.where(strict, log_keep, 0.0)
                hi = log_keep.astype(BF16)
                lo = (log_keep - hi.astype(F32)).astype(BF16)
                log_beta.append(lb)
                sums.append(jnp.dot(jnp.concatenate([hi, lo], axis=1), u_ref[...],
                                    preferred_element_type=F32))
            for n in heads:
                w = jnp.exp2(log_beta[n] + sums[n][:, :ATT_BLOCK] + later_ref[n][...])
                if strict is not None:
                    w = jnp.where(strict, w, 0.0)
                acc_ref[n][...] += jnp.dot(w.astype(BF16), v2[n // 2], preferred_element_type=F32)
                later_ref[n][...] += sums[n][:, ATT_BLOCK:]

        n_off = i * ndiag
        diag = [(d, scores(n_off + d)) for d in reversed(range(ndiag))]
        first_off = scores(jnp.maximum(n_off - 1, 0))
        for d, zd in diag:
            consume(zd, n_off + d, d * ATT_BLOCK + col < row)

        for n, zn in enumerate(first_off):
            z_ref[n][...] = zn

        def off_diagonal(carry):
            jj, _ = carry
            z2 = [z_ref[n][...] for n in heads]
            ahead = scores(jnp.maximum(n_off - 2 - jj, 0))
            consume(z2, n_off - 1 - jj, None)
            for n in heads:
                z_ref[n][...] = ahead[n]
            top = later_ref[0][...]
            for n in range(1, 2 * ATT_PAIRS):
                top = jnp.maximum(top, later_ref[n][...])
            return jj + 1, jnp.max(top) > EXP2_UNDERFLOW

        lax.while_loop(lambda c: (c[0] < n_off) & c[1], off_diagonal, (jnp.int32(0), n_off > 0))
        for p in range(ATT_PAIRS):
            o_ref[rows, p * LANES:(p + 1) * LANES] = jnp.where(
                head0, acc_ref[2 * p][...], acc_ref[2 * p + 1][...]).astype(o_ref.dtype)
        return 0

    lax.fori_loop(0, nq, q_body, 0)


def _suffix_sum_matrix():
    j = jnp.arange(2 * ATT_BLOCK)[:, None] % ATT_BLOCK
    s = jnp.arange(2 * ATT_BLOCK)[None, :]
    return jnp.where(s < ATT_BLOCK, j > s, True).astype(BF16)


def _attention_specs(seq):
    n_steps = ATT_HEADS // 2 // ATT_PAIRS
    blk = lambda off: pl.BlockSpec((seq, ATT_STEP_LANES), lambda b, p: (b, off + p))
    specs = [blk(0), blk(n_steps), blk(2 * n_steps)]
    return n_steps, specs, pl.BlockSpec((seq, ATT_STEP_LANES), lambda b, p: (b, p))


def sb_attention(qkv, batch, seq):
    n_steps, in_specs, out_spec = _attention_specs(seq)
    u = _suffix_sum_matrix()
    return pl.pallas_call(
        functools.partial(_sb_kernel, seq=seq),
        grid=(batch, n_steps),
        in_specs=in_specs + [pl.BlockSpec(u.shape, lambda b, p: (0, 0))],
        out_specs=out_spec,
        out_shape=jax.ShapeDtypeStruct((batch * seq, ATT_DIM), BF16),
        scratch_shapes=[pltpu.VMEM((ATT_Q, LANES), F32)] * (6 * ATT_PAIRS),
        compiler_params=_params("parallel", "parallel"),
        name="sb_attention",
    )(qkv, qkv, qkv, u)


def sb_mixer(x, w_qkv, batch, seq):
    return sb_attention(matmul(x, w_qkv.astype(BF16), BF16), batch, seq)


CUMSUM_BLOCK = 256


def _fox_decay_kernel(f_ref, bf_ref, ccol_ref, crow_ref, *, seq):
    tri = _tril_f32(CUMSUM_BLOCK)
    carry = jnp.zeros((1, LANES), F32)
    for blk in range(seq // CUMSUM_BLOCK):
        rows = slice(blk * CUMSUM_BLOCK, (blk + 1) * CUMSUM_BLOCK)
        log_f = jax.nn.log_sigmoid(f_ref[rows, :] + bf_ref[...])
        c = jnp.dot(tri, log_f, preferred_element_type=F32, precision=HIGHEST) + carry
        carry = c[CUMSUM_BLOCK - 1:CUMSUM_BLOCK, :]
        ccol_ref[rows, :] = c
        c_t = c.T
        for p in range(ATT_HEADS // 2):
            crow_ref[0, p, :, rows] = c_t[2 * p:2 * p + 2, :]


def fox_decay(f_raw, b_f, batch, seq):
    return pl.pallas_call(
        functools.partial(_fox_decay_kernel, seq=seq),
        grid=(batch,),
        in_specs=[pl.BlockSpec((seq, LANES), lambda b: (b, 0)),
                  pl.BlockSpec((1, LANES), lambda b: (0, 0))],
        out_specs=[pl.BlockSpec((seq, LANES), lambda b: (b, 0)),
                   pl.BlockSpec((1, ATT_HEADS // 2, 2, seq), lambda b: (b, 0, 0, 0))],
        out_shape=[jax.ShapeDtypeStruct((batch * seq, LANES), F32),
                   jax.ShapeDtypeStruct((batch, ATT_HEADS // 2, 2, seq), F32)],
        compiler_params=_params("parallel"),
        name="fox_decay",
    )(f_raw, jnp.pad(b_f, (0, LANES - ATT_HEADS)).reshape(1, LANES))


def _fox_kernel(q_ref, k_ref, v_ref, ccol_ref, crow_ref, o_ref, crep_ref, vt_ref, *scratch, seq):
    n_heads = 2 * ATT_PAIRS
    m_ref, acc_ref, s_ref = scratch[:n_heads], scratch[n_heads:2 * n_heads], scratch[2 * n_heads:]
    nq = seq // ATT_Q
    ndiag = ATT_Q // ATT_BLOCK
    step = pl.program_id(1)
    heads = range(n_heads)
    head0 = lax.broadcasted_iota(jnp.int32, (ATT_Q, LANES), 1) < ATT_HEAD_DIM
    head0_rows = lax.broadcasted_iota(jnp.int32, (LANES, ATT_BLOCK), 0) < ATT_HEAD_DIM
    key = lax.broadcasted_iota(jnp.int32, (ATT_BLOCK, ATT_Q), 0)
    qry = lax.broadcasted_iota(jnp.int32, (ATT_BLOCK, ATT_Q), 1)

    pieces = jnp.concatenate(_split_bf16(ccol_ref[...], 3), axis=1)
    sel_row = lax.broadcasted_iota(jnp.int32, (3 * LANES, LANES), 0) & (LANES - 1)
    for n in heads:
        sel = jnp.where(sel_row == 2 * ATT_PAIRS * step + n, 1.0, 0.0).astype(BF16)
        crep_ref[n] = jnp.dot(pieces, sel, preferred_element_type=F32) * LOG2E
    for p in range(ATT_PAIRS):
        for blk in range(seq // ATT_BLOCK):
            rows = slice(blk * ATT_BLOCK, (blk + 1) * ATT_BLOCK)
            vt_ref[p, :, rows] = v_ref[rows, p * LANES:(p + 1) * LANES].astype(F32).T.astype(BF16)
    lane_r = lax.broadcasted_iota(jnp.int32, (LANES, LANES), 0) < ATT_HEAD_DIM
    lane_c = lax.broadcasted_iota(jnp.int32, (LANES, LANES), 1) < ATT_HEAD_DIM
    same_head = jnp.where(lane_r == lane_c, 1.0, 0.0).astype(BF16)
    k_norm = []
    for p in range(ATT_PAIRS):
        kf = k_ref[:, p * LANES:(p + 1) * LANES].astype(F32)
        sq = jnp.dot((kf * kf).astype(BF16), same_head, preferred_element_type=F32)
        top = jnp.sqrt(jnp.max(sq, axis=0, keepdims=True))
        k_norm += [top[:, 0:1], top[:, ATT_HEAD_DIM:ATT_HEAD_DIM + 1]]
    ones_rows = jnp.ones((SUBLANES, LANES), BF16)

    def q_body(i, _):
        q0 = pl.multiple_of(i * ATT_Q, ATT_Q)
        rows = pl.ds(q0, ATT_Q)
        qs = _split_heads(q_ref, rows, head0)
        c_q = [crow_ref[0, n // 2, n % 2:n % 2 + 1, rows] * LOG2E for n in heads]
        qk_bound = []
        for n in heads:
            qf = qs[n].astype(F32)
            q_sq = lax.dot_general(ones_rows, (qf * qf).astype(BF16), (((1,), (1,)), ((), ())),
                                   preferred_element_type=F32)[0:1, :]
            qk_bound.append(jnp.sqrt(q_sq) * k_norm[n] * (ATT_SCALE * LOG2E * FOX_BOUND_SLACK))
        for n in heads:
            m_ref[n][...] = jnp.full((1, ATT_Q), NEG_BIG, F32)
            acc_ref[n][...] = jnp.zeros((LANES, ATT_Q), F32)

        def key_rows(j):
            return pl.ds(pl.multiple_of(j * ATT_BLOCK, ATT_BLOCK), ATT_BLOCK)

        def scores(j):
            k2 = [k_ref[key_rows(j), p * LANES:(p + 1) * LANES] for p in range(ATT_PAIRS)]
            return [lax.dot_general(k2[n // 2], qs[n], (((1,), (1,)), ((), ())),
                                    preferred_element_type=F32) * (ATT_SCALE * LOG2E) for n in heads]

        def consume(s, j, causal):
            keys = key_rows(j)
            vt = [vt_ref[p, :, keys] for p in range(ATT_PAIRS)]
            for n in heads:
                c_k = crep_ref[n, keys, :]
                sn = s[n] + (c_q[n] - jnp.concatenate([c_k] * (ATT_Q // LANES), axis=1))
                if causal is not None:
                    sn = jnp.where(causal, sn, NEG_BIG)
                m_old = m_ref[n][...]
                m_new = jnp.maximum(m_old, jnp.max(sn, axis=0, keepdims=True))
                prob = jnp.exp2(sn - m_new).astype(BF16)
                one = jnp.ones_like(vt[n // 2])
                v1t = jnp.where(head0_rows, vt[n // 2], one) if n % 2 == 0 else jnp.where(
                    head0_rows, one, vt[n // 2])
                acc_ref[n][...] = jnp.exp2(m_old - m_new) * acc_ref[n][...] + jnp.dot(
                    v1t, prob, preferred_element_type=F32)
                m_ref[n][...] = m_new

        n_off = i * ndiag
        diag = [(d, scores(n_off + d)) for d in range(ndiag)]
        first_off = scores(jnp.maximum(n_off - 1, 0))
        for d, sd in diag:
            consume(sd, n_off + d, d * ATT_BLOCK + key <= qry)

        for n, sn in enumerate(first_off):
            s_ref[n][...] = sn

        def off_diagonal(carry):
            jj, _ = carry
            j = n_off - 1 - jj
            s = [s_ref[n][...] for n in heads]
            ahead = scores(jnp.maximum(j - 1, 0))
            consume(s, j, None)
            for n in heads:
                s_ref[n][...] = ahead[n]
            prev_keys = key_rows(jnp.maximum(j - 1, 0))
            worst = None
            for n in heads:
                c_last = crow_ref[0, n // 2, n % 2:n % 2 + 1, prev_keys][:, LANES - 1:LANES] * LOG2E
                gap = qk_bound[n] + c_q[n] - c_last - m_ref[n][...]
                worst = gap if worst is None else jnp.maximum(worst, gap)
            return jj + 1, jnp.max(worst) > EXP2_UNDERFLOW

        lax.while_loop(lambda c: (c[0] < n_off) & c[1], off_diagonal, (jnp.int32(0), n_off > 0))
        for p in range(ATT_PAIRS):
            a0, a1 = acc_ref[2 * p][...], acc_ref[2 * p + 1][...]
            o_t = jnp.concatenate([a0[:ATT_HEAD_DIM] / a0[ATT_HEAD_DIM:],
                                   a1[ATT_HEAD_DIM:] / a1[:ATT_HEAD_DIM]], axis=0)
            o_ref[rows, p * LANES:(p + 1) * LANES] = o_t.T.astype(o_ref.dtype)
        return 0

    lax.fori_loop(0, nq, q_body, 0)


def fox_attention(qkv, ccol, crow, batch, seq):
    n_steps, in_specs, out_spec = _attention_specs(seq)
    in_specs += [pl.BlockSpec((seq, LANES), lambda b, p: (b, 0)),
                 pl.BlockSpec((1, ATT_PAIRS, 2, seq), lambda b, p: (b, p, 0, 0))]
    return pl.pallas_call(
        functools.partial(_fox_kernel, seq=seq),
        grid=(batch, n_steps),
        in_specs=in_specs,
        out_specs=out_spec,
        out_shape=jax.ShapeDtypeStruct((batch * seq, ATT_DIM), BF16),
        scratch_shapes=([pltpu.VMEM((2 * ATT_PAIRS, seq, LANES), F32),
                         pltpu.VMEM((ATT_PAIRS, LANES, seq), BF16)]
                        + [pltpu.VMEM((1, ATT_Q), F32)] * (2 * ATT_PAIRS)
                        + [pltpu.VMEM((LANES, ATT_Q), F32)] * (2 * ATT_PAIRS)
                        + [pltpu.VMEM((ATT_BLOCK, ATT_Q), F32)] * (2 * ATT_PAIRS)),
        compiler_params=_params("parallel", "parallel"),
        name="fox_attention",
    )(qkv, qkv, qkv, ccol, crow)


def fox_mixer(x, w_qkvf, b_f, batch, seq):
    qkv = matmul(x, w_qkvf[:, :3 * ATT_DIM].astype(BF16), BF16)
    w_f = jnp.pad(w_qkvf[:, 3 * ATT_DIM:], ((0, 0), (0, LANES - ATT_HEADS)))
    ccol, crow = fox_decay(matmul_f32(x, w_f), b_f, batch, seq)
    return fox_attention(qkv, ccol, crow, batch, seq)


MOE_TILE = 512
ROUTE_BLOCK = 512


def _top2(logits):
    lane = lax.broadcasted_iota(jnp.int32, logits.shape, 1)
    logits = jnp.where(lane < N_EXPERTS, logits, NEG_BIG)
    m1 = jnp.max(logits, axis=-1, keepdims=True)
    i1 = jnp.min(jnp.where(logits == m1, lane, LANES), axis=-1, keepdims=True)
    rest = jnp.where(lane == i1, NEG_BIG, logits)
    m2 = jnp.max(rest, axis=-1, keepdims=True)
    i2 = jnp.min(jnp.where(rest == m2, lane, LANES), axis=-1, keepdims=True)
    e2 = jnp.exp(m2 - m1)
    denom = 1.0 + e2
    idx = jnp.where(lane == 0, i1, jnp.where(lane == 1, i2, 0))
    gates = jnp.where(lane == 0, 1.0 / denom, jnp.where(lane == 1, e2 / denom, 0.0))
    return idx, gates


def _rank_block(idx, run_ref):
    tb = idx.shape[0]
    lane = lax.broadcasted_iota(jnp.int32, idx.shape, 1)
    oh0 = lane == idx[:, 0:1]
    oh1 = lane == idx[:, 1:2]
    both = jnp.where(oh0 | oh1, 1.0, 0.0)
    before = jnp.dot(_tril_f32(tb, strict=True).astype(BF16), both.astype(BF16),
                     preferred_element_type=F32) + run_ref[...]
    r0 = jnp.sum(jnp.where(oh0, before, 0.0), axis=-1, keepdims=True)
    r1 = jnp.sum(jnp.where(oh1, before, 0.0), axis=-1, keepdims=True)
    run_ref[...] += jnp.sum(both, axis=0, keepdims=True)
    return jnp.where(lane == 0, r0, jnp.where(lane == 1, r1, 0.0)).astype(jnp.int32)


def _pos_kernel(idx_ref, rank_ref, off_ref, pos_ref):
    idx = idx_ref[...]
    lane = lax.broadcasted_iota(jnp.int32, idx.shape, 1)
    off = off_ref[...]
    p0 = jnp.sum(jnp.where(lane == idx[:, 0:1], off, 0), axis=-1, keepdims=True)
    p1 = jnp.sum(jnp.where(lane == idx[:, 1:2], off, 0), axis=-1, keepdims=True)
    pos_ref[...] = rank_ref[...] + jnp.where(lane == 0, p0, jnp.where(lane == 1, p1, 0))


def moe_positions(idx, rank, offsets):
    t = idx.shape[0]
    tb = min(4 * ROUTE_BLOCK, t)
    off = jnp.pad(offsets, (0, LANES - N_EXPERTS)).reshape(1, LANES)
    blk = pl.BlockSpec((tb, LANES), lambda i: (i, 0))
    return pl.pallas_call(
        _pos_kernel,
        grid=(t // tb,),
        in_specs=[blk, blk, pl.BlockSpec((1, LANES), lambda i: (0, 0))],
        out_specs=blk,
        out_shape=jax.ShapeDtypeStruct((t, LANES), jnp.int32),
        compiler_params=_params("parallel"),
        name="moe_positions",
    )(idx, rank, off)


DMA_ISSUE_UNROLL = 8


def _tile_rows(row):
    return pl.ds(pl.multiple_of(row * ROW_CHUNKS, ROW_CHUNKS), ROW_CHUNKS)


def _dispatch_kernel(pos_ref, cnt_ref, off_ref, x_ref, xs_ref, zero_ref, sem, pad_sem, *, tb):
    i = pl.program_id(0)
    base = i * (TOP_K * tb)

    def issue(t, _):
        src = x_ref.at[_tile_rows(i * tb + t)]
        for k in range(TOP_K):
            pltpu.make_async_copy(src, xs_ref.at[_tile_rows(pos_ref[base + TOP_K * t + k])], sem).start()
        return 0

    lax.fori_loop(0, tb, issue, 0, unroll=DMA_ISSUE_UNROLL)

    @pl.when(pl.program_id(0) == 0)
    def _():
        zero_ref[...] = jnp.zeros_like(zero_ref)

        def fill_range(first, count):
            def copy(r):
                return pltpu.make_async_copy(zero_ref, xs_ref.at[_tile_rows(first + r)], pad_sem)

            def fill(r, _):
                copy(r).start()
                return 0

            def drain(r, _):
                copy(r).wait()
                return 0

            lax.fori_loop(0, count, fill, 0)
            lax.fori_loop(0, count, drain, 0)

        for e in range(N_EXPERTS):
            fill_range(off_ref[e] + cnt_ref[e], (-cnt_ref[e]) & (MOE_TILE - 1))
        last = N_EXPERTS - 1
        used = off_ref[last] + cnt_ref[last] + ((-cnt_ref[last]) & (MOE_TILE - 1))
        fill_range(used, xs_ref.shape[0] // ROW_CHUNKS - used)

    def wait_block():
        block = pl.ds(0, tb * ROW_CHUNKS)
        for _ in range(TOP_K):
            pltpu.make_async_copy(x_ref.at[block], xs_ref.at[block], sem).wait()

    @pl.when(i > 0)
    def _():
        wait_block()

    @pl.when(i == pl.num_programs(0) - 1)
    def _():
        wait_block()


def moe_dispatch(x_tiled, pos_flat, counts, offsets, n_rows):
    t = x_tiled.shape[0] // ROW_CHUNKS
    tb = min(ROUTE_BLOCK, t)
    grid_spec = pltpu.PrefetchScalarGridSpec(
        num_scalar_prefetch=3,
        grid=(t // tb,),
        in_specs=[pl.BlockSpec(memory_space=pl.ANY)],
        out_specs=pl.BlockSpec(memory_space=pl.ANY),
        scratch_shapes=[pltpu.VMEM((ROW_CHUNKS, LANES), F32), pltpu.SemaphoreType.DMA(()),
                        pltpu.SemaphoreType.DMA(())],
    )
    return pl.pallas_call(
        functools.partial(_dispatch_kernel, tb=tb),
        grid_spec=grid_spec,
        out_shape=jax.ShapeDtypeStruct((n_rows * ROW_CHUNKS, LANES), F32),
        compiler_params=_params("arbitrary"),
        name="moe_dispatch",
    )(pos_flat, counts, offsets, x_tiled)


def _moe_ffn_kernel(te_ref, nt_ref, x_ref, wg_ref, wu_ref, wd_ref, o_ref):
    i = pl.program_id(0)

    @pl.when(i < nt_ref[0])
    def _():
        xb = jnp.concatenate(
            [_load_row_tiled(x_ref, j, MOE_TILE).astype(BF16) for j in range(ROW_CHUNKS)], axis=1)
        gate = jnp.dot(xb, wg_ref[0], preferred_element_type=F32)
        up = jnp.dot(xb, wu_ref[0], preferred_element_type=F32)
        h = (_silu(gate) * up).astype(BF16)
        _store_row_tiled(o_ref, jnp.dot(h, wd_ref[0], preferred_element_type=F32))

    @pl.when(i >= nt_ref[0])
    def _():
        o_ref[...] = jnp.zeros_like(o_ref)


def moe_ffn(xs, tile_expert, n_tiles_used, wg, wu, wd):
    n_rows, d = xs.shape[0] // ROW_CHUNKS, D_MODEL
    n_tiles = n_rows // MOE_TILE
    tile_spec = lambda index: pl.BlockSpec((MOE_TILE * ROW_CHUNKS, LANES), index)
    expert = lambda w, buffers: pl.BlockSpec((1,) + w.shape[1:], lambda i, te, nt: (te[i], 0, 0),
                                             pipeline_mode=pl.Buffered(buffers))
    grid_spec = pltpu.PrefetchScalarGridSpec(
        num_scalar_prefetch=2,
        grid=(n_tiles,),
        in_specs=[tile_spec(lambda i, te, nt: (jnp.minimum(i, nt[0] - 1), 0)),
                  expert(wg, 2), expert(wu, 2), expert(wd, 1)],
        out_specs=tile_spec(lambda i, te, nt: (i, 0)),
    )
    return pl.pallas_call(
        _moe_ffn_kernel,
        grid_spec=grid_spec,
        out_shape=jax.ShapeDtypeStruct(xs.shape, F32),
        compiler_params=_params("arbitrary"),
        name="moe_ffn",
    )(tile_expert, n_tiles_used, xs, wg, wu, wd)


def _combine_kernel(pos_ref, x_ref, gate_ref, g_ref, b_ref, ys_ref, o_ref, buf_ref, sem, *, tb):
    i = pl.program_id(0)
    slot = i % 2

    def start_gathers(block, into):
        base = block * (TOP_K * tb)

        def issue(t, _):
            for k in range(TOP_K):
                pltpu.make_async_copy(ys_ref.at[_tile_rows(pos_ref[base + TOP_K * t + k])],
                                      buf_ref.at[into, k, _tile_rows(t)], sem.at[into]).start()
            return 0

        lax.fori_loop(0, tb, issue, 0, unroll=DMA_ISSUE_UNROLL)

    @pl.when(i == 0)
    def _():
        start_gathers(0, 0)

    @pl.when(i + 1 < pl.num_programs(0))
    def _():
        start_gathers(i + 1, 1 - slot)

    for k in range(TOP_K):
        pltpu.make_async_copy(ys_ref.at[pl.ds(0, tb * ROW_CHUNKS)], buf_ref.at[slot, k],
                              sem.at[slot]).wait()
    gates = gate_ref[...]
    y = jnp.concatenate(
        [buf_ref[slot, 0, pl.ds(j, tb, stride=ROW_CHUNKS), :] * gates[:, 0:1]
         + buf_ref[slot, 1, pl.ds(j, tb, stride=ROW_CHUNKS), :] * gates[:, 1:2]
         for j in range(ROW_CHUNKS)], axis=1)
    o_ref[...] = _layer_norm_rows(DN_ALPHA * x_ref[...] + y, g_ref[...], b_ref[...])


def moe_combine(x, ys, pos_flat, gates, g, b):
    t, d = x.shape
    tb = min(ROUTE_BLOCK, t)
    grid_spec = pltpu.PrefetchScalarGridSpec(
        num_scalar_prefetch=1,
        grid=(t // tb,),
        in_specs=[pl.BlockSpec((tb, d), lambda i, *_: (i, 0)),
                  pl.BlockSpec((tb, LANES), lambda i, *_: (i, 0)),
                  pl.BlockSpec((1, d), lambda i, *_: (0, 0)),
                  pl.BlockSpec((1, d), lambda i, *_: (0, 0)),
                  pl.BlockSpec(memory_space=pl.ANY)],
        out_specs=pl.BlockSpec((tb, d), lambda i, *_: (i, 0)),
        scratch_shapes=[pltpu.VMEM((2, TOP_K, tb * ROW_CHUNKS, LANES), F32),
                        pltpu.SemaphoreType.DMA((2,))],
    )
    return pl.pallas_call(
        functools.partial(_combine_kernel, tb=tb),
        grid_spec=grid_spec,
        out_shape=jax.ShapeDtypeStruct((t, d), F32),
        compiler_params=_params("arbitrary"),
        name="moe_combine",
    )(pos_flat, x, gates, g.reshape(1, d), b.reshape(1, d), ys)


def moe_deepnorm(routed, wg, wu, wd, g, b):
    x, x_tiled, idx, gates, rank, counts_f = routed
    t = x.shape[0]
    counts = counts_f[0, :N_EXPERTS].astype(jnp.int32)
    padded = (counts + MOE_TILE - 1) // MOE_TILE * MOE_TILE
    ends = jnp.cumsum(padded)
    offsets = ends - padded
    n_tiles = (TOP_K * t) // MOE_TILE + N_EXPERTS
    n_used = (ends[-1] // MOE_TILE).astype(jnp.int32)
    tile_start = jnp.arange(n_tiles, dtype=jnp.int32) * MOE_TILE
    tile_start = jnp.minimum(tile_start, ends[-1] - MOE_TILE)
    tile_expert = jnp.sum(tile_start[:, None] >= ends[None, :], axis=1).astype(jnp.int32)
    pos = moe_positions(idx, rank, offsets)
    pos_flat = pos[:, :TOP_K].reshape(-1)
    xs = moe_dispatch(x_tiled, pos_flat, counts, offsets, n_tiles * MOE_TILE)
    ys = moe_ffn(xs, tile_expert, n_used.reshape(1), wg.astype(BF16), wu.astype(BF16), wd.astype(BF16))
    return moe_combine(x, ys, pos_flat, gates, g, b)


def kernel(x, l0_ssd_w_in, l0_ssd_conv_w, l0_ssd_conv_b, l0_ssd_dt_bias, l0_ssd_a_log, l0_ssd_d_skip, l0_ssd_norm_w, l0_ssd_w_out, l0_ln_mix_g, l0_ln_mix_b, l0_ffn_w_gate, l0_ffn_w_up, l0_ffn_w_down, l0_ln_ffn_g, l0_ln_ffn_b, l1_sb_w_qkv, l1_sb_w_out, l1_ln_mix_g, l1_ln_mix_b, l1_moe_w_router, l1_moe_w_gate, l1_moe_w_up, l1_moe_w_down, l1_ln_ffn_g, l1_ln_ffn_b, l2_fox_w_qkvf, l2_fox_b_f, l2_fox_w_out, l2_ln_mix_g, l2_ln_mix_b, l2_ffn_w_gate, l2_ffn_w_up, l2_ffn_w_down, l2_ln_ffn_g, l2_ln_ffn_b, l3_ssd_w_in, l3_ssd_conv_w, l3_ssd_conv_b, l3_ssd_dt_bias, l3_ssd_a_log, l3_ssd_d_skip, l3_ssd_norm_w, l3_ssd_w_out, l3_ln_mix_g, l3_ln_mix_b, l3_moe_w_router, l3_moe_w_gate, l3_moe_w_up, l3_moe_w_down, l3_ln_ffn_g, l3_ln_ffn_b):
    batch, seq, d = x.shape
    h = x.reshape(batch * seq, d)
    bf = lambda w: w.astype(BF16)
    y = ssd_mixer(h, l0_ssd_w_in, l0_ssd_conv_w, l0_ssd_conv_b, l0_ssd_dt_bias, l0_ssd_a_log,
                  l0_ssd_d_skip, l0_ssd_norm_w, batch, seq)
    h = mixer_out_ffn_deepnorm(y, bf(l0_ssd_w_out), h, l0_ln_mix_g, l0_ln_mix_b,
                               bf(l0_ffn_w_gate), bf(l0_ffn_w_up), bf(l0_ffn_w_down), l0_ln_ffn_g, l0_ln_ffn_b)
    o = sb_mixer(h, l1_sb_w_qkv, batch, seq)
    routed = matmul_deepnorm_route(o, bf(l1_sb_w_out), h, l1_ln_mix_g, l1_ln_mix_b, l1_moe_w_router)
    h = moe_deepnorm(routed, l1_moe_w_gate, l1_moe_w_up, l1_moe_w_down, l1_ln_ffn_g, l1_ln_ffn_b)
    o = fox_mixer(h, l2_fox_w_qkvf, l2_fox_b_f, batch, seq)
    h = mixer_out_ffn_deepnorm(o, bf(l2_fox_w_out), h, l2_ln_mix_g, l2_ln_mix_b,
                               bf(l2_ffn_w_gate), bf(l2_ffn_w_up), bf(l2_ffn_w_down), l2_ln_ffn_g, l2_ln_ffn_b)
    y = ssd_mixer(h, l3_ssd_w_in, l3_ssd_conv_w, l3_ssd_conv_b, l3_ssd_dt_bias, l3_ssd_a_log,
                  l3_ssd_d_skip, l3_ssd_norm_w, batch, seq)
    routed = matmul_deepnorm_route(y, bf(l3_ssd_w_out), h, l3_ln_mix_g, l3_ln_mix_b, l3_moe_w_router)
    h = moe_deepnorm(routed, l3_moe_w_gate, l3_moe_w_up, l3_moe_w_down, l3_ln_ffn_g, l3_ln_ffn_b)
    return h.reshape(batch, seq, d)
```

```python
import functools
import math

import jax
import jax.numpy as jnp
from jax import lax
from jax.experimental import pallas as pl
from jax.experimental.pallas import tpu as pltpu

F32 = jnp.float32
BF16 = jnp.bfloat16
HIGHEST = lax.Precision.HIGHEST

LANES = 128
SUBLANES = 8
VMEM_LIMIT_BYTES = 56 * 1024 * 1024

D_MODEL = 1024
DEPTH = 4
SSD_D_INNER = 2048
SSD_HEAD_DIM = 64
SSD_HEADS = 32
SSD_GROUPS = 4
SSD_STATE = 128
SSD_CONV = 4
SSD_GROUP_DIM = SSD_D_INNER // SSD_GROUPS
SSD_BC_DIM = 2 * SSD_GROUPS * SSD_STATE
ATT_HEAD_DIM = 64
ATT_HEADS = 16
ATT_DIM = 1024
N_EXPERTS = 8
TOP_K = 2
DN_ALPHA = (2.0 * DEPTH) ** 0.25
LN_EPS = 1e-5
RMS_EPS = 1e-5
NEG_BIG = -1e30


def _params(*semantics):
    return pltpu.CompilerParams(dimension_semantics=semantics, vmem_limit_bytes=VMEM_LIMIT_BYTES)


def _layer_norm_rows(h, g, b):
    mu = jnp.mean(h, axis=-1, keepdims=True)
    d = h - mu
    var = jnp.mean(d * d, axis=-1, keepdims=True)
    return d * lax.rsqrt(var + LN_EPS) * g + b


def _silu(x):
    half = 0.5 * x
    return half * (1.0 + jnp.tanh(half))


def _mm_kernel(x_ref, w_ref, o_ref, xb_ref):
    @pl.when(pl.program_id(1) == 0)
    def _():
        xb_ref[...] = x_ref[...].astype(BF16)

    o_ref[...] = jnp.dot(xb_ref[...], w_ref[...], preferred_element_type=F32).astype(o_ref.dtype)


def matmul(x, w, out_dtype, tm=1024, tn=1024):
    m, k = x.shape
    tm = min(tm, m)
    n = w.shape[1]
    return pl.pallas_call(
        _mm_kernel,
        grid=(m // tm, n // tn),
        in_specs=[pl.BlockSpec((tm, k), lambda i, j: (i, 0)),
                  pl.BlockSpec((k, tn), lambda i, j: (0, j))],
        out_specs=pl.BlockSpec((tm, tn), lambda i, j: (i, j)),
        out_shape=jax.ShapeDtypeStruct((m, n), out_dtype),
        scratch_shapes=[pltpu.VMEM((tm, k), BF16)],
        compiler_params=_params("parallel", "arbitrary"),
        name="matmul",
    )(x, w)


def _dot_split(x, w):
    xh = x.astype(BF16)
    xl = (x - xh.astype(F32)).astype(BF16)
    wh = w.astype(BF16)
    wl = (w - wh.astype(F32)).astype(BF16)
    dot = functools.partial(jnp.dot, preferred_element_type=F32)
    return dot(xh, wh) + (dot(xh, wl) + dot(xl, wh))


def _mm_f32_kernel(x_ref, w_ref, o_ref):
    o_ref[...] = _dot_split(x_ref[...], w_ref[...])


def matmul_f32(x, w, tm=1024):
    m, k = x.shape
    tm = min(tm, m)
    n = w.shape[1]
    return pl.pallas_call(
        _mm_f32_kernel,
        grid=(m // tm,),
        in_specs=[pl.BlockSpec((tm, k), lambda i: (i, 0)),
                  pl.BlockSpec((k, n), lambda i: (0, 0))],
        out_specs=pl.BlockSpec((tm, n), lambda i: (i, 0)),
        out_shape=jax.ShapeDtypeStruct((m, n), F32),
        compiler_params=_params("parallel"),
        name="matmul_f32",
    )(x, w)


ROW_CHUNKS = D_MODEL // LANES


def _store_row_tiled(ref, value):
    n = value.shape[0]
    for j in range(ROW_CHUNKS):
        ref[pl.ds(j, n, stride=ROW_CHUNKS), :] = value[:, j * LANES:(j + 1) * LANES]


def _load_row_tiled(ref, j, n):
    return ref[pl.ds(j, n, stride=ROW_CHUNKS), :]


def _mm_ln_route_kernel(x_ref, w_ref, r_ref, g_ref, b_ref, wr_ref,
                        o_ref, tiled_ref, idx_ref, gate_ref, rank_ref, count_ref, run_ref):
    @pl.when(pl.program_id(0) == 0)
    def _():
        run_ref[...] = jnp.zeros_like(run_ref)

    y = jnp.dot(x_ref[...], w_ref[...], preferred_element_type=F32)
    o = _layer_norm_rows(DN_ALPHA * r_ref[...] + y, g_ref[...], b_ref[...])
    o_ref[...] = o
    _store_row_tiled(tiled_ref, o)
    idx, gates = _top2(_dot_split(o, wr_ref[...]))
    idx_ref[...] = idx
    gate_ref[...] = gates
    rank_ref[...] = _rank_block(idx, run_ref)
    count_ref[...] = run_ref[...]


def matmul_deepnorm_route(x, w, resid, g, b, w_router):
    m, k = x.shape
    tm = min(ROUTE_BLOCK, m)
    d = w.shape[1]
    rows = lambda width: pl.BlockSpec((tm, width), lambda i: (i, 0))
    fixed = lambda a: pl.BlockSpec(a.shape, lambda i: (0, 0))
    args = [x, w, resid, g.reshape(1, d), b.reshape(1, d),
            jnp.pad(w_router, ((0, 0), (0, LANES - N_EXPERTS)))]
    return pl.pallas_call(
        _mm_ln_route_kernel,
        grid=(m // tm,),
        in_specs=[rows(k), fixed(w), rows(d), fixed(args[3]), fixed(args[4]), fixed(args[5])],
        out_specs=[rows(d), pl.BlockSpec((tm * ROW_CHUNKS, LANES), lambda i: (i, 0)),
                   rows(LANES), rows(LANES), rows(LANES), pl.BlockSpec((1, LANES), lambda i: (0, 0))],
        out_shape=[jax.ShapeDtypeStruct((m, d), F32),
                   jax.ShapeDtypeStruct((m * ROW_CHUNKS, LANES), F32),
                   jax.ShapeDtypeStruct((m, LANES), jnp.int32),
                   jax.ShapeDtypeStruct((m, LANES), F32),
                   jax.ShapeDtypeStruct((m, LANES), jnp.int32),
                   jax.ShapeDtypeStruct((1, LANES), F32)],
        scratch_shapes=[pltpu.VMEM((1, LANES), F32)],
        compiler_params=_params("arbitrary"),
        name="matmul_deepnorm_route",
    )(*args)


def _mix_ffn_kernel(o_ref, wo_ref, r_ref, g1_ref, b1_ref, wg_ref, wu_ref, wd_ref, g2_ref, b2_ref, out_ref):
    x = _layer_norm_rows(
        DN_ALPHA * r_ref[...] + jnp.dot(o_ref[...], wo_ref[...], preferred_element_type=F32),
        g1_ref[...], b1_ref[...])
    xb = x.astype(BF16)
    gate = jnp.dot(xb, wg_ref[...], preferred_element_type=F32)
    up = jnp.dot(xb, wu_ref[...], preferred_element_type=F32)
    h = (_silu(gate) * up).astype(BF16)
    y = jnp.dot(h, wd_ref[...], preferred_element_type=F32)
    out_ref[...] = _layer_norm_rows(DN_ALPHA * x + y, g2_ref[...], b2_ref[...])


def mixer_out_ffn_deepnorm(o, w_out, resid, g1, b1, wg, wu, wd, g2, b2, tm=512):
    m, k = o.shape
    d = w_out.shape[1]
    tm = min(tm, m)
    resident = lambda a: pl.BlockSpec(a.shape, lambda i: (0, 0), pipeline_mode=pl.Buffered(1))
    vec = lambda v: v.reshape(1, d)
    args = (o, w_out, resid, vec(g1), vec(b1), wg, wu, wd, vec(g2), vec(b2))
    return pl.pallas_call(
        _mix_ffn_kernel,
        grid=(m // tm,),
        in_specs=[pl.BlockSpec((tm, k), lambda i: (i, 0)), resident(w_out),
                  pl.BlockSpec((tm, d), lambda i: (i, 0)), resident(args[3]), resident(args[4]),
                  resident(wg), resident(wu), resident(wd), resident(args[8]), resident(args[9])],
        out_specs=pl.BlockSpec((tm, d), lambda i: (i, 0)),
        out_shape=jax.ShapeDtypeStruct((m, d), F32),
        compiler_params=_params("parallel"),
        name="mixer_out_ffn_deepnorm",
    )(*args)


def _split_bf16(v, pieces):
    out = []
    r = v
    for _ in range(pieces - 1):
        p = r.astype(BF16)
        out.append(p)
        r = r - p.astype(F32)
    out.append(r.astype(BF16))
    return out


def _expand(v, e_ref, pieces):
    stacked = jnp.concatenate(_split_bf16(v, pieces), axis=1)
    return jnp.dot(stacked, e_ref[...], preferred_element_type=F32)


def _tril_f32(n, strict=False):
    r = lax.broadcasted_iota(jnp.int32, (n, n), 0)
    c = lax.broadcasted_iota(jnp.int32, (n, n), 1)
    return ((r > c) if strict else (r >= c)).astype(F32)


def _ssd_kernel(z_ref, xs_ref, bc_ref, dt_ref,
                dtb_ref, alog_ref, dskip_ref, normw_ref, e64_ref, o_ref,
                state_ref, xdt_ref, cb_ref, ydiag_ref,
                *, chunk):
    L = chunk
    c = pl.program_id(1)

    @pl.when(c == 0)
    def _():
        state_ref[...] = jnp.zeros_like(state_ref)

    xs = xs_ref[...]
    bcv = bc_ref[...]

    dt = jax.nn.softplus(dt_ref[...] + dtb_ref[...])
    da = dt * (-jnp.exp(alog_ref[...]))
    a_cs = jnp.dot(_tril_f32(L), da, preferred_element_type=F32, precision=HIGHEST)
    ea = jnp.exp(a_cs)
    dte = jnp.exp(a_cs[L - 1:L, :] - a_cs)

    spread = _expand(jnp.concatenate([dt, ea, dte], axis=0), e64_ref, 2)
    dt_x, ea_x, dte_x = spread[0:L], spread[L:2 * L], spread[2 * L:3 * L]
    a_t = a_cs.T

    xdt = xs * dt_x
    xdt_ref[...] = xdt.astype(BF16)
    xdte = (xdt * dte_x).astype(BF16)

    y_off = []
    for g in range(SSD_GROUPS):
        bm = bcv[:, g * SSD_STATE:(g + 1) * SSD_STATE]
        cm = bcv[:, (SSD_GROUPS + g) * SSD_STATE:(SSD_GROUPS + g + 1) * SSD_STATE].astype(BF16)
        cb_ref[g] = lax.dot_general(cm, bm.astype(BF16), (((1,), (1,)), ((), ())),
                                    preferred_element_type=F32)
        gs = slice(g * SSD_GROUP_DIM, (g + 1) * SSD_GROUP_DIM)
        st = state_ref[g]
        y_off.append(jnp.dot(cm, st.astype(BF16), preferred_element_type=F32) * ea_x[:, gs])
        state_ref[g] = st * ea_x[L - 1:L, gs] + jnp.dot(
            bm.T.astype(BF16), xdte[:, gs], preferred_element_type=F32)

    row = lax.broadcasted_iota(jnp.int32, (L, L), 0)
    col = lax.broadcasted_iota(jnp.int32, (L, L), 1)
    causal = row >= col
    head0 = lax.broadcasted_iota(jnp.int32, (L, LANES), 1) < SSD_HEAD_DIM

    heads_per_group = SSD_HEADS // SSD_GROUPS

    for g in range(SSD_GROUPS):
        cbg = cb_ref[g]
        scores = []
        for j in range(heads_per_group):
            h = g * heads_per_group + j
            a_l = a_cs[:, h:h + 1]
            a_s = a_t[h:h + 1, :]
            scores.append((cbg * jnp.exp(jnp.where(causal, a_l - a_s, NEG_BIG))).astype(BF16))
        for pp in range(heads_per_group // 2):
            lanes = slice((g * (heads_per_group // 2) + pp) * LANES,
                          (g * (heads_per_group // 2) + pp + 1) * LANES)
            xpair = xdt_ref[:, lanes]
            ys = [jnp.dot(scores[2 * pp + j], xpair, preferred_element_type=F32) for j in range(2)]
            ydiag_ref[:, lanes] = jnp.where(head0, ys[0], ys[1])

    y = ydiag_ref[...] + jnp.concatenate(y_off, axis=1) + dskip_ref[...] * xs
    y = y * _silu(z_ref[...])
    parts = []
    for g in range(SSD_GROUPS):
        yg = y[:, g * SSD_GROUP_DIM:(g + 1) * SSD_GROUP_DIM]
        parts.append(yg * lax.rsqrt(jnp.mean(yg * yg, axis=-1, keepdims=True) + RMS_EPS))
    o_ref[...] = (jnp.concatenate(parts, axis=1) * normw_ref[...]).astype(o_ref.dtype)


def _expansion_matrix(width, pieces):
    h = jnp.arange(LANES)[:, None]
    lane = jnp.arange(SSD_HEADS * width)[None, :]
    e = (lane // width == h).astype(BF16)
    return jnp.concatenate([e] * pieces, axis=0)


def _mm_conv_kernel(x_ref, halo_ref, w_ref, cw_ref, cb_ref, o_ref, ext_ref, xb_ref, *, tiles_per_seq):
    tm = x_ref.shape[0]

    @pl.when(pl.program_id(1) == 0)
    def _():
        xb_ref[...] = x_ref[...].astype(BF16)

    w = w_ref[...]
    halo = jnp.dot(halo_ref[...].astype(BF16), w, preferred_element_type=F32)
    first = pl.program_id(0) % tiles_per_seq == 0
    ext_ref[0:SUBLANES, :] = jnp.where(first, 0.0, halo)
    ext_ref[SUBLANES:SUBLANES + tm, :] = jnp.dot(xb_ref[...], w, preferred_element_type=F32)
    ext = ext_ref[...]
    acc = cw_ref[0:1, :] * ext
    for k in range(1, SSD_CONV):
        acc = pltpu.roll(acc, 1, axis=0) + cw_ref[k:k + 1, :] * ext
    o_ref[...] = _silu(acc[SUBLANES:SUBLANES + tm, :] + cb_ref[...])


def matmul_conv_silu(x, w, conv_w, conv_b, seq, tm=1024, tn=1024):
    m, k = x.shape
    tm = min(tm, seq)
    n = w.shape[1]
    halo_blocks = tm // SUBLANES
    return pl.pallas_call(
        functools.partial(_mm_conv_kernel, tiles_per_seq=seq // tm),
        grid=(m // tm, n // tn),
        in_specs=[pl.BlockSpec((tm, k), lambda i, j: (i, 0)),
                  pl.BlockSpec((SUBLANES, k), lambda i, j: (jnp.maximum(i * halo_blocks - 1, 0), 0)),
                  pl.BlockSpec((k, tn), lambda i, j: (0, j)),
                  pl.BlockSpec((SSD_CONV, tn), lambda i, j: (0, j)),
                  pl.BlockSpec((1, tn), lambda i, j: (0, j))],
        out_specs=pl.BlockSpec((tm, tn), lambda i, j: (i, j)),
        out_shape=jax.ShapeDtypeStruct((m, n), F32),
        scratch_shapes=[pltpu.VMEM((tm + SUBLANES, tn), F32), pltpu.VMEM((tm, k), BF16)],
        compiler_params=_params("parallel", "arbitrary"),
        name="matmul_conv_silu",
    )(x, x, w, conv_w, conv_b.reshape(1, n))


def ssd_core(z, xbc, dt_raw, dt_bias, a_log, d_skip, norm_w, batch, seq, chunk=128):
    t = z.shape[0]
    nc = seq // chunk
    pad = LANES - SSD_HEADS
    row = lambda v: v.reshape(1, -1).astype(F32)
    args = (
        z, xbc, xbc, dt_raw,
        row(jnp.pad(dt_bias, (0, pad))), row(jnp.pad(a_log, (0, pad))),
        row(jnp.repeat(d_skip, SSD_HEAD_DIM)), row(norm_w),
        _expansion_matrix(SSD_HEAD_DIM, 2),
    )
    blk = lambda b, c: (b * nc + c, 0)
    const = lambda b, c: (0, 0)
    full = lambda a: pl.BlockSpec(a.shape, const)
    in_specs = [
        pl.BlockSpec((chunk, SSD_D_INNER), blk),
        pl.BlockSpec((chunk, SSD_D_INNER), blk),
        pl.BlockSpec((chunk, SSD_BC_DIM), lambda b, c: (b * nc + c, 2)),
        pl.BlockSpec((chunk, LANES), blk),
    ] + [full(a) for a in args[4:]]
    return pl.pallas_call(
        functools.partial(_ssd_kernel, chunk=chunk),
        grid=(batch, nc),
        in_specs=in_specs,
        out_specs=pl.BlockSpec((chunk, SSD_D_INNER), blk),
        out_shape=jax.ShapeDtypeStruct((t, SSD_D_INNER), BF16),
        scratch_shapes=[
            pltpu.VMEM((SSD_GROUPS, SSD_STATE, SSD_GROUP_DIM), F32),
            pltpu.VMEM((chunk, SSD_D_INNER), BF16),
            pltpu.VMEM((SSD_GROUPS, chunk, chunk), F32),
            pltpu.VMEM((chunk, SSD_D_INNER), F32),
        ],
        compiler_params=_params("parallel", "arbitrary"),
        name="ssd_core",
    )(*args)


def ssd_mixer(x, w_in, conv_w, conv_b, dt_bias, a_log, d_skip, norm_w, batch, seq):
    n_zx = SSD_D_INNER + SSD_D_INNER + SSD_BC_DIM
    z = matmul(x, w_in[:, :SSD_D_INNER].astype(BF16), F32)
    xbc = matmul_conv_silu(x, w_in[:, SSD_D_INNER:n_zx].astype(BF16), conv_w, conv_b, seq)
    w_dt = jnp.pad(w_in[:, n_zx:], ((0, 0), (0, LANES - SSD_HEADS)))
    dt_raw = matmul_f32(x, w_dt)
    return ssd_core(z, xbc, dt_raw, dt_bias, a_log, d_skip, norm_w, batch, seq)


ATT_BLOCK = 128
ATT_Q = 256
ATT_PAIRS = 4
ATT_STEP_LANES = ATT_PAIRS * LANES
ATT_SCALE = ATT_HEAD_DIM ** -0.5
LOG2E = 1.4426950408889634
EXP2_UNDERFLOW = -160.0
FOX_BOUND_SLACK = 1.05


def _split_heads(q_ref, rows, head0):
    qs = []
    for p in range(ATT_PAIRS):
        q2 = q_ref[rows, p * LANES:(p + 1) * LANES]
        zero = jnp.zeros_like(q2)
        qs += [jnp.where(head0, q2, zero), jnp.where(head0, zero, q2)]
    return qs


def _sb_kernel(q_ref, k_ref, v_ref, u_ref, o_ref, *scratch, seq):
    n_heads = 2 * ATT_PAIRS
    later_ref, acc_ref, z_ref = scratch[:n_heads], scratch[n_heads:2 * n_heads], scratch[2 * n_heads:]
    nq = seq // ATT_Q
    ndiag = ATT_Q // ATT_BLOCK
    head0 = lax.broadcasted_iota(jnp.int32, (ATT_Q, LANES), 1) < ATT_HEAD_DIM
    row = lax.broadcasted_iota(jnp.int32, (ATT_Q, ATT_BLOCK), 0)
    col = lax.broadcasted_iota(jnp.int32, (ATT_Q, ATT_BLOCK), 1)

    def q_body(i, _):
        q0 = pl.multiple_of(i * ATT_Q, ATT_Q)
        rows = pl.ds(q0, ATT_Q)
        qs = _split_heads(q_ref, rows, head0)
        for n in range(2 * ATT_PAIRS):
            later_ref[n][...] = jnp.zeros((ATT_Q, LANES), F32)
            acc_ref[n][...] = jnp.zeros((ATT_Q, LANES), F32)

        heads = range(2 * ATT_PAIRS)

        def key_rows(j):
            return pl.ds(pl.multiple_of(j * ATT_BLOCK, ATT_BLOCK), ATT_BLOCK)

        def scores(j):
            k2 = [k_ref[key_rows(j), p * LANES:(p + 1) * LANES] for p in range(ATT_PAIRS)]
            return [lax.dot_general(qs[n], k2[n // 2], (((1,), (1,)), ((), ())),
                                    preferred_element_type=F32) * (ATT_SCALE * LOG2E) for n in heads]

        def consume(z2, j, strict):
            keys = key_rows(j)
            v2 = [v_ref[keys, p * LANES:(p + 1) * LANES] for p in range(ATT_PAIRS)]
            log_beta, sums = [], []
            for n in heads:
                lb = jnp.minimum(z2[n], 0.0) - jnp.log2(1.0 + jnp.exp2(-jnp.abs(z2[n])))
                log_keep = lb - z2[n]
                if strict is not None:
                    log_keep = jnp.where(strict, log_keep, 0.0)
                hi = log_keep.astype(BF16)
                lo = (log_keep - hi.astype(F32)).astype(BF16)
                log_beta.append(lb)
                sums.append(jnp.dot(jnp.concatenate([hi, lo], axis=1), u_ref[...],
                                    preferred_element_type=F32))
            for n in heads:
                w = jnp.exp2(log_beta[n] + sums[n][:, :ATT_BLOCK] + later_ref[n][...])
                if strict is not None:
                    w = jnp.where(strict, w, 0.0)
                acc_ref[n][...] += jnp.dot(w.astype(BF16), v2[n // 2], preferred_element_type=F32)
                later_ref[n][...] += sums[n][:, ATT_BLOCK:]

        n_off = i * ndiag
        diag = [(d, scores(n_off + d)) for d in reversed(range(ndiag))]
        first_off = scores(jnp.maximum(n_off - 1, 0))
        for d, zd in diag:
            consume(zd, n_off + d, d * ATT_BLOCK + col < row)

        for n, zn in enumerate(first_off):
            z_ref[n][...] = zn

        def off_diagonal(carry):
            jj, _ = carry
            z2 = [z_ref[n][...] for n in heads]
            ahead = scores(jnp.maximum(n_off - 2 - jj, 0))
            consume(z2, n_off - 1 - jj, None)
            for n in heads:
                z_ref[n][...] = ahead[n]
            top = later_ref[0][...]
            for n in range(1, 2 * ATT_PAIRS):
                top = jnp.maximum(top, later_ref[n][...])
            return jj + 1, jnp.max(top) > EXP2_UNDERFLOW

        lax.while_loop(lambda c: (c[0] < n_off) & c[1], off_diagonal, (jnp.int32(0), n_off > 0))
        for p in range(ATT_PAIRS):
            o_ref[rows, p * LANES:(p + 1) * LANES] = jnp.where(
                head0, acc_ref[2 * p][...], acc_ref[2 * p + 1][...]).astype(o_ref.dtype)
        return 0

    lax.fori_loop(0, nq, q_body, 0)


def _suffix_sum_matrix():
    j = jnp.arange(2 * ATT_BLOCK)[:, None] % ATT_BLOCK
    s = jnp.arange(2 * ATT_BLOCK)[None, :]
    return jnp.where(s < ATT_BLOCK, j > s, True).astype(BF16)


def _attention_specs(seq):
    n_steps = ATT_HEADS // 2 // ATT_PAIRS
    blk = lambda off: pl.BlockSpec((seq, ATT_STEP_LANES), lambda b, p: (b, off + p))
    specs = [blk(0), blk(n_steps), blk(2 * n_steps)]
    return n_steps, specs, pl.BlockSpec((seq, ATT_STEP_LANES), lambda b, p: (b, p))


def sb_attention(qkv, batch, seq):
    n_steps, in_specs, out_spec = _attention_specs(seq)
    u = _suffix_sum_matrix()
    return pl.pallas_call(
        functools.partial(_sb_kernel, seq=seq),
        grid=(batch, n_steps),
        in_specs=in_specs + [pl.BlockSpec(u.shape, lambda b, p: (0, 0))],
        out_specs=out_spec,
        out_shape=jax.ShapeDtypeStruct((batch * seq, ATT_DIM), BF16),
        scratch_shapes=[pltpu.VMEM((ATT_Q, LANES), F32)] * (6 * ATT_PAIRS),
        compiler_params=_params("parallel", "parallel"),
        name="sb_attention",
    )(qkv, qkv, qkv, u)


def sb_mixer(x, w_qkv, batch, seq):
    return sb_attention(matmul(x, w_qkv.astype(BF16), BF16), batch, seq)


CUMSUM_BLOCK = 256


def _fox_decay_kernel(f_ref, bf_ref, ccol_ref, crow_ref, *, seq):
    tri = _tril_f32(CUMSUM_BLOCK)
    carry = jnp.zeros((1, LANES), F32)
    for blk in range(seq // CUMSUM_BLOCK):
        rows = slice(blk * CUMSUM_BLOCK, (blk + 1) * CUMSUM_BLOCK)
        log_f = jax.nn.log_sigmoid(f_ref[rows, :] + bf_ref[...])
        c = jnp.dot(tri, log_f, preferred_element_type=F32, precision=HIGHEST) + carry
        carry = c[CUMSUM_BLOCK - 1:CUMSUM_BLOCK, :]
        ccol_ref[rows, :] = c
        c_t = c.T
        for p in range(ATT_HEADS // 2):
            crow_ref[0, p, :, rows] = c_t[2 * p:2 * p + 2, :]


def fox_decay(f_raw, b_f, batch, seq):
    return pl.pallas_call(
        functools.partial(_fox_decay_kernel, seq=seq),
        grid=(batch,),
        in_specs=[pl.BlockSpec((seq, LANES), lambda b: (b, 0)),
                  pl.BlockSpec((1, LANES), lambda b: (0, 0))],
        out_specs=[pl.BlockSpec((seq, LANES), lambda b: (b, 0)),
                   pl.BlockSpec((1, ATT_HEADS // 2, 2, seq), lambda b: (b, 0, 0, 0))],
        out_shape=[jax.ShapeDtypeStruct((batch * seq, LANES), F32),
                   jax.ShapeDtypeStruct((batch, ATT_HEADS // 2, 2, seq), F32)],
        compiler_params=_params("parallel"),
        name="fox_decay",
    )(f_raw, jnp.pad(b_f, (0, LANES - ATT_HEADS)).reshape(1, LANES))


def _fox_kernel(q_ref, k_ref, v_ref, ccol_ref, crow_ref, o_ref, crep_ref, vt_ref, *scratch, seq):
    n_heads = 2 * ATT_PAIRS
    m_ref, acc_ref, s_ref = scratch[:n_heads], scratch[n_heads:2 * n_heads], scratch[2 * n_heads:]
    nq = seq // ATT_Q
    ndiag = ATT_Q // ATT_BLOCK
    step = pl.program_id(1)
    heads = range(n_heads)
    head0 = lax.broadcasted_iota(jnp.int32, (ATT_Q, LANES), 1) < ATT_HEAD_DIM
    head0_rows = lax.broadcasted_iota(jnp.int32, (LANES, ATT_BLOCK), 0) < ATT_HEAD_DIM
    key = lax.broadcasted_iota(jnp.int32, (ATT_BLOCK, ATT_Q), 0)
    qry = lax.broadcasted_iota(jnp.int32, (ATT_BLOCK, ATT_Q), 1)

    pieces = jnp.concatenate(_split_bf16(ccol_ref[...], 3), axis=1)
    sel_row = lax.broadcasted_iota(jnp.int32, (3 * LANES, LANES), 0) & (LANES - 1)
    for n in heads:
        sel = jnp.where(sel_row == 2 * ATT_PAIRS * step + n, 1.0, 0.0).astype(BF16)
        crep_ref[n] = jnp.dot(pieces, sel, preferred_element_type=F32) * LOG2E
    for p in range(ATT_PAIRS):
        for blk in range(seq // ATT_BLOCK):
            rows = slice(blk * ATT_BLOCK, (blk + 1) * ATT_BLOCK)
            vt_ref[p, :, rows] = v_ref[rows, p * LANES:(p + 1) * LANES].astype(F32).T.astype(BF16)
    lane_r = lax.broadcasted_iota(jnp.int32, (LANES, LANES), 0) < ATT_HEAD_DIM
    lane_c = lax.broadcasted_iota(jnp.int32, (LANES, LANES), 1) < ATT_HEAD_DIM
    same_head = jnp.where(lane_r == lane_c, 1.0, 0.0).astype(BF16)
    k_norm = []
    for p in range(ATT_PAIRS):
        kf = k_ref[:, p * LANES:(p + 1) * LANES].astype(F32)
        sq = jnp.dot((kf * kf).astype(BF16), same_head, preferred_element_type=F32)
        top = jnp.sqrt(jnp.max(sq, axis=0, keepdims=True))
        k_norm += [top[:, 0:1], top[:, ATT_HEAD_DIM:ATT_HEAD_DIM + 1]]
    ones_rows = jnp.ones((SUBLANES, LANES), BF16)

    def q_body(i, _):
        q0 = pl.multiple_of(i * ATT_Q, ATT_Q)
        rows = pl.ds(q0, ATT_Q)
        qs = _split_heads(q_ref, rows, head0)
        c_q = [crow_ref[0, n // 2, n % 2:n % 2 + 1, rows] * LOG2E for n in heads]
        qk_bound = []
        for n in heads:
            qf = qs[n].astype(F32)
            q_sq = lax.dot_general(ones_rows, (qf * qf).astype(BF16), (((1,), (1,)), ((), ())),
                                   preferred_element_type=F32)[0:1, :]
            qk_bound.append(jnp.sqrt(q_sq) * k_norm[n] * (ATT_SCALE * LOG2E * FOX_BOUND_SLACK))
        for n in heads:
            m_ref[n][...] = jnp.full((1, ATT_Q), NEG_BIG, F32)
            acc_ref[n][...] = jnp.zeros((LANES, ATT_Q), F32)

        def key_rows(j):
            return pl.ds(pl.multiple_of(j * ATT_BLOCK, ATT_BLOCK), ATT_BLOCK)

        def scores(j):
            k2 = [k_ref[key_rows(j), p * LANES:(p + 1) * LANES] for p in range(ATT_PAIRS)]
            return [lax.dot_general(k2[n // 2], qs[n], (((1,), (1,)), ((), ())),
                                    preferred_element_type=F32) * (ATT_SCALE * LOG2E) for n in heads]

        def consume(s, j, causal):
            keys = key_rows(j)
            vt = [vt_ref[p, :, keys] for p in range(ATT_PAIRS)]
            for n in heads:
                c_k = crep_ref[n, keys, :]
                sn = s[n] + (c_q[n] - jnp.concatenate([c_k] * (ATT_Q // LANES), axis=1))
                if causal is not None:
                    sn = jnp.where(causal, sn, NEG_BIG)
                m_old = m_ref[n][...]
                m_new = jnp.maximum(m_old, jnp.max(sn, axis=0, keepdims=True))
                prob = jnp.exp2(sn - m_new).astype(BF16)
                one = jnp.ones_like(vt[n // 2])
                v1t = jnp.where(head0_rows, vt[n // 2], one) if n % 2 == 0 else jnp.where(
                    head0_rows, one, vt[n // 2])
                acc_ref[n][...] = jnp.exp2(m_old - m_new) * acc_ref[n][...] + jnp.dot(
                    v1t, prob, preferred_element_type=F32)
                m_ref[n][...] = m_new

        n_off = i * ndiag
        diag = [(d, scores(n_off + d)) for d in range(ndiag)]
        first_off = scores(jnp.maximum(n_off - 1, 0))
        for d, sd in diag:
            consume(sd, n_off + d, d * ATT_BLOCK + key <= qry)

        for n, sn in enumerate(first_off):
            s_ref[n][...] = sn

        def off_diagonal(carry):
            jj, _ = carry
            j = n_off - 1 - jj
            s = [s_ref[n][...] for n in heads]
            ahead = scores(jnp.maximum(j - 1, 0))
            consume(s, j, None)
            for n in heads:
                s_ref[n][...] = ahead[n]
            prev_keys = key_rows(jnp.maximum(j - 1, 0))
            worst = None
            for n in heads:
                c_last = crow_ref[0, n // 2, n % 2:n % 2 + 1, prev_keys][:, LANES - 1:LANES] * LOG2E
                gap = qk_bound[n] + c_q[n] - c_last - m_ref[n][...]
                worst = gap if worst is None else jnp.maximum(worst, gap)
            return jj + 1, jnp.max(worst) > EXP2_UNDERFLOW

        lax.while_loop(lambda c: (c[0] < n_off) & c[1], off_diagonal, (jnp.int32(0), n_off > 0))
        for p in range(ATT_PAIRS):
            a0, a1 = acc_ref[2 * p][...], acc_ref[2 * p + 1][...]
            o_t = jnp.concatenate([a0[:ATT_HEAD_DIM] / a0[ATT_HEAD_DIM:],
                                   a1[ATT_HEAD_DIM:] / a1[:ATT_HEAD_DIM]], axis=0)
            o_ref[rows, p * LANES:(p + 1) * LANES] = o_t.T.astype(o_ref.dtype)
        return 0

    lax.fori_loop(0, nq, q_body, 0)


def fox_attention(qkv, ccol, crow, batch, seq):
    n_steps, in_specs, out_spec = _attention_specs(seq)
    in_specs += [pl.BlockSpec((seq, LANES), lambda b, p: (b, 0)),
                 pl.BlockSpec((1, ATT_PAIRS, 2, seq), lambda b, p: (b, p, 0, 0))]
    return pl.pallas_call(
        functools.partial(_fox_kernel, seq=seq),
        grid=(batch, n_steps),
        in_specs=in_specs,
        out_specs=out_spec,
        out_shape=jax.ShapeDtypeStruct((batch * seq, ATT_DIM), BF16),
        scratch_shapes=([pltpu.VMEM((2 * ATT_PAIRS, seq, LANES), F32),
                         pltpu.VMEM((ATT_PAIRS, LANES, seq), BF16)]
                        + [pltpu.VMEM((1, ATT_Q), F32)] * (2 * ATT_PAIRS)
                        + [pltpu.VMEM((LANES, ATT_Q), F32)] * (2 * ATT_PAIRS)
                        + [pltpu.VMEM((ATT_BLOCK, ATT_Q), F32)] * (2 * ATT_PAIRS)),
        compiler_params=_params("parallel", "parallel"),
        name="fox_attention",
    )(qkv, qkv, qkv, ccol, crow)


def fox_mixer(x, w_qkvf, b_f, batch, seq):
    qkv = matmul(x, w_qkvf[:, :3 * ATT_DIM].astype(BF16), BF16)
    w_f = jnp.pad(w_qkvf[:, 3 * ATT_DIM:], ((0, 0), (0, LANES - ATT_HEADS)))
    ccol, crow = fox_decay(matmul_f32(x, w_f), b_f, batch, seq)
    return fox_attention(qkv, ccol, crow, batch, seq)


MOE_TILE = 512
ROUTE_BLOCK = 512


def _top2(logits):
    lane = lax.broadcasted_iota(jnp.int32, logits.shape, 1)
    logits = jnp.where(lane < N_EXPERTS, logits, NEG_BIG)
    m1 = jnp.max(logits, axis=-1, keepdims=True)
    i1 = jnp.min(jnp.where(logits == m1, lane, LANES), axis=-1, keepdims=True)
    rest = jnp.where(lane == i1, NEG_BIG, logits)
    m2 = jnp.max(rest, axis=-1, keepdims=True)
    i2 = jnp.min(jnp.where(rest == m2, lane, LANES), axis=-1, keepdims=True)
    e2 = jnp.exp(m2 - m1)
    denom = 1.0 + e2
    idx = jnp.where(lane == 0, i1, jnp.where(lane == 1, i2, 0))
    gates = jnp.where(lane == 0, 1.0 / denom, jnp.where(lane == 1, e2 / denom, 0.0))
    return idx, gates


def _rank_block(idx, run_ref):
    tb = idx.shape[0]
    lane = lax.broadcasted_iota(jnp.int32, idx.shape, 1)
    oh0 = lane == idx[:, 0:1]
    oh1 = lane == idx[:, 1:2]
    both = jnp.where(oh0 | oh1, 1.0, 0.0)
    before = jnp.dot(_tril_f32(tb, strict=True).astype(BF16), both.astype(BF16),
                     preferred_element_type=F32) + run_ref[...]
    r0 = jnp.sum(jnp.where(oh0, before, 0.0), axis=-1, keepdims=True)
    r1 = jnp.sum(jnp.where(oh1, before, 0.0), axis=-1, keepdims=True)
    run_ref[...] += jnp.sum(both, axis=0, keepdims=True)
    return jnp.where(lane == 0, r0, jnp.where(lane == 1, r1, 0.0)).astype(jnp.int32)


def _pos_kernel(idx_ref, rank_ref, off_ref, pos_ref):
    idx = idx_ref[...]
    lane = lax.broadcasted_iota(jnp.int32, idx.shape, 1)
    off = off_ref[...]
    p0 = jnp.sum(jnp.where(lane == idx[:, 0:1], off, 0), axis=-1, keepdims=True)
    p1 = jnp.sum(jnp.where(lane == idx[:, 1:2], off, 0), axis=-1, keepdims=True)
    pos_ref[...] = rank_ref[...] + jnp.where(lane == 0, p0, jnp.where(lane == 1, p1, 0))


def moe_positions(idx, rank, offsets):
    t = idx.shape[0]
    tb = min(4 * ROUTE_BLOCK, t)
    off = jnp.pad(offsets, (0, LANES - N_EXPERTS)).reshape(1, LANES)
    blk = pl.BlockSpec((tb, LANES), lambda i: (i, 0))
    return pl.pallas_call(
        _pos_kernel,
        grid=(t // tb,),
        in_specs=[blk, blk, pl.BlockSpec((1, LANES), lambda i: (0, 0))],
        out_specs=blk,
        out_shape=jax.ShapeDtypeStruct((t, LANES), jnp.int32),
        compiler_params=_params("parallel"),
        name="moe_positions",
    )(idx, rank, off)


DMA_ISSUE_UNROLL = 8


def _tile_rows(row):
    return pl.ds(pl.multiple_of(row * ROW_CHUNKS, ROW_CHUNKS), ROW_CHUNKS)


def _dispatch_kernel(pos_ref, cnt_ref, off_ref, x_ref, xs_ref, zero_ref, sem, pad_sem, *, tb):
    base = pl.program_id(0) * (TOP_K * tb)

    def issue(t, _):
        for k in range(TOP_K):
            pltpu.make_async_copy(x_ref.at[_tile_rows(t)],
                                  xs_ref.at[_tile_rows(pos_ref[base + TOP_K * t + k])], sem).start()
        return 0

    lax.fori_loop(0, tb, issue, 0, unroll=DMA_ISSUE_UNROLL)

    @pl.when(pl.program_id(0) == 0)
    def _():
        zero_ref[...] = jnp.zeros_like(zero_ref)

        def fill_range(first, count):
            def copy(r):
                return pltpu.make_async_copy(zero_ref, xs_ref.at[_tile_rows(first + r)], pad_sem)

            def fill(r, _):
                copy(r).start()
                return 0

            def drain(r, _):
                copy(r).wait()
                return 0

            lax.fori_loop(0, count, fill, 0)
            lax.fori_loop(0, count, drain, 0)

        for e in range(N_EXPERTS):
            fill_range(off_ref[e] + cnt_ref[e], (-cnt_ref[e]) & (MOE_TILE - 1))
        last = N_EXPERTS - 1
        used = off_ref[last] + cnt_ref[last] + ((-cnt_ref[last]) & (MOE_TILE - 1))
        fill_range(used, xs_ref.shape[0] // ROW_CHUNKS - used)

    for _ in range(TOP_K):
        pltpu.make_async_copy(x_ref, xs_ref.at[pl.ds(0, tb * ROW_CHUNKS)], sem).wait()


def moe_dispatch(x_tiled, pos_flat, counts, offsets, n_rows):
    t = x_tiled.shape[0] // ROW_CHUNKS
    tb = min(ROUTE_BLOCK, t)
    grid_spec = pltpu.PrefetchScalarGridSpec(
        num_scalar_prefetch=3,
        grid=(t // tb,),
        in_specs=[pl.BlockSpec((tb * ROW_CHUNKS, LANES), lambda i, *_: (i, 0))],
        out_specs=pl.BlockSpec(memory_space=pl.ANY),
        scratch_shapes=[pltpu.VMEM((ROW_CHUNKS, LANES), F32), pltpu.SemaphoreType.DMA(()),
                        pltpu.SemaphoreType.DMA(())],
    )
    return pl.pallas_call(
        functools.partial(_dispatch_kernel, tb=tb),
        grid_spec=grid_spec,
        out_shape=jax.ShapeDtypeStruct((n_rows * ROW_CHUNKS, LANES), F32),
        compiler_params=_params("arbitrary"),
        name="moe_dispatch",
    )(pos_flat, counts, offsets, x_tiled)


def _moe_ffn_kernel(te_ref, nt_ref, x_ref, wg_ref, wu_ref, wd_ref, o_ref):
    i = pl.program_id(0)

    @pl.when(i < nt_ref[0])
    def _():
        xb = jnp.concatenate(
            [_load_row_tiled(x_ref, j, MOE_TILE).astype(BF16) for j in range(ROW_CHUNKS)], axis=1)
        gate = jnp.dot(xb, wg_ref[0], preferred_element_type=F32)
        up = jnp.dot(xb, wu_ref[0], preferred_element_type=F32)
        h = (_silu(gate) * up).astype(BF16)
        _store_row_tiled(o_ref, jnp.dot(h, wd_ref[0], preferred_element_type=F32))

    @pl.when(i >= nt_ref[0])
    def _():
        o_ref[...] = jnp.zeros_like(o_ref)


def moe_ffn(xs, tile_expert, n_tiles_used, wg, wu, wd):
    n_rows, d = xs.shape[0] // ROW_CHUNKS, D_MODEL
    n_tiles = n_rows // MOE_TILE
    tile_spec = lambda index: pl.BlockSpec((MOE_TILE * ROW_CHUNKS, LANES), index)
    expert = lambda w, buffers: pl.BlockSpec((1,) + w.shape[1:], lambda i, te, nt: (te[i], 0, 0),
                                             pipeline_mode=pl.Buffered(buffers))
    grid_spec = pltpu.PrefetchScalarGridSpec(
        num_scalar_prefetch=2,
        grid=(n_tiles,),
        in_specs=[tile_spec(lambda i, te, nt: (jnp.minimum(i, nt[0] - 1), 0)),
                  expert(wg, 2), expert(wu, 2), expert(wd, 1)],
        out_specs=tile_spec(lambda i, te, nt: (i, 0)),
    )
    return pl.pallas_call(
        _moe_ffn_kernel,
        grid_spec=grid_spec,
        out_shape=jax.ShapeDtypeStruct(xs.shape, F32),
        compiler_params=_params("arbitrary"),
        name="moe_ffn",
    )(tile_expert, n_tiles_used, xs, wg, wu, wd)


def _combine_kernel(pos_ref, x_ref, gate_ref, g_ref, b_ref, ys_ref, o_ref, buf_ref, sem, *, tb):
    i = pl.program_id(0)
    slot = i % 2

    def start_gathers(block, into):
        base = block * (TOP_K * tb)

        def issue(t, _):
            for k in range(TOP_K):
                pltpu.make_async_copy(ys_ref.at[_tile_rows(pos_ref[base + TOP_K * t + k])],
                                      buf_ref.at[into, k, _tile_rows(t)], sem.at[into]).start()
            return 0

        lax.fori_loop(0, tb, issue, 0, unroll=DMA_ISSUE_UNROLL)

    @pl.when(i == 0)
    def _():
        start_gathers(0, 0)

    @pl.when(i + 1 < pl.num_programs(0))
    def _():
        start_gathers(i + 1, 1 - slot)

    for k in range(TOP_K):
        pltpu.make_async_copy(ys_ref.at[pl.ds(0, tb * ROW_CHUNKS)], buf_ref.at[slot, k],
                              sem.at[slot]).wait()
    gates = gate_ref[...]
    y = jnp.concatenate(
        [buf_ref[slot, 0, pl.ds(j, tb, stride=ROW_CHUNKS), :] * gates[:, 0:1]
         + buf_ref[slot, 1, pl.ds(j, tb, stride=ROW_CHUNKS), :] * gates[:, 1:2]
         for j in range(ROW_CHUNKS)], axis=1)
    o_ref[...] = _layer_norm_rows(DN_ALPHA * x_ref[...] + y, g_ref[...], b_ref[...])


def moe_combine(x, ys, pos_flat, gates, g, b):
    t, d = x.shape
    tb = min(ROUTE_BLOCK, t)
    grid_spec = pltpu.PrefetchScalarGridSpec(
        num_scalar_prefetch=1,
        grid=(t // tb,),
        in_specs=[pl.BlockSpec((tb, d), lambda i, *_: (i, 0)),
                  pl.BlockSpec((tb, LANES), lambda i, *_: (i, 0)),
                  pl.BlockSpec((1, d), lambda i, *_: (0, 0)),
                  pl.BlockSpec((1, d), lambda i, *_: (0, 0)),
                  pl.BlockSpec(memory_space=pl.ANY)],
        out_specs=pl.BlockSpec((tb, d), lambda i, *_: (i, 0)),
        scratch_shapes=[pltpu.VMEM((2, TOP_K, tb * ROW_CHUNKS, LANES), F32),
                        pltpu.SemaphoreType.DMA((2,))],
    )
    return pl.pallas_call(
        functools.partial(_combine_kernel, tb=tb),
        grid_spec=grid_spec,
        out_shape=jax.ShapeDtypeStruct((t, d), F32),
        compiler_params=_params("arbitrary"),
        name="moe_combine",
    )(pos_flat, x, gates, g.reshape(1, d), b.reshape(1, d), ys)


def moe_deepnorm(routed, wg, wu, wd, g, b):
    x, x_tiled, idx, gates, rank, counts_f = routed
    t = x.shape[0]
    counts = counts_f[0, :N_EXPERTS].astype(jnp.int32)
    padded = (counts + MOE_TILE - 1) // MOE_TILE * MOE_TILE
    ends = jnp.cumsum(padded)
    offsets = ends - padded
    n_tiles = (TOP_K * t) // MOE_TILE + N_EXPERTS
    n_used = (ends[-1] // MOE_TILE).astype(jnp.int32)
    tile_start = jnp.arange(n_tiles, dtype=jnp.int32) * MOE_TILE
    tile_start = jnp.minimum(tile_start, ends[-1] - MOE_TILE)
    tile_expert = jnp.sum(tile_start[:, None] >= ends[None, :], axis=1).astype(jnp.int32)
    pos = moe_positions(idx, rank, offsets)
    pos_flat = pos[:, :TOP_K].reshape(-1)
    xs = moe_dispatch(x_tiled, pos_flat, counts, offsets, n_tiles * MOE_TILE)
    ys = moe_ffn(xs, tile_expert, n_used.reshape(1), wg.astype(BF16), wu.astype(BF16), wd.astype(BF16))
    return moe_combine(x, ys, pos_flat, gates, g, b)


def kernel(x, l0_ssd_w_in, l0_ssd_conv_w, l0_ssd_conv_b, l0_ssd_dt_bias, l0_ssd_a_log, l0_ssd_d_skip, l0_ssd_norm_w, l0_ssd_w_out, l0_ln_mix_g, l0_ln_mix_b, l0_ffn_w_gate, l0_ffn_w_up, l0_ffn_w_down, l0_ln_ffn_g, l0_ln_ffn_b, l1_sb_w_qkv, l1_sb_w_out, l1_ln_mix_g, l1_ln_mix_b, l1_moe_w_router, l1_moe_w_gate, l1_moe_w_up, l1_moe_w_down, l1_ln_ffn_g, l1_ln_ffn_b, l2_fox_w_qkvf, l2_fox_b_f, l2_fox_w_out, l2_ln_mix_g, l2_ln_mix_b, l2_ffn_w_gate, l2_ffn_w_up, l2_ffn_w_down, l2_ln_ffn_g, l2_ln_ffn_b, l3_ssd_w_in, l3_ssd_conv_w, l3_ssd_conv_b, l3_ssd_dt_bias, l3_ssd_a_log, l3_ssd_d_skip, l3_ssd_norm_w, l3_ssd_w_out, l3_ln_mix_g, l3_ln_mix_b, l3_moe_w_router, l3_moe_w_gate, l3_moe_w_up, l3_moe_w_down, l3_ln_ffn_g, l3_ln_ffn_b):
    batch, seq, d = x.shape
    h = x.reshape(batch * seq, d)
    bf = lambda w: w.astype(BF16)
    y = ssd_mixer(h, l0_ssd_w_in, l0_ssd_conv_w, l0_ssd_conv_b, l0_ssd_dt_bias, l0_ssd_a_log,
                  l0_ssd_d_skip, l0_ssd_norm_w, batch, seq)
    h = mixer_out_ffn_deepnorm(y, bf(l0_ssd_w_out), h, l0_ln_mix_g, l0_ln_mix_b,
                               bf(l0_ffn_w_gate), bf(l0_ffn_w_up), bf(l0_ffn_w_down), l0_ln_ffn_g, l0_ln_ffn_b)
    o = sb_mixer(h, l1_sb_w_qkv, batch, seq)
    routed = matmul_deepnorm_route(o, bf(l1_sb_w_out), h, l1_ln_mix_g, l1_ln_mix_b, l1_moe_w_router)
    h = moe_deepnorm(routed, l1_moe_w_gate, l1_moe_w_up, l1_moe_w_down, l1_ln_ffn_g, l1_ln_ffn_b)
    o = fox_mixer(h, l2_fox_w_qkvf, l2_fox_b_f, batch, seq)
    h = mixer_out_ffn_deepnorm(o, bf(l2_fox_w_out), h, l2_ln_mix_g, l2_ln_mix_b,
                               bf(l2_ffn_w_gate), bf(l2_ffn_w_up), bf(l2_ffn_w_down), l2_ln_ffn_g, l2_ln_ffn_b)
    y = ssd_mixer(h, l3_ssd_w_in, l3_ssd_conv_w, l3_ssd_conv_b, l3_ssd_dt_bias, l3_ssd_a_log,
                  l3_ssd_d_skip, l3_ssd_norm_w, batch, seq)
    routed = matmul_deepnorm_route(y, bf(l3_ssd_w_out), h, l3_ln_mix_g, l3_ln_mix_b, l3_moe_w_router)
    h = moe_deepnorm(routed, l3_moe_w_gate, l3_moe_w_up, l3_moe_w_down, l3_ln_ffn_g, l3_ln_ffn_b)
    return h.reshape(batch, seq, d)
```

```python
import functools

import jax
import jax.numpy as jnp
from jax import lax
from jax.experimental import pallas as pl
from jax.experimental.pallas import tpu as pltpu

F32 = jnp.float32
BF16 = jnp.bfloat16
HIGHEST = lax.Precision.HIGHEST

LANES = 128
SUBLANES = 8
VMEM_LIMIT_BYTES = 56 * 1024 * 1024

D_MODEL = 1024
DEPTH = 4
SSD_D_INNER = 2048
SSD_HEAD_DIM = 64
SSD_HEADS = 32
SSD_GROUPS = 4
SSD_STATE = 128
SSD_CONV = 4
SSD_GROUP_DIM = SSD_D_INNER // SSD_GROUPS
SSD_BC_DIM = 2 * SSD_GROUPS * SSD_STATE
ATT_HEAD_DIM = 64
ATT_HEADS = 16
ATT_DIM = 1024
N_EXPERTS = 8
TOP_K = 2
DN_ALPHA = (2.0 * DEPTH) ** 0.25
LN_EPS = 1e-5
RMS_EPS = 1e-5
NEG_BIG = -1e30


def _params(*semantics):
    return pltpu.CompilerParams(dimension_semantics=semantics, vmem_limit_bytes=VMEM_LIMIT_BYTES)


def _layer_norm_rows(h, g, b):
    mu = jnp.mean(h, axis=-1, keepdims=True)
    d = h - mu
    var = jnp.mean(d * d, axis=-1, keepdims=True)
    return d * lax.rsqrt(var + LN_EPS) * g + b


def _silu(x):
    half = 0.5 * x
    return half * (1.0 + jnp.tanh(half))


def _mm_kernel(x_ref, w_ref, o_ref, xb_ref):
    @pl.when(pl.program_id(1) == 0)
    def _():
        xb_ref[...] = x_ref[...].astype(BF16)

    o_ref[...] = jnp.dot(xb_ref[...], w_ref[...], preferred_element_type=F32).astype(o_ref.dtype)


def matmul(x, w, out_dtype, tm=1024, tn=1024):
    m, k = x.shape
    tm = min(tm, m)
    n = w.shape[1]
    return pl.pallas_call(
        _mm_kernel,
        grid=(m // tm, n // tn),
        in_specs=[pl.BlockSpec((tm, k), lambda i, j: (i, 0)),
                  pl.BlockSpec((k, tn), lambda i, j: (0, j))],
        out_specs=pl.BlockSpec((tm, tn), lambda i, j: (i, j)),
        out_shape=jax.ShapeDtypeStruct((m, n), out_dtype),
        scratch_shapes=[pltpu.VMEM((tm, k), BF16)],
        compiler_params=_params("parallel", "arbitrary"),
        name="matmul",
    )(x, w)


def _dot_split(x, w):
    xh = x.astype(BF16)
    xl = (x - xh.astype(F32)).astype(BF16)
    wh = w.astype(BF16)
    wl = (w - wh.astype(F32)).astype(BF16)
    dot = functools.partial(jnp.dot, preferred_element_type=F32)
    return dot(xh, wh) + (dot(xh, wl) + dot(xl, wh))


def _mm_f32_kernel(x_ref, w_ref, o_ref):
    o_ref[...] = _dot_split(x_ref[...], w_ref[...])


def matmul_f32(x, w, tm=1024):
    m, k = x.shape
    tm = min(tm, m)
    n = w.shape[1]
    return pl.pallas_call(
        _mm_f32_kernel,
        grid=(m // tm,),
        in_specs=[pl.BlockSpec((tm, k), lambda i: (i, 0)),
                  pl.BlockSpec((k, n), lambda i: (0, 0))],
        out_specs=pl.BlockSpec((tm, n), lambda i: (i, 0)),
        out_shape=jax.ShapeDtypeStruct((m, n), F32),
        compiler_params=_params("parallel"),
        name="matmul_f32",
    )(x, w)


ROW_CHUNKS = D_MODEL // LANES


def _store_row_tiled(ref, value):
    n = value.shape[0]
    for j in range(ROW_CHUNKS):
        ref[pl.ds(j, n, stride=ROW_CHUNKS), :] = value[:, j * LANES:(j + 1) * LANES]


def _load_row_tiled(ref, j, n):
    return ref[pl.ds(j, n, stride=ROW_CHUNKS), :]


def _mm_ln_route_kernel(x_ref, w_ref, r_ref, g_ref, b_ref, wr_ref,
                        o_ref, tiled_ref, idx_ref, gate_ref, rank_ref, count_ref, run_ref):
    @pl.when(pl.program_id(0) == 0)
    def _():
        run_ref[...] = jnp.zeros_like(run_ref)

    y = jnp.dot(x_ref[...], w_ref[...], preferred_element_type=F32)
    o = _layer_norm_rows(DN_ALPHA * r_ref[...] + y, g_ref[...], b_ref[...])
    o_ref[...] = o
    _store_row_tiled(tiled_ref, o)
    idx, gates = _top2(_dot_split(o, wr_ref[...]))
    idx_ref[...] = idx
    gate_ref[...] = gates
    rank_ref[...] = _rank_block(idx, run_ref)
    count_ref[...] = run_ref[...]


def matmul_deepnorm_route(x, w, resid, g, b, w_router):
    m, k = x.shape
    tm = min(ROUTE_BLOCK, m)
    d = w.shape[1]
    rows = lambda width: pl.BlockSpec((tm, width), lambda i: (i, 0))
    fixed = lambda a: pl.BlockSpec(a.shape, lambda i: (0, 0))
    args = [x, w, resid, g.reshape(1, d), b.reshape(1, d),
            jnp.pad(w_router, ((0, 0), (0, LANES - N_EXPERTS)))]
    return pl.pallas_call(
        _mm_ln_route_kernel,
        grid=(m // tm,),
        in_specs=[rows(k), fixed(w), rows(d), fixed(args[3]), fixed(args[4]), fixed(args[5])],
        out_specs=[rows(d), pl.BlockSpec((tm * ROW_CHUNKS, LANES), lambda i: (i, 0)),
                   rows(LANES), rows(LANES), rows(LANES), pl.BlockSpec((1, LANES), lambda i: (0, 0))],
        out_shape=[jax.ShapeDtypeStruct((m, d), F32),
                   jax.ShapeDtypeStruct((m * ROW_CHUNKS, LANES), F32),
                   jax.ShapeDtypeStruct((m, LANES), jnp.int32),
                   jax.ShapeDtypeStruct((m, LANES), F32),
                   jax.ShapeDtypeStruct((m, LANES), jnp.int32),
                   jax.ShapeDtypeStruct((1, LANES), F32)],
        scratch_shapes=[pltpu.VMEM((1, LANES), F32)],
        compiler_params=_params("arbitrary"),
        name="matmul_deepnorm_route",
    )(*args)


def _mix_ffn_kernel(o_ref, wo_ref, r_ref, g1_ref, b1_ref, wg_ref, wu_ref, wd_ref, g2_ref, b2_ref, out_ref):
    x = _layer_norm_rows(
        DN_ALPHA * r_ref[...] + jnp.dot(o_ref[...], wo_ref[...], preferred_element_type=F32),
        g1_ref[...], b1_ref[...])
    xb = x.astype(BF16)
    gate = jnp.dot(xb, wg_ref[...], preferred_element_type=F32)
    up = jnp.dot(xb, wu_ref[...], preferred_element_type=F32)
    h = (_silu(gate) * up).astype(BF16)
    y = jnp.dot(h, wd_ref[...], preferred_element_type=F32)
    out_ref[...] = _layer_norm_rows(DN_ALPHA * x + y, g2_ref[...], b2_ref[...])


def mixer_out_ffn_deepnorm(o, w_out, resid, g1, b1, wg, wu, wd, g2, b2, tm=512):
    m, k = o.shape
    d = w_out.shape[1]
    tm = min(tm, m)
    resident = lambda a: pl.BlockSpec(a.shape, lambda i: (0, 0), pipeline_mode=pl.Buffered(1))
    vec = lambda v: v.reshape(1, d)
    args = (o, w_out, resid, vec(g1), vec(b1), wg, wu, wd, vec(g2), vec(b2))
    return pl.pallas_call(
        _mix_ffn_kernel,
        grid=(m // tm,),
        in_specs=[pl.BlockSpec((tm, k), lambda i: (i, 0)), resident(w_out),
                  pl.BlockSpec((tm, d), lambda i: (i, 0)), resident(args[3]), resident(args[4]),
                  resident(wg), resident(wu), resident(wd), resident(args[8]), resident(args[9])],
        out_specs=pl.BlockSpec((tm, d), lambda i: (i, 0)),
        out_shape=jax.ShapeDtypeStruct((m, d), F32),
        compiler_params=_params("parallel"),
        name="mixer_out_ffn_deepnorm",
    )(*args)


def _split_bf16(v, pieces):
    out = []
    r = v
    for _ in range(pieces - 1):
        p = r.astype(BF16)
        out.append(p)
        r = r - p.astype(F32)
    out.append(r.astype(BF16))
    return out


def _expand(v, e_ref, pieces):
    stacked = jnp.concatenate(_split_bf16(v, pieces), axis=1)
    return jnp.dot(stacked, e_ref[...], preferred_element_type=F32)


def _tril_f32(n, strict=False):
    r = lax.broadcasted_iota(jnp.int32, (n, n), 0)
    c = lax.broadcasted_iota(jnp.int32, (n, n), 1)
    return ((r > c) if strict else (r >= c)).astype(F32)


def _ssd_kernel(z_ref, xs_ref, bc_ref, dt_ref,
                dtb_ref, alog_ref, dskip_ref, normw_ref, e64_ref, o_ref,
                state_ref, xdt_ref, cb_ref, ydiag_ref,
                *, chunk):
    L = chunk
    c = pl.program_id(1)

    @pl.when(c == 0)
    def _():
        state_ref[...] = jnp.zeros_like(state_ref)

    xs = xs_ref[...]
    bcv = bc_ref[...]

    dt = jax.nn.softplus(dt_ref[...] + dtb_ref[...])
    da = dt * (-jnp.exp(alog_ref[...]))
    a_cs = jnp.dot(_tril_f32(L), da, preferred_element_type=F32, precision=HIGHEST)
    ea = jnp.exp(a_cs)
    dte = jnp.exp(a_cs[L - 1:L, :] - a_cs)

    spread = _expand(jnp.concatenate([dt, ea, dte], axis=0), e64_ref, 2)
    dt_x, ea_x, dte_x = spread[0:L], spread[L:2 * L], spread[2 * L:3 * L]
    a_t = a_cs.T

    xdt = xs * dt_x
    xdt_ref[...] = xdt.astype(BF16)
    xdte = (xdt * dte_x).astype(BF16)

    y_off = []
    for g in range(SSD_GROUPS):
        bm = bcv[:, g * SSD_STATE:(g + 1) * SSD_STATE]
        cm = bcv[:, (SSD_GROUPS + g) * SSD_STATE:(SSD_GROUPS + g + 1) * SSD_STATE].astype(BF16)
        cb_ref[g] = lax.dot_general(cm, bm.astype(BF16), (((1,), (1,)), ((), ())),
                                    preferred_element_type=F32)
        gs = slice(g * SSD_GROUP_DIM, (g + 1) * SSD_GROUP_DIM)
        st = state_ref[g]
        y_off.append(jnp.dot(cm, st.astype(BF16), preferred_element_type=F32) * ea_x[:, gs])
        state_ref[g] = st * ea_x[L - 1:L, gs] + jnp.dot(
            bm.T.astype(BF16), xdte[:, gs], preferred_element_type=F32)

    row = lax.broadcasted_iota(jnp.int32, (L, L), 0)
    col = lax.broadcasted_iota(jnp.int32, (L, L), 1)
    causal = row >= col
    head0 = lax.broadcasted_iota(jnp.int32, (L, LANES), 1) < SSD_HEAD_DIM

    heads_per_group = SSD_HEADS // SSD_GROUPS

    for g in range(SSD_GROUPS):
        cbg = cb_ref[g]
        scores = []
        for j in range(heads_per_group):
            h = g * heads_per_group + j
            a_l = a_cs[:, h:h + 1]
            a_s = a_t[h:h + 1, :]
            scores.append((cbg * jnp.exp(jnp.where(causal, a_l - a_s, NEG_BIG))).astype(BF16))
        for pp in range(heads_per_group // 2):
            lanes = slice((g * (heads_per_group // 2) + pp) * LANES,
                          (g * (heads_per_group // 2) + pp + 1) * LANES)
            xpair = xdt_ref[:, lanes]
            ys = [jnp.dot(scores[2 * pp + j], xpair, preferred_element_type=F32) for j in range(2)]
            ydiag_ref[:, lanes] = jnp.where(head0, ys[0], ys[1])

    y = ydiag_ref[...] + jnp.concatenate(y_off, axis=1) + dskip_ref[...] * xs
    y = y * _silu(z_ref[...])
    parts = []
    for g in range(SSD_GROUPS):
        yg = y[:, g * SSD_GROUP_DIM:(g + 1) * SSD_GROUP_DIM]
        parts.append(yg * lax.rsqrt(jnp.mean(yg * yg, axis=-1, keepdims=True) + RMS_EPS))
    o_ref[...] = (jnp.concatenate(parts, axis=1) * normw_ref[...]).astype(o_ref.dtype)


def _expansion_matrix(width, pieces):
    h = jnp.arange(LANES)[:, None]
    lane = jnp.arange(SSD_HEADS * width)[None, :]
    e = (lane // width == h).astype(BF16)
    return jnp.concatenate([e] * pieces, axis=0)


def _mm_conv_kernel(x_ref, halo_ref, w_ref, cw_ref, cb_ref, o_ref, ext_ref, xb_ref, *, tiles_per_seq):
    tm = x_ref.shape[0]

    @pl.when(pl.program_id(1) == 0)
    def _():
        xb_ref[...] = x_ref[...].astype(BF16)

    w = w_ref[...]
    halo = jnp.dot(halo_ref[...].astype(BF16), w, preferred_element_type=F32)
    first = pl.program_id(0) % tiles_per_seq == 0
    ext_ref[0:SUBLANES, :] = jnp.where(first, 0.0, halo)
    ext_ref[SUBLANES:SUBLANES + tm, :] = jnp.dot(xb_ref[...], w, preferred_element_type=F32)
    ext = ext_ref[...]
    acc = cw_ref[0:1, :] * ext
    for k in range(1, SSD_CONV):
        acc = pltpu.roll(acc, 1, axis=0) + cw_ref[k:k + 1, :] * ext
    o_ref[...] = _silu(acc[SUBLANES:SUBLANES + tm, :] + cb_ref[...])


def matmul_conv_silu(x, w, conv_w, conv_b, seq, tm=1024, tn=1024):
    m, k = x.shape
    tm = min(tm, seq)
    n = w.shape[1]
    halo_blocks = tm // SUBLANES
    return pl.pallas_call(
        functools.partial(_mm_conv_kernel, tiles_per_seq=seq // tm),
        grid=(m // tm, n // tn),
        in_specs=[pl.BlockSpec((tm, k), lambda i, j: (i, 0)),
                  pl.BlockSpec((SUBLANES, k), lambda i, j: (jnp.maximum(i * halo_blocks - 1, 0), 0)),
                  pl.BlockSpec((k, tn), lambda i, j: (0, j)),
                  pl.BlockSpec((SSD_CONV, tn), lambda i, j: (0, j)),
                  pl.BlockSpec((1, tn), lambda i, j: (0, j))],
        out_specs=pl.BlockSpec((tm, tn), lambda i, j: (i, j)),
        out_shape=jax.ShapeDtypeStruct((m, n), F32),
        scratch_shapes=[pltpu.VMEM((tm + SUBLANES, tn), F32), pltpu.VMEM((tm, k), BF16)],
        compiler_params=_params("parallel", "arbitrary"),
        name="matmul_conv_silu",
    )(x, x, w, conv_w, conv_b.reshape(1, n))


def ssd_core(z, xbc, dt_raw, dt_bias, a_log, d_skip, norm_w, batch, seq, chunk=128):
    t = z.shape[0]
    nc = seq // chunk
    pad = LANES - SSD_HEADS
    row = lambda v: v.reshape(1, -1).astype(F32)
    args = (
        z, xbc, xbc, dt_raw,
        row(jnp.pad(dt_bias, (0, pad))), row(jnp.pad(a_log, (0, pad))),
        row(jnp.repeat(d_skip, SSD_HEAD_DIM)), row(norm_w),
        _expansion_matrix(SSD_HEAD_DIM, 2),
    )
    blk = lambda b, c: (b * nc + c, 0)
    const = lambda b, c: (0, 0)
    full = lambda a: pl.BlockSpec(a.shape, const)
    in_specs = [
        pl.BlockSpec((chunk, SSD_D_INNER), blk),
        pl.BlockSpec((chunk, SSD_D_INNER), blk),
        pl.BlockSpec((chunk, SSD_BC_DIM), lambda b, c: (b * nc + c, 2)),
        pl.BlockSpec((chunk, LANES), blk),
    ] + [full(a) for a in args[4:]]
    return pl.pallas_call(
        functools.partial(_ssd_kernel, chunk=chunk),
        grid=(batch, nc),
        in_specs=in_specs,
        out_specs=pl.BlockSpec((chunk, SSD_D_INNER), blk),
        out_shape=jax.ShapeDtypeStruct((t, SSD_D_INNER), BF16),
        scratch_shapes=[
            pltpu.VMEM((SSD_GROUPS, SSD_STATE, SSD_GROUP_DIM), F32),
            pltpu.VMEM((chunk, SSD_D_INNER), BF16),
            pltpu.VMEM((SSD_GROUPS, chunk, chunk), F32),
            pltpu.VMEM((chunk, SSD_D_INNER), F32),
        ],
        compiler_params=_params("parallel", "arbitrary"),
        name="ssd_core",
    )(*args)


def ssd_mixer(x, w_in, conv_w, conv_b, dt_bias, a_log, d_skip, norm_w, batch, seq):
    n_zx = SSD_D_INNER + SSD_D_INNER + SSD_BC_DIM
    z = matmul(x, w_in[:, :SSD_D_INNER].astype(BF16), F32)
    xbc = matmul_conv_silu(x, w_in[:, SSD_D_INNER:n_zx].astype(BF16), conv_w, conv_b, seq)
    w_dt = jnp.pad(w_in[:, n_zx:], ((0, 0), (0, LANES - SSD_HEADS)))
    dt_raw = matmul_f32(x, w_dt)
    return ssd_core(z, xbc, dt_raw, dt_bias, a_log, d_skip, norm_w, batch, seq)


ATT_BLOCK = 128
ATT_Q = 256
ATT_PAIRS = 4
ATT_STEP_LANES = ATT_PAIRS * LANES
ATT_SCALE = ATT_HEAD_DIM ** -0.5
LOG2E = 1.4426950408889634
EXP2_UNDERFLOW = -160.0
FOX_BOUND_SLACK = 1.05


def _split_heads(q_ref, rows, head0):
    qs = []
    for p in range(ATT_PAIRS):
        q2 = q_ref[rows, p * LANES:(p + 1) * LANES]
        zero = jnp.zeros_like(q2)
        qs += [jnp.where(head0, q2, zero), jnp.where(head0, zero, q2)]
    return qs


def _sb_kernel(q_ref, k_ref, v_ref, u_ref, o_ref, *scratch, seq):
    n_heads = 2 * ATT_PAIRS
    later_ref, acc_ref, z_ref = scratch[:n_heads], scratch[n_heads:2 * n_heads], scratch[2 * n_heads:]
    nq = seq // ATT_Q
    ndiag = ATT_Q // ATT_BLOCK
    head0 = lax.broadcasted_iota(jnp.int32, (ATT_Q, LANES), 1) < ATT_HEAD_DIM
    row = lax.broadcasted_iota(jnp.int32, (ATT_Q, ATT_BLOCK), 0)
    col = lax.broadcasted_iota(jnp.int32, (ATT_Q, ATT_BLOCK), 1)

    def q_body(i, _):
        q0 = pl.multiple_of(i * ATT_Q, ATT_Q)
        rows = pl.ds(q0, ATT_Q)
        qs = _split_heads(q_ref, rows, head0)
        for n in range(2 * ATT_PAIRS):
            later_ref[n][...] = jnp.zeros((ATT_Q, LANES), F32)
            acc_ref[n][...] = jnp.zeros((ATT_Q, LANES), F32)

        heads = range(2 * ATT_PAIRS)

        def key_rows(j):
            return pl.ds(pl.multiple_of(j * ATT_BLOCK, ATT_BLOCK), ATT_BLOCK)

        def scores(j):
            k2 = [k_ref[key_rows(j), p * LANES:(p + 1) * LANES] for p in range(ATT_PAIRS)]
            return [lax.dot_general(qs[n], k2[n // 2], (((1,), (1,)), ((), ())),
                                    preferred_element_type=F32) * (ATT_SCALE * LOG2E) for n in heads]

        def consume(z2, j, strict):
            keys = key_rows(j)
            v2 = [v_ref[keys, p * LANES:(p + 1) * LANES] for p in range(ATT_PAIRS)]
            log_beta, sums = [], []
            for n in heads:
                lb = jnp.minimum(z2[n], 0.0) - jnp.log2(1.0 + jnp.exp2(-jnp.abs(z2[n])))
                log_keep = lb - z2[n]
                if strict is not None:
                    log_keep = jnp.where(strict, log_keep, 0.0)
                hi = log_keep.astype(BF16)
                lo = (log_keep - hi.astype(F32)).astype(BF16)
                log_beta.append(lb)
                sums.append(jnp.dot(jnp.concatenate([hi, lo], axis=1), u_ref[...],
                                    preferred_element_type=F32))
            for n in heads:
                w = jnp.exp2(log_beta[n] + sums[n][:, :ATT_BLOCK] + later_ref[n][...])
                if strict is not None:
                    w = jnp.where(strict, w, 0.0)
                acc_ref[n][...] += jnp.dot(w.astype(BF16), v2[n // 2], preferred_element_type=F32)
                later_ref[n][...] += sums[n][:, ATT_BLOCK:]

        n_off = i * ndiag
        diag = [(d, scores(n_off + d)) for d in reversed(range(ndiag))]
        first_off = scores(jnp.maximum(n_off - 1, 0))
        for d, zd in diag:
            consume(zd, n_off + d, d * ATT_BLOCK + col < row)

        for n, zn in enumerate(first_off):
            z_ref[n][...] = zn

        def off_diagonal(carry):
            jj, _ = carry
            z2 = [z_ref[n][...] for n in heads]
            ahead = scores(jnp.maximum(n_off - 2 - jj, 0))
            consume(z2, n_off - 1 - jj, None)
            for n in heads:
                z_ref[n][...] = ahead[n]
            top = later_ref[0][...]
            for n in range(1, 2 * ATT_PAIRS):
                top = jnp.maximum(top, later_ref[n][...])
            return jj + 1, jnp.max(top) > EXP2_UNDERFLOW

        lax.while_loop(lambda c: (c[0] < n_off) & c[1], off_diagonal, (jnp.int32(0), n_off > 0))
        for p in range(ATT_PAIRS):
            o_ref[rows, p * LANES:(p + 1) * LANES] = jnp.where(
                head0, acc_ref[2 * p][...], acc_ref[2 * p + 1][...]).astype(o_ref.dtype)
        return 0

    lax.fori_loop(0, nq, q_body, 0)


def _suffix_sum_matrix():
    j = jnp.arange(2 * ATT_BLOCK)[:, None] % ATT_BLOCK
    s = jnp.arange(2 * ATT_BLOCK)[None, :]
    return jnp.where(s < ATT_BLOCK, j > s, True).astype(BF16)


def _attention_specs(seq):
    n_steps = ATT_HEADS // 2 // ATT_PAIRS
    blk = lambda off: pl.BlockSpec((seq, ATT_STEP_LANES), lambda b, p: (b, off + p))
    specs = [blk(0), blk(n_steps), blk(2 * n_steps)]
    return n_steps, specs, pl.BlockSpec((seq, ATT_STEP_LANES), lambda b, p: (b, p))


def sb_attention(qkv, batch, seq):
    n_steps, in_specs, out_spec = _attention_specs(seq)
    u = _suffix_sum_matrix()
    return pl.pallas_call(
        functools.partial(_sb_kernel, seq=seq),
        grid=(batch, n_steps),
        in_specs=in_specs + [pl.BlockSpec(u.shape, lambda b, p: (0, 0))],
        out_specs=out_spec,
        out_shape=jax.ShapeDtypeStruct((batch * seq, ATT_DIM), BF16),
        scratch_shapes=[pltpu.VMEM((ATT_Q, LANES), F32)] * (6 * ATT_PAIRS),
        compiler_params=_params("parallel", "parallel"),
        name="sb_attention",
    )(qkv, qkv, qkv, u)


def sb_mixer(x, w_qkv, batch, seq):
    return sb_attention(matmul(x, w_qkv.astype(BF16), BF16), batch, seq)


CUMSUM_BLOCK = 256


def _fox_decay_kernel(f_ref, bf_ref, ccol_ref, crow_ref, *, seq):
    tri = _tril_f32(CUMSUM_BLOCK)
    carry = jnp.zeros((1, LANES), F32)
    for blk in range(seq // CUMSUM_BLOCK):
        rows = slice(blk * CUMSUM_BLOCK, (blk + 1) * CUMSUM_BLOCK)
        log_f = jax.nn.log_sigmoid(f_ref[rows, :] + bf_ref[...])
        c = jnp.dot(tri, log_f, preferred_element_type=F32, precision=HIGHEST) + carry
        carry = c[CUMSUM_BLOCK - 1:CUMSUM_BLOCK, :]
        ccol_ref[rows, :] = c
        c_t = c.T
        for p in range(ATT_HEADS // 2):
            crow_ref[0, p, :, rows] = c_t[2 * p:2 * p + 2, :]


def fox_decay(f_raw, b_f, batch, seq):
    return pl.pallas_call(
        functools.partial(_fox_decay_kernel, seq=seq),
        grid=(batch,),
        in_specs=[pl.BlockSpec((seq, LANES), lambda b: (b, 0)),
                  pl.BlockSpec((1, LANES), lambda b: (0, 0))],
        out_specs=[pl.BlockSpec((seq, LANES), lambda b: (b, 0)),
                   pl.BlockSpec((1, ATT_HEADS // 2, 2, seq), lambda b: (b, 0, 0, 0))],
        out_shape=[jax.ShapeDtypeStruct((batch * seq, LANES), F32),
                   jax.ShapeDtypeStruct((batch, ATT_HEADS // 2, 2, seq), F32)],
        compiler_params=_params("parallel"),
        name="fox_decay",
    )(f_raw, jnp.pad(b_f, (0, LANES - ATT_HEADS)).reshape(1, LANES))


def _fox_kernel(q_ref, k_ref, v_ref, ccol_ref, crow_ref, o_ref, crep_ref, vt_ref, *scratch, seq):
    n_heads = 2 * ATT_PAIRS
    m_ref, acc_ref, s_ref = scratch[:n_heads], scratch[n_heads:2 * n_heads], scratch[2 * n_heads:]
    nq = seq // ATT_Q
    ndiag = ATT_Q // ATT_BLOCK
    step = pl.program_id(1)
    heads = range(n_heads)
    head0 = lax.broadcasted_iota(jnp.int32, (ATT_Q, LANES), 1) < ATT_HEAD_DIM
    head0_rows = lax.broadcasted_iota(jnp.int32, (LANES, ATT_BLOCK), 0) < ATT_HEAD_DIM
    key = lax.broadcasted_iota(jnp.int32, (ATT_BLOCK, ATT_Q), 0)
    qry = lax.broadcasted_iota(jnp.int32, (ATT_BLOCK, ATT_Q), 1)

    pieces = jnp.concatenate(_split_bf16(ccol_ref[...], 3), axis=1)
    sel_row = lax.broadcasted_iota(jnp.int32, (3 * LANES, LANES), 0) & (LANES - 1)
    for n in heads:
        sel = jnp.where(sel_row == 2 * ATT_PAIRS * step + n, 1.0, 0.0).astype(BF16)
        crep_ref[n] = jnp.dot(pieces, sel, preferred_element_type=F32) * LOG2E
    for p in range(ATT_PAIRS):
        for blk in range(seq // ATT_BLOCK):
            rows = slice(blk * ATT_BLOCK, (blk + 1) * ATT_BLOCK)
            vt_ref[p, :, rows] = v_ref[rows, p * LANES:(p + 1) * LANES].astype(F32).T.astype(BF16)
    lane_r = lax.broadcasted_iota(jnp.int32, (LANES, LANES), 0) < ATT_HEAD_DIM
    lane_c = lax.broadcasted_iota(jnp.int32, (LANES, LANES), 1) < ATT_HEAD_DIM
    same_head = jnp.where(lane_r == lane_c, 1.0, 0.0).astype(BF16)
    k_norm = []
    for p in range(ATT_PAIRS):
        kf = k_ref[:, p * LANES:(p + 1) * LANES].astype(F32)
        sq = jnp.dot((kf * kf).astype(BF16), same_head, preferred_element_type=F32)
        top = jnp.sqrt(jnp.max(sq, axis=0, keepdims=True))
        k_norm += [top[:, 0:1], top[:, ATT_HEAD_DIM:ATT_HEAD_DIM + 1]]
    ones_rows = jnp.ones((SUBLANES, LANES), BF16)

    def q_body(i, _):
        q0 = pl.multiple_of(i * ATT_Q, ATT_Q)
        rows = pl.ds(q0, ATT_Q)
        qs = _split_heads(q_ref, rows, head0)
        c_q = [crow_ref[0, n // 2, n % 2:n % 2 + 1, rows] * LOG2E for n in heads]
        qk_bound = []
        for n in heads:
            qf = qs[n].astype(F32)
            q_sq = lax.dot_general(ones_rows, (qf * qf).astype(BF16), (((1,), (1,)), ((), ())),
                                   preferred_element_type=F32)[0:1, :]
            qk_bound.append(jnp.sqrt(q_sq) * k_norm[n] * (ATT_SCALE * LOG2E * FOX_BOUND_SLACK))
        for n in heads:
            m_ref[n][...] = jnp.full((1, ATT_Q), NEG_BIG, F32)
            acc_ref[n][...] = jnp.zeros((LANES, ATT_Q), F32)

        def key_rows(j):
            return pl.ds(pl.multiple_of(j * ATT_BLOCK, ATT_BLOCK), ATT_BLOCK)

        def scores(j):
            k2 = [k_ref[key_rows(j), p * LANES:(p + 1) * LANES] for p in range(ATT_PAIRS)]
            return [lax.dot_general(k2[n // 2], qs[n], (((1,), (1,)), ((), ())),
                                    preferred_element_type=F32) * (ATT_SCALE * LOG2E) for n in heads]

        def consume(s, j, causal):
            keys = key_rows(j)
            vt = [vt_ref[p, :, keys] for p in range(ATT_PAIRS)]
            for n in heads:
                c_k = crep_ref[n, keys, :]
                sn = s[n] + (c_q[n] - jnp.concatenate([c_k] * (ATT_Q // LANES), axis=1))
                if causal is not None:
                    sn = jnp.where(causal, sn, NEG_BIG)
                m_old = m_ref[n][...]
                m_new = jnp.maximum(m_old, jnp.max(sn, axis=0, keepdims=True))
                prob = jnp.exp2(sn - m_new).astype(BF16)
                one = jnp.ones_like(vt[n // 2])
                v1t = jnp.where(head0_rows, vt[n // 2], one) if n % 2 == 0 else jnp.where(
                    head0_rows, one, vt[n // 2])
                acc_ref[n][...] = jnp.exp2(m_old - m_new) * acc_ref[n][...] + jnp.dot(
                    v1t, prob, preferred_element_type=F32)
                m_ref[n][...] = m_new

        n_off = i * ndiag
        diag = [(d, scores(n_off + d)) for d in range(ndiag)]
        first_off = scores(jnp.maximum(n_off - 1, 0))
        for d, sd in diag:
            consume(sd, n_off + d, d * ATT_BLOCK + key <= qry)

        for n, sn in enumerate(first_off):
            s_ref[n][...] = sn

        def off_diagonal(carry):
            jj, _ = carry
            j = n_off - 1 - jj
            s = [s_ref[n][...] for n in heads]
            ahead = scores(jnp.maximum(j - 1, 0))
            consume(s, j, None)
            for n in heads:
                s_ref[n][...] = ahead[n]
            prev_keys = key_rows(jnp.maximum(j - 1, 0))
            worst = None
            for n in heads:
                c_last = crow_ref[0, n // 2, n % 2:n % 2 + 1, prev_keys][:, LANES - 1:LANES] * LOG2E
                gap = qk_bound[n] + c_q[n] - c_last - m_ref[n][...]
                worst = gap if worst is None else jnp.maximum(worst, gap)
            return jj + 1, jnp.max(worst) > EXP2_UNDERFLOW

        lax.while_loop(lambda c: (c[0] < n_off) & c[1], off_diagonal, (jnp.int32(0), n_off > 0))
        for p in range(ATT_PAIRS):
            a0, a1 = acc_ref[2 * p][...], acc_ref[2 * p + 1][...]
            o_t = jnp.concatenate([a0[:ATT_HEAD_DIM] / a0[ATT_HEAD_DIM:],
                                   a1[ATT_HEAD_DIM:] / a1[:ATT_HEAD_DIM]], axis=0)
            o_ref[rows, p * LANES:(p + 1) * LANES] = o_t.T.astype(o_ref.dtype)
        return 0

    lax.fori_loop(0, nq, q_body, 0)


def fox_attention(qkv, ccol, crow, batch, seq):
    n_steps, in_specs, out_spec = _attention_specs(seq)
    in_specs += [pl.BlockSpec((seq, LANES), lambda b, p: (b, 0)),
                 pl.BlockSpec((1, ATT_PAIRS, 2, seq), lambda b, p: (b, p, 0, 0))]
    return pl.pallas_call(
        functools.partial(_fox_kernel, seq=seq),
        grid=(batch, n_steps),
        in_specs=in_specs,
        out_specs=out_spec,
        out_shape=jax.ShapeDtypeStruct((batch * seq, ATT_DIM), BF16),
        scratch_shapes=([pltpu.VMEM((2 * ATT_PAIRS, seq, LANES), F32),
                         pltpu.VMEM((ATT_PAIRS, LANES, seq), BF16)]
                        + [pltpu.VMEM((1, ATT_Q), F32)] * (2 * ATT_PAIRS)
                        + [pltpu.VMEM((LANES, ATT_Q), F32)] * (2 * ATT_PAIRS)
                        + [pltpu.VMEM((ATT_BLOCK, ATT_Q), F32)] * (2 * ATT_PAIRS)),
        compiler_params=_params("parallel", "parallel"),
        name="fox_attention",
    )(qkv, qkv, qkv, ccol, crow)


def fox_mixer(x, w_qkvf, b_f, batch, seq):
    qkv = matmul(x, w_qkvf[:, :3 * ATT_DIM].astype(BF16), BF16)
    w_f = jnp.pad(w_qkvf[:, 3 * ATT_DIM:], ((0, 0), (0, LANES - ATT_HEADS)))
    ccol, crow = fox_decay(matmul_f32(x, w_f), b_f, batch, seq)
    return fox_attention(qkv, ccol, crow, batch, seq)


MOE_TILE = 512
ROUTE_BLOCK = 512


def _top2(logits):
    lane = lax.broadcasted_iota(jnp.int32, logits.shape, 1)
    logits = jnp.where(lane < N_EXPERTS, logits, NEG_BIG)
    m1 = jnp.max(logits, axis=-1, keepdims=True)
    i1 = jnp.min(jnp.where(logits == m1, lane, LANES), axis=-1, keepdims=True)
    rest = jnp.where(lane == i1, NEG_BIG, logits)
    m2 = jnp.max(rest, axis=-1, keepdims=True)
    i2 = jnp.min(jnp.where(rest == m2, lane, LANES), axis=-1, keepdims=True)
    e2 = jnp.exp(m2 - m1)
    denom = 1.0 + e2
    idx = jnp.where(lane == 0, i1, jnp.where(lane == 1, i2, 0))
    gates = jnp.where(lane == 0, 1.0 / denom, jnp.where(lane == 1, e2 / denom, 0.0))
    return idx, gates


def _rank_block(idx, run_ref):
    tb = idx.shape[0]
    lane = lax.broadcasted_iota(jnp.int32, idx.shape, 1)
    oh0 = lane == idx[:, 0:1]
    oh1 = lane == idx[:, 1:2]
    both = jnp.where(oh0 | oh1, 1.0, 0.0)
    before = jnp.dot(_tril_f32(tb, strict=True).astype(BF16), both.astype(BF16),
                     preferred_element_type=F32) + run_ref[...]
    r0 = jnp.sum(jnp.where(oh0, before, 0.0), axis=-1, keepdims=True)
    r1 = jnp.sum(jnp.where(oh1, before, 0.0), axis=-1, keepdims=True)
    run_ref[...] += jnp.sum(both, axis=0, keepdims=True)
    return jnp.where(lane == 0, r0, jnp.where(lane == 1, r1, 0.0)).astype(jnp.int32)


def _pos_kernel(idx_ref, rank_ref, off_ref, pos_ref):
    idx = idx_ref[...]
    lane = lax.broadcasted_iota(jnp.int32, idx.shape, 1)
    off = off_ref[...]
    p0 = jnp.sum(jnp.where(lane == idx[:, 0:1], off, 0), axis=-1, keepdims=True)
    p1 = jnp.sum(jnp.where(lane == idx[:, 1:2], off, 0), axis=-1, keepdims=True)
    pos_ref[...] = rank_ref[...] + jnp.where(lane == 0, p0, jnp.where(lane == 1, p1, 0))


def moe_positions(idx, rank, offsets):
    t = idx.shape[0]
    tb = min(4 * ROUTE_BLOCK, t)
    off = jnp.pad(offsets, (0, LANES - N_EXPERTS)).reshape(1, LANES)
    blk = pl.BlockSpec((tb, LANES), lambda i: (i, 0))
    return pl.pallas_call(
        _pos_kernel,
        grid=(t // tb,),
        in_specs=[blk, blk, pl.BlockSpec((1, LANES), lambda i: (0, 0))],
        out_specs=blk,
        out_shape=jax.ShapeDtypeStruct((t, LANES), jnp.int32),
        compiler_params=_params("parallel"),
        name="moe_positions",
    )(idx, rank, off)


DMA_ISSUE_UNROLL = 8


def _tile_rows(row):
    return pl.ds(pl.multiple_of(row * ROW_CHUNKS, ROW_CHUNKS), ROW_CHUNKS)


def _dispatch_kernel(pos_ref, cnt_ref, off_ref, x_ref, xs_ref, zero_ref, sem, pad_sem, *, tb):
    base = pl.program_id(0) * (TOP_K * tb)

    def issue(t, _):
        for k in range(TOP_K):
            pltpu.make_async_copy(x_ref.at[_tile_rows(t)],
                                  xs_ref.at[_tile_rows(pos_ref[base + TOP_K * t + k])], sem).start()
        return 0

    lax.fori_loop(0, tb, issue, 0, unroll=DMA_ISSUE_UNROLL)

    @pl.when(pl.program_id(0) == 0)
    def _():
        zero_ref[...] = jnp.zeros_like(zero_ref)

        def fill_range(first, count):
            def copy(r):
                return pltpu.make_async_copy(zero_ref, xs_ref.at[_tile_rows(first + r)], pad_sem)

            def fill(r, _):
                copy(r).start()
                return 0

            def drain(r, _):
                copy(r).wait()
                return 0

            lax.fori_loop(0, count, fill, 0)
            lax.fori_loop(0, count, drain, 0)

        for e in range(N_EXPERTS):
            fill_range(off_ref[e] + cnt_ref[e], (-cnt_ref[e]) & (MOE_TILE - 1))
        last = N_EXPERTS - 1
        used = off_ref[last] + cnt_ref[last] + ((-cnt_ref[last]) & (MOE_TILE - 1))
        fill_range(used, xs_ref.shape[0] // ROW_CHUNKS - used)

    for _ in range(TOP_K):
        pltpu.make_async_copy(x_ref, xs_ref.at[pl.ds(0, tb * ROW_CHUNKS)], sem).wait()


def moe_dispatch(x_tiled, pos_flat, counts, offsets, n_rows):
    t = x_tiled.shape[0] // ROW_CHUNKS
    tb = min(ROUTE_BLOCK, t)
    grid_spec = pltpu.PrefetchScalarGridSpec(
        num_scalar_prefetch=3,
        grid=(t // tb,),
        in_specs=[pl.BlockSpec((tb * ROW_CHUNKS, LANES), lambda i, *_: (i, 0))],
        out_specs=pl.BlockSpec(memory_space=pl.ANY),
        scratch_shapes=[pltpu.VMEM((ROW_CHUNKS, LANES), F32), pltpu.SemaphoreType.DMA(()),
                        pltpu.SemaphoreType.DMA(())],
    )
    return pl.pallas_call(
        functools.partial(_dispatch_kernel, tb=tb),
        grid_spec=grid_spec,
        out_shape=jax.ShapeDtypeStruct((n_rows * ROW_CHUNKS, LANES), F32),
        compiler_params=_params("arbitrary"),
        name="moe_dispatch",
    )(pos_flat, counts, offsets, x_tiled)


def _moe_ffn_kernel(te_ref, nt_ref, x_ref, wg_ref, wu_ref, wd_ref, o_ref):
    i = pl.program_id(0)

    @pl.when(i < nt_ref[0])
    def _():
        xb = jnp.concatenate(
            [_load_row_tiled(x_ref, j, MOE_TILE).astype(BF16) for j in range(ROW_CHUNKS)], axis=1)
        gate = jnp.dot(xb, wg_ref[0], preferred_element_type=F32)
        up = jnp.dot(xb, wu_ref[0], preferred_element_type=F32)
        h = (_silu(gate) * up).astype(BF16)
        _store_row_tiled(o_ref, jnp.dot(h, wd_ref[0], preferred_element_type=F32))

    @pl.when(i >= nt_ref[0])
    def _():
        o_ref[...] = jnp.zeros_like(o_ref)


def moe_ffn(xs, tile_expert, n_tiles_used, wg, wu, wd):
    n_rows, d = xs.shape[0] // ROW_CHUNKS, D_MODEL
    n_tiles = n_rows // MOE_TILE
    tile_spec = lambda index: pl.BlockSpec((MOE_TILE * ROW_CHUNKS, LANES), index)
    expert = lambda w, buffers: pl.BlockSpec((1,) + w.shape[1:], lambda i, te, nt: (te[i], 0, 0),
                                             pipeline_mode=pl.Buffered(buffers))
    grid_spec = pltpu.PrefetchScalarGridSpec(
        num_scalar_prefetch=2,
        grid=(n_tiles,),
        in_specs=[tile_spec(lambda i, te, nt: (jnp.minimum(i, nt[0] - 1), 0)),
                  expert(wg, 2), expert(wu, 2), expert(wd, 1)],
        out_specs=tile_spec(lambda i, te, nt: (i, 0)),
    )
    return pl.pallas_call(
        _moe_ffn_kernel,
        grid_spec=grid_spec,
        out_shape=jax.ShapeDtypeStruct(xs.shape, F32),
        compiler_params=_params("arbitrary"),
        name="moe_ffn",
    )(tile_expert, n_tiles_used, xs, wg, wu, wd)


def _combine_kernel(pos_ref, x_ref, gate_ref, g_ref, b_ref, ys_ref, o_ref, buf_ref, sem, *, tb):
    i = pl.program_id(0)
    slot = i % 2

    def start_gathers(block, into):
        base = block * (TOP_K * tb)

        def issue(t, _):
            for k in range(TOP_K):
                pltpu.make_async_copy(ys_ref.at[_tile_rows(pos_ref[base + TOP_K * t + k])],
                                      buf_ref.at[into, k, _tile_rows(t)], sem.at[into]).start()
            return 0

        lax.fori_loop(0, tb, issue, 0, unroll=DMA_ISSUE_UNROLL)

    @pl.when(i == 0)
    def _():
        start_gathers(0, 0)

    @pl.when(i + 1 < pl.num_programs(0))
    def _():
        start_gathers(i + 1, 1 - slot)

    for k in range(TOP_K):
        pltpu.make_async_copy(ys_ref.at[pl.ds(0, tb * ROW_CHUNKS)], buf_ref.at[slot, k],
                              sem.at[slot]).wait()
    gates = gate_ref[...]
    y = jnp.concatenate(
        [buf_ref[slot, 0, pl.ds(j, tb, stride=ROW_CHUNKS), :] * gates[:, 0:1]
         + buf_ref[slot, 1, pl.ds(j, tb, stride=ROW_CHUNKS), :] * gates[:, 1:2]
         for j in range(ROW_CHUNKS)], axis=1)
    o_ref[...] = _layer_norm_rows(DN_ALPHA * x_ref[...] + y, g_ref[...], b_ref[...])


def moe_combine(x, ys, pos_flat, gates, g, b):
    t, d = x.shape
    tb = min(ROUTE_BLOCK, t)
    grid_spec = pltpu.PrefetchScalarGridSpec(
        num_scalar_prefetch=1,
        grid=(t // tb,),
        in_specs=[pl.BlockSpec((tb, d), lambda i, *_: (i, 0)),
                  pl.BlockSpec((tb, LANES), lambda i, *_: (i, 0)),
                  pl.BlockSpec((1, d), lambda i, *_: (0, 0)),
                  pl.BlockSpec((1, d), lambda i, *_: (0, 0)),
                  pl.BlockSpec(memory_space=pl.ANY)],
        out_specs=pl.BlockSpec((tb, d), lambda i, *_: (i, 0)),
        scratch_shapes=[pltpu.VMEM((2, TOP_K, tb * ROW_CHUNKS, LANES), F32),
                        pltpu.SemaphoreType.DMA((2,))],
    )
    return pl.pallas_call(
        functools.partial(_combine_kernel, tb=tb),
        grid_spec=grid_spec,
        out_shape=jax.ShapeDtypeStruct((t, d), F32),
        compiler_params=_params("arbitrary"),
        name="moe_combine",
    )(pos_flat, x, gates, g.reshape(1, d), b.reshape(1, d), ys)


def moe_deepnorm(routed, wg, wu, wd, g, b):
    x, x_tiled, idx, gates, rank, counts_f = routed
    t = x.shape[0]
    counts = counts_f[0, :N_EXPERTS].astype(jnp.int32)
    padded = (counts + MOE_TILE - 1) // MOE_TILE * MOE_TILE
    ends = jnp.cumsum(padded)
    offsets = ends - padded
    n_tiles = (TOP_K * t) // MOE_TILE + N_EXPERTS
    n_used = (ends[-1] // MOE_TILE).astype(jnp.int32)
    tile_start = jnp.arange(n_tiles, dtype=jnp.int32) * MOE_TILE
    tile_start = jnp.minimum(tile_start, ends[-1] - MOE_TILE)
    tile_expert = jnp.sum(tile_start[:, None] >= ends[None, :], axis=1).astype(jnp.int32)
    pos = moe_positions(idx, rank, offsets)
    pos_flat = pos[:, :TOP_K].reshape(-1)
    xs = moe_dispatch(x_tiled, pos_flat, counts, offsets, n_tiles * MOE_TILE)
    ys = moe_ffn(xs, tile_expert, n_used.reshape(1), wg.astype(BF16), wu.astype(BF16), wd.astype(BF16))
    return moe_combine(x, ys, pos_flat, gates, g, b)


def kernel(x, l0_ssd_w_in, l0_ssd_conv_w, l0_ssd_conv_b, l0_ssd_dt_bias, l0_ssd_a_log, l0_ssd_d_skip, l0_ssd_norm_w, l0_ssd_w_out, l0_ln_mix_g, l0_ln_mix_b, l0_ffn_w_gate, l0_ffn_w_up, l0_ffn_w_down, l0_ln_ffn_g, l0_ln_ffn_b, l1_sb_w_qkv, l1_sb_w_out, l1_ln_mix_g, l1_ln_mix_b, l1_moe_w_router, l1_moe_w_gate, l1_moe_w_up, l1_moe_w_down, l1_ln_ffn_g, l1_ln_ffn_b, l2_fox_w_qkvf, l2_fox_b_f, l2_fox_w_out, l2_ln_mix_g, l2_ln_mix_b, l2_ffn_w_gate, l2_ffn_w_up, l2_ffn_w_down, l2_ln_ffn_g, l2_ln_ffn_b, l3_ssd_w_in, l3_ssd_conv_w, l3_ssd_conv_b, l3_ssd_dt_bias, l3_ssd_a_log, l3_ssd_d_skip, l3_ssd_norm_w, l3_ssd_w_out, l3_ln_mix_g, l3_ln_mix_b, l3_moe_w_router, l3_moe_w_gate, l3_moe_w_up, l3_moe_w_down, l3_ln_ffn_g, l3_ln_ffn_b):
    batch, seq, d = x.shape
    h = x.reshape(batch * seq, d)
    bf = lambda w: w.astype(BF16)
    y = ssd_mixer(h, l0_ssd_w_in, l0_ssd_conv_w, l0_ssd_conv_b, l0_ssd_dt_bias, l0_ssd_a_log,
                  l0_ssd_d_skip, l0_ssd_norm_w, batch, seq)
    h = mixer_out_ffn_deepnorm(y, bf(l0_ssd_w_out), h, l0_ln_mix_g, l0_ln_mix_b,
                               bf(l0_ffn_w_gate), bf(l0_ffn_w_up), bf(l0_ffn_w_down), l0_ln_ffn_g, l0_ln_ffn_b)
    o = sb_mixer(h, l1_sb_w_qkv, batch, seq)
    routed = matmul_deepnorm_route(o, bf(l1_sb_w_out), h, l1_ln_mix_g, l1_ln_mix_b, l1_moe_w_router)
    h = moe_deepnorm(routed, l1_moe_w_gate, l1_moe_w_up, l1_moe_w_down, l1_ln_ffn_g, l1_ln_ffn_b)
    o = fox_mixer(h, l2_fox_w_qkvf, l2_fox_b_f, batch, seq)
    h = mixer_out_ffn_deepnorm(o, bf(l2_fox_w_out), h, l2_ln_mix_g, l2_ln_mix_b,
                               bf(l2_ffn_w_gate), bf(l2_ffn_w_up), bf(l2_ffn_w_down), l2_ln_ffn_g, l2_ln_ffn_b)
    y = ssd_mixer(h, l3_ssd_w_in, l3_ssd_conv_w, l3_ssd_conv_b, l3_ssd_dt_bias, l3_ssd_a_log,
                  l3_ssd_d_skip, l3_ssd_norm_w, batch, seq)
    routed = matmul_deepnorm_route(y, bf(l3_ssd_w_out), h, l3_ln_mix_g, l3_ln_mix_b, l3_moe_w_router)
    h = moe_deepnorm(routed, l3_moe_w_gate, l3_moe_w_up, l3_moe_w_down, l3_ln_ffn_g, l3_ln_ffn_b)
    return h.reshape(batch, seq, d)
```
